```python
import math
import jax, jax.numpy as jnp
from jax import lax
import numpy as np

D_MODEL = 1024
BATCH = 8
SEQ = 2048
DEPTH = 2

CHUNK = 64
Q_BLOCK = 128
ROPE_THETA = 10000.0
LN_EPS = 1e-5
RMS_EPS = 1e-6
DN_ALPHA = (2 * DEPTH) ** 0.25
DN_BETA = (8 * DEPTH) ** -0.25
N_MIXERS = 2
N_MLA = (DEPTH + 1) // 2
N_DSA = DEPTH // 2

MLA_HEADS = 8
MLA_NOPE = 128
MLA_ROPE = 64
MLA_V = 128
MLA_Q_RANK = 384
MLA_KV_RANK = 256
MLA_IN = MLA_Q_RANK + MLA_KV_RANK + MLA_ROPE

DSA_HEADS = 8
DSA_HEAD_DIM = D_MODEL // DSA_HEADS
IDX_HEADS = 8
IDX_DIM = 64
DSA_TOPK_MAX = 256
DSA_HD = DSA_HEADS * DSA_HEAD_DIM
DSA_IN = 3 * DSA_HD + IDX_HEADS * IDX_DIM + IDX_DIM + IDX_HEADS

PEER_HEADS = 8
PEER_NKEYS = 128
PEER_EXPERTS = PEER_NKEYS * PEER_NKEYS
PEER_QDIM = 256
PEER_TOPK = 16
PEER_TOK_BLOCK = 128

kernel_name = "hybrid_mla_dsa_peer_chunk_causal"


def _normal(key, shape, scale):
    return jax.random.normal(key, shape, jnp.float32) * scale


def _layernorm(x, g, b):
    xf = x.astype(jnp.float32)
    mu = jnp.mean(xf, axis=-1, keepdims=True)
    var = jnp.mean(jnp.square(xf - mu), axis=-1, keepdims=True)
    return ((xf - mu) * lax.rsqrt(var + LN_EPS) * g + b).astype(x.dtype)


def _rmsnorm(x, g):
    xf = x.astype(jnp.float32)
    ms = jnp.mean(jnp.square(xf), axis=-1, keepdims=True)
    return (xf * lax.rsqrt(ms + RMS_EPS) * g).astype(x.dtype)


def _rope_tables(seq, dim):
    inv = ROPE_THETA ** (-jnp.arange(0, dim, 2, dtype=jnp.float32) / dim)
    ang = jnp.arange(seq, dtype=jnp.float32)[:, None] * inv[None, :]
    return jnp.cos(ang), jnp.sin(ang)


def _apply_rope(x, cos, sin):
    x1, x2 = jnp.split(x.astype(jnp.float32), 2, axis=-1)
    c = cos[:, None, :]
    s = sin[:, None, :]
    return jnp.concatenate([x1 * c - x2 * s, x2 * c + x1 * s], axis=-1).astype(x.dtype)


def _chunk_causal_attention(q, k, v, scale):
    B, S, H, Dq = q.shape
    nqb = S // Q_BLOCK
    qb = q.reshape(B, nqb, Q_BLOCK, H, Dq).transpose(1, 0, 2, 3, 4)
    key_chunk = jnp.arange(S) // CHUNK

    def one_block(args):
        q_blk, blk = args
        q_chunk = (blk * Q_BLOCK + jnp.arange(Q_BLOCK)) // CHUNK
        s = jnp.einsum('bqhd,bkhd->bhqk', q_blk, k).astype(jnp.float32) * scale
        mask = key_chunk[None, :] <= q_chunk[:, None]
        s = jnp.where(mask[None, None], s, -jnp.inf)
        p = jax.nn.softmax(s, axis=-1).astype(v.dtype)
        return jnp.einsum('bhqk,bkhd->bqhd', p, v)

    out = lax.map(one_block, (qb, jnp.arange(nqb)))
    return out.transpose(1, 0, 2, 3, 4).reshape(B, S, H, v.shape[-1])


def _mla_mixer(x, w_in, q_norm, kv_norm, w_uq, w_ukv, w_o):
    B, S, _ = x.shape
    cos, sin = _rope_tables(S, MLA_ROPE)
    h = x @ w_in
    cq, ckv, k_rope = jnp.split(h, [MLA_Q_RANK, MLA_Q_RANK + MLA_KV_RANK], axis=-1)
    q = jnp.einsum('bsr,rhd->bshd', _rmsnorm(cq, q_norm), w_uq)
    kv = jnp.einsum('bsr,rhd->bshd', _rmsnorm(ckv, kv_norm), w_ukv)
    q_nope, q_rope = jnp.split(q, [MLA_NOPE], axis=-1)
    k_nope, v = jnp.split(kv, [MLA_NOPE], axis=-1)
    q_rope = _apply_rope(q_rope, cos, sin)
    k_rope = _apply_rope(k_rope[:, :, None, :], cos, sin)
    q = jnp.concatenate([q_nope, q_rope], axis=-1)
    k = jnp.concatenate([k_nope, jnp.broadcast_to(k_rope, (B, S, MLA_HEADS, MLA_ROPE))], axis=-1)
    o = _chunk_causal_attention(q, k, v, (MLA_NOPE + MLA_ROPE) ** -0.5)
    return o.reshape(B, S, MLA_HEADS * MLA_V) @ w_o


def _dsa_mixer(x, w_in, w_o):
    B, S, _ = x.shape
    cos_h, sin_h = _rope_tables(S, DSA_HEAD_DIM)
    cos_i, sin_i = _rope_tables(S, IDX_DIM)
    h = x @ w_in
    o1 = 3 * DSA_HD + IDX_HEADS * IDX_DIM
    q, k, v, q_idx, k_idx, w_idx = jnp.split(
        h, [DSA_HD, 2 * DSA_HD, 3 * DSA_HD, o1, o1 + IDX_DIM], axis=-1)
    q = _apply_rope(q.reshape(B, S, DSA_HEADS, DSA_HEAD_DIM), cos_h, sin_h)
    k = _apply_rope(k.reshape(B, S, DSA_HEADS, DSA_HEAD_DIM), cos_h, sin_h)
    v = v.reshape(B, S, DSA_HEADS, DSA_HEAD_DIM)
    q_idx = _apply_rope(q_idx.reshape(B, S, IDX_HEADS, IDX_DIM), cos_i, sin_i)
    k_idx = _apply_rope(k_idx[:, :, None, :], cos_i, sin_i)[:, :, 0, :]
    w_idx = w_idx * (IDX_HEADS ** -0.5 * IDX_DIM ** -0.5)
    topk = min(DSA_TOPK_MAX, S // 4)
    nqb = S // Q_BLOCK
    key_chunk = jnp.arange(S) // CHUNK
    scale = DSA_HEAD_DIM ** -0.5

    def per_batch(args):
        q_b, k_b, v_b, qi_b, ki_b, wi_b = args
        qb = q_b.reshape(nqb, Q_BLOCK, DSA_HEADS, DSA_HEAD_DIM)
        qib = qi_b.reshape(nqb, Q_BLOCK, IDX_HEADS, IDX_DIM)
        wib = wi_b.reshape(nqb, Q_BLOCK, IDX_HEADS)

        def per_block(bargs):
            q_blk, qi_blk, wi_blk, blk = bargs
            q_chunk = (blk * Q_BLOCK + jnp.arange(Q_BLOCK)) // CHUNK
            logits = jnp.einsum('qhd,kd->qhk', qi_blk, ki_b).astype(jnp.float32)
            idx_score = jnp.einsum('qh,qhk->qk', wi_blk.astype(jnp.float32), jax.nn.relu(logits))
            admissible = key_chunk[None, :] <= q_chunk[:, None]
            idx_score = jnp.where(admissible, idx_score, -jnp.inf)
            _, sel = lax.top_k(idx_score, topk)
            k_sel = k_b[sel]
            v_sel = v_b[sel]
            valid = key_chunk[sel] <= q_chunk[:, None]
            s = jnp.einsum('qhd,qkhd->qhk', q_blk, k_sel).astype(jnp.float32) * scale
            s = jnp.where(valid[:, None, :], s, -jnp.inf)
            p = jax.nn.softmax(s, axis=-1).astype(v_sel.dtype)
            return jnp.einsum('qhk,qkhd->qhd', p, v_sel)

        ob = lax.map(per_block, (qb, qib, wib, jnp.arange(nqb)))
        return ob.reshape(S, DSA_HD)

    o = lax.map(per_batch, (q, k, v, q_idx, k_idx, w_idx))
    return o @ w_o


def _peer(x, w_q, sub_keys, w_down, w_up):
    B, S, D = x.shape
    T = B * S
    xt = x.reshape(T, D)
    q = (xt @ w_q).reshape(T, PEER_HEADS, 2, PEER_QDIM // 2)
    s1 = jnp.einsum('thd,hnd->thn', q[:, :, 0], sub_keys[0]).astype(jnp.float32)
    s2 = jnp.einsum('thd,hnd->thn', q[:, :, 1], sub_keys[1]).astype(jnp.float32)
    v1, i1 = lax.top_k(s1, PEER_TOPK)
    v2, i2 = lax.top_k(s2, PEER_TOPK)
    cand = (v1[..., :, None] + v2[..., None, :]).reshape(T, PEER_HEADS, PEER_TOPK * PEER_TOPK)
    cand_idx = (i1[..., :, None] * PEER_NKEYS + i2[..., None, :]).reshape(
        T, PEER_HEADS, PEER_TOPK * PEER_TOPK)
    best, pos = lax.top_k(cand, PEER_TOPK)
    expert = jnp.take_along_axis(cand_idx, pos, axis=-1)
    gate = jax.nn.softmax(best, axis=-1)
    n_e = PEER_HEADS * PEER_TOPK
    nb = T // PEER_TOK_BLOCK
    xb = xt.reshape(nb, PEER_TOK_BLOCK, D)
    eb = expert.reshape(nb, PEER_TOK_BLOCK, n_e)
    gb = gate.reshape(nb, PEER_TOK_BLOCK, n_e)

    def per_block(args):
        x_blk, e_blk, g_blk = args
        u = w_down[e_blk]
        a = jnp.einsum('td,ted->te', x_blk, u).astype(jnp.float32)
        hg = (jax.nn.gelu(a, approximate=False) * g_blk).astype(w_up.dtype)
        return jnp.einsum('te,ted->td', hg, w_up[e_blk]).astype(x_blk.dtype)

    y = lax.map(per_block, (xb, eb, gb))
    return y.reshape(B, S, D)


def setup_inputs(seed: int = 0) -> dict:
    key = jax.random.key(seed)
    ks = jax.random.split(key, 20)
    D = D_MODEL
    return {
        "x": _normal(ks[0], (BATCH, SEQ, D), 1.0),
        "mla_w_in": _normal(ks[1], (N_MLA, D, MLA_IN), D ** -0.5),
        "mla_q_norm": 1.0 + _normal(ks[2], (N_MLA, MLA_Q_RANK), 0.02),
        "mla_kv_norm": 1.0 + _normal(ks[3], (N_MLA, MLA_KV_RANK), 0.02),
        "mla_w_uq": _normal(ks[4], (N_MLA, MLA_Q_RANK, MLA_HEADS, MLA_NOPE + MLA_ROPE), MLA_Q_RANK ** -0.5),
        "mla_w_ukv": _normal(ks[5], (N_MLA, MLA_KV_RANK, MLA_HEADS, MLA_NOPE + MLA_V), MLA_KV_RANK ** -0.5),
        "mla_w_o": _normal(ks[6], (N_MLA, MLA_HEADS * MLA_V, D), DN_BETA * (MLA_HEADS * MLA_V) ** -0.5),
        "dsa_w_in": _normal(ks[7], (N_DSA, D, DSA_IN), D ** -0.5),
        "dsa_w_o": _normal(ks[8], (N_DSA, DSA_HD, D), DN_BETA * DSA_HD ** -0.5),
        "peer_w_q": _normal(ks[9], (DEPTH, D, PEER_HEADS * PEER_QDIM), D ** -0.5),
        "peer_sub_keys": _normal(ks[10], (DEPTH, 2, PEER_HEADS, PEER_NKEYS, PEER_QDIM // 2), (PEER_QDIM // 2) ** -0.5),
        "peer_w_down": _normal(ks[11], (DEPTH, PEER_EXPERTS, D), D ** -0.5),
        "peer_w_up": _normal(ks[12], (DEPTH, PEER_EXPERTS, D), DN_BETA * PEER_HEADS ** -0.5),
        "ln_gain": 1.0 + _normal(ks[13], (DEPTH, 2, D), 0.02),
        "ln_bias": _normal(ks[14], (DEPTH, 2, D), 0.02),
    }


def reference(x, mla_w_in, mla_q_norm, mla_kv_norm, mla_w_uq, mla_w_ukv, mla_w_o,
              dsa_w_in, dsa_w_o, peer_w_q, peer_sub_keys, peer_w_down, peer_w_up,
              ln_gain, ln_bias):
    for i in range(DEPTH):
        j = i // N_MIXERS
        if i % N_MIXERS == 0:
            m = _mla_mixer(x, mla_w_in[j], mla_q_norm[j], mla_kv_norm[j],
                           mla_w_uq[j], mla_w_ukv[j], mla_w_o[j])
        else:
            m = _dsa_mixer(x, dsa_w_in[j], dsa_w_o[j])
        x = _layernorm(DN_ALPHA * x + m, ln_gain[i, 0], ln_bias[i, 0])
        f = _peer(x, peer_w_q[i], peer_sub_keys[i], peer_w_down[i], peer_w_up[i])
        x = _layernorm(DN_ALPHA * x + f, ln_gain[i, 1], ln_bias[i, 1])
    return x
```

```python
import functools

import jax
import jax.numpy as jnp
from jax import lax
from jax.experimental import pallas as pl
from jax.experimental.pallas import tpu as pltpu

F32 = jnp.float32
BF16 = jnp.bfloat16
I32 = jnp.int32

D_MODEL = 1024
DEPTH = 2
CHUNK = 64
ROPE_THETA = 10000.0
LN_EPS = 1e-5
RMS_EPS = 1e-6
DN_ALPHA = (2 * DEPTH) ** 0.25

MLA_HEADS = 8
MLA_NOPE = 128
MLA_ROPE = 64
MLA_V = 128
MLA_Q_RANK = 384
MLA_KV_RANK = 256
MLA_QK_PAD = 256

DSA_HEADS = 8
DSA_HEAD_DIM = 128
IDX_HEADS = 8
IDX_DIM = 64
DSA_TOPK_MAX = 256
DSA_HD = DSA_HEADS * DSA_HEAD_DIM

PEER_HEADS = 8
PEER_NKEYS = 128
PEER_EXPERTS = PEER_NKEYS * PEER_NKEYS
PEER_QDIM = 256
PEER_TOPK = 16

LANES = 128
SUBLANES = 8
VMEM_LIMIT = 56 * 1024 * 1024

NEG_INF = float("-inf")
INT_MIN = -2 ** 31


def _cparams(*sem):
    return pltpu.CompilerParams(dimension_semantics=sem, vmem_limit_bytes=VMEM_LIMIT)


def _dot(a, b):
    return jnp.dot(a, b, preferred_element_type=F32)


def _dot_nt(a, b):
    return lax.dot_general(a, b, (((1,), (1,)), ((), ())), preferred_element_type=F32)


def _layernorm_rows(y, g, b):
    mu = jnp.mean(y, axis=-1, keepdims=True)
    yc = y - mu
    var = jnp.mean(yc * yc, axis=-1, keepdims=True)
    return yc * lax.rsqrt(var + LN_EPS) * g + b


def _rms_rows(h, g):
    ms = jnp.mean(h * h, axis=-1, keepdims=True)
    return h * lax.rsqrt(ms + RMS_EPS) * g


def _mla_proj_kernel(x_ref, win_ref, qn_ref, kvn_ref, wuq_ref, wukv_ref, cc_ref, ss_ref,
                     q_ref, k_ref, v_ref):
    xb = x_ref[...].astype(BF16)
    h = _dot(xb, win_ref[...])
    cq = h[:, :MLA_Q_RANK]
    ckv = h[:, MLA_Q_RANK:MLA_Q_RANK + MLA_KV_RANK]
    o = MLA_Q_RANK + MLA_KV_RANK
    cc = cc_ref[...]
    ss = ss_ref[...]
    k_rope = h[:, o:o + LANES] * cc + h[:, o + LANES:o + 2 * LANES] * ss
    qall = _dot(_rms_rows(cq, qn_ref[...]).astype(BF16), wuq_ref[...])
    kvall = _dot(_rms_rows(ckv, kvn_ref[...]).astype(BF16), wukv_ref[...])
    k_rope_b = k_rope.astype(BF16)
    for hd in range(MLA_HEADS):
        b0 = hd * 3 * LANES
        q_ref[0, hd, :, 0:LANES] = qall[:, b0:b0 + LANES].astype(BF16)
        q_rope = qall[:, b0 + LANES:b0 + 2 * LANES] * cc + qall[:, b0 + 2 * LANES:b0 + 3 * LANES] * ss
        q_ref[0, hd, :, LANES:2 * LANES] = q_rope.astype(BF16)
        k_ref[0, hd, :, 0:LANES] = kvall[:, hd * LANES:(hd + 1) * LANES].astype(BF16)
        k_ref[0, hd, :, LANES:2 * LANES] = k_rope_b
        v0 = MLA_HEADS * MLA_NOPE + hd * MLA_V
        v_ref[0, hd] = kvall[:, v0:v0 + MLA_V].astype(BF16)


def _mla_proj(xt, B, S, w_in, q_norm, kv_norm, w_uq, w_ukv, cos, sin, tm):
    T = B * S
    n_s = S // tm
    half = MLA_ROPE // 2
    def swap_rope(w):
        return jnp.concatenate([w[..., half:], w[..., :half]], axis=-1)
    zpad = lambda w: jnp.concatenate([w, jnp.zeros_like(w)], axis=-1)
    w_kr = w_in[:, MLA_Q_RANK + MLA_KV_RANK:]
    win_ext = jnp.concatenate(
        [w_in[:, :MLA_Q_RANK + MLA_KV_RANK], zpad(w_kr), zpad(swap_rope(w_kr))], axis=-1).astype(BF16)
    uq_nope = w_uq[:, :, :MLA_NOPE]
    uq_rope = w_uq[:, :, MLA_NOPE:]
    wuq_ext = jnp.concatenate([uq_nope, zpad(uq_rope), zpad(swap_rope(uq_rope))], axis=-1)
    wuq_ext = wuq_ext.reshape(MLA_Q_RANK, MLA_HEADS * 3 * LANES).astype(BF16)
    wukv_ext = jnp.concatenate(
        [w_ukv[:, :, :MLA_NOPE].reshape(MLA_KV_RANK, -1), w_ukv[:, :, MLA_NOPE:].reshape(MLA_KV_RANK, -1)],
        axis=-1).astype(BF16)
    z = jnp.zeros((S, LANES - MLA_ROPE), F32)
    cc = jnp.concatenate([cos, cos, z], axis=-1)
    ss = jnp.concatenate([-sin, sin, z], axis=-1)
    full = lambda a: pl.BlockSpec(a.shape, lambda i: (0,) * a.ndim)
    qn = q_norm.reshape(1, -1)
    kvn = kv_norm.reshape(1, -1)
    head_map = lambda i: (i // n_s, 0, i % n_s, 0)
    return pl.pallas_call(
        _mla_proj_kernel,
        grid=(T // tm,),
        in_specs=[pl.BlockSpec((tm, D_MODEL), lambda i: (i, 0)),
                  full(win_ext), full(qn), full(kvn), full(wuq_ext), full(wukv_ext),
                  pl.BlockSpec((tm, LANES), lambda i: (i % n_s, 0)),
                  pl.BlockSpec((tm, LANES), lambda i: (i % n_s, 0))],
        out_specs=[pl.BlockSpec((1, MLA_HEADS, tm, MLA_QK_PAD), head_map),
                   pl.BlockSpec((1, MLA_HEADS, tm, MLA_QK_PAD), head_map),
                   pl.BlockSpec((1, MLA_HEADS, tm, MLA_V), head_map)],
        out_shape=[jax.ShapeDtypeStruct((B, MLA_HEADS, S, MLA_QK_PAD), BF16),
                   jax.ShapeDtypeStruct((B, MLA_HEADS, S, MLA_QK_PAD), BF16),
                   jax.ShapeDtypeStruct((B, MLA_HEADS, S, MLA_V), BF16)],
        compiler_params=_cparams("parallel"),
        name="mla_proj",
    )(xt, win_ext, qn, kvn, wuq_ext, wukv_ext, cc, ss)


def _mla_attn_kernel(q_ref, k_ref, v_ref, o_ref, *, tq, scale):
    qi = pl.program_id(2)
    q = q_ref[0, 0]

    def step(j, carry, diagonal):
        m, l, acc = carry
        start = pl.multiple_of(j * tq, tq)
        kb = k_ref[0, 0, pl.ds(start, tq), :]
        vb = v_ref[0, 0, pl.ds(start, tq), :]
        s = _dot_nt(q, kb) * scale
        if diagonal:
            row = lax.broadcasted_iota(I32, (tq, tq), 0)
            col = lax.broadcasted_iota(I32, (tq, tq), 1)
            s = jnp.where(col // CHUNK <= row // CHUNK, s, NEG_INF)
        m_new = jnp.maximum(m, jnp.max(s, axis=-1, keepdims=True))
        p = jnp.exp(s - m_new)
        alpha = jnp.exp(m - m_new)
        l = alpha * l + jnp.sum(p, axis=-1, keepdims=True)
        acc = alpha * acc + _dot(p.astype(BF16), vb)
        return m_new, l, acc

    init = (jnp.full((tq, 1), NEG_INF, F32), jnp.zeros((tq, 1), F32), jnp.zeros((tq, MLA_V), F32))
    carry = lax.fori_loop(0, qi, lambda j, c: step(j, c, False), init)
    _, l, acc = step(qi, carry, True)
    o_ref[0] = (acc / l).astype(BF16)


def _mla_attn(q, k, v, tq):
    B, H, S, _ = q.shape
    scale = (MLA_NOPE + MLA_ROPE) ** -0.5
    return pl.pallas_call(
        functools.partial(_mla_attn_kernel, tq=tq, scale=scale),
        grid=(B, H, S // tq),
        in_specs=[pl.BlockSpec((1, 1, tq, MLA_QK_PAD), lambda b, h, i: (b, h, i, 0)),
                  pl.BlockSpec((1, 1, S, MLA_QK_PAD), lambda b, h, i: (b, h, 0, 0)),
                  pl.BlockSpec((1, 1, S, MLA_V), lambda b, h, i: (b, h, 0, 0))],
        out_specs=pl.BlockSpec((1, tq, MLA_V), lambda b, h, i: (b, i, h)),
        out_shape=jax.ShapeDtypeStruct((B, S, H * MLA_V), BF16),
        compiler_params=_cparams("parallel", "parallel", "arbitrary"),
        name="mla_attn",
    )(q, k, v)


def _proj_ln_kernel(a_ref, w_ref, x_ref, g_ref, b_ref, o_ref):
    m = _dot(a_ref[...], w_ref[...])
    y = DN_ALPHA * x_ref[...] + m
    o_ref[...] = _layernorm_rows(y, g_ref[...], b_ref[...])


def _proj_ln(a, w, xt, gain, bias, tm):
    T = xt.shape[0]
    wb = w.astype(BF16)
    g = gain.reshape(1, -1)
    b = bias.reshape(1, -1)
    return pl.pallas_call(
        _proj_ln_kernel,
        grid=(T // tm,),
        in_specs=[pl.BlockSpec((tm, a.shape[1]), lambda i: (i, 0)),
                  pl.BlockSpec(wb.shape, lambda i: (0, 0)),
                  pl.BlockSpec((tm, D_MODEL), lambda i: (i, 0)),
                  pl.BlockSpec(g.shape, lambda i: (0, 0)),
                  pl.BlockSpec(b.shape, lambda i: (0, 0))],
        out_specs=pl.BlockSpec((tm, D_MODEL), lambda i: (i, 0)),
        out_shape=jax.ShapeDtypeStruct((T, D_MODEL), F32),
        compiler_params=_cparams("parallel"),
        name="proj_ln",
    )(a, wb, xt, g, b)


def _dsa_proj_kernel(x_ref, w_ref, ch_ref, sh_ref, ci_ref, ck_ref, sk_ref,
                     q_ref, k_ref, v_ref, qi_ref, ki_ref, wi_ref, *, w_scale):
    xb = x_ref[...].astype(BF16)
    h = _dot(xb, w_ref[...])
    ch = ch_ref[...]
    sh = sh_ref[...]
    ci = ci_ref[...]
    half = LANES // 2
    for hd in range(DSA_HEADS):
        c0 = hd * DSA_HEAD_DIM
        qh = h[:, c0:c0 + LANES]
        q_ref[0, :, c0:c0 + LANES] = (qh * ch + pltpu.roll(qh, half, 1) * sh).astype(BF16)
        kh = h[:, DSA_HD + c0:DSA_HD + c0 + LANES]
        k_ref[0, :, c0:c0 + LANES] = (kh * ch + pltpu.roll(kh, half, 1) * sh).astype(BF16)
        t = h[:, 3 * DSA_HD + c0:3 * DSA_HD + c0 + LANES] * ci
        qi_ref[0, :, c0:c0 + LANES] = (t + pltpu.roll(t, half, 1)).astype(BF16)
    v_ref[0] = h[:, 2 * DSA_HD:3 * DSA_HD].astype(BF16)
    o = 4 * DSA_HD
    ki = h[:, o:o + LANES] * ck_ref[...] + h[:, o + LANES:o + 2 * LANES] * sk_ref[...]
    ki_ref[0] = ki.astype(BF16)
    wi_ref[0] = h[:, o + 2 * LANES:o + 3 * LANES] * w_scale


def _dsa_proj(xt, B, S, w_in, tm):
    T = B * S
    n_s = S // tm
    o1 = 3 * DSA_HD + IDX_HEADS * IDX_DIM
    ih = IDX_DIM // 2
    w_qi = w_in[:, 3 * DSA_HD:o1].reshape(D_MODEL, IDX_HEADS, IDX_DIM)
    w_qi_sw = jnp.concatenate([w_qi[..., ih:], w_qi[..., :ih]], axis=-1)
    w_qi_ext = jnp.concatenate([w_qi, w_qi_sw], axis=-1).reshape(D_MODEL, IDX_HEADS * LANES)
    w_ki = w_in[:, o1:o1 + IDX_DIM]
    w_ki_sw = jnp.concatenate([w_ki[:, ih:], w_ki[:, :ih]], axis=-1)
    zpad = lambda w: jnp.concatenate([w, jnp.zeros((w.shape[0], LANES - w.shape[1]), w.dtype)], axis=-1)
    w_ext = jnp.concatenate(
        [w_in[:, :3 * DSA_HD], w_qi_ext, zpad(w_ki), zpad(w_ki_sw), zpad(w_in[:, o1 + IDX_DIM:])],
        axis=-1).astype(BF16)
    cos_h, sin_h = _rope_tables(S, DSA_HEAD_DIM)
    cos_i, sin_i = _rope_tables(S, IDX_DIM)
    ch = jnp.concatenate([cos_h, cos_h], axis=-1)
    sh = jnp.concatenate([-sin_h, sin_h], axis=-1)
    ci = jnp.concatenate([cos_i, cos_i, -sin_i, sin_i], axis=-1)
    z = jnp.zeros((S, LANES - IDX_DIM), F32)
    ck = jnp.concatenate([cos_i, cos_i, z], axis=-1)
    sk = jnp.concatenate([-sin_i, sin_i, z], axis=-1)
    tab = pl.BlockSpec((tm, LANES), lambda i: (i % n_s, 0))
    row_map = lambda i: (i // n_s, i % n_s, 0)
    wide = pl.BlockSpec((1, tm, DSA_HD), row_map)
    narrow = pl.BlockSpec((1, tm, LANES), row_map)
    w_scale = IDX_HEADS ** -0.5 * IDX_DIM ** -0.5
    return pl.pallas_call(
        functools.partial(_dsa_proj_kernel, w_scale=w_scale),
        grid=(T // tm,),
        in_specs=[pl.BlockSpec((tm, D_MODEL), lambda i: (i, 0)),
                  pl.BlockSpec(w_ext.shape, lambda i: (0, 0)),
                  tab, tab, tab, tab, tab],
        out_specs=[wide, wide, wide, wide, narrow, narrow],
        out_shape=[jax.ShapeDtypeStruct((B, S, DSA_HD), BF16),
                   jax.ShapeDtypeStruct((B, S, DSA_HD), BF16),
                   jax.ShapeDtypeStruct((B, S, DSA_HD), BF16),
                   jax.ShapeDtypeStruct((B, S, IDX_HEADS * LANES), BF16),
                   jax.ShapeDtypeStruct((B, S, LANES), BF16),
                   jax.ShapeDtypeStruct((B, S, LANES), F32)],
        compiler_params=_cparams("parallel"),
        name="dsa_proj",
    )(xt, w_ext, ch, sh, ci, ck, sk)


def _dsa_attn_kernel(q_ref, qi_ref, wi_ref, k_ref, v_ref, ki_ref, o_ref, *, tq, S, topk, scale):
    blk = pl.program_id(1)
    ki = ki_ref[0]
    qi = qi_ref[0]
    wi = wi_ref[0]
    score = jnp.zeros((tq, S), F32)
    for hd in range(IDX_HEADS):
        logit = _dot_nt(qi[:, hd * LANES:(hd + 1) * LANES], ki)
        score = score + wi[:, hd:hd + 1] * jnp.maximum(logit, 0.0)
    row = lax.broadcasted_iota(I32, (tq, S), 0)
    col = lax.broadcasted_iota(I32, (tq, S), 1)
    adm = col // CHUNK <= (blk * tq + row) // CHUNK
    score = jnp.where(adm, score + 0.0, NEG_INF)
    bits = lax.bitcast_convert_type(score, I32)
    u = jnp.where(bits < 0, bits ^ 0x7FFFFFFF, bits)

    def search(i, thr):
        cand = thr + lax.shift_left(jnp.int32(1), 31 - i)
        cnt = jnp.sum((u >= cand).astype(F32), axis=-1, keepdims=True)
        return jnp.where(cnt >= topk, cand, thr)
    thr = lax.fori_loop(0, 32, search, jnp.full((tq, 1), INT_MIN, I32))

    gt = u > thr
    eq = u == thr
    need = topk - jnp.sum(gt.astype(F32), axis=-1, keepdims=True)
    r_i = lax.broadcasted_iota(I32, (LANES, LANES), 0)
    c_i = lax.broadcasted_iota(I32, (LANES, LANES), 1)
    tri = (r_i < c_i).astype(BF16)
    eq_b = eq.astype(BF16)
    run = jnp.zeros((tq, 1), F32)
    ranks = []
    for j in range(S // LANES):
        e = eq_b[:, j * LANES:(j + 1) * LANES]
        ranks.append(_dot(e, tri) + run)
        run = run + jnp.sum(e.astype(F32), axis=-1, keepdims=True)
    rank = jnp.concatenate(ranks, axis=-1)
    sel = adm & (gt | (eq & (rank < need)))

    for hd in range(DSA_HEADS):
        c0 = hd * DSA_HEAD_DIM
        s = _dot_nt(q_ref[0, :, c0:c0 + LANES], k_ref[0, :, c0:c0 + LANES]) * scale
        s = jnp.where(sel, s, NEG_INF)
        m = jnp.max(s, axis=-1, keepdims=True)
        p = jnp.exp(s - m)
        l = jnp.sum(p, axis=-1, keepdims=True)
        o = _dot(p.astype(BF16), v_ref[0, :, c0:c0 + LANES])
        o_ref[0, :, c0:c0 + LANES] = (o / l).astype(BF16)


def _dsa_attn(q, k, v, qi, ki, wi, tq):
    B, S, _ = q.shape
    topk = min(DSA_TOPK_MAX, S // 4)
    scale = DSA_HEAD_DIM ** -0.5
    qmap = lambda b, i: (b, i, 0)
    kmap = lambda b, i: (b, 0, 0)
    return pl.pallas_call(
        functools.partial(_dsa_attn_kernel, tq=tq, S=S, topk=topk, scale=scale),
        grid=(B, S // tq),
        in_specs=[pl.BlockSpec((1, tq, DSA_HD), qmap),
                  pl.BlockSpec((1, tq, IDX_HEADS * LANES), qmap),
                  pl.BlockSpec((1, tq, LANES), qmap),
                  pl.BlockSpec((1, S, DSA_HD), kmap),
                  pl.BlockSpec((1, S, DSA_HD), kmap),
                  pl.BlockSpec((1, S, LANES), kmap)],
        out_specs=pl.BlockSpec((1, tq, DSA_HD), qmap),
        out_shape=jax.ShapeDtypeStruct((B, S, DSA_HD), BF16),
        compiler_params=_cparams("parallel", "arbitrary"),
        name="dsa_attn",
    )(q, qi, wi, k, v, ki)


_PEER_NB = [PEER_TOPK // (a + 1) for a in range(PEER_TOPK)]


def _top16_rows(s):
    vals = []
    cur = s
    for _ in range(PEER_TOPK):
        m = jnp.max(cur, axis=0, keepdims=True)
        vals.append(m)
        cur = jnp.where(cur == m, NEG_INF, cur)
    return vals


def _peer_route_kernel(x_ref, wq_ref, keys_ref, s1_ref, e1_ref, s2_ref, e2_ref, th_ref, v2_ref):
    tt = x_ref.shape[0]
    q = _dot(x_ref[...].astype(BF16), wq_ref[...]).astype(BF16)
    half = PEER_QDIM // 2
    for hd in range(PEER_HEADS):
        c0 = hd * PEER_QDIM
        s1 = _dot_nt(keys_ref[0, hd], q[:, c0:c0 + half])
        s2 = _dot_nt(keys_ref[1, hd], q[:, c0 + half:c0 + PEER_QDIM])
        v1 = _top16_rows(s1)
        v2 = _top16_rows(s2)
        for a in range(PEER_TOPK):
            v2_ref[a:a + 1, :] = v2[a]
        pieces = []
        for a in range(PEER_TOPK):
            nb = _PEER_NB[a]
            rows = PEER_TOPK if nb > SUBLANES else SUBLANES
            blk = v2_ref[0:rows, :] + v1[a]
            ridx = lax.broadcasted_iota(I32, (rows, tt), 0)
            pieces.append(jnp.where(ridx < nb, blk, NEG_INF))
        cand = jnp.concatenate(pieces, axis=0)
        cur = cand
        theta = None
        for _ in range(PEER_TOPK):
            theta = jnp.max(cur, axis=0, keepdims=True)
            cur = jnp.where(cur == theta, NEG_INF, cur)
        top = v1[0] + v2[0]
        z = jnp.sum(jnp.where(cand >= theta, jnp.exp(cand - top), 0.0), axis=0, keepdims=True)
        s1_ref[hd] = s1
        s2_ref[hd] = s2
        e1_ref[hd] = jnp.exp(s1 - v1[0]) / z
        e2_ref[hd] = jnp.exp(s2 - v2[0])
        th_ref[hd:hd + 1, :] = theta


def _peer_route(xt, w_q, sub_keys, tt):
    T = xt.shape[0]
    wq = w_q.astype(BF16)
    keys = sub_keys.astype(BF16)
    big = jax.ShapeDtypeStruct((PEER_HEADS, PEER_NKEYS, T), F32)
    big_spec = pl.BlockSpec((PEER_HEADS, PEER_NKEYS, tt), lambda i: (0, 0, i))
    return pl.pallas_call(
        _peer_route_kernel,
        grid=(T // tt,),
        in_specs=[pl.BlockSpec((tt, D_MODEL), lambda i: (i, 0)),
                  pl.BlockSpec(wq.shape, lambda i: (0, 0)),
                  pl.BlockSpec(keys.shape, lambda i: (0, 0, 0, 0))],
        out_specs=[big_spec, big_spec, big_spec, big_spec,
                   pl.BlockSpec((PEER_HEADS, tt), lambda i: (0, i))],
        out_shape=[big, big, big, big, jax.ShapeDtypeStruct((PEER_HEADS, T), F32)],
        scratch_shapes=[pltpu.VMEM((PEER_TOPK, tt), F32)],
        compiler_params=_cparams("parallel"),
        name="peer_route",
    )(xt, wq, keys)


def _peer_dense_kernel(x_ref, wd_ref, wu_ref, s1_ref, e1_ref, s2_ref, e2_ref, th_ref, g_ref, b_ref,
                       o_ref, xb_ref, a_ref, p_ref, acc_ref, s1r_ref, e1r_ref, *, te, tt):
    e = pl.program_id(1)

    @pl.when(e == 0)
    def _():
        xb_ref[...] = x_ref[...].astype(BF16)
        acc_ref[...] = jnp.zeros_like(acc_ref)

        def relayout(g, carry):
            g0 = pl.multiple_of(g * SUBLANES, SUBLANES)
            for hd in range(PEER_HEADS):
                s1g = s1_ref[hd, pl.ds(g0, SUBLANES), :]
                e1g = e1_ref[hd, pl.ds(g0, SUBLANES), :]
                for r in range(SUBLANES):
                    s1r_ref[g0 + r, hd:hd + 1, :] = s1g[r:r + 1, :]
                    e1r_ref[g0 + r, hd:hd + 1, :] = e1g[r:r + 1, :]
            return carry
        lax.fori_loop(0, PEER_NKEYS // SUBLANES, relayout, 0)

    a_ref[...] = _dot_nt(wd_ref[...], xb_ref[...])
    n_i1 = te // PEER_NKEYS

    def fill(i1l, carry):
        i1 = e * n_i1 + i1l
        r0 = pl.multiple_of(i1l * PEER_NKEYS, PEER_NKEYS)
        for lg in range(tt // LANES):
            ls = slice(lg * LANES, (lg + 1) * LANES)
            gate = jnp.zeros((PEER_NKEYS, LANES), F32)
            for hd in range(PEER_HEADS):
                s1b = s1r_ref[i1, hd:hd + 1, ls]
                e1b = e1r_ref[i1, hd:hd + 1, ls]
                thb = th_ref[hd:hd + 1, ls]
                t = s2_ref[hd, :, ls] + s1b
                gate = gate + jnp.where(t >= thb, e2_ref[hd, :, ls] * e1b, 0.0)
            a = a_ref[pl.ds(r0, PEER_NKEYS), ls]
            act = 0.5 * a * (1.0 + lax.erf(a * (2.0 ** -0.5)))
            p_ref[pl.ds(r0, PEER_NKEYS), ls] = (act * gate).astype(BF16)
        return carry
    lax.fori_loop(0, n_i1, fill, 0)

    acc_ref[...] += _dot(wu_ref[...], p_ref[...])

    @pl.when(e == pl.num_programs(1) - 1)
    def _():
        y = DN_ALPHA * x_ref[...] + acc_ref[...].T
        o_ref[...] = _layernorm_rows(y, g_ref[...], b_ref[...])


def _peer_dense(xt, w_down, w_up, route, gain, bias, tt, te):
    T = xt.shape[0]
    s1, e1, s2, e2, th = route
    wd = w_down.astype(BF16)
    wu = w_up.astype(BF16).T
    g = gain.reshape(1, -1)
    b = bias.reshape(1, -1)
    big_spec = pl.BlockSpec((PEER_HEADS, PEER_NKEYS, tt), lambda i, e: (0, 0, i))
    return pl.pallas_call(
        functools.partial(_peer_dense_kernel, te=te, tt=tt),
        grid=(T // tt, PEER_EXPERTS // te),
        in_specs=[pl.BlockSpec((tt, D_MODEL), lambda i, e: (i, 0)),
                  pl.BlockSpec((te, D_MODEL), lambda i, e: (e, 0)),
                  pl.BlockSpec((D_MODEL, te), lambda i, e: (0, e)),
                  big_spec, big_spec, big_spec, big_spec,
                  pl.BlockSpec((PEER_HEADS, tt), lambda i, e: (0, i)),
                  pl.BlockSpec(g.shape, lambda i, e: (0, 0)),
                  pl.BlockSpec(b.shape, lambda i, e: (0, 0))],
        out_specs=pl.BlockSpec((tt, D_MODEL), lambda i, e: (i, 0)),
        out_shape=jax.ShapeDtypeStruct((T, D_MODEL), F32),
        scratch_shapes=[pltpu.VMEM((tt, D_MODEL), BF16),
                        pltpu.VMEM((te, tt), F32),
                        pltpu.VMEM((te, tt), BF16),
                        pltpu.VMEM((D_MODEL, tt), F32),
                        pltpu.VMEM((PEER_NKEYS, PEER_HEADS, tt), F32),
                        pltpu.VMEM((PEER_NKEYS, PEER_HEADS, tt), F32)],
        compiler_params=_cparams("parallel", "arbitrary"),
        name="peer_dense",
    )(xt, wd, wu, s1, e1, s2, e2, th, g, b)


def _rope_tables(seq, dim):
    inv = ROPE_THETA ** (-jnp.arange(0, dim, 2, dtype=F32) / dim)
    ang = jnp.arange(seq, dtype=F32)[:, None] * inv[None, :]
    return jnp.cos(ang), jnp.sin(ang)


def _tiles(S, T):
    row = min(256, S)
    tq_mla = min(256, S)
    tq_dsa = min(128, S)
    tt_route = min(256, T)
    tt_dense = min(512, T)
    te_dense = 1024
    return row, tq_mla, tq_dsa, tt_route, tt_dense, te_dense


def _peer_layer(xt, w_q, sub_keys, w_down, w_up, gain, bias, tt_route, tt_dense, te_dense):
    route = _peer_route(xt, w_q, sub_keys, tt_route)
    return _peer_dense(xt, w_down, w_up, route, gain, bias, tt_dense, te_dense)


def kernel(x, mla_w_in, mla_q_norm, mla_kv_norm, mla_w_uq, mla_w_ukv, mla_w_o,
           dsa_w_in, dsa_w_o, peer_w_q, peer_sub_keys, peer_w_down, peer_w_up,
           ln_gain, ln_bias):
    B, S, D = x.shape
    T = B * S
    row, tq_mla, tq_dsa, tt_route, tt_dense, te_dense = _tiles(S, T)
    xt = x.reshape(T, D)

    cos, sin = _rope_tables(S, MLA_ROPE)
    q, k, v = _mla_proj(xt, B, S, mla_w_in[0], mla_q_norm[0], mla_kv_norm[0],
                        mla_w_uq[0], mla_w_ukv[0], cos, sin, row)
    o = _mla_attn(q, k, v, tq_mla).reshape(T, MLA_HEADS * MLA_V)
    xt = _proj_ln(o, mla_w_o[0], xt, ln_gain[0, 0], ln_bias[0, 0], row)
    xt = _peer_layer(xt, peer_w_q[0], peer_sub_keys[0], peer_w_down[0], peer_w_up[0],
                     ln_gain[0, 1], ln_bias[0, 1], tt_route, tt_dense, te_dense)

    q, k, v, qi, ki, wi = _dsa_proj(xt, B, S, dsa_w_in[0], row)
    o = _dsa_attn(q, k, v, qi, ki, wi, tq_dsa).reshape(T, DSA_HD)
    xt = _proj_ln(o, dsa_w_o[0], xt, ln_gain[1, 0], ln_bias[1, 0], row)
    xt = _peer_layer(xt, peer_w_q[1], peer_sub_keys[1], peer_w_down[1], peer_w_up[1],
                     ln_gain[1, 1], ln_bias[1, 1], tt_route, tt_dense, te_dense)
    return xt.reshape(B, S, D)
```

```python
import functools

import jax
import jax.numpy as jnp
from jax import lax
from jax.experimental import pallas as pl
from jax.experimental.pallas import tpu as pltpu

F32 = jnp.float32
BF16 = jnp.bfloat16
I32 = jnp.int32

D_MODEL = 1024
DEPTH = 2
CHUNK = 64
ROPE_THETA = 10000.0
LN_EPS = 1e-5
RMS_EPS = 1e-6
DN_ALPHA = (2 * DEPTH) ** 0.25

MLA_HEADS = 8
MLA_NOPE = 128
MLA_ROPE = 64
MLA_V = 128
MLA_Q_RANK = 384
MLA_KV_RANK = 256
MLA_QK_PAD = 256

DSA_HEADS = 8
DSA_HEAD_DIM = 128
IDX_HEADS = 8
IDX_DIM = 64
DSA_TOPK_MAX = 256
DSA_HD = DSA_HEADS * DSA_HEAD_DIM

PEER_HEADS = 8
PEER_NKEYS = 128
PEER_EXPERTS = PEER_NKEYS * PEER_NKEYS
PEER_QDIM = 256
PEER_TOPK = 16

LANES = 128
SUBLANES = 8
VMEM_LIMIT = 56 * 1024 * 1024

NEG_INF = float("-inf")
INT_MIN = -2 ** 31


def _cparams(*sem):
    return pltpu.CompilerParams(dimension_semantics=sem, vmem_limit_bytes=VMEM_LIMIT)


def _dot(a, b):
    return jnp.dot(a, b, preferred_element_type=F32)


def _dot_nt(a, b):
    return lax.dot_general(a, b, (((1,), (1,)), ((), ())), preferred_element_type=F32)


def _layernorm_rows(y, g, b):
    mu = jnp.mean(y, axis=-1, keepdims=True)
    yc = y - mu
    var = jnp.mean(yc * yc, axis=-1, keepdims=True)
    return yc * lax.rsqrt(var + LN_EPS) * g + b


def _rms_rows(h, g):
    ms = jnp.mean(h * h, axis=-1, keepdims=True)
    return h * lax.rsqrt(ms + RMS_EPS) * g


def _mla_proj_kernel(x_ref, win_ref, qn_ref, kvn_ref, wuq_ref, wukv_ref, cc_ref, ss_ref,
                     q_ref, k_ref, v_ref):
    xb = x_ref[...].astype(BF16)
    h = _dot(xb, win_ref[...])
    cq = h[:, :MLA_Q_RANK]
    ckv = h[:, MLA_Q_RANK:MLA_Q_RANK + MLA_KV_RANK]
    o = MLA_Q_RANK + MLA_KV_RANK
    cc = cc_ref[...]
    ss = ss_ref[...]
    k_rope = h[:, o:o + LANES] * cc + h[:, o + LANES:o + 2 * LANES] * ss
    qall = _dot(_rms_rows(cq, qn_ref[...]).astype(BF16), wuq_ref[...])
    kvall = _dot(_rms_rows(ckv, kvn_ref[...]).astype(BF16), wukv_ref[...])
    k_rope_b = k_rope.astype(BF16)
    for hd in range(MLA_HEADS):
        b0 = hd * 3 * LANES
        q_ref[0, hd, :, 0:LANES] = qall[:, b0:b0 + LANES].astype(BF16)
        q_rope = qall[:, b0 + LANES:b0 + 2 * LANES] * cc + qall[:, b0 + 2 * LANES:b0 + 3 * LANES] * ss
        q_ref[0, hd, :, LANES:2 * LANES] = q_rope.astype(BF16)
        k_ref[0, hd, :, 0:LANES] = kvall[:, hd * LANES:(hd + 1) * LANES].astype(BF16)
        k_ref[0, hd, :, LANES:2 * LANES] = k_rope_b
        v0 = MLA_HEADS * MLA_NOPE + hd * MLA_V
        v_ref[0, hd] = kvall[:, v0:v0 + MLA_V].astype(BF16)


def _mla_proj(xt, B, S, w_in, q_norm, kv_norm, w_uq, w_ukv, cos, sin, tm):
    T = B * S
    n_s = S // tm
    half = MLA_ROPE // 2
    def swap_rope(w):
        return jnp.concatenate([w[..., half:], w[..., :half]], axis=-1)
    zpad = lambda w: jnp.concatenate([w, jnp.zeros_like(w)], axis=-1)
    w_kr = w_in[:, MLA_Q_RANK + MLA_KV_RANK:]
    win_ext = jnp.concatenate(
        [w_in[:, :MLA_Q_RANK + MLA_KV_RANK], zpad(w_kr), zpad(swap_rope(w_kr))], axis=-1).astype(BF16)
    uq_nope = w_uq[:, :, :MLA_NOPE]
    uq_rope = w_uq[:, :, MLA_NOPE:]
    wuq_ext = jnp.concatenate([uq_nope, zpad(uq_rope), zpad(swap_rope(uq_rope))], axis=-1)
    wuq_ext = wuq_ext.reshape(MLA_Q_RANK, MLA_HEADS * 3 * LANES).astype(BF16)
    wukv_ext = jnp.concatenate(
        [w_ukv[:, :, :MLA_NOPE].reshape(MLA_KV_RANK, -1), w_ukv[:, :, MLA_NOPE:].reshape(MLA_KV_RANK, -1)],
        axis=-1).astype(BF16)
    z = jnp.zeros((S, LANES - MLA_ROPE), F32)
    cc = jnp.concatenate([cos, cos, z], axis=-1)
    ss = jnp.concatenate([-sin, sin, z], axis=-1)
    full = lambda a: pl.BlockSpec(a.shape, lambda i: (0,) * a.ndim)
    qn = q_norm.reshape(1, -1)
    kvn = kv_norm.reshape(1, -1)
    head_map = lambda i: (i // n_s, 0, i % n_s, 0)
    return pl.pallas_call(
        _mla_proj_kernel,
        grid=(T // tm,),
        in_specs=[pl.BlockSpec((tm, D_MODEL), lambda i: (i, 0)),
                  full(win_ext), full(qn), full(kvn), full(wuq_ext), full(wukv_ext),
                  pl.BlockSpec((tm, LANES), lambda i: (i % n_s, 0)),
                  pl.BlockSpec((tm, LANES), lambda i: (i % n_s, 0))],
        out_specs=[pl.BlockSpec((1, MLA_HEADS, tm, MLA_QK_PAD), head_map),
                   pl.BlockSpec((1, MLA_HEADS, tm, MLA_QK_PAD), head_map),
                   pl.BlockSpec((1, MLA_HEADS, tm, MLA_V), head_map)],
        out_shape=[jax.ShapeDtypeStruct((B, MLA_HEADS, S, MLA_QK_PAD), BF16),
                   jax.ShapeDtypeStruct((B, MLA_HEADS, S, MLA_QK_PAD), BF16),
                   jax.ShapeDtypeStruct((B, MLA_HEADS, S, MLA_V), BF16)],
        compiler_params=_cparams("parallel"),
        name="mla_proj",
    )(xt, win_ext, qn, kvn, wuq_ext, wukv_ext, cc, ss)


def _mla_attn_kernel(q_ref, k_ref, v_ref, o_ref, *, tq, scale):
    qi = pl.program_id(2)
    q = q_ref[0, 0]

    def step(j, carry, diagonal):
        m, l, acc = carry
        start = pl.multiple_of(j * tq, tq)
        kb = k_ref[0, 0, pl.ds(start, tq), :]
        vb = v_ref[0, 0, pl.ds(start, tq), :]
        s = _dot_nt(q, kb) * scale
        if diagonal:
            row = lax.broadcasted_iota(I32, (tq, tq), 0)
            col = lax.broadcasted_iota(I32, (tq, tq), 1)
            s = jnp.where(col // CHUNK <= row // CHUNK, s, NEG_INF)
        m_new = jnp.maximum(m, jnp.max(s, axis=-1, keepdims=True))
        p = jnp.exp(s - m_new)
        alpha = jnp.exp(m - m_new)
        l = alpha * l + jnp.sum(p, axis=-1, keepdims=True)
        acc = alpha * acc + _dot(p.astype(BF16), vb)
        return m_new, l, acc

    init = (jnp.full((tq, 1), NEG_INF, F32), jnp.zeros((tq, 1), F32), jnp.zeros((tq, MLA_V), F32))
    carry = lax.fori_loop(0, qi, lambda j, c: step(j, c, False), init)
    _, l, acc = step(qi, carry, True)
    o_ref[0] = (acc / l).astype(BF16)


def _mla_attn(q, k, v, tq):
    B, H, S, _ = q.shape
    scale = (MLA_NOPE + MLA_ROPE) ** -0.5
    return pl.pallas_call(
        functools.partial(_mla_attn_kernel, tq=tq, scale=scale),
        grid=(B, H, S // tq),
        in_specs=[pl.BlockSpec((1, 1, tq, MLA_QK_PAD), lambda b, h, i: (b, h, i, 0)),
                  pl.BlockSpec((1, 1, S, MLA_QK_PAD), lambda b, h, i: (b, h, 0, 0)),
                  pl.BlockSpec((1, 1, S, MLA_V), lambda b, h, i: (b, h, 0, 0))],
        out_specs=pl.BlockSpec((1, tq, MLA_V), lambda b, h, i: (b, i, h)),
        out_shape=jax.ShapeDtypeStruct((B, S, H * MLA_V), BF16),
        compiler_params=_cparams("parallel", "parallel", "arbitrary"),
        name="mla_attn",
    )(q, k, v)


def _proj_ln_kernel(a_ref, w_ref, x_ref, g_ref, b_ref, o_ref):
    m = _dot(a_ref[...], w_ref[...])
    y = DN_ALPHA * x_ref[...] + m
    o_ref[...] = _layernorm_rows(y, g_ref[...], b_ref[...])


def _proj_ln(a, w, xt, gain, bias, tm):
    T = xt.shape[0]
    wb = w.astype(BF16)
    g = gain.reshape(1, -1)
    b = bias.reshape(1, -1)
    return pl.pallas_call(
        _proj_ln_kernel,
        grid=(T // tm,),
        in_specs=[pl.BlockSpec((tm, a.shape[1]), lambda i: (i, 0)),
                  pl.BlockSpec(wb.shape, lambda i: (0, 0)),
                  pl.BlockSpec((tm, D_MODEL), lambda i: (i, 0)),
                  pl.BlockSpec(g.shape, lambda i: (0, 0)),
                  pl.BlockSpec(b.shape, lambda i: (0, 0))],
        out_specs=pl.BlockSpec((tm, D_MODEL), lambda i: (i, 0)),
        out_shape=jax.ShapeDtypeStruct((T, D_MODEL), F32),
        compiler_params=_cparams("parallel"),
        name="proj_ln",
    )(a, wb, xt, g, b)


def _dsa_proj_kernel(x_ref, w_ref, ch_ref, sh_ref, ci_ref, ck_ref, sk_ref,
                     q_ref, k_ref, v_ref, qi_ref, ki_ref, wi_ref, *, w_scale):
    xb = x_ref[...].astype(BF16)
    h = _dot(xb, w_ref[...])
    ch = ch_ref[...]
    sh = sh_ref[...]
    ci = ci_ref[...]
    half = LANES // 2
    for hd in range(DSA_HEADS):
        c0 = hd * DSA_HEAD_DIM
        qh = h[:, c0:c0 + LANES]
        q_ref[0, :, c0:c0 + LANES] = (qh * ch + pltpu.roll(qh, half, 1) * sh).astype(BF16)
        kh = h[:, DSA_HD + c0:DSA_HD + c0 + LANES]
        k_ref[0, :, c0:c0 + LANES] = (kh * ch + pltpu.roll(kh, half, 1) * sh).astype(BF16)
        t = h[:, 3 * DSA_HD + c0:3 * DSA_HD + c0 + LANES] * ci
        qi_ref[0, :, c0:c0 + LANES] = (t + pltpu.roll(t, half, 1)).astype(BF16)
    v_ref[0] = h[:, 2 * DSA_HD:3 * DSA_HD].astype(BF16)
    o = 4 * DSA_HD
    ki = h[:, o:o + LANES] * ck_ref[...] + h[:, o + LANES:o + 2 * LANES] * sk_ref[...]
    ki_ref[0] = ki.astype(BF16)
    wi_ref[0] = h[:, o + 2 * LANES:o + 3 * LANES] * w_scale


def _dsa_proj(xt, B, S, w_in, tm):
    T = B * S
    n_s = S // tm
    o1 = 3 * DSA_HD + IDX_HEADS * IDX_DIM
    ih = IDX_DIM // 2
    w_qi = w_in[:, 3 * DSA_HD:o1].reshape(D_MODEL, IDX_HEADS, IDX_DIM)
    w_qi_sw = jnp.concatenate([w_qi[..., ih:], w_qi[..., :ih]], axis=-1)
    w_qi_ext = jnp.concatenate([w_qi, w_qi_sw], axis=-1).reshape(D_MODEL, IDX_HEADS * LANES)
    w_ki = w_in[:, o1:o1 + IDX_DIM]
    w_ki_sw = jnp.concatenate([w_ki[:, ih:], w_ki[:, :ih]], axis=-1)
    zpad = lambda w: jnp.concatenate([w, jnp.zeros((w.shape[0], LANES - w.shape[1]), w.dtype)], axis=-1)
    w_ext = jnp.concatenate(
        [w_in[:, :3 * DSA_HD], w_qi_ext, zpad(w_ki), zpad(w_ki_sw), zpad(w_in[:, o1 + IDX_DIM:])],
        axis=-1).astype(BF16)
    cos_h, sin_h = _rope_tables(S, DSA_HEAD_DIM)
    cos_i, sin_i = _rope_tables(S, IDX_DIM)
    ch = jnp.concatenate([cos_h, cos_h], axis=-1)
    sh = jnp.concatenate([-sin_h, sin_h], axis=-1)
    ci = jnp.concatenate([cos_i, cos_i, -sin_i, sin_i], axis=-1)
    z = jnp.zeros((S, LANES - IDX_DIM), F32)
    ck = jnp.concatenate([cos_i, cos_i, z], axis=-1)
    sk = jnp.concatenate([-sin_i, sin_i, z], axis=-1)
    tab = pl.BlockSpec((tm, LANES), lambda i: (i % n_s, 0))
    row_map = lambda i: (i // n_s, i % n_s, 0)
    wide = pl.BlockSpec((1, tm, DSA_HD), row_map)
    narrow = pl.BlockSpec((1, tm, LANES), row_map)
    w_scale = IDX_HEADS ** -0.5 * IDX_DIM ** -0.5
    return pl.pallas_call(
        functools.partial(_dsa_proj_kernel, w_scale=w_scale),
        grid=(T // tm,),
        in_specs=[pl.BlockSpec((tm, D_MODEL), lambda i: (i, 0)),
                  pl.BlockSpec(w_ext.shape, lambda i: (0, 0)),
                  tab, tab, tab, tab, tab],
        out_specs=[wide, wide, wide, wide, narrow, narrow],
        out_shape=[jax.ShapeDtypeStruct((B, S, DSA_HD), BF16),
                   jax.ShapeDtypeStruct((B, S, DSA_HD), BF16),
                   jax.ShapeDtypeStruct((B, S, DSA_HD), BF16),
                   jax.ShapeDtypeStruct((B, S, IDX_HEADS * LANES), BF16),
                   jax.ShapeDtypeStruct((B, S, LANES), BF16),
                   jax.ShapeDtypeStruct((B, S, LANES), F32)],
        compiler_params=_cparams("parallel"),
        name="dsa_proj",
    )(xt, w_ext, ch, sh, ci, ck, sk)


def _dsa_attn_kernel(q_ref, qi_ref, wi_ref, k_ref, v_ref, ki_ref, o_ref, *, tq, S, topk, scale):
    blk = pl.program_id(1)
    ki = ki_ref[0]
    qi = qi_ref[0]
    wi = wi_ref[0]
    score = jnp.zeros((tq, S), F32)
    for hd in range(IDX_HEADS):
        logit = _dot_nt(qi[:, hd * LANES:(hd + 1) * LANES], ki)
        score = score + wi[:, hd:hd + 1] * jnp.maximum(logit, 0.0)
    row = lax.broadcasted_iota(I32, (tq, S), 0)
    col = lax.broadcasted_iota(I32, (tq, S), 1)
    adm = col // CHUNK <= (blk * tq + row) // CHUNK
    score = jnp.where(adm, score + 0.0, NEG_INF)
    bits = lax.bitcast_convert_type(score, I32)
    u = jnp.where(bits < 0, bits ^ 0x7FFFFFFF, bits)

    def search(i, thr):
        cand = thr + lax.shift_left(jnp.int32(1), 31 - i)
        cnt = jnp.sum((u >= cand).astype(F32), axis=-1, keepdims=True)
        return jnp.where(cnt >= topk, cand, thr)
    thr = lax.fori_loop(0, 32, search, jnp.full((tq, 1), INT_MIN, I32))

    gt = u > thr
    eq = u == thr
    need = topk - jnp.sum(gt.astype(F32), axis=-1, keepdims=True)
    r_i = lax.broadcasted_iota(I32, (LANES, LANES), 0)
    c_i = lax.broadcasted_iota(I32, (LANES, LANES), 1)
    tri = (r_i < c_i).astype(BF16)
    eq_b = eq.astype(BF16)
    run = jnp.zeros((tq, 1), F32)
    ranks = []
    for j in range(S // LANES):
        e = eq_b[:, j * LANES:(j + 1) * LANES]
        ranks.append(_dot(e, tri) + run)
        run = run + jnp.sum(e.astype(F32), axis=-1, keepdims=True)
    rank = jnp.concatenate(ranks, axis=-1)
    sel = adm & (gt | (eq & (rank < need)))

    for hd in range(DSA_HEADS):
        c0 = hd * DSA_HEAD_DIM
        s = _dot_nt(q_ref[0, :, c0:c0 + LANES], k_ref[0, :, c0:c0 + LANES]) * scale
        s = jnp.where(sel, s, NEG_INF)
        m = jnp.max(s, axis=-1, keepdims=True)
        p = jnp.exp(s - m)
        l = jnp.sum(p, axis=-1, keepdims=True)
        o = _dot(p.astype(BF16), v_ref[0, :, c0:c0 + LANES])
        o_ref[0, :, c0:c0 + LANES] = (o / l).astype(BF16)


def _dsa_attn(q, k, v, qi, ki, wi, tq):
    B, S, _ = q.shape
    topk = min(DSA_TOPK_MAX, S // 4)
    scale = DSA_HEAD_DIM ** -0.5
    qmap = lambda b, i: (b, i, 0)
    kmap = lambda b, i: (b, 0, 0)
    return pl.pallas_call(
        functools.partial(_dsa_attn_kernel, tq=tq, S=S, topk=topk, scale=scale),
        grid=(B, S // tq),
        in_specs=[pl.BlockSpec((1, tq, DSA_HD), qmap),
                  pl.BlockSpec((1, tq, IDX_HEADS * LANES), qmap),
                  pl.BlockSpec((1, tq, LANES), qmap),
                  pl.BlockSpec((1, S, DSA_HD), kmap),
                  pl.BlockSpec((1, S, DSA_HD), kmap),
                  pl.BlockSpec((1, S, LANES), kmap)],
        out_specs=pl.BlockSpec((1, tq, DSA_HD), qmap),
        out_shape=jax.ShapeDtypeStruct((B, S, DSA_HD), BF16),
        compiler_params=_cparams("parallel", "arbitrary"),
        name="dsa_attn",
    )(q, qi, wi, k, v, ki)


PEER_RANKS = PEER_TOPK + 1
_PEER_NB = [PEER_RANKS // (a + 1) for a in range(PEER_RANKS)]
_PEER_V2_ROWS = -(-PEER_RANKS // SUBLANES) * SUBLANES


def _top_rows(s, n):
    vals = []
    cur = s
    for _ in range(n):
        m = jnp.max(cur, axis=0, keepdims=True)
        vals.append(m)
        cur = jnp.where(cur == m, NEG_INF, cur)
    return vals


def _peer_route_kernel(x_ref, wq_ref, keys_ref, c1_ref, e1_ref, s2_ref, e2_ref, v2_ref):
    tt = x_ref.shape[0]
    q = _dot(x_ref[...].astype(BF16), wq_ref[...]).astype(BF16)
    half = PEER_QDIM // 2
    v2_ref[PEER_TOPK:, :] = jnp.full((_PEER_V2_ROWS - PEER_TOPK, tt), NEG_INF, F32)
    for hd in range(PEER_HEADS):
        c0 = hd * PEER_QDIM
        s1 = _dot_nt(keys_ref[0, hd], q[:, c0:c0 + half])
        s2 = _dot_nt(keys_ref[1, hd], q[:, c0 + half:c0 + PEER_QDIM])
        v1 = _top_rows(s1, PEER_RANKS)
        v2 = _top_rows(s2, PEER_RANKS)
        for a in range(PEER_RANKS):
            v2_ref[a:a + 1, :] = v2[a]
        pieces = []
        for a in range(PEER_RANKS):
            nb = _PEER_NB[a]
            rows = -(-nb // SUBLANES) * SUBLANES
            blk = v2_ref[0:rows, :] + v1[a]
            ridx = lax.broadcasted_iota(I32, (rows, tt), 0)
            pieces.append(jnp.where(ridx < nb, blk, NEG_INF))
        cand = jnp.concatenate(pieces, axis=0)
        cur = cand
        tops = []
        for _ in range(PEER_RANKS):
            m = jnp.max(cur, axis=0, keepdims=True)
            tops.append(m)
            cur = jnp.where(cur == m, NEG_INF, cur)
        th16 = tops[PEER_TOPK - 1]
        theta = 0.5 * (th16 + tops[PEER_TOPK])
        top = v1[0] + v2[0]
        z = jnp.sum(jnp.where(cand >= th16, jnp.exp(cand - top), 0.0), axis=0, keepdims=True)
        c1_ref[hd] = theta - s1
        s2_ref[hd] = s2
        e1_ref[hd] = jnp.exp(s1 - v1[0]) / z
        e2_ref[hd] = jnp.exp(s2 - v2[0])


def _peer_route(xt, w_q, sub_keys, tt):
    T = xt.shape[0]
    wq = w_q.astype(BF16)
    keys = sub_keys.astype(BF16)
    big = jax.ShapeDtypeStruct((PEER_HEADS, PEER_NKEYS, T), F32)
    big_spec = pl.BlockSpec((PEER_HEADS, PEER_NKEYS, tt), lambda i: (0, 0, i))
    return pl.pallas_call(
        _peer_route_kernel,
        grid=(T // tt,),
        in_specs=[pl.BlockSpec((tt, D_MODEL), lambda i: (i, 0)),
                  pl.BlockSpec(wq.shape, lambda i: (0, 0)),
                  pl.BlockSpec(keys.shape, lambda i: (0, 0, 0, 0))],
        out_specs=[big_spec, big_spec, big_spec, big_spec],
        out_shape=[big, big, big, big],
        scratch_shapes=[pltpu.VMEM((_PEER_V2_ROWS, tt), F32)],
        compiler_params=_cparams("parallel"),
        name="peer_route",
    )(xt, wq, keys)


PEER_CHUNK = 2 * PEER_NKEYS


def _peer_dense_kernel(x_ref, wd_ref, wu_ref, c1_ref, e1_ref, s2_ref, e2_ref, g_ref, b_ref,
                       o_ref, xb_ref, p_ref, acc_ref, c1r_ref, e1r_ref, *, te, tt):
    e = pl.program_id(1)

    @pl.when(e == 0)
    def _():
        xb_ref[...] = x_ref[...].astype(BF16)
        acc_ref[...] = jnp.zeros_like(acc_ref)

        def relayout(g, carry):
            g0 = pl.multiple_of(g * SUBLANES, SUBLANES)
            for hd in range(PEER_HEADS):
                c1g = c1_ref[hd, pl.ds(g0, SUBLANES), :]
                e1g = e1_ref[hd, pl.ds(g0, SUBLANES), :]
                for r in range(SUBLANES):
                    c1r_ref[g0 + r, hd:hd + 1, :] = c1g[r:r + 1, :]
                    e1r_ref[g0 + r, hd:hd + 1, :] = e1g[r:r + 1, :]
            return carry
        lax.fori_loop(0, PEER_NKEYS // SUBLANES, relayout, 0)

    xb = xb_ref[...]
    n_ch = te // PEER_CHUNK
    i1_0 = e * (te // PEER_NKEYS)
    a_next = _dot_nt(wd_ref[0:PEER_CHUNK, :], xb)
    for c in range(n_ch):
        a_cur = a_next
        if c + 1 < n_ch:
            a_next = _dot_nt(wd_ref[(c + 1) * PEER_CHUNK:(c + 2) * PEER_CHUNK, :], xb)
        for j in range(PEER_CHUNK // PEER_NKEYS):
            i1 = i1_0 + c * (PEER_CHUNK // PEER_NKEYS) + j
            r0 = c * PEER_CHUNK + j * PEER_NKEYS
            for lg in range(tt // LANES):
                ls = slice(lg * LANES, (lg + 1) * LANES)
                gate = jnp.zeros((PEER_NKEYS, LANES), F32)
                for hd in range(PEER_HEADS):
                    c1b = c1r_ref[i1, hd:hd + 1, ls]
                    e1b = e1r_ref[i1, hd:hd + 1, ls]
                    gate = gate + jnp.where(s2_ref[hd, :, ls] >= c1b, e2_ref[hd, :, ls] * e1b, 0.0)
                a = a_cur[j * PEER_NKEYS:(j + 1) * PEER_NKEYS, ls]
                act = 0.5 * a * (1.0 + lax.erf(a * (2.0 ** -0.5)))
                p_ref[r0:r0 + PEER_NKEYS, ls] = (act * gate).astype(BF16)
        acc_ref[...] += _dot(wu_ref[c], p_ref[c * PEER_CHUNK:(c + 1) * PEER_CHUNK, :])

    @pl.when(e == pl.num_programs(1) - 1)
    def _():
        y = DN_ALPHA * x_ref[...] + acc_ref[...].T
        o_ref[...] = _layernorm_rows(y, g_ref[...], b_ref[...])


def _peer_dense(xt, w_down, w_up, route, gain, bias, tt, te):
    T = xt.shape[0]
    c1, e1, s2, e2 = route
    wd = w_down.astype(BF16)
    wu = w_up.astype(BF16).reshape(PEER_EXPERTS // PEER_CHUNK, PEER_CHUNK, D_MODEL).transpose(0, 2, 1)
    g = gain.reshape(1, -1)
    b = bias.reshape(1, -1)
    big_spec = pl.BlockSpec((PEER_HEADS, PEER_NKEYS, tt), lambda i, e: (0, 0, i))
    return pl.pallas_call(
        functools.partial(_peer_dense_kernel, te=te, tt=tt),
        grid=(T // tt, PEER_EXPERTS // te),
        in_specs=[pl.BlockSpec((tt, D_MODEL), lambda i, e: (i, 0)),
                  pl.BlockSpec((te, D_MODEL), lambda i, e: (e, 0)),
                  pl.BlockSpec((te // PEER_CHUNK, D_MODEL, PEER_CHUNK), lambda i, e: (e, 0, 0)),
                  big_spec, big_spec, big_spec, big_spec,
                  pl.BlockSpec(g.shape, lambda i, e: (0, 0)),
                  pl.BlockSpec(b.shape, lambda i, e: (0, 0))],
        out_specs=pl.BlockSpec((tt, D_MODEL), lambda i, e: (i, 0)),
        out_shape=jax.ShapeDtypeStruct((T, D_MODEL), F32),
        scratch_shapes=[pltpu.VMEM((tt, D_MODEL), BF16),
                        pltpu.VMEM((te, tt), BF16),
                        pltpu.VMEM((D_MODEL, tt), F32),
                        pltpu.VMEM((PEER_NKEYS, PEER_HEADS, tt), F32),
                        pltpu.VMEM((PEER_NKEYS, PEER_HEADS, tt), F32)],
        compiler_params=_cparams("parallel", "arbitrary"),
        name="peer_dense",
    )(xt, wd, wu, c1, e1, s2, e2, g, b)


def _rope_tables(seq, dim):
    inv = ROPE_THETA ** (-jnp.arange(0, dim, 2, dtype=F32) / dim)
    ang = jnp.arange(seq, dtype=F32)[:, None] * inv[None, :]
    return jnp.cos(ang), jnp.sin(ang)


def _tiles(S, T):
    row = min(256, S)
    tq_mla = min(256, S)
    tq_dsa = min(128, S)
    tt_route = min(256, T)
    tt_dense = min(512, T)
    te_dense = 1024
    return row, tq_mla, tq_dsa, tt_route, tt_dense, te_dense


def _peer_layer(xt, w_q, sub_keys, w_down, w_up, gain, bias, tt_route, tt_dense, te_dense):
    route = _peer_route(xt, w_q, sub_keys, tt_route)
    return _peer_dense(xt, w_down, w_up, route, gain, bias, tt_dense, te_dense)


def kernel(x, mla_w_in, mla_q_norm, mla_kv_norm, mla_w_uq, mla_w_ukv, mla_w_o,
           dsa_w_in, dsa_w_o, peer_w_q, peer_sub_keys, peer_w_down, peer_w_up,
           ln_gain, ln_bias):
    B, S, D = x.shape
    T = B * S
    row, tq_mla, tq_dsa, tt_route, tt_dense, te_dense = _tiles(S, T)
    xt = x.reshape(T, D)

    cos, sin = _rope_tables(S, MLA_ROPE)
    q, k, v = _mla_proj(xt, B, S, mla_w_in[0], mla_q_norm[0], mla_kv_norm[0],
                        mla_w_uq[0], mla_w_ukv[0], cos, sin, row)
    o = _mla_attn(q, k, v, tq_mla).reshape(T, MLA_HEADS * MLA_V)
    xt = _proj_ln(o, mla_w_o[0], xt, ln_gain[0, 0], ln_bias[0, 0], row)
    xt = _peer_layer(xt, peer_w_q[0], peer_sub_keys[0], peer_w_down[0], peer_w_up[0],
                     ln_gain[0, 1], ln_bias[0, 1], tt_route, tt_dense, te_dense)

    q, k, v, qi, ki, wi = _dsa_proj(xt, B, S, dsa_w_in[0], row)
    o = _dsa_attn(q, k, v, qi, ki, wi, tq_dsa).reshape(T, DSA_HD)
    xt = _proj_ln(o, dsa_w_o[0], xt, ln_gain[1, 0], ln_bias[1, 0], row)
    xt = _peer_layer(xt, peer_w_q[1], peer_sub_keys[1], peer_w_down[1], peer_w_up[1],
                     ln_gain[1, 1], ln_bias[1, 1], tt_route, tt_dense, te_dense)
    return xt.reshape(B, S, D)
```

```python
import functools

import jax
import jax.numpy as jnp
from jax import lax
from jax.experimental import pallas as pl
from jax.experimental.pallas import tpu as pltpu

F32 = jnp.float32
BF16 = jnp.bfloat16
I32 = jnp.int32

D_MODEL = 1024
DEPTH = 2
CHUNK = 64
ROPE_THETA = 10000.0
LN_EPS = 1e-5
RMS_EPS = 1e-6
DN_ALPHA = (2 * DEPTH) ** 0.25

MLA_HEADS = 8
MLA_NOPE = 128
MLA_ROPE = 64
MLA_V = 128
MLA_Q_RANK = 384
MLA_KV_RANK = 256
MLA_QK_PAD = 256

DSA_HEADS = 8
DSA_HEAD_DIM = 128
IDX_HEADS = 8
IDX_DIM = 64
DSA_TOPK_MAX = 256
DSA_HD = DSA_HEADS * DSA_HEAD_DIM

PEER_HEADS = 8
PEER_NKEYS = 128
PEER_EXPERTS = PEER_NKEYS * PEER_NKEYS
PEER_QDIM = 256
PEER_TOPK = 16

LANES = 128
SUBLANES = 8
VMEM_LIMIT = 56 * 1024 * 1024

NEG_INF = float("-inf")
INT_MIN = -2 ** 31


def _cparams(*sem):
    return pltpu.CompilerParams(dimension_semantics=sem, vmem_limit_bytes=VMEM_LIMIT)


def _dot(a, b):
    return jnp.dot(a, b, preferred_element_type=F32)


def _dot_nt(a, b):
    return lax.dot_general(a, b, (((1,), (1,)), ((), ())), preferred_element_type=F32)


def _layernorm_rows(y, g, b):
    mu = jnp.mean(y, axis=-1, keepdims=True)
    yc = y - mu
    var = jnp.mean(yc * yc, axis=-1, keepdims=True)
    return yc * lax.rsqrt(var + LN_EPS) * g + b


def _rms_rows(h, g):
    ms = jnp.mean(h * h, axis=-1, keepdims=True)
    return h * lax.rsqrt(ms + RMS_EPS) * g


def _mla_proj_kernel(x_ref, win_ref, qn_ref, kvn_ref, wuq_ref, wukv_ref, cc_ref, ss_ref,
                     q_ref, k_ref, v_ref):
    xb = x_ref[...].astype(BF16)
    h = _dot(xb, win_ref[...])
    cq = h[:, :MLA_Q_RANK]
    ckv = h[:, MLA_Q_RANK:MLA_Q_RANK + MLA_KV_RANK]
    o = MLA_Q_RANK + MLA_KV_RANK
    cc = cc_ref[...]
    ss = ss_ref[...]
    k_rope = h[:, o:o + LANES] * cc + h[:, o + LANES:o + 2 * LANES] * ss
    qall = _dot(_rms_rows(cq, qn_ref[...]).astype(BF16), wuq_ref[...])
    kvall = _dot(_rms_rows(ckv, kvn_ref[...]).astype(BF16), wukv_ref[...])
    k_rope_b = k_rope.astype(BF16)
    for hd in range(MLA_HEADS):
        b0 = hd * 3 * LANES
        q_ref[0, hd, :, 0:LANES] = qall[:, b0:b0 + LANES].astype(BF16)
        q_rope = qall[:, b0 + LANES:b0 + 2 * LANES] * cc + qall[:, b0 + 2 * LANES:b0 + 3 * LANES] * ss
        q_ref[0, hd, :, LANES:2 * LANES] = q_rope.astype(BF16)
        k_ref[0, hd, :, 0:LANES] = kvall[:, hd * LANES:(hd + 1) * LANES].astype(BF16)
        k_ref[0, hd, :, LANES:2 * LANES] = k_rope_b
        v0 = MLA_HEADS * MLA_NOPE + hd * MLA_V
        v_ref[0, hd] = kvall[:, v0:v0 + MLA_V].astype(BF16)


def _mla_proj(xt, B, S, w_in, q_norm, kv_norm, w_uq, w_ukv, cos, sin, tm):
    T = B * S
    n_s = S // tm
    half = MLA_ROPE // 2
    def swap_rope(w):
        return jnp.concatenate([w[..., half:], w[..., :half]], axis=-1)
    zpad = lambda w: jnp.concatenate([w, jnp.zeros_like(w)], axis=-1)
    w_kr = w_in[:, MLA_Q_RANK + MLA_KV_RANK:]
    win_ext = jnp.concatenate(
        [w_in[:, :MLA_Q_RANK + MLA_KV_RANK], zpad(w_kr), zpad(swap_rope(w_kr))], axis=-1).astype(BF16)
    uq_nope = w_uq[:, :, :MLA_NOPE]
    uq_rope = w_uq[:, :, MLA_NOPE:]
    wuq_ext = jnp.concatenate([uq_nope, zpad(uq_rope), zpad(swap_rope(uq_rope))], axis=-1)
    wuq_ext = wuq_ext.reshape(MLA_Q_RANK, MLA_HEADS * 3 * LANES).astype(BF16)
    wukv_ext = jnp.concatenate(
        [w_ukv[:, :, :MLA_NOPE].reshape(MLA_KV_RANK, -1), w_ukv[:, :, MLA_NOPE:].reshape(MLA_KV_RANK, -1)],
        axis=-1).astype(BF16)
    z = jnp.zeros((S, LANES - MLA_ROPE), F32)
    cc = jnp.concatenate([cos, cos, z], axis=-1)
    ss = jnp.concatenate([-sin, sin, z], axis=-1)
    full = lambda a: pl.BlockSpec(a.shape, lambda i: (0,) * a.ndim)
    qn = q_norm.reshape(1, -1)
    kvn = kv_norm.reshape(1, -1)
    head_map = lambda i: (i // n_s, 0, i % n_s, 0)
    return pl.pallas_call(
        _mla_proj_kernel,
        grid=(T // tm,),
        in_specs=[pl.BlockSpec((tm, D_MODEL), lambda i: (i, 0)),
                  full(win_ext), full(qn), full(kvn), full(wuq_ext), full(wukv_ext),
                  pl.BlockSpec((tm, LANES), lambda i: (i % n_s, 0)),
                  pl.BlockSpec((tm, LANES), lambda i: (i % n_s, 0))],
        out_specs=[pl.BlockSpec((1, MLA_HEADS, tm, MLA_QK_PAD), head_map),
                   pl.BlockSpec((1, MLA_HEADS, tm, MLA_QK_PAD), head_map),
                   pl.BlockSpec((1, MLA_HEADS, tm, MLA_V), head_map)],
        out_shape=[jax.ShapeDtypeStruct((B, MLA_HEADS, S, MLA_QK_PAD), BF16),
                   jax.ShapeDtypeStruct((B, MLA_HEADS, S, MLA_QK_PAD), BF16),
                   jax.ShapeDtypeStruct((B, MLA_HEADS, S, MLA_V), BF16)],
        compiler_params=_cparams("parallel"),
        name="mla_proj",
    )(xt, win_ext, qn, kvn, wuq_ext, wukv_ext, cc, ss)


MLA_HEADS_PER_STEP = 4


def _mla_attn_kernel(q_ref, k_ref, v_ref, o_ref, *, tq, scale):
    qi = pl.program_id(2)
    hp = MLA_HEADS_PER_STEP

    def step(j, carry, diagonal):
        start = pl.multiple_of(j * tq, tq)
        out = []
        for h in range(hp):
            m, l, acc = carry[h]
            kb = k_ref[0, h, pl.ds(start, tq), :]
            vb = v_ref[0, h, pl.ds(start, tq), :]
            s = _dot_nt(q_ref[0, h], kb) * scale
            if diagonal:
                row = lax.broadcasted_iota(I32, (tq, tq), 0)
                col = lax.broadcasted_iota(I32, (tq, tq), 1)
                s = jnp.where(col // CHUNK <= row // CHUNK, s, NEG_INF)
            m_new = jnp.maximum(m, jnp.max(s, axis=-1, keepdims=True))
            p = jnp.exp(s - m_new)
            alpha = jnp.exp(m - m_new)
            l = alpha * l + jnp.sum(p, axis=-1, keepdims=True)
            acc = alpha * acc + _dot(p.astype(BF16), vb)
            out.append((m_new, l, acc))
        return tuple(out)

    init = tuple((jnp.full((tq, 1), NEG_INF, F32), jnp.zeros((tq, 1), F32), jnp.zeros((tq, MLA_V), F32))
                 for _ in range(hp))
    carry = lax.fori_loop(0, qi, lambda j, c: step(j, c, False), init)
    carry = step(qi, carry, True)
    for h in range(hp):
        _, l, acc = carry[h]
        o_ref[0, :, h * MLA_V:(h + 1) * MLA_V] = (acc / l).astype(BF16)


def _mla_attn(q, k, v, tq):
    B, H, S, _ = q.shape
    hp = MLA_HEADS_PER_STEP
    scale = (MLA_NOPE + MLA_ROPE) ** -0.5
    return pl.pallas_call(
        functools.partial(_mla_attn_kernel, tq=tq, scale=scale),
        grid=(B, H // hp, S // tq),
        in_specs=[pl.BlockSpec((1, hp, tq, MLA_QK_PAD), lambda b, h, i: (b, h, i, 0)),
                  pl.BlockSpec((1, hp, S, MLA_QK_PAD), lambda b, h, i: (b, h, 0, 0)),
                  pl.BlockSpec((1, hp, S, MLA_V), lambda b, h, i: (b, h, 0, 0))],
        out_specs=pl.BlockSpec((1, tq, hp * MLA_V), lambda b, h, i: (b, i, h)),
        out_shape=jax.ShapeDtypeStruct((B, S, H * MLA_V), BF16),
        compiler_params=_cparams("parallel", "parallel", "arbitrary"),
        name="mla_attn",
    )(q, k, v)


def _proj_ln_kernel(a_ref, w_ref, x_ref, g_ref, b_ref, o_ref):
    m = _dot(a_ref[...], w_ref[...])
    y = DN_ALPHA * x_ref[...] + m
    o_ref[...] = _layernorm_rows(y, g_ref[...], b_ref[...])


def _proj_ln(a, w, xt, gain, bias, tm):
    T = xt.shape[0]
    wb = w.astype(BF16)
    g = gain.reshape(1, -1)
    b = bias.reshape(1, -1)
    return pl.pallas_call(
        _proj_ln_kernel,
        grid=(T // tm,),
        in_specs=[pl.BlockSpec((tm, a.shape[1]), lambda i: (i, 0)),
                  pl.BlockSpec(wb.shape, lambda i: (0, 0)),
                  pl.BlockSpec((tm, D_MODEL), lambda i: (i, 0)),
                  pl.BlockSpec(g.shape, lambda i: (0, 0)),
                  pl.BlockSpec(b.shape, lambda i: (0, 0))],
        out_specs=pl.BlockSpec((tm, D_MODEL), lambda i: (i, 0)),
        out_shape=jax.ShapeDtypeStruct((T, D_MODEL), F32),
        compiler_params=_cparams("parallel"),
        name="proj_ln",
    )(a, wb, xt, g, b)


def _dsa_proj_kernel(x_ref, w_ref, ch_ref, sh_ref, ci_ref, ck_ref, sk_ref,
                     q_ref, k_ref, v_ref, qi_ref, ki_ref, wi_ref, *, w_scale):
    xb = x_ref[...].astype(BF16)
    h = _dot(xb, w_ref[...])
    ch = ch_ref[...]
    sh = sh_ref[...]
    ci = ci_ref[...]
    half = LANES // 2
    for hd in range(DSA_HEADS):
        c0 = hd * DSA_HEAD_DIM
        qh = h[:, c0:c0 + LANES]
        q_ref[0, :, c0:c0 + LANES] = (qh * ch + pltpu.roll(qh, half, 1) * sh).astype(BF16)
        kh = h[:, DSA_HD + c0:DSA_HD + c0 + LANES]
        k_ref[0, :, c0:c0 + LANES] = (kh * ch + pltpu.roll(kh, half, 1) * sh).astype(BF16)
        t = h[:, 3 * DSA_HD + c0:3 * DSA_HD + c0 + LANES] * ci
        qi_ref[0, :, c0:c0 + LANES] = (t + pltpu.roll(t, half, 1)).astype(BF16)
    v_ref[0] = h[:, 2 * DSA_HD:3 * DSA_HD].astype(BF16)
    o = 4 * DSA_HD
    ki = h[:, o:o + LANES] * ck_ref[...] + h[:, o + LANES:o + 2 * LANES] * sk_ref[...]
    ki_ref[0] = ki.astype(BF16)
    wi_ref[0] = h[:, o + 2 * LANES:o + 3 * LANES] * w_scale


def _dsa_proj(xt, B, S, w_in, tm):
    T = B * S
    n_s = S // tm
    o1 = 3 * DSA_HD + IDX_HEADS * IDX_DIM
    ih = IDX_DIM // 2
    w_qi = w_in[:, 3 * DSA_HD:o1].reshape(D_MODEL, IDX_HEADS, IDX_DIM)
    w_qi_sw = jnp.concatenate([w_qi[..., ih:], w_qi[..., :ih]], axis=-1)
    w_qi_ext = jnp.concatenate([w_qi, w_qi_sw], axis=-1).reshape(D_MODEL, IDX_HEADS * LANES)
    w_ki = w_in[:, o1:o1 + IDX_DIM]
    w_ki_sw = jnp.concatenate([w_ki[:, ih:], w_ki[:, :ih]], axis=-1)
    zpad = lambda w: jnp.concatenate([w, jnp.zeros((w.shape[0], LANES - w.shape[1]), w.dtype)], axis=-1)
    w_ext = jnp.concatenate(
        [w_in[:, :3 * DSA_HD], w_qi_ext, zpad(w_ki), zpad(w_ki_sw), zpad(w_in[:, o1 + IDX_DIM:])],
        axis=-1).astype(BF16)
    cos_h, sin_h = _rope_tables(S, DSA_HEAD_DIM)
    cos_i, sin_i = _rope_tables(S, IDX_DIM)
    ch = jnp.concatenate([cos_h, cos_h], axis=-1)
    sh = jnp.concatenate([-sin_h, sin_h], axis=-1)
    ci = jnp.concatenate([cos_i, cos_i, -sin_i, sin_i], axis=-1)
    z = jnp.zeros((S, LANES - IDX_DIM), F32)
    ck = jnp.concatenate([cos_i, cos_i, z], axis=-1)
    sk = jnp.concatenate([-sin_i, sin_i, z], axis=-1)
    tab = pl.BlockSpec((tm, LANES), lambda i: (i % n_s, 0))
    row_map = lambda i: (i // n_s, i % n_s, 0)
    wide = pl.BlockSpec((1, tm, DSA_HD), row_map)
    narrow = pl.BlockSpec((1, tm, LANES), row_map)
    w_scale = IDX_HEADS ** -0.5 * IDX_DIM ** -0.5
    return pl.pallas_call(
        functools.partial(_dsa_proj_kernel, w_scale=w_scale),
        grid=(T // tm,),
        in_specs=[pl.BlockSpec((tm, D_MODEL), lambda i: (i, 0)),
                  pl.BlockSpec(w_ext.shape, lambda i: (0, 0)),
                  tab, tab, tab, tab, tab],
        out_specs=[wide, wide, wide, wide, narrow, narrow],
        out_shape=[jax.ShapeDtypeStruct((B, S, DSA_HD), BF16),
                   jax.ShapeDtypeStruct((B, S, DSA_HD), BF16),
                   jax.ShapeDtypeStruct((B, S, DSA_HD), BF16),
                   jax.ShapeDtypeStruct((B, S, IDX_HEADS * LANES), BF16),
                   jax.ShapeDtypeStruct((B, S, LANES), BF16),
                   jax.ShapeDtypeStruct((B, S, LANES), F32)],
        compiler_params=_cparams("parallel"),
        name="dsa_proj",
    )(xt, w_ext, ch, sh, ci, ck, sk)


def _dsa_attn_kernel(q_ref, qi_ref, wi_ref, k_ref, v_ref, ki_ref, o_ref, *, tq, S, topk, scale):
    blk = pl.program_id(1)
    ki = ki_ref[0]
    qi = qi_ref[0]
    wi = wi_ref[0]
    score = jnp.zeros((tq, S), F32)
    for hd in range(IDX_HEADS):
        logit = _dot_nt(qi[:, hd * LANES:(hd + 1) * LANES], ki)
        score = score + wi[:, hd:hd + 1] * jnp.maximum(logit, 0.0)
    row = lax.broadcasted_iota(I32, (tq, S), 0)
    col = lax.broadcasted_iota(I32, (tq, S), 1)
    adm = col // CHUNK <= (blk * tq + row) // CHUNK
    score = jnp.where(adm, score + 0.0, NEG_INF)
    bits = lax.bitcast_convert_type(score, I32)
    u = jnp.where(bits < 0, bits ^ 0x7FFFFFFF, bits)

    def search(i, thr):
        cand = thr + lax.shift_left(jnp.int32(1), 31 - i)
        cnt = jnp.sum((u >= cand).astype(F32), axis=-1, keepdims=True)
        return jnp.where(cnt >= topk, cand, thr)
    thr = lax.fori_loop(0, 32, search, jnp.full((tq, 1), INT_MIN, I32))

    gt = u > thr
    eq = u == thr
    need = topk - jnp.sum(gt.astype(F32), axis=-1, keepdims=True)
    r_i = lax.broadcasted_iota(I32, (LANES, LANES), 0)
    c_i = lax.broadcasted_iota(I32, (LANES, LANES), 1)
    tri = (r_i < c_i).astype(BF16)
    eq_b = eq.astype(BF16)
    run = jnp.zeros((tq, 1), F32)
    ranks = []
    for j in range(S // LANES):
        e = eq_b[:, j * LANES:(j + 1) * LANES]
        ranks.append(_dot(e, tri) + run)
        run = run + jnp.sum(e.astype(F32), axis=-1, keepdims=True)
    rank = jnp.concatenate(ranks, axis=-1)
    sel = adm & (gt | (eq & (rank < need)))

    for hd in range(DSA_HEADS):
        c0 = hd * DSA_HEAD_DIM
        s = _dot_nt(q_ref[0, :, c0:c0 + LANES], k_ref[0, :, c0:c0 + LANES]) * scale
        s = jnp.where(sel, s, NEG_INF)
        m = jnp.max(s, axis=-1, keepdims=True)
        p = jnp.exp(s - m)
        l = jnp.sum(p, axis=-1, keepdims=True)
        o = _dot(p.astype(BF16), v_ref[0, :, c0:c0 + LANES])
        o_ref[0, :, c0:c0 + LANES] = (o / l).astype(BF16)


def _dsa_attn(q, k, v, qi, ki, wi, tq):
    B, S, _ = q.shape
    topk = min(DSA_TOPK_MAX, S // 4)
    scale = DSA_HEAD_DIM ** -0.5
    qmap = lambda b, i: (b, i, 0)
    kmap = lambda b, i: (b, 0, 0)
    return pl.pallas_call(
        functools.partial(_dsa_attn_kernel, tq=tq, S=S, topk=topk, scale=scale),
        grid=(B, S // tq),
        in_specs=[pl.BlockSpec((1, tq, DSA_HD), qmap),
                  pl.BlockSpec((1, tq, IDX_HEADS * LANES), qmap),
                  pl.BlockSpec((1, tq, LANES), qmap),
                  pl.BlockSpec((1, S, DSA_HD), kmap),
                  pl.BlockSpec((1, S, DSA_HD), kmap),
                  pl.BlockSpec((1, S, LANES), kmap)],
        out_specs=pl.BlockSpec((1, tq, DSA_HD), qmap),
        out_shape=jax.ShapeDtypeStruct((B, S, DSA_HD), BF16),
        compiler_params=_cparams("parallel", "arbitrary"),
        name="dsa_attn",
    )(q, qi, wi, k, v, ki)


_PEER_NB = [PEER_TOPK // (a + 1) for a in range(PEER_TOPK)]


def _top_rows(s, n):
    vals = []
    cur = s
    for _ in range(n):
        m = jnp.max(cur, axis=0, keepdims=True)
        vals.append(m)
        cur = jnp.where(cur == m, NEG_INF, cur)
    return vals


def _peer_route_kernel(x_ref, wq_ref, keys_ref, lim_ref, e1_ref, r2_ref, e2_ref, v2_ref):
    tt = x_ref.shape[0]
    q = _dot(x_ref[...].astype(BF16), wq_ref[...]).astype(BF16)
    half = PEER_QDIM // 2
    for hd in range(PEER_HEADS):
        c0 = hd * PEER_QDIM
        s1 = _dot_nt(keys_ref[0, hd], q[:, c0:c0 + half])
        s2 = _dot_nt(keys_ref[1, hd], q[:, c0 + half:c0 + PEER_QDIM])
        v1 = _top_rows(s1, PEER_TOPK)
        v2 = _top_rows(s2, PEER_TOPK)
        for a in range(PEER_TOPK):
            v2_ref[a:a + 1, :] = v2[a]
        pieces = []
        for a in range(PEER_TOPK):
            nb = _PEER_NB[a]
            rows = -(-nb // SUBLANES) * SUBLANES
            blk = v2_ref[0:rows, :] + v1[a]
            ridx = lax.broadcasted_iota(I32, (rows, tt), 0)
            pieces.append(jnp.where(ridx < nb, blk, NEG_INF))
        cand = jnp.concatenate(pieces, axis=0)
        cur = cand
        theta = None
        for _ in range(PEER_TOPK):
            theta = jnp.max(cur, axis=0, keepdims=True)
            cur = jnp.where(cur == theta, NEG_INF, cur)
        top = v1[0] + v2[0]
        z = jnp.sum(jnp.where(cand >= theta, jnp.exp(cand - top), 0.0), axis=0, keepdims=True)
        r2 = jnp.full(s2.shape, float(PEER_TOPK), F32)
        for b in reversed(range(PEER_TOPK)):
            r2 = jnp.where(s2 >= v2[b], float(b), r2)
        lim = jnp.zeros(s1.shape, F32)
        for b in range(PEER_TOPK):
            lim = jnp.where(s1 + v2[b] >= theta, float(b + 1), lim)
        lim_ref[hd] = lim
        r2_ref[hd] = r2.astype(BF16)
        e1_ref[hd] = jnp.exp(s1 - v1[0]) / z
        e2_ref[hd] = jnp.exp(s2 - v2[0]).astype(BF16)


def _peer_route(xt, w_q, sub_keys, tt):
    T = xt.shape[0]
    wq = w_q.astype(BF16)
    keys = sub_keys.astype(BF16)
    big = jax.ShapeDtypeStruct((PEER_HEADS, PEER_NKEYS, T), F32)
    big_b = jax.ShapeDtypeStruct((PEER_HEADS, PEER_NKEYS, T), BF16)
    big_spec = pl.BlockSpec((PEER_HEADS, PEER_NKEYS, tt), lambda i: (0, 0, i))
    return pl.pallas_call(
        _peer_route_kernel,
        grid=(T // tt,),
        in_specs=[pl.BlockSpec((tt, D_MODEL), lambda i: (i, 0)),
                  pl.BlockSpec(wq.shape, lambda i: (0, 0)),
                  pl.BlockSpec(keys.shape, lambda i: (0, 0, 0, 0))],
        out_specs=[big_spec, big_spec, big_spec, big_spec],
        out_shape=[big, big, big_b, big_b],
        scratch_shapes=[pltpu.VMEM((PEER_TOPK, tt), F32)],
        compiler_params=_cparams("parallel"),
        name="peer_route",
    )(xt, wq, keys)


PEER_CHUNK = 2 * PEER_NKEYS


BF16_SUBLANES = 2 * SUBLANES


def _row_bf16(ref, i1, hd, ls):
    row = jnp.broadcast_to(ref[i1, hd:hd + 1, ls], (BF16_SUBLANES, LANES)).astype(BF16)
    return jnp.tile(row, (PEER_NKEYS // BF16_SUBLANES, 1))


def _peer_dense_kernel(x_ref, wd_ref, wu_ref, lim_ref, e1_ref, r2_ref, e2_ref, g_ref, b_ref,
                       o_ref, xb_ref, p_ref, acc_ref, limr_ref, e1r_ref, r2s_ref, e2s_ref, *, te, tt):
    e = pl.program_id(1)

    @pl.when(e == 0)
    def _():
        xb_ref[...] = x_ref[...].astype(BF16)
        acc_ref[...] = jnp.zeros_like(acc_ref)
        r2s_ref[...] = r2_ref[...]
        e2s_ref[...] = e2_ref[...]

        def relayout(g, carry):
            g0 = pl.multiple_of(g * SUBLANES, SUBLANES)
            for hd in range(PEER_HEADS):
                limg = lim_ref[hd, pl.ds(g0, SUBLANES), :]
                e1g = e1_ref[hd, pl.ds(g0, SUBLANES), :]
                for r in range(SUBLANES):
                    limr_ref[g0 + r, hd:hd + 1, :] = limg[r:r + 1, :]
                    e1r_ref[g0 + r, hd:hd + 1, :] = e1g[r:r + 1, :]
            return carry
        lax.fori_loop(0, PEER_NKEYS // SUBLANES, relayout, 0)

    xb = xb_ref[...]
    n_ch = te // PEER_CHUNK
    i1_0 = e * (te // PEER_NKEYS)
    a_next = _dot_nt(wd_ref[0:PEER_CHUNK, :], xb)
    for c in range(n_ch):
        a_cur = a_next
        if c + 1 < n_ch:
            a_next = _dot_nt(wd_ref[(c + 1) * PEER_CHUNK:(c + 2) * PEER_CHUNK, :], xb)
        for j in range(PEER_CHUNK // PEER_NKEYS):
            i1 = i1_0 + c * (PEER_CHUNK // PEER_NKEYS) + j
            r0 = c * PEER_CHUNK + j * PEER_NKEYS
            for lg in range(tt // LANES):
                ls = slice(lg * LANES, (lg + 1) * LANES)
                gate = jnp.zeros((PEER_NKEYS, LANES), BF16)
                zero = jnp.zeros((), BF16)
                for hd in range(PEER_HEADS):
                    limb = _row_bf16(limr_ref, i1, hd, ls)
                    e1b = _row_bf16(e1r_ref, i1, hd, ls)
                    gate = gate + jnp.where(r2s_ref[hd, :, ls] < limb, e2s_ref[hd, :, ls] * e1b, zero)
                a = a_cur[j * PEER_NKEYS:(j + 1) * PEER_NKEYS, ls]
                act = 0.5 * a * (1.0 + lax.erf(a * (2.0 ** -0.5)))
                p_ref[r0:r0 + PEER_NKEYS, ls] = act.astype(BF16) * gate
        acc_ref[...] += _dot(wu_ref[c], p_ref[c * PEER_CHUNK:(c + 1) * PEER_CHUNK, :])

    @pl.when(e == pl.num_programs(1) - 1)
    def _():
        y = DN_ALPHA * x_ref[...] + acc_ref[...].T
        o_ref[...] = _layernorm_rows(y, g_ref[...], b_ref[...])


def _peer_dense(xt, w_down, w_up, route, gain, bias, tt, te):
    T = xt.shape[0]
    c1, e1, s2, e2 = route
    wd = w_down.astype(BF16)
    wu = w_up.astype(BF16).reshape(PEER_EXPERTS // PEER_CHUNK, PEER_CHUNK, D_MODEL).transpose(0, 2, 1)
    g = gain.reshape(1, -1)
    b = bias.reshape(1, -1)
    big_spec = pl.BlockSpec((PEER_HEADS, PEER_NKEYS, tt), lambda i, e: (0, 0, i))
    return pl.pallas_call(
        functools.partial(_peer_dense_kernel, te=te, tt=tt),
        grid=(T // tt, PEER_EXPERTS // te),
        in_specs=[pl.BlockSpec((tt, D_MODEL), lambda i, e: (i, 0)),
                  pl.BlockSpec((te, D_MODEL), lambda i, e: (e, 0)),
                  pl.BlockSpec((te // PEER_CHUNK, D_MODEL, PEER_CHUNK), lambda i, e: (e, 0, 0)),
                  big_spec, big_spec, big_spec, big_spec,
                  pl.BlockSpec(g.shape, lambda i, e: (0, 0)),
                  pl.BlockSpec(b.shape, lambda i, e: (0, 0))],
        out_specs=pl.BlockSpec((tt, D_MODEL), lambda i, e: (i, 0)),
        out_shape=jax.ShapeDtypeStruct((T, D_MODEL), F32),
        scratch_shapes=[pltpu.VMEM((tt, D_MODEL), BF16),
                        pltpu.VMEM((te, tt), BF16),
                        pltpu.VMEM((D_MODEL, tt), F32),
                        pltpu.VMEM((PEER_NKEYS, PEER_HEADS, tt), F32),
                        pltpu.VMEM((PEER_NKEYS, PEER_HEADS, tt), F32),
                        pltpu.VMEM((PEER_HEADS, PEER_NKEYS, tt), BF16),
                        pltpu.VMEM((PEER_HEADS, PEER_NKEYS, tt), BF16)],
        compiler_params=_cparams("parallel", "arbitrary"),
        name="peer_dense",
    )(xt, wd, wu, c1, e1, s2, e2, g, b)


def _rope_tables(seq, dim):
    inv = ROPE_THETA ** (-jnp.arange(0, dim, 2, dtype=F32) / dim)
    ang = jnp.arange(seq, dtype=F32)[:, None] * inv[None, :]
    return jnp.cos(ang), jnp.sin(ang)


def _tiles(S, T):
    row = min(256, S)
    tq_mla = min(256, S)
    tq_dsa = min(128, S)
    tt_route = min(256, T)
    tt_dense = min(512, T)
    te_dense = 1024
    return row, tq_mla, tq_dsa, tt_route, tt_dense, te_dense


def _peer_layer(xt, w_q, sub_keys, w_down, w_up, gain, bias, tt_route, tt_dense, te_dense):
    route = _peer_route(xt, w_q, sub_keys, tt_route)
    return _peer_dense(xt, w_down, w_up, route, gain, bias, tt_dense, te_dense)


def kernel(x, mla_w_in, mla_q_norm, mla_kv_norm, mla_w_uq, mla_w_ukv, mla_w_o,
           dsa_w_in, dsa_w_o, peer_w_q, peer_sub_keys, peer_w_down, peer_w_up,
           ln_gain, ln_bias):
    B, S, D = x.shape
    T = B * S
    row, tq_mla, tq_dsa, tt_route, tt_dense, te_dense = _tiles(S, T)
    xt = x.reshape(T, D)

    cos, sin = _rope_tables(S, MLA_ROPE)
    q, k, v = _mla_proj(xt, B, S, mla_w_in[0], mla_q_norm[0], mla_kv_norm[0],
                        mla_w_uq[0], mla_w_ukv[0], cos, sin, row)
    o = _mla_attn(q, k, v, tq_mla).reshape(T, MLA_HEADS * MLA_V)
    xt = _proj_ln(o, mla_w_o[0], xt, ln_gain[0, 0], ln_bias[0, 0], row)
    xt = _peer_layer(xt, peer_w_q[0], peer_sub_keys[0], peer_w_down[0], peer_w_up[0],
                     ln_gain[0, 1], ln_bias[0, 1], tt_route, tt_dense, te_dense)

    q, k, v, qi, ki, wi = _dsa_proj(xt, B, S, dsa_w_in[0], row)
    o = _dsa_attn(q, k, v, qi, ki, wi, tq_dsa).reshape(T, DSA_HD)
    xt = _proj_ln(o, dsa_w_o[0], xt, ln_gain[1, 0], ln_bias[1, 0], row)
    xt = _peer_layer(xt, peer_w_q[1], peer_sub_keys[1], peer_w_down[1], peer_w_up[1],
                     ln_gain[1, 1], ln_bias[1, 1], tt_route, tt_dense, te_dense)
    return xt.reshape(B, S, D)
```

```python
import functools

import jax
import jax.numpy as jnp
from jax import lax
from jax.experimental import pallas as pl
from jax.experimental.pallas import tpu as pltpu

F32 = jnp.float32
BF16 = jnp.bfloat16
I32 = jnp.int32

D_MODEL = 1024
DEPTH = 2
CHUNK = 64
ROPE_THETA = 10000.0
LN_EPS = 1e-5
RMS_EPS = 1e-6
DN_ALPHA = (2 * DEPTH) ** 0.25

MLA_HEADS = 8
MLA_NOPE = 128
MLA_ROPE = 64
MLA_V = 128
MLA_Q_RANK = 384
MLA_KV_RANK = 256
MLA_QK_PAD = 256

DSA_HEADS = 8
DSA_HEAD_DIM = 128
IDX_HEADS = 8
IDX_DIM = 64
DSA_TOPK_MAX = 256
DSA_HD = DSA_HEADS * DSA_HEAD_DIM

PEER_HEADS = 8
PEER_NKEYS = 128
PEER_EXPERTS = PEER_NKEYS * PEER_NKEYS
PEER_QDIM = 256
PEER_TOPK = 16

LANES = 128
SUBLANES = 8
VMEM_LIMIT = 56 * 1024 * 1024

NEG_INF = float("-inf")
INT_MIN = -2 ** 31


def _cparams(*sem):
    return pltpu.CompilerParams(dimension_semantics=sem, vmem_limit_bytes=VMEM_LIMIT)


def _dot(a, b):
    return jnp.dot(a, b, preferred_element_type=F32)


def _dot_nt(a, b):
    return lax.dot_general(a, b, (((1,), (1,)), ((), ())), preferred_element_type=F32)


def _layernorm_rows(y, g, b):
    mu = jnp.mean(y, axis=-1, keepdims=True)
    yc = y - mu
    var = jnp.mean(yc * yc, axis=-1, keepdims=True)
    return yc * lax.rsqrt(var + LN_EPS) * g + b


def _rms_rows(h, g):
    ms = jnp.mean(h * h, axis=-1, keepdims=True)
    return h * lax.rsqrt(ms + RMS_EPS) * g


def _mla_proj_kernel(x_ref, win_ref, qn_ref, kvn_ref, wuq_ref, wukv_ref, cc_ref, ss_ref,
                     q_ref, k_ref, v_ref):
    xb = x_ref[...].astype(BF16)
    h = _dot(xb, win_ref[...])
    cq = h[:, :MLA_Q_RANK]
    ckv = h[:, MLA_Q_RANK:MLA_Q_RANK + MLA_KV_RANK]
    o = MLA_Q_RANK + MLA_KV_RANK
    cc = cc_ref[...]
    ss = ss_ref[...]
    k_rope = h[:, o:o + LANES] * cc + h[:, o + LANES:o + 2 * LANES] * ss
    qall = _dot(_rms_rows(cq, qn_ref[...]).astype(BF16), wuq_ref[...])
    kvall = _dot(_rms_rows(ckv, kvn_ref[...]).astype(BF16), wukv_ref[...])
    k_rope_b = k_rope.astype(BF16)
    for hd in range(MLA_HEADS):
        b0 = hd * 3 * LANES
        q_ref[0, hd, :, 0:LANES] = qall[:, b0:b0 + LANES].astype(BF16)
        q_rope = qall[:, b0 + LANES:b0 + 2 * LANES] * cc + qall[:, b0 + 2 * LANES:b0 + 3 * LANES] * ss
        q_ref[0, hd, :, LANES:2 * LANES] = q_rope.astype(BF16)
        k_ref[0, hd, :, 0:LANES] = kvall[:, hd * LANES:(hd + 1) * LANES].astype(BF16)
        k_ref[0, hd, :, LANES:2 * LANES] = k_rope_b
        v0 = MLA_HEADS * MLA_NOPE + hd * MLA_V
        v_ref[0, hd] = kvall[:, v0:v0 + MLA_V].astype(BF16)


def _mla_proj(xt, B, S, w_in, q_norm, kv_norm, w_uq, w_ukv, cos, sin, tm):
    T = B * S
    n_s = S // tm
    half = MLA_ROPE // 2
    def swap_rope(w):
        return jnp.concatenate([w[..., half:], w[..., :half]], axis=-1)
    zpad = lambda w: jnp.concatenate([w, jnp.zeros_like(w)], axis=-1)
    w_kr = w_in[:, MLA_Q_RANK + MLA_KV_RANK:]
    win_ext = jnp.concatenate(
        [w_in[:, :MLA_Q_RANK + MLA_KV_RANK], zpad(w_kr), zpad(swap_rope(w_kr))], axis=-1).astype(BF16)
    uq_nope = w_uq[:, :, :MLA_NOPE]
    uq_rope = w_uq[:, :, MLA_NOPE:]
    wuq_ext = jnp.concatenate([uq_nope, zpad(uq_rope), zpad(swap_rope(uq_rope))], axis=-1)
    wuq_ext = wuq_ext.reshape(MLA_Q_RANK, MLA_HEADS * 3 * LANES).astype(BF16)
    wukv_ext = jnp.concatenate(
        [w_ukv[:, :, :MLA_NOPE].reshape(MLA_KV_RANK, -1), w_ukv[:, :, MLA_NOPE:].reshape(MLA_KV_RANK, -1)],
        axis=-1).astype(BF16)
    z = jnp.zeros((S, LANES - MLA_ROPE), F32)
    cc = jnp.concatenate([cos, cos, z], axis=-1)
    ss = jnp.concatenate([-sin, sin, z], axis=-1)
    full = lambda a: pl.BlockSpec(a.shape, lambda i: (0,) * a.ndim)
    qn = q_norm.reshape(1, -1)
    kvn = kv_norm.reshape(1, -1)
    head_map = lambda i: (i // n_s, 0, i % n_s, 0)
    return pl.pallas_call(
        _mla_proj_kernel,
        grid=(T // tm,),
        in_specs=[pl.BlockSpec((tm, D_MODEL), lambda i: (i, 0)),
                  full(win_ext), full(qn), full(kvn), full(wuq_ext), full(wukv_ext),
                  pl.BlockSpec((tm, LANES), lambda i: (i % n_s, 0)),
                  pl.BlockSpec((tm, LANES), lambda i: (i % n_s, 0))],
        out_specs=[pl.BlockSpec((1, MLA_HEADS, tm, MLA_QK_PAD), head_map),
                   pl.BlockSpec((1, MLA_HEADS, tm, MLA_QK_PAD), head_map),
                   pl.BlockSpec((1, MLA_HEADS, tm, MLA_V), head_map)],
        out_shape=[jax.ShapeDtypeStruct((B, MLA_HEADS, S, MLA_QK_PAD), BF16),
                   jax.ShapeDtypeStruct((B, MLA_HEADS, S, MLA_QK_PAD), BF16),
                   jax.ShapeDtypeStruct((B, MLA_HEADS, S, MLA_V), BF16)],
        compiler_params=_cparams("parallel"),
        name="mla_proj",
    )(xt, win_ext, qn, kvn, wuq_ext, wukv_ext, cc, ss)


MLA_HEADS_PER_STEP = 4


def _mla_attn_kernel(q_ref, k_ref, v_ref, o_ref, *, tq, scale):
    qi = pl.program_id(2)
    hp = MLA_HEADS_PER_STEP

    def step(j, carry, diagonal):
        start = pl.multiple_of(j * tq, tq)
        out = []
        for h in range(hp):
            m, l, acc = carry[h]
            kb = k_ref[0, h, pl.ds(start, tq), :]
            vb = v_ref[0, h, pl.ds(start, tq), :]
            s = _dot_nt(q_ref[0, h], kb) * scale
            if diagonal:
                row = lax.broadcasted_iota(I32, (tq, tq), 0)
                col = lax.broadcasted_iota(I32, (tq, tq), 1)
                s = jnp.where(col // CHUNK <= row // CHUNK, s, NEG_INF)
            m_new = jnp.maximum(m, jnp.max(s, axis=-1, keepdims=True))
            p = jnp.exp(s - m_new)
            alpha = jnp.exp(m - m_new)
            l = alpha * l + jnp.sum(p, axis=-1, keepdims=True)
            acc = alpha * acc + _dot(p.astype(BF16), vb)
            out.append((m_new, l, acc))
        return tuple(out)

    init = tuple((jnp.full((tq, 1), NEG_INF, F32), jnp.zeros((tq, 1), F32), jnp.zeros((tq, MLA_V), F32))
                 for _ in range(hp))
    carry = lax.fori_loop(0, qi, lambda j, c: step(j, c, False), init)
    carry = step(qi, carry, True)
    for h in range(hp):
        _, l, acc = carry[h]
        o_ref[0, :, h * MLA_V:(h + 1) * MLA_V] = (acc / l).astype(BF16)


def _mla_attn(q, k, v, tq):
    B, H, S, _ = q.shape
    hp = MLA_HEADS_PER_STEP
    scale = (MLA_NOPE + MLA_ROPE) ** -0.5
    return pl.pallas_call(
        functools.partial(_mla_attn_kernel, tq=tq, scale=scale),
        grid=(B, H // hp, S // tq),
        in_specs=[pl.BlockSpec((1, hp, tq, MLA_QK_PAD), lambda b, h, i: (b, h, i, 0)),
                  pl.BlockSpec((1, hp, S, MLA_QK_PAD), lambda b, h, i: (b, h, 0, 0)),
                  pl.BlockSpec((1, hp, S, MLA_V), lambda b, h, i: (b, h, 0, 0))],
        out_specs=pl.BlockSpec((1, tq, hp * MLA_V), lambda b, h, i: (b, i, h)),
        out_shape=jax.ShapeDtypeStruct((B, S, H * MLA_V), BF16),
        compiler_params=_cparams("parallel", "parallel", "arbitrary"),
        name="mla_attn",
    )(q, k, v)


def _proj_ln_kernel(a_ref, w_ref, x_ref, g_ref, b_ref, o_ref):
    m = _dot(a_ref[...], w_ref[...])
    y = DN_ALPHA * x_ref[...] + m
    o_ref[...] = _layernorm_rows(y, g_ref[...], b_ref[...])


def _proj_ln(a, w, xt, gain, bias, tm):
    T = xt.shape[0]
    wb = w.astype(BF16)
    g = gain.reshape(1, -1)
    b = bias.reshape(1, -1)
    return pl.pallas_call(
        _proj_ln_kernel,
        grid=(T // tm,),
        in_specs=[pl.BlockSpec((tm, a.shape[1]), lambda i: (i, 0)),
                  pl.BlockSpec(wb.shape, lambda i: (0, 0)),
                  pl.BlockSpec((tm, D_MODEL), lambda i: (i, 0)),
                  pl.BlockSpec(g.shape, lambda i: (0, 0)),
                  pl.BlockSpec(b.shape, lambda i: (0, 0))],
        out_specs=pl.BlockSpec((tm, D_MODEL), lambda i: (i, 0)),
        out_shape=jax.ShapeDtypeStruct((T, D_MODEL), F32),
        compiler_params=_cparams("parallel"),
        name="proj_ln",
    )(a, wb, xt, g, b)


def _dsa_proj_kernel(x_ref, w_ref, ch_ref, sh_ref, ci_ref, ck_ref, sk_ref,
                     q_ref, k_ref, v_ref, qi_ref, ki_ref, wi_ref, *, w_scale):
    xb = x_ref[...].astype(BF16)
    h = _dot(xb, w_ref[...])
    ch = ch_ref[...]
    sh = sh_ref[...]
    ci = ci_ref[...]
    half = LANES // 2
    for hd in range(DSA_HEADS):
        c0 = hd * DSA_HEAD_DIM
        qh = h[:, c0:c0 + LANES]
        q_ref[0, :, c0:c0 + LANES] = (qh * ch + pltpu.roll(qh, half, 1) * sh).astype(BF16)
        kh = h[:, DSA_HD + c0:DSA_HD + c0 + LANES]
        k_ref[0, :, c0:c0 + LANES] = (kh * ch + pltpu.roll(kh, half, 1) * sh).astype(BF16)
        t = h[:, 3 * DSA_HD + c0:3 * DSA_HD + c0 + LANES] * ci
        qi_ref[0, :, c0:c0 + LANES] = (t + pltpu.roll(t, half, 1)).astype(BF16)
    v_ref[0] = h[:, 2 * DSA_HD:3 * DSA_HD].astype(BF16)
    o = 4 * DSA_HD
    ki = h[:, o:o + LANES] * ck_ref[...] + h[:, o + LANES:o + 2 * LANES] * sk_ref[...]
    ki_ref[0] = ki.astype(BF16)
    wi_ref[0] = h[:, o + 2 * LANES:o + 3 * LANES] * w_scale


def _dsa_proj(xt, B, S, w_in, tm):
    T = B * S
    n_s = S // tm
    o1 = 3 * DSA_HD + IDX_HEADS * IDX_DIM
    ih = IDX_DIM // 2
    w_qi = w_in[:, 3 * DSA_HD:o1].reshape(D_MODEL, IDX_HEADS, IDX_DIM)
    w_qi_sw = jnp.concatenate([w_qi[..., ih:], w_qi[..., :ih]], axis=-1)
    w_qi_ext = jnp.concatenate([w_qi, w_qi_sw], axis=-1).reshape(D_MODEL, IDX_HEADS * LANES)
    w_ki = w_in[:, o1:o1 + IDX_DIM]
    w_ki_sw = jnp.concatenate([w_ki[:, ih:], w_ki[:, :ih]], axis=-1)
    zpad = lambda w: jnp.concatenate([w, jnp.zeros((w.shape[0], LANES - w.shape[1]), w.dtype)], axis=-1)
    w_ext = jnp.concatenate(
        [w_in[:, :3 * DSA_HD], w_qi_ext, zpad(w_ki), zpad(w_ki_sw), zpad(w_in[:, o1 + IDX_DIM:])],
        axis=-1).astype(BF16)
    cos_h, sin_h = _rope_tables(S, DSA_HEAD_DIM)
    cos_i, sin_i = _rope_tables(S, IDX_DIM)
    ch = jnp.concatenate([cos_h, cos_h], axis=-1)
    sh = jnp.concatenate([-sin_h, sin_h], axis=-1)
    ci = jnp.concatenate([cos_i, cos_i, -sin_i, sin_i], axis=-1)
    z = jnp.zeros((S, LANES - IDX_DIM), F32)
    ck = jnp.concatenate([cos_i, cos_i, z], axis=-1)
    sk = jnp.concatenate([-sin_i, sin_i, z], axis=-1)
    tab = pl.BlockSpec((tm, LANES), lambda i: (i % n_s, 0))
    row_map = lambda i: (i // n_s, i % n_s, 0)
    wide = pl.BlockSpec((1, tm, DSA_HD), row_map)
    narrow = pl.BlockSpec((1, tm, LANES), row_map)
    w_scale = IDX_HEADS ** -0.5 * IDX_DIM ** -0.5
    return pl.pallas_call(
        functools.partial(_dsa_proj_kernel, w_scale=w_scale),
        grid=(T // tm,),
        in_specs=[pl.BlockSpec((tm, D_MODEL), lambda i: (i, 0)),
                  pl.BlockSpec(w_ext.shape, lambda i: (0, 0)),
                  tab, tab, tab, tab, tab],
        out_specs=[wide, wide, wide, wide, narrow, narrow],
        out_shape=[jax.ShapeDtypeStruct((B, S, DSA_HD), BF16),
                   jax.ShapeDtypeStruct((B, S, DSA_HD), BF16),
                   jax.ShapeDtypeStruct((B, S, DSA_HD), BF16),
                   jax.ShapeDtypeStruct((B, S, IDX_HEADS * LANES), BF16),
                   jax.ShapeDtypeStruct((B, S, LANES), BF16),
                   jax.ShapeDtypeStruct((B, S, LANES), F32)],
        compiler_params=_cparams("parallel"),
        name="dsa_proj",
    )(xt, w_ext, ch, sh, ci, ck, sk)


def _dsa_attn_kernel(q_ref, qi_ref, wi_ref, k_ref, v_ref, ki_ref, o_ref, *, tq, S, topk, scale):
    blk = pl.program_id(1)
    ki = ki_ref[0]
    qi = qi_ref[0]
    wi = wi_ref[0]
    score = jnp.zeros((tq, S), F32)
    for hd in range(IDX_HEADS):
        logit = _dot_nt(qi[:, hd * LANES:(hd + 1) * LANES], ki)
        score = score + wi[:, hd:hd + 1] * jnp.maximum(logit, 0.0)
    row = lax.broadcasted_iota(I32, (tq, S), 0)
    col = lax.broadcasted_iota(I32, (tq, S), 1)
    adm = col // CHUNK <= (blk * tq + row) // CHUNK
    score = jnp.where(adm, score + 0.0, NEG_INF)
    bits = lax.bitcast_convert_type(score, I32)
    u = jnp.where(bits < 0, bits ^ 0x7FFFFFFF, bits)

    def search(i, thr):
        cand = thr + lax.shift_left(jnp.int32(1), 31 - i)
        cnt = jnp.sum((u >= cand).astype(F32), axis=-1, keepdims=True)
        return jnp.where(cnt >= topk, cand, thr)
    thr = lax.fori_loop(0, 32, search, jnp.full((tq, 1), INT_MIN, I32))

    gt = u > thr
    eq = u == thr
    need = topk - jnp.sum(gt.astype(F32), axis=-1, keepdims=True)
    r_i = lax.broadcasted_iota(I32, (LANES, LANES), 0)
    c_i = lax.broadcasted_iota(I32, (LANES, LANES), 1)
    tri = (r_i < c_i).astype(BF16)
    eq_b = eq.astype(BF16)
    run = jnp.zeros((tq, 1), F32)
    ranks = []
    for j in range(S // LANES):
        e = eq_b[:, j * LANES:(j + 1) * LANES]
        ranks.append(_dot(e, tri) + run)
        run = run + jnp.sum(e.astype(F32), axis=-1, keepdims=True)
    rank = jnp.concatenate(ranks, axis=-1)
    sel = adm & (gt | (eq & (rank < need)))

    for hd in range(DSA_HEADS):
        c0 = hd * DSA_HEAD_DIM
        s = _dot_nt(q_ref[0, :, c0:c0 + LANES], k_ref[0, :, c0:c0 + LANES]) * scale
        s = jnp.where(sel, s, NEG_INF)
        m = jnp.max(s, axis=-1, keepdims=True)
        p = jnp.exp(s - m)
        l = jnp.sum(p, axis=-1, keepdims=True)
        o = _dot(p.astype(BF16), v_ref[0, :, c0:c0 + LANES])
        o_ref[0, :, c0:c0 + LANES] = (o / l).astype(BF16)


def _dsa_attn(q, k, v, qi, ki, wi, tq):
    B, S, _ = q.shape
    topk = min(DSA_TOPK_MAX, S // 4)
    scale = DSA_HEAD_DIM ** -0.5
    qmap = lambda b, i: (b, i, 0)
    kmap = lambda b, i: (b, 0, 0)
    return pl.pallas_call(
        functools.partial(_dsa_attn_kernel, tq=tq, S=S, topk=topk, scale=scale),
        grid=(B, S // tq),
        in_specs=[pl.BlockSpec((1, tq, DSA_HD), qmap),
                  pl.BlockSpec((1, tq, IDX_HEADS * LANES), qmap),
                  pl.BlockSpec((1, tq, LANES), qmap),
                  pl.BlockSpec((1, S, DSA_HD), kmap),
                  pl.BlockSpec((1, S, DSA_HD), kmap),
                  pl.BlockSpec((1, S, LANES), kmap)],
        out_specs=pl.BlockSpec((1, tq, DSA_HD), qmap),
        out_shape=jax.ShapeDtypeStruct((B, S, DSA_HD), BF16),
        compiler_params=_cparams("parallel", "arbitrary"),
        name="dsa_attn",
    )(q, qi, wi, k, v, ki)


_PEER_NB = [PEER_TOPK // (a + 1) for a in range(PEER_TOPK)]


def _top_rows(s, n):
    vals = []
    cur = s
    for _ in range(n):
        m = jnp.max(cur, axis=0, keepdims=True)
        vals.append(m)
        cur = jnp.where(cur == m, NEG_INF, cur)
    return vals


def _peer_route_kernel(x_ref, wq_ref, keys_ref, lim_ref, e1_ref, r2_ref, e2_ref, v2_ref):
    tt = x_ref.shape[0]
    q = _dot(x_ref[...].astype(BF16), wq_ref[...]).astype(BF16)
    half = PEER_QDIM // 2
    for hd in range(PEER_HEADS):
        c0 = hd * PEER_QDIM
        s1 = _dot_nt(keys_ref[0, hd], q[:, c0:c0 + half])
        s2 = _dot_nt(keys_ref[1, hd], q[:, c0 + half:c0 + PEER_QDIM])
        v1 = _top_rows(s1, PEER_TOPK)
        v2 = _top_rows(s2, PEER_TOPK)
        for a in range(PEER_TOPK):
            v2_ref[a:a + 1, :] = v2[a]
        pieces = []
        for a in range(PEER_TOPK):
            nb = _PEER_NB[a]
            rows = -(-nb // SUBLANES) * SUBLANES
            blk = v2_ref[0:rows, :] + v1[a]
            ridx = lax.broadcasted_iota(I32, (rows, tt), 0)
            pieces.append(jnp.where(ridx < nb, blk, NEG_INF))
        cand = jnp.concatenate(pieces, axis=0)
        cur = cand
        theta = None
        for _ in range(PEER_TOPK):
            theta = jnp.max(cur, axis=0, keepdims=True)
            cur = jnp.where(cur == theta, NEG_INF, cur)
        top = v1[0] + v2[0]
        z = jnp.sum(jnp.where(cand >= theta, jnp.exp(cand - top), 0.0), axis=0, keepdims=True)
        r2 = jnp.full(s2.shape, float(PEER_TOPK), F32)
        for b in reversed(range(PEER_TOPK)):
            r2 = jnp.where(s2 >= v2[b], float(b), r2)
        lim = jnp.zeros(s1.shape, F32)
        for b in range(PEER_TOPK):
            lim = jnp.where(s1 + v2[b] >= theta, float(b + 1), lim)
        lim_ref[hd] = lim
        r2_ref[hd] = r2.astype(BF16)
        e1_ref[hd] = jnp.exp(s1 - v1[0]) / z
        e2_ref[hd] = jnp.exp(s2 - v2[0]).astype(BF16)


def _peer_route(xt, w_q, sub_keys, tt):
    T = xt.shape[0]
    wq = w_q.astype(BF16)
    keys = sub_keys.astype(BF16)
    big = jax.ShapeDtypeStruct((PEER_HEADS, PEER_NKEYS, T), F32)
    big_b = jax.ShapeDtypeStruct((PEER_HEADS, PEER_NKEYS, T), BF16)
    big_spec = pl.BlockSpec((PEER_HEADS, PEER_NKEYS, tt), lambda i: (0, 0, i))
    return pl.pallas_call(
        _peer_route_kernel,
        grid=(T // tt,),
        in_specs=[pl.BlockSpec((tt, D_MODEL), lambda i: (i, 0)),
                  pl.BlockSpec(wq.shape, lambda i: (0, 0)),
                  pl.BlockSpec(keys.shape, lambda i: (0, 0, 0, 0))],
        out_specs=[big_spec, big_spec, big_spec, big_spec],
        out_shape=[big, big, big_b, big_b],
        scratch_shapes=[pltpu.VMEM((PEER_TOPK, tt), F32)],
        compiler_params=_cparams("parallel"),
        name="peer_route",
    )(xt, wq, keys)


PEER_CHUNK = 2 * PEER_NKEYS


BF16_SUBLANES = 2 * SUBLANES


def _row_bf16(ref, i1, hd, ls):
    row = jnp.broadcast_to(ref[i1, hd:hd + 1, ls], (BF16_SUBLANES, LANES)).astype(BF16)
    return jnp.tile(row, (PEER_NKEYS // BF16_SUBLANES, 1))


def _peer_dense_kernel(x_ref, wd_ref, wu_ref, lim_ref, e1_ref, r2_ref, e2_ref, g_ref, b_ref,
                       o_ref, xb_ref, p_ref, acc_ref, limr_ref, e1r_ref, r2s_ref, e2s_ref, *, te, tt):
    e = pl.program_id(1)

    @pl.when(e == 0)
    def _():
        xb_ref[...] = x_ref[...].astype(BF16)
        acc_ref[...] = jnp.zeros_like(acc_ref)
        r2s_ref[...] = r2_ref[...]
        e2s_ref[...] = e2_ref[...]

        def relayout(g, carry):
            g0 = pl.multiple_of(g * SUBLANES, SUBLANES)
            for hd in range(PEER_HEADS):
                limg = lim_ref[hd, pl.ds(g0, SUBLANES), :]
                e1g = e1_ref[hd, pl.ds(g0, SUBLANES), :]
                for r in range(SUBLANES):
                    limr_ref[g0 + r, hd:hd + 1, :] = limg[r:r + 1, :]
                    e1r_ref[g0 + r, hd:hd + 1, :] = e1g[r:r + 1, :]
            return carry
        lax.fori_loop(0, PEER_NKEYS // SUBLANES, relayout, 0)

    xb = xb_ref[...]
    n_ch = te // PEER_CHUNK
    i1_0 = e * (te // PEER_NKEYS)
    a_next = _dot_nt(wd_ref[0:PEER_CHUNK, :], xb)
    for c in range(n_ch):
        a_cur = a_next
        if c + 1 < n_ch:
            a_next = _dot_nt(wd_ref[(c + 1) * PEER_CHUNK:(c + 2) * PEER_CHUNK, :], xb)
        for j in range(PEER_CHUNK // PEER_NKEYS):
            i1 = i1_0 + c * (PEER_CHUNK // PEER_NKEYS) + j
            r0 = c * PEER_CHUNK + j * PEER_NKEYS
            for lg in range(tt // LANES):
                ls = slice(lg * LANES, (lg + 1) * LANES)
                gate = jnp.zeros((PEER_NKEYS, LANES), BF16)
                zero = jnp.zeros((), BF16)
                for hd in range(PEER_HEADS):
                    limb = _row_bf16(limr_ref, i1, hd, ls)
                    e1b = _row_bf16(e1r_ref, i1, hd, ls)
                    gate = gate + jnp.where(r2s_ref[hd, :, ls] < limb, e2s_ref[hd, :, ls] * e1b, zero)
                a = a_cur[j * PEER_NKEYS:(j + 1) * PEER_NKEYS, ls]
                act = 0.5 * a * (1.0 + lax.erf(a * (2.0 ** -0.5)))
                p_ref[r0:r0 + PEER_NKEYS, ls] = act.astype(BF16) * gate
        acc_ref[...] += _dot(wu_ref[c], p_ref[c * PEER_CHUNK:(c + 1) * PEER_CHUNK, :])

    @pl.when(e == pl.num_programs(1) - 1)
    def _():
        y = DN_ALPHA * x_ref[...] + acc_ref[...].T
        o_ref[...] = _layernorm_rows(y, g_ref[...], b_ref[...])


def _peer_dense(xt, w_down, w_up, route, gain, bias, tt, te):
    T = xt.shape[0]
    c1, e1, s2, e2 = route
    wd = w_down.astype(BF16)
    wu = w_up.astype(BF16).reshape(PEER_EXPERTS // PEER_CHUNK, PEER_CHUNK, D_MODEL).transpose(0, 2, 1)
    g = gain.reshape(1, -1)
    b = bias.reshape(1, -1)
    big_spec = pl.BlockSpec((PEER_HEADS, PEER_NKEYS, tt), lambda i, e: (0, 0, i))
    return pl.pallas_call(
        functools.partial(_peer_dense_kernel, te=te, tt=tt),
        grid=(T // tt, PEER_EXPERTS // te),
        in_specs=[pl.BlockSpec((tt, D_MODEL), lambda i, e: (i, 0)),
                  pl.BlockSpec((te, D_MODEL), lambda i, e: (e, 0)),
                  pl.BlockSpec((te // PEER_CHUNK, D_MODEL, PEER_CHUNK), lambda i, e: (e, 0, 0)),
                  big_spec, big_spec, big_spec, big_spec,
                  pl.BlockSpec(g.shape, lambda i, e: (0, 0)),
                  pl.BlockSpec(b.shape, lambda i, e: (0, 0))],
        out_specs=pl.BlockSpec((tt, D_MODEL), lambda i, e: (i, 0)),
        out_shape=jax.ShapeDtypeStruct((T, D_MODEL), F32),
        scratch_shapes=[pltpu.VMEM((tt, D_MODEL), BF16),
                        pltpu.VMEM((te, tt), BF16),
                        pltpu.VMEM((D_MODEL, tt), F32),
                        pltpu.VMEM((PEER_NKEYS, PEER_HEADS, tt), F32),
                        pltpu.VMEM((PEER_NKEYS, PEER_HEADS, tt), F32),
                        pltpu.VMEM((PEER_HEADS, PEER_NKEYS, tt), BF16),
                        pltpu.VMEM((PEER_HEADS, PEER_NKEYS, tt), BF16)],
        compiler_params=_cparams("parallel", "arbitrary"),
        name="peer_dense",
    )(xt, wd, wu, c1, e1, s2, e2, g, b)


def _rope_tables(seq, dim):
    inv = ROPE_THETA ** (-jnp.arange(0, dim, 2, dtype=F32) / dim)
    ang = jnp.arange(seq, dtype=F32)[:, None] * inv[None, :]
    return jnp.cos(ang), jnp.sin(ang)


def _tiles(S, T):
    row = min(256, S)
    tq_mla = min(256, S)
    tq_dsa = min(128, S)
    tt_route = min(256, T)
    tt_dense = min(512, T)
    te_dense = 2048
    return row, tq_mla, tq_dsa, tt_route, tt_dense, te_dense


def _peer_layer(xt, w_q, sub_keys, w_down, w_up, gain, bias, tt_route, tt_dense, te_dense):
    route = _peer_route(xt, w_q, sub_keys, tt_route)
    return _peer_dense(xt, w_down, w_up, route, gain, bias, tt_dense, te_dense)


def kernel(x, mla_w_in, mla_q_norm, mla_kv_norm, mla_w_uq, mla_w_ukv, mla_w_o,
           dsa_w_in, dsa_w_o, peer_w_q, peer_sub_keys, peer_w_down, peer_w_up,
           ln_gain, ln_bias):
    B, S, D = x.shape
    T = B * S
    row, tq_mla, tq_dsa, tt_route, tt_dense, te_dense = _tiles(S, T)
    xt = x.reshape(T, D)

    cos, sin = _rope_tables(S, MLA_ROPE)
    q, k, v = _mla_proj(xt, B, S, mla_w_in[0], mla_q_norm[0], mla_kv_norm[0],
                        mla_w_uq[0], mla_w_ukv[0], cos, sin, row)
    o = _mla_attn(q, k, v, tq_mla).reshape(T, MLA_HEADS * MLA_V)
    xt = _proj_ln(o, mla_w_o[0], xt, ln_gain[0, 0], ln_bias[0, 0], row)
    xt = _peer_layer(xt, peer_w_q[0], peer_sub_keys[0], peer_w_down[0], peer_w_up[0],
                     ln_gain[0, 1], ln_bias[0, 1], tt_route, tt_dense, te_dense)

    q, k, v, qi, ki, wi = _dsa_proj(xt, B, S, dsa_w_in[0], row)
    o = _dsa_attn(q, k, v, qi, ki, wi, tq_dsa).reshape(T, DSA_HD)
    xt = _proj_ln(o, dsa_w_o[0], xt, ln_gain[1, 0], ln_bias[1, 0], row)
    xt = _peer_layer(xt, peer_w_q[1], peer_sub_keys[1], peer_w_down[1], peer_w_up[1],
                     ln_gain[1, 1], ln_bias[1, 1], tt_route, tt_dense, te_dense)
    return xt.reshape(B, S, D)
```

```python
import functools

import jax
import jax.numpy as jnp
from jax import lax
from jax.experimental import pallas as pl
from jax.experimental.pallas import tpu as pltpu

F32 = jnp.float32
BF16 = jnp.bfloat16
I32 = jnp.int32

D_MODEL = 1024
DEPTH = 2
CHUNK = 64
ROPE_THETA = 10000.0
LN_EPS = 1e-5
RMS_EPS = 1e-6
DN_ALPHA = (2 * DEPTH) ** 0.25

MLA_HEADS = 8
MLA_NOPE = 128
MLA_ROPE = 64
MLA_V = 128
MLA_Q_RANK = 384
MLA_KV_RANK = 256
MLA_QK_PAD = 256

DSA_HEADS = 8
DSA_HEAD_DIM = 128
IDX_HEADS = 8
IDX_DIM = 64
DSA_TOPK_MAX = 256
DSA_HD = DSA_HEADS * DSA_HEAD_DIM

PEER_HEADS = 8
PEER_NKEYS = 128
PEER_EXPERTS = PEER_NKEYS * PEER_NKEYS
PEER_QDIM = 256
PEER_TOPK = 16

LANES = 128
SUBLANES = 8
VMEM_LIMIT = 56 * 1024 * 1024

NEG_INF = float("-inf")
INT_MIN = -2 ** 31


def _cparams(*sem):
    return pltpu.CompilerParams(dimension_semantics=sem, vmem_limit_bytes=VMEM_LIMIT)


def _dot(a, b):
    return jnp.dot(a, b, preferred_element_type=F32)


def _dot_nt(a, b):
    return lax.dot_general(a, b, (((1,), (1,)), ((), ())), preferred_element_type=F32)


def _layernorm_rows(y, g, b):
    mu = jnp.mean(y, axis=-1, keepdims=True)
    yc = y - mu
    var = jnp.mean(yc * yc, axis=-1, keepdims=True)
    return yc * lax.rsqrt(var + LN_EPS) * g + b


def _rms_rows(h, g):
    ms = jnp.mean(h * h, axis=-1, keepdims=True)
    return h * lax.rsqrt(ms + RMS_EPS) * g


def _mla_proj_kernel(x_ref, win_ref, qn_ref, kvn_ref, wuq_ref, wukv_ref, cc_ref, ss_ref,
                     q_ref, k_ref, v_ref):
    xb = x_ref[...].astype(BF16)
    h = _dot(xb, win_ref[...])
    cq = h[:, :MLA_Q_RANK]
    ckv = h[:, MLA_Q_RANK:MLA_Q_RANK + MLA_KV_RANK]
    o = MLA_Q_RANK + MLA_KV_RANK
    cc = cc_ref[...]
    ss = ss_ref[...]
    k_rope = h[:, o:o + LANES] * cc + h[:, o + LANES:o + 2 * LANES] * ss
    qall = _dot(_rms_rows(cq, qn_ref[...]).astype(BF16), wuq_ref[...])
    kvall = _dot(_rms_rows(ckv, kvn_ref[...]).astype(BF16), wukv_ref[...])
    k_rope_b = k_rope.astype(BF16)
    for hd in range(MLA_HEADS):
        b0 = hd * 3 * LANES
        q_ref[0, hd, :, 0:LANES] = qall[:, b0:b0 + LANES].astype(BF16)
        q_rope = qall[:, b0 + LANES:b0 + 2 * LANES] * cc + qall[:, b0 + 2 * LANES:b0 + 3 * LANES] * ss
        q_ref[0, hd, :, LANES:2 * LANES] = q_rope.astype(BF16)
        k_ref[0, hd, :, 0:LANES] = kvall[:, hd * LANES:(hd + 1) * LANES].astype(BF16)
        k_ref[0, hd, :, LANES:2 * LANES] = k_rope_b
        v0 = MLA_HEADS * MLA_NOPE + hd * MLA_V
        v_ref[0, hd] = kvall[:, v0:v0 + MLA_V].astype(BF16)


def _mla_proj(xt, B, S, w_in, q_norm, kv_norm, w_uq, w_ukv, cos, sin, tm):
    T = B * S
    n_s = S // tm
    half = MLA_ROPE // 2
    def swap_rope(w):
        return jnp.concatenate([w[..., half:], w[..., :half]], axis=-1)
    zpad = lambda w: jnp.concatenate([w, jnp.zeros_like(w)], axis=-1)
    w_kr = w_in[:, MLA_Q_RANK + MLA_KV_RANK:]
    win_ext = jnp.concatenate(
        [w_in[:, :MLA_Q_RANK + MLA_KV_RANK], zpad(w_kr), zpad(swap_rope(w_kr))], axis=-1).astype(BF16)
    uq_nope = w_uq[:, :, :MLA_NOPE]
    uq_rope = w_uq[:, :, MLA_NOPE:]
    wuq_ext = jnp.concatenate([uq_nope, zpad(uq_rope), zpad(swap_rope(uq_rope))], axis=-1)
    wuq_ext = wuq_ext.reshape(MLA_Q_RANK, MLA_HEADS * 3 * LANES).astype(BF16)
    wukv_ext = jnp.concatenate(
        [w_ukv[:, :, :MLA_NOPE].reshape(MLA_KV_RANK, -1), w_ukv[:, :, MLA_NOPE:].reshape(MLA_KV_RANK, -1)],
        axis=-1).astype(BF16)
    z = jnp.zeros((S, LANES - MLA_ROPE), F32)
    cc = jnp.concatenate([cos, cos, z], axis=-1)
    ss = jnp.concatenate([-sin, sin, z], axis=-1)
    full = lambda a: pl.BlockSpec(a.shape, lambda i: (0,) * a.ndim)
    qn = q_norm.reshape(1, -1)
    kvn = kv_norm.reshape(1, -1)
    head_map = lambda i: (i // n_s, 0, i % n_s, 0)
    return pl.pallas_call(
        _mla_proj_kernel,
        grid=(T // tm,),
        in_specs=[pl.BlockSpec((tm, D_MODEL), lambda i: (i, 0)),
                  full(win_ext), full(qn), full(kvn), full(wuq_ext), full(wukv_ext),
                  pl.BlockSpec((tm, LANES), lambda i: (i % n_s, 0)),
                  pl.BlockSpec((tm, LANES), lambda i: (i % n_s, 0))],
        out_specs=[pl.BlockSpec((1, MLA_HEADS, tm, MLA_QK_PAD), head_map),
                   pl.BlockSpec((1, MLA_HEADS, tm, MLA_QK_PAD), head_map),
                   pl.BlockSpec((1, MLA_HEADS, tm, MLA_V), head_map)],
        out_shape=[jax.ShapeDtypeStruct((B, MLA_HEADS, S, MLA_QK_PAD), BF16),
                   jax.ShapeDtypeStruct((B, MLA_HEADS, S, MLA_QK_PAD), BF16),
                   jax.ShapeDtypeStruct((B, MLA_HEADS, S, MLA_V), BF16)],
        compiler_params=_cparams("parallel"),
        name="mla_proj",
    )(xt, win_ext, qn, kvn, wuq_ext, wukv_ext, cc, ss)


MLA_HEADS_PER_STEP = 4


def _mla_attn_kernel(q_ref, k_ref, v_ref, o_ref, *, tq, scale):
    qi = pl.program_id(2)
    hp = MLA_HEADS_PER_STEP

    def step(j, carry, diagonal):
        start = pl.multiple_of(j * tq, tq)
        out = []
        for h in range(hp):
            m, l, acc = carry[h]
            kb = k_ref[0, h, pl.ds(start, tq), :]
            vb = v_ref[0, h, pl.ds(start, tq), :]
            s = _dot_nt(q_ref[0, h], kb) * scale
            if diagonal:
                row = lax.broadcasted_iota(I32, (tq, tq), 0)
                col = lax.broadcasted_iota(I32, (tq, tq), 1)
                s = jnp.where(col // CHUNK <= row // CHUNK, s, NEG_INF)
            m_new = jnp.maximum(m, jnp.max(s, axis=-1, keepdims=True))
            p = jnp.exp(s - m_new)
            alpha = jnp.exp(m - m_new)
            l = alpha * l + jnp.sum(p, axis=-1, keepdims=True)
            acc = alpha * acc + _dot(p.astype(BF16), vb)
            out.append((m_new, l, acc))
        return tuple(out)

    init = tuple((jnp.full((tq, 1), NEG_INF, F32), jnp.zeros((tq, 1), F32), jnp.zeros((tq, MLA_V), F32))
                 for _ in range(hp))
    carry = lax.fori_loop(0, qi, lambda j, c: step(j, c, False), init)
    carry = step(qi, carry, True)
    for h in range(hp):
        _, l, acc = carry[h]
        o_ref[0, :, h * MLA_V:(h + 1) * MLA_V] = (acc / l).astype(BF16)


def _mla_attn(q, k, v, tq):
    B, H, S, _ = q.shape
    hp = MLA_HEADS_PER_STEP
    scale = (MLA_NOPE + MLA_ROPE) ** -0.5
    return pl.pallas_call(
        functools.partial(_mla_attn_kernel, tq=tq, scale=scale),
        grid=(B, H // hp, S // tq),
        in_specs=[pl.BlockSpec((1, hp, tq, MLA_QK_PAD), lambda b, h, i: (b, h, i, 0)),
                  pl.BlockSpec((1, hp, S, MLA_QK_PAD), lambda b, h, i: (b, h, 0, 0)),
                  pl.BlockSpec((1, hp, S, MLA_V), lambda b, h, i: (b, h, 0, 0))],
        out_specs=pl.BlockSpec((1, tq, hp * MLA_V), lambda b, h, i: (b, i, h)),
        out_shape=jax.ShapeDtypeStruct((B, S, H * MLA_V), BF16),
        compiler_params=_cparams("parallel", "parallel", "arbitrary"),
        name="mla_attn",
    )(q, k, v)


def _proj_ln_kernel(a_ref, w_ref, x_ref, g_ref, b_ref, o_ref):
    m = _dot(a_ref[...], w_ref[...])
    y = DN_ALPHA * x_ref[...] + m
    o_ref[...] = _layernorm_rows(y, g_ref[...], b_ref[...])


def _proj_ln(a, w, xt, gain, bias, tm):
    T = xt.shape[0]
    wb = w.astype(BF16)
    g = gain.reshape(1, -1)
    b = bias.reshape(1, -1)
    return pl.pallas_call(
        _proj_ln_kernel,
        grid=(T // tm,),
        in_specs=[pl.BlockSpec((tm, a.shape[1]), lambda i: (i, 0)),
                  pl.BlockSpec(wb.shape, lambda i: (0, 0)),
                  pl.BlockSpec((tm, D_MODEL), lambda i: (i, 0)),
                  pl.BlockSpec(g.shape, lambda i: (0, 0)),
                  pl.BlockSpec(b.shape, lambda i: (0, 0))],
        out_specs=pl.BlockSpec((tm, D_MODEL), lambda i: (i, 0)),
        out_shape=jax.ShapeDtypeStruct((T, D_MODEL), F32),
        compiler_params=_cparams("parallel"),
        name="proj_ln",
    )(a, wb, xt, g, b)


def _dsa_proj_kernel(x_ref, w_ref, ch_ref, sh_ref, ci_ref, ck_ref, sk_ref,
                     q_ref, k_ref, v_ref, qi_ref, ki_ref, wi_ref, *, w_scale):
    xb = x_ref[...].astype(BF16)
    h = _dot(xb, w_ref[...])
    ch = ch_ref[...]
    sh = sh_ref[...]
    ci = ci_ref[...]
    half = LANES // 2
    for hd in range(DSA_HEADS):
        c0 = hd * DSA_HEAD_DIM
        qh = h[:, c0:c0 + LANES]
        q_ref[0, :, c0:c0 + LANES] = (qh * ch + pltpu.roll(qh, half, 1) * sh).astype(BF16)
        kh = h[:, DSA_HD + c0:DSA_HD + c0 + LANES]
        k_ref[0, :, c0:c0 + LANES] = (kh * ch + pltpu.roll(kh, half, 1) * sh).astype(BF16)
        t = h[:, 3 * DSA_HD + c0:3 * DSA_HD + c0 + LANES] * ci
        qi_ref[0, :, c0:c0 + LANES] = (t + pltpu.roll(t, half, 1)).astype(BF16)
    v_ref[0] = h[:, 2 * DSA_HD:3 * DSA_HD].astype(BF16)
    o = 4 * DSA_HD
    ki = h[:, o:o + LANES] * ck_ref[...] + h[:, o + LANES:o + 2 * LANES] * sk_ref[...]
    ki_ref[0] = ki.astype(BF16)
    wi_ref[0] = h[:, o + 2 * LANES:o + 3 * LANES] * w_scale


def _dsa_proj(xt, B, S, w_in, tm):
    T = B * S
    n_s = S // tm
    o1 = 3 * DSA_HD + IDX_HEADS * IDX_DIM
    ih = IDX_DIM // 2
    w_qi = w_in[:, 3 * DSA_HD:o1].reshape(D_MODEL, IDX_HEADS, IDX_DIM)
    w_qi_sw = jnp.concatenate([w_qi[..., ih:], w_qi[..., :ih]], axis=-1)
    w_qi_ext = jnp.concatenate([w_qi, w_qi_sw], axis=-1).reshape(D_MODEL, IDX_HEADS * LANES)
    w_ki = w_in[:, o1:o1 + IDX_DIM]
    w_ki_sw = jnp.concatenate([w_ki[:, ih:], w_ki[:, :ih]], axis=-1)
    zpad = lambda w: jnp.concatenate([w, jnp.zeros((w.shape[0], LANES - w.shape[1]), w.dtype)], axis=-1)
    w_ext = jnp.concatenate(
        [w_in[:, :3 * DSA_HD], w_qi_ext, zpad(w_ki), zpad(w_ki_sw), zpad(w_in[:, o1 + IDX_DIM:])],
        axis=-1).astype(BF16)
    cos_h, sin_h = _rope_tables(S, DSA_HEAD_DIM)
    cos_i, sin_i = _rope_tables(S, IDX_DIM)
    ch = jnp.concatenate([cos_h, cos_h], axis=-1)
    sh = jnp.concatenate([-sin_h, sin_h], axis=-1)
    ci = jnp.concatenate([cos_i, cos_i, -sin_i, sin_i], axis=-1)
    z = jnp.zeros((S, LANES - IDX_DIM), F32)
    ck = jnp.concatenate([cos_i, cos_i, z], axis=-1)
    sk = jnp.concatenate([-sin_i, sin_i, z], axis=-1)
    tab = pl.BlockSpec((tm, LANES), lambda i: (i % n_s, 0))
    row_map = lambda i: (i // n_s, i % n_s, 0)
    wide = pl.BlockSpec((1, tm, DSA_HD), row_map)
    narrow = pl.BlockSpec((1, tm, LANES), row_map)
    w_scale = IDX_HEADS ** -0.5 * IDX_DIM ** -0.5
    return pl.pallas_call(
        functools.partial(_dsa_proj_kernel, w_scale=w_scale),
        grid=(T // tm,),
        in_specs=[pl.BlockSpec((tm, D_MODEL), lambda i: (i, 0)),
                  pl.BlockSpec(w_ext.shape, lambda i: (0, 0)),
                  tab, tab, tab, tab, tab],
        out_specs=[wide, wide, wide, wide, narrow, narrow],
        out_shape=[jax.ShapeDtypeStruct((B, S, DSA_HD), BF16),
                   jax.ShapeDtypeStruct((B, S, DSA_HD), BF16),
                   jax.ShapeDtypeStruct((B, S, DSA_HD), BF16),
                   jax.ShapeDtypeStruct((B, S, IDX_HEADS * LANES), BF16),
                   jax.ShapeDtypeStruct((B, S, LANES), BF16),
                   jax.ShapeDtypeStruct((B, S, LANES), F32)],
        compiler_params=_cparams("parallel"),
        name="dsa_proj",
    )(xt, w_ext, ch, sh, ci, ck, sk)


DSA_KEY_CLASSES = 4


def _dsa_attn_block(q_ref, qi_ref, wi_ref, k_ref, v_ref, ki_ref, o_ref, blk, *, tq, L, topk, scale):
    ki = ki_ref[0, 0:L, :]
    qi = qi_ref[0]
    wi = wi_ref[0]
    score = jnp.zeros((tq, L), F32)
    for hd in range(IDX_HEADS):
        logit = _dot_nt(qi[:, hd * LANES:(hd + 1) * LANES], ki)
        score = score + wi[:, hd:hd + 1] * jnp.maximum(logit, 0.0)
    row = lax.broadcasted_iota(I32, (tq, L), 0)
    col = lax.broadcasted_iota(I32, (tq, L), 1)
    adm = col // CHUNK <= (blk * tq + row) // CHUNK
    score = jnp.where(adm, score + 0.0, NEG_INF)
    bits = lax.bitcast_convert_type(score, I32)
    u = jnp.where(bits < 0, bits ^ 0x7FFFFFFF, bits)

    def search(i, thr):
        cand = thr + lax.shift_left(jnp.int32(1), 31 - i)
        cnt = jnp.sum((u >= cand).astype(F32), axis=-1, keepdims=True)
        return jnp.where(cnt >= topk, cand, thr)
    thr = lax.fori_loop(0, 32, search, jnp.full((tq, 1), INT_MIN, I32), unroll=8)

    gt = u > thr
    eq = u == thr
    need = topk - jnp.sum(gt.astype(F32), axis=-1, keepdims=True)
    r_i = lax.broadcasted_iota(I32, (LANES, LANES), 0)
    c_i = lax.broadcasted_iota(I32, (LANES, LANES), 1)
    tri = (r_i < c_i).astype(BF16)
    eq_b = eq.astype(BF16)
    run = jnp.zeros((tq, 1), F32)
    ranks = []
    for j in range(L // LANES):
        e = eq_b[:, j * LANES:(j + 1) * LANES]
        ranks.append(_dot(e, tri) + run)
        run = run + jnp.sum(e.astype(F32), axis=-1, keepdims=True)
    rank = jnp.concatenate(ranks, axis=-1)
    sel = adm & (gt | (eq & (rank < need)))

    for hd in range(DSA_HEADS):
        c0 = hd * DSA_HEAD_DIM
        s = _dot_nt(q_ref[0, :, c0:c0 + LANES], k_ref[0, 0:L, c0:c0 + LANES]) * scale
        s = jnp.where(sel, s, NEG_INF)
        m = jnp.max(s, axis=-1, keepdims=True)
        p = jnp.exp(s - m)
        l = jnp.sum(p, axis=-1, keepdims=True)
        o = _dot(p.astype(BF16), v_ref[0, 0:L, c0:c0 + LANES])
        o_ref[0, :, c0:c0 + LANES] = (o / l).astype(BF16)


def _dsa_attn_kernel(q_ref, qi_ref, wi_ref, k_ref, v_ref, ki_ref, o_ref, *, tq, S, topk, scale):
    blk = pl.program_id(1)
    width = S // DSA_KEY_CLASSES
    cls = ((blk + 1) * tq - 1) // width
    for c in range(DSA_KEY_CLASSES):
        @pl.when(cls == c)
        def _(c=c):
            _dsa_attn_block(q_ref, qi_ref, wi_ref, k_ref, v_ref, ki_ref, o_ref, blk,
                            tq=tq, L=(c + 1) * width, topk=topk, scale=scale)


def _dsa_attn(q, k, v, qi, ki, wi, tq):
    B, S, _ = q.shape
    topk = min(DSA_TOPK_MAX, S // 4)
    scale = DSA_HEAD_DIM ** -0.5
    qmap = lambda b, i: (b, i, 0)
    kmap = lambda b, i: (b, 0, 0)
    return pl.pallas_call(
        functools.partial(_dsa_attn_kernel, tq=tq, S=S, topk=topk, scale=scale),
        grid=(B, S // tq),
        in_specs=[pl.BlockSpec((1, tq, DSA_HD), qmap),
                  pl.BlockSpec((1, tq, IDX_HEADS * LANES), qmap),
                  pl.BlockSpec((1, tq, LANES), qmap),
                  pl.BlockSpec((1, S, DSA_HD), kmap),
                  pl.BlockSpec((1, S, DSA_HD), kmap),
                  pl.BlockSpec((1, S, LANES), kmap)],
        out_specs=pl.BlockSpec((1, tq, DSA_HD), qmap),
        out_shape=jax.ShapeDtypeStruct((B, S, DSA_HD), BF16),
        compiler_params=_cparams("parallel", "arbitrary"),
        name="dsa_attn",
    )(q, qi, wi, k, v, ki)


_PEER_NB = [PEER_TOPK // (a + 1) for a in range(PEER_TOPK)]


def _top_rows(s, n):
    vals = []
    cur = s
    for _ in range(n):
        m = jnp.max(cur, axis=0, keepdims=True)
        vals.append(m)
        cur = jnp.where(cur == m, NEG_INF, cur)
    return vals


def _peer_route_kernel(x_ref, wq_ref, keys_ref, lim_ref, e1_ref, r2_ref, e2_ref, v2_ref):
    tt = x_ref.shape[0]
    q = _dot(x_ref[...].astype(BF16), wq_ref[...]).astype(BF16)
    half = PEER_QDIM // 2
    for hd in range(PEER_HEADS):
        c0 = hd * PEER_QDIM
        s1 = _dot_nt(keys_ref[0, hd], q[:, c0:c0 + half])
        s2 = _dot_nt(keys_ref[1, hd], q[:, c0 + half:c0 + PEER_QDIM])
        v1 = _top_rows(s1, PEER_TOPK)
        v2 = _top_rows(s2, PEER_TOPK)
        for a in range(PEER_TOPK):
            v2_ref[a:a + 1, :] = v2[a]
        pieces = []
        for a in range(PEER_TOPK):
            nb = _PEER_NB[a]
            rows = -(-nb // SUBLANES) * SUBLANES
            blk = v2_ref[0:rows, :] + v1[a]
            ridx = lax.broadcasted_iota(I32, (rows, tt), 0)
            pieces.append(jnp.where(ridx < nb, blk, NEG_INF))
        cand = jnp.concatenate(pieces, axis=0)
        cur = cand
        theta = None
        for _ in range(PEER_TOPK):
            theta = jnp.max(cur, axis=0, keepdims=True)
            cur = jnp.where(cur == theta, NEG_INF, cur)
        top = v1[0] + v2[0]
        z = jnp.sum(jnp.where(cand >= theta, jnp.exp(cand - top), 0.0), axis=0, keepdims=True)
        r2 = jnp.full(s2.shape, float(PEER_TOPK), F32)
        for b in reversed(range(PEER_TOPK)):
            r2 = jnp.where(s2 >= v2[b], float(b), r2)
        lim = jnp.zeros(s1.shape, F32)
        for b in range(PEER_TOPK):
            lim = jnp.where(s1 + v2[b] >= theta, float(b + 1), lim)
        lim_ref[hd] = lim
        r2_ref[hd] = r2.astype(BF16)
        e1_ref[hd] = jnp.exp(s1 - v1[0]) / z
        e2_ref[hd] = jnp.exp(s2 - v2[0]).astype(BF16)


def _peer_route(xt, w_q, sub_keys, tt):
    T = xt.shape[0]
    wq = w_q.astype(BF16)
    keys = sub_keys.astype(BF16)
    big = jax.ShapeDtypeStruct((PEER_HEADS, PEER_NKEYS, T), F32)
    big_b = jax.ShapeDtypeStruct((PEER_HEADS, PEER_NKEYS, T), BF16)
    big_spec = pl.BlockSpec((PEER_HEADS, PEER_NKEYS, tt), lambda i: (0, 0, i))
    return pl.pallas_call(
        _peer_route_kernel,
        grid=(T // tt,),
        in_specs=[pl.BlockSpec((tt, D_MODEL), lambda i: (i, 0)),
                  pl.BlockSpec(wq.shape, lambda i: (0, 0)),
                  pl.BlockSpec(keys.shape, lambda i: (0, 0, 0, 0))],
        out_specs=[big_spec, big_spec, big_spec, big_spec],
        out_shape=[big, big, big_b, big_b],
        scratch_shapes=[pltpu.VMEM((PEER_TOPK, tt), F32)],
        compiler_params=_cparams("parallel"),
        name="peer_route",
    )(xt, wq, keys)


PEER_CHUNK = 2 * PEER_NKEYS


BF16_SUBLANES = 2 * SUBLANES


def _row_bf16(ref, i1, hd, ls):
    row = jnp.broadcast_to(ref[i1, hd:hd + 1, ls], (BF16_SUBLANES, LANES)).astype(BF16)
    return jnp.tile(row, (PEER_NKEYS // BF16_SUBLANES, 1))


def _peer_dense_kernel(x_ref, wd_ref, wu_ref, lim_ref, e1_ref, r2_ref, e2_ref, g_ref, b_ref,
                       o_ref, xb_ref, p_ref, acc_ref, limr_ref, e1r_ref, r2s_ref, e2s_ref, *, te, tt):
    e = pl.program_id(1)

    @pl.when(e == 0)
    def _():
        xb_ref[...] = x_ref[...].astype(BF16)
        acc_ref[...] = jnp.zeros_like(acc_ref)
        r2s_ref[...] = r2_ref[...]
        e2s_ref[...] = e2_ref[...]

        def relayout(g, carry):
            g0 = pl.multiple_of(g * SUBLANES, SUBLANES)
            for hd in range(PEER_HEADS):
                limg = lim_ref[hd, pl.ds(g0, SUBLANES), :]
                e1g = e1_ref[hd, pl.ds(g0, SUBLANES), :]
                for r in range(SUBLANES):
                    limr_ref[g0 + r, hd:hd + 1, :] = limg[r:r + 1, :]
                    e1r_ref[g0 + r, hd:hd + 1, :] = e1g[r:r + 1, :]
            return carry
        lax.fori_loop(0, PEER_NKEYS // SUBLANES, relayout, 0)

    xb = xb_ref[...]
    n_ch = te // PEER_CHUNK
    i1_0 = e * (te // PEER_NKEYS)
    a_next = _dot_nt(wd_ref[0:PEER_CHUNK, :], xb)
    for c in range(n_ch):
        a_cur = a_next
        if c + 1 < n_ch:
            a_next = _dot_nt(wd_ref[(c + 1) * PEER_CHUNK:(c + 2) * PEER_CHUNK, :], xb)
        for j in range(PEER_CHUNK // PEER_NKEYS):
            i1 = i1_0 + c * (PEER_CHUNK // PEER_NKEYS) + j
            r0 = c * PEER_CHUNK + j * PEER_NKEYS
            for lg in range(tt // LANES):
                ls = slice(lg * LANES, (lg + 1) * LANES)
                gate = jnp.zeros((PEER_NKEYS, LANES), BF16)
                zero = jnp.zeros((), BF16)
                for hd in range(PEER_HEADS):
                    limb = _row_bf16(limr_ref, i1, hd, ls)
                    e1b = _row_bf16(e1r_ref, i1, hd, ls)
                    gate = gate + jnp.where(r2s_ref[hd, :, ls] < limb, e2s_ref[hd, :, ls] * e1b, zero)
                a = a_cur[j * PEER_NKEYS:(j + 1) * PEER_NKEYS, ls]
                act = 0.5 * a * (1.0 + lax.erf(a * (2.0 ** -0.5)))
                p_ref[r0:r0 + PEER_NKEYS, ls] = act.astype(BF16) * gate
        acc_ref[...] += _dot(wu_ref[c], p_ref[c * PEER_CHUNK:(c + 1) * PEER_CHUNK, :])

    @pl.when(e == pl.num_programs(1) - 1)
    def _():
        y = DN_ALPHA * x_ref[...] + acc_ref[...].T
        o_ref[...] = _layernorm_rows(y, g_ref[...], b_ref[...])


def _peer_dense(xt, w_down, w_up, route, gain, bias, tt, te):
    T = xt.shape[0]
    c1, e1, s2, e2 = route
    wd = w_down.astype(BF16)
    wu = w_up.astype(BF16).reshape(PEER_EXPERTS // PEER_CHUNK, PEER_CHUNK, D_MODEL).transpose(0, 2, 1)
    g = gain.reshape(1, -1)
    b = bias.reshape(1, -1)
    big_spec = pl.BlockSpec((PEER_HEADS, PEER_NKEYS, tt), lambda i, e: (0, 0, i))
    return pl.pallas_call(
        functools.partial(_peer_dense_kernel, te=te, tt=tt),
        grid=(T // tt, PEER_EXPERTS // te),
        in_specs=[pl.BlockSpec((tt, D_MODEL), lambda i, e: (i, 0)),
                  pl.BlockSpec((te, D_MODEL), lambda i, e: (e, 0)),
                  pl.BlockSpec((te // PEER_CHUNK, D_MODEL, PEER_CHUNK), lambda i, e: (e, 0, 0)),
                  big_spec, big_spec, big_spec, big_spec,
                  pl.BlockSpec(g.shape, lambda i, e: (0, 0)),
                  pl.BlockSpec(b.shape, lambda i, e: (0, 0))],
        out_specs=pl.BlockSpec((tt, D_MODEL), lambda i, e: (i, 0)),
        out_shape=jax.ShapeDtypeStruct((T, D_MODEL), F32),
        scratch_shapes=[pltpu.VMEM((tt, D_MODEL), BF16),
                        pltpu.VMEM((te, tt), BF16),
                        pltpu.VMEM((D_MODEL, tt), F32),
                        pltpu.VMEM((PEER_NKEYS, PEER_HEADS, tt), F32),
                        pltpu.VMEM((PEER_NKEYS, PEER_HEADS, tt), F32),
                        pltpu.VMEM((PEER_HEADS, PEER_NKEYS, tt), BF16),
                        pltpu.VMEM((PEER_HEADS, PEER_NKEYS, tt), BF16)],
        compiler_params=_cparams("parallel", "arbitrary"),
        name="peer_dense",
    )(xt, wd, wu, c1, e1, s2, e2, g, b)


def _rope_tables(seq, dim):
    inv = ROPE_THETA ** (-jnp.arange(0, dim, 2, dtype=F32) / dim)
    ang = jnp.arange(seq, dtype=F32)[:, None] * inv[None, :]
    return jnp.cos(ang), jnp.sin(ang)


def _tiles(S, T):
    row = min(256, S)
    tq_mla = min(256, S)
    tq_dsa = min(256, S)
    tt_route = min(256, T)
    tt_dense = min(512, T)
    te_dense = 2048
    return row, tq_mla, tq_dsa, tt_route, tt_dense, te_dense


def _peer_layer(xt, w_q, sub_keys, w_down, w_up, gain, bias, tt_route, tt_dense, te_dense):
    route = _peer_route(xt, w_q, sub_keys, tt_route)
    return _peer_dense(xt, w_down, w_up, route, gain, bias, tt_dense, te_dense)


def kernel(x, mla_w_in, mla_q_norm, mla_kv_norm, mla_w_uq, mla_w_ukv, mla_w_o,
           dsa_w_in, dsa_w_o, peer_w_q, peer_sub_keys, peer_w_down, peer_w_up,
           ln_gain, ln_bias):
    B, S, D = x.shape
    T = B * S
    row, tq_mla, tq_dsa, tt_route, tt_dense, te_dense = _tiles(S, T)
    xt = x.reshape(T, D)

    cos, sin = _rope_tables(S, MLA_ROPE)
    q, k, v = _mla_proj(xt, B, S, mla_w_in[0], mla_q_norm[0], mla_kv_norm[0],
                        mla_w_uq[0], mla_w_ukv[0], cos, sin, row)
    o = _mla_attn(q, k, v, tq_mla).reshape(T, MLA_HEADS * MLA_V)
    xt = _proj_ln(o, mla_w_o[0], xt, ln_gain[0, 0], ln_bias[0, 0], row)
    xt = _peer_layer(xt, peer_w_q[0], peer_sub_keys[0], peer_w_down[0], peer_w_up[0],
                     ln_gain[0, 1], ln_bias[0, 1], tt_route, tt_dense, te_dense)

    q, k, v, qi, ki, wi = _dsa_proj(xt, B, S, dsa_w_in[0], row)
    o = _dsa_attn(q, k, v, qi, ki, wi, tq_dsa).reshape(T, DSA_HD)
    xt = _proj_ln(o, dsa_w_o[0], xt, ln_gain[1, 0], ln_bias[1, 0], row)
    xt = _peer_layer(xt, peer_w_q[1], peer_sub_keys[1], peer_w_down[1], peer_w_up[1],
                     ln_gain[1, 1], ln_bias[1, 1], tt_route, tt_dense, te_dense)
    return xt.reshape(B, S, D)
```

```python
import functools

import jax
import jax.numpy as jnp
from jax import lax
from jax.experimental import pallas as pl
from jax.experimental.pallas import tpu as pltpu

F32 = jnp.float32
BF16 = jnp.bfloat16
I32 = jnp.int32

D_MODEL = 1024
DEPTH = 2
CHUNK = 64
ROPE_THETA = 10000.0
LN_EPS = 1e-5
RMS_EPS = 1e-6
DN_ALPHA = (2 * DEPTH) ** 0.25

MLA_HEADS = 8
MLA_NOPE = 128
MLA_ROPE = 64
MLA_V = 128
MLA_Q_RANK = 384
MLA_KV_RANK = 256
MLA_QK_PAD = 256

DSA_HEADS = 8
DSA_HEAD_DIM = 128
IDX_HEADS = 8
IDX_DIM = 64
DSA_TOPK_MAX = 256
DSA_HD = DSA_HEADS * DSA_HEAD_DIM

PEER_HEADS = 8
PEER_NKEYS = 128
PEER_EXPERTS = PEER_NKEYS * PEER_NKEYS
PEER_QDIM = 256
PEER_TOPK = 16

LANES = 128
SUBLANES = 8
VMEM_LIMIT = 56 * 1024 * 1024

NEG_INF = float("-inf")
INT_MIN = -2 ** 31


def _cparams(*sem):
    return pltpu.CompilerParams(dimension_semantics=sem, vmem_limit_bytes=VMEM_LIMIT)


def _dot(a, b):
    return jnp.dot(a, b, preferred_element_type=F32)


def _dot_nt(a, b):
    return lax.dot_general(a, b, (((1,), (1,)), ((), ())), preferred_element_type=F32)


def _layernorm_rows(y, g, b):
    mu = jnp.mean(y, axis=-1, keepdims=True)
    yc = y - mu
    var = jnp.mean(yc * yc, axis=-1, keepdims=True)
    return yc * lax.rsqrt(var + LN_EPS) * g + b


def _rms_rows(h, g):
    ms = jnp.mean(h * h, axis=-1, keepdims=True)
    return h * lax.rsqrt(ms + RMS_EPS) * g


def _mla_proj_kernel(x_ref, win_ref, qn_ref, kvn_ref, wuq_ref, wukv_ref, cc_ref, ss_ref,
                     q_ref, k_ref, v_ref):
    xb = x_ref[...].astype(BF16)
    h = _dot(xb, win_ref[...])
    cq = h[:, :MLA_Q_RANK]
    ckv = h[:, MLA_Q_RANK:MLA_Q_RANK + MLA_KV_RANK]
    o = MLA_Q_RANK + MLA_KV_RANK
    cc = cc_ref[...]
    ss = ss_ref[...]
    k_rope = h[:, o:o + LANES] * cc + h[:, o + LANES:o + 2 * LANES] * ss
    qall = _dot(_rms_rows(cq, qn_ref[...]).astype(BF16), wuq_ref[...])
    kvall = _dot(_rms_rows(ckv, kvn_ref[...]).astype(BF16), wukv_ref[...])
    k_rope_b = k_rope.astype(BF16)
    for hd in range(MLA_HEADS):
        b0 = hd * 3 * LANES
        q_ref[0, hd, :, 0:LANES] = qall[:, b0:b0 + LANES].astype(BF16)
        q_rope = qall[:, b0 + LANES:b0 + 2 * LANES] * cc + qall[:, b0 + 2 * LANES:b0 + 3 * LANES] * ss
        q_ref[0, hd, :, LANES:2 * LANES] = q_rope.astype(BF16)
        k_ref[0, hd, :, 0:LANES] = kvall[:, hd * LANES:(hd + 1) * LANES].astype(BF16)
        k_ref[0, hd, :, LANES:2 * LANES] = k_rope_b
        v0 = MLA_HEADS * MLA_NOPE + hd * MLA_V
        v_ref[0, hd] = kvall[:, v0:v0 + MLA_V].astype(BF16)


def _mla_proj(xt, B, S, w_in, q_norm, kv_norm, w_uq, w_ukv, cos, sin, tm):
    T = B * S
    n_s = S // tm
    half = MLA_ROPE // 2
    def swap_rope(w):
        return jnp.concatenate([w[..., half:], w[..., :half]], axis=-1)
    zpad = lambda w: jnp.concatenate([w, jnp.zeros_like(w)], axis=-1)
    w_kr = w_in[:, MLA_Q_RANK + MLA_KV_RANK:]
    win_ext = jnp.concatenate(
        [w_in[:, :MLA_Q_RANK + MLA_KV_RANK], zpad(w_kr), zpad(swap_rope(w_kr))], axis=-1).astype(BF16)
    uq_nope = w_uq[:, :, :MLA_NOPE]
    uq_rope = w_uq[:, :, MLA_NOPE:]
    wuq_ext = jnp.concatenate([uq_nope, zpad(uq_rope), zpad(swap_rope(uq_rope))], axis=-1)
    wuq_ext = wuq_ext.reshape(MLA_Q_RANK, MLA_HEADS * 3 * LANES).astype(BF16)
    wukv_ext = jnp.concatenate(
        [w_ukv[:, :, :MLA_NOPE].reshape(MLA_KV_RANK, -1), w_ukv[:, :, MLA_NOPE:].reshape(MLA_KV_RANK, -1)],
        axis=-1).astype(BF16)
    z = jnp.zeros((S, LANES - MLA_ROPE), F32)
    cc = jnp.concatenate([cos, cos, z], axis=-1)
    ss = jnp.concatenate([-sin, sin, z], axis=-1)
    full = lambda a: pl.BlockSpec(a.shape, lambda i: (0,) * a.ndim)
    qn = q_norm.reshape(1, -1)
    kvn = kv_norm.reshape(1, -1)
    head_map = lambda i: (i // n_s, 0, i % n_s, 0)
    return pl.pallas_call(
        _mla_proj_kernel,
        grid=(T // tm,),
        in_specs=[pl.BlockSpec((tm, D_MODEL), lambda i: (i, 0)),
                  full(win_ext), full(qn), full(kvn), full(wuq_ext), full(wukv_ext),
                  pl.BlockSpec((tm, LANES), lambda i: (i % n_s, 0)),
                  pl.BlockSpec((tm, LANES), lambda i: (i % n_s, 0))],
        out_specs=[pl.BlockSpec((1, MLA_HEADS, tm, MLA_QK_PAD), head_map),
                   pl.BlockSpec((1, MLA_HEADS, tm, MLA_QK_PAD), head_map),
                   pl.BlockSpec((1, MLA_HEADS, tm, MLA_V), head_map)],
        out_shape=[jax.ShapeDtypeStruct((B, MLA_HEADS, S, MLA_QK_PAD), BF16),
                   jax.ShapeDtypeStruct((B, MLA_HEADS, S, MLA_QK_PAD), BF16),
                   jax.ShapeDtypeStruct((B, MLA_HEADS, S, MLA_V), BF16)],
        compiler_params=_cparams("parallel"),
        name="mla_proj",
    )(xt, win_ext, qn, kvn, wuq_ext, wukv_ext, cc, ss)


MLA_HEADS_PER_STEP = 4


def _mla_attn_kernel(q_ref, k_ref, v_ref, o_ref, *, tq, scale):
    qi = pl.program_id(2)
    hp = MLA_HEADS_PER_STEP

    def step(j, carry, diagonal):
        start = pl.multiple_of(j * tq, tq)
        out = []
        for h in range(hp):
            m, l, acc = carry[h]
            kb = k_ref[0, h, pl.ds(start, tq), :]
            vb = v_ref[0, h, pl.ds(start, tq), :]
            s = _dot_nt(q_ref[0, h], kb) * scale
            if diagonal:
                row = lax.broadcasted_iota(I32, (tq, tq), 0)
                col = lax.broadcasted_iota(I32, (tq, tq), 1)
                s = jnp.where(col // CHUNK <= row // CHUNK, s, NEG_INF)
            m_new = jnp.maximum(m, jnp.max(s, axis=-1, keepdims=True))
            p = jnp.exp(s - m_new)
            alpha = jnp.exp(m - m_new)
            l = alpha * l + jnp.sum(p, axis=-1, keepdims=True)
            acc = alpha * acc + _dot(p.astype(BF16), vb)
            out.append((m_new, l, acc))
        return tuple(out)

    init = tuple((jnp.full((tq, 1), NEG_INF, F32), jnp.zeros((tq, 1), F32), jnp.zeros((tq, MLA_V), F32))
                 for _ in range(hp))
    carry = lax.fori_loop(0, qi, lambda j, c: step(j, c, False), init)
    carry = step(qi, carry, True)
    for h in range(hp):
        _, l, acc = carry[h]
        o_ref[0, :, h * MLA_V:(h + 1) * MLA_V] = (acc / l).astype(BF16)


def _mla_attn(q, k, v, tq):
    B, H, S, _ = q.shape
    hp = MLA_HEADS_PER_STEP
    scale = (MLA_NOPE + MLA_ROPE) ** -0.5
    return pl.pallas_call(
        functools.partial(_mla_attn_kernel, tq=tq, scale=scale),
        grid=(B, H // hp, S // tq),
        in_specs=[pl.BlockSpec((1, hp, tq, MLA_QK_PAD), lambda b, h, i: (b, h, i, 0)),
                  pl.BlockSpec((1, hp, S, MLA_QK_PAD), lambda b, h, i: (b, h, 0, 0)),
                  pl.BlockSpec((1, hp, S, MLA_V), lambda b, h, i: (b, h, 0, 0))],
        out_specs=pl.BlockSpec((1, tq, hp * MLA_V), lambda b, h, i: (b, i, h)),
        out_shape=jax.ShapeDtypeStruct((B, S, H * MLA_V), BF16),
        compiler_params=_cparams("parallel", "parallel", "arbitrary"),
        name="mla_attn",
    )(q, k, v)


def _proj_ln_kernel(a_ref, w_ref, x_ref, g_ref, b_ref, o_ref):
    m = _dot(a_ref[...], w_ref[...])
    y = DN_ALPHA * x_ref[...] + m
    o_ref[...] = _layernorm_rows(y, g_ref[...], b_ref[...])


def _proj_ln(a, w, xt, gain, bias, tm):
    T = xt.shape[0]
    wb = w.astype(BF16)
    g = gain.reshape(1, -1)
    b = bias.reshape(1, -1)
    return pl.pallas_call(
        _proj_ln_kernel,
        grid=(T // tm,),
        in_specs=[pl.BlockSpec((tm, a.shape[1]), lambda i: (i, 0)),
                  pl.BlockSpec(wb.shape, lambda i: (0, 0)),
                  pl.BlockSpec((tm, D_MODEL), lambda i: (i, 0)),
                  pl.BlockSpec(g.shape, lambda i: (0, 0)),
                  pl.BlockSpec(b.shape, lambda i: (0, 0))],
        out_specs=pl.BlockSpec((tm, D_MODEL), lambda i: (i, 0)),
        out_shape=jax.ShapeDtypeStruct((T, D_MODEL), F32),
        compiler_params=_cparams("parallel"),
        name="proj_ln",
    )(a, wb, xt, g, b)


def _dsa_proj_kernel(x_ref, w_ref, ch_ref, sh_ref, ci_ref, ck_ref, sk_ref,
                     q_ref, k_ref, v_ref, qi_ref, ki_ref, wi_ref, *, w_scale):
    xb = x_ref[...].astype(BF16)
    h = _dot(xb, w_ref[...])
    ch = ch_ref[...]
    sh = sh_ref[...]
    ci = ci_ref[...]
    half = LANES // 2
    for hd in range(DSA_HEADS):
        c0 = hd * DSA_HEAD_DIM
        qh = h[:, c0:c0 + LANES]
        q_ref[0, :, c0:c0 + LANES] = (qh * ch + pltpu.roll(qh, half, 1) * sh).astype(BF16)
        kh = h[:, DSA_HD + c0:DSA_HD + c0 + LANES]
        k_ref[0, :, c0:c0 + LANES] = (kh * ch + pltpu.roll(kh, half, 1) * sh).astype(BF16)
        t = h[:, 3 * DSA_HD + c0:3 * DSA_HD + c0 + LANES] * ci
        qi_ref[0, :, c0:c0 + LANES] = (t + pltpu.roll(t, half, 1)).astype(BF16)
    v_ref[0] = h[:, 2 * DSA_HD:3 * DSA_HD].astype(BF16)
    o = 4 * DSA_HD
    ki = h[:, o:o + LANES] * ck_ref[...] + h[:, o + LANES:o + 2 * LANES] * sk_ref[...]
    ki_ref[0] = ki.astype(BF16)
    wi_ref[0] = h[:, o + 2 * LANES:o + 3 * LANES] * w_scale


def _dsa_proj(xt, B, S, w_in, tm):
    T = B * S
    n_s = S // tm
    o1 = 3 * DSA_HD + IDX_HEADS * IDX_DIM
    ih = IDX_DIM // 2
    w_qi = w_in[:, 3 * DSA_HD:o1].reshape(D_MODEL, IDX_HEADS, IDX_DIM)
    w_qi_sw = jnp.concatenate([w_qi[..., ih:], w_qi[..., :ih]], axis=-1)
    w_qi_ext = jnp.concatenate([w_qi, w_qi_sw], axis=-1).reshape(D_MODEL, IDX_HEADS * LANES)
    w_ki = w_in[:, o1:o1 + IDX_DIM]
    w_ki_sw = jnp.concatenate([w_ki[:, ih:], w_ki[:, :ih]], axis=-1)
    zpad = lambda w: jnp.concatenate([w, jnp.zeros((w.shape[0], LANES - w.shape[1]), w.dtype)], axis=-1)
    w_ext = jnp.concatenate(
        [w_in[:, :3 * DSA_HD], w_qi_ext, zpad(w_ki), zpad(w_ki_sw), zpad(w_in[:, o1 + IDX_DIM:])],
        axis=-1).astype(BF16)
    cos_h, sin_h = _rope_tables(S, DSA_HEAD_DIM)
    cos_i, sin_i = _rope_tables(S, IDX_DIM)
    ch = jnp.concatenate([cos_h, cos_h], axis=-1)
    sh = jnp.concatenate([-sin_h, sin_h], axis=-1)
    ci = jnp.concatenate([cos_i, cos_i, -sin_i, sin_i], axis=-1)
    z = jnp.zeros((S, LANES - IDX_DIM), F32)
    ck = jnp.concatenate([cos_i, cos_i, z], axis=-1)
    sk = jnp.concatenate([-sin_i, sin_i, z], axis=-1)
    tab = pl.BlockSpec((tm, LANES), lambda i: (i % n_s, 0))
    row_map = lambda i: (i // n_s, i % n_s, 0)
    wide = pl.BlockSpec((1, tm, DSA_HD), row_map)
    narrow = pl.BlockSpec((1, tm, LANES), row_map)
    w_scale = IDX_HEADS ** -0.5 * IDX_DIM ** -0.5
    return pl.pallas_call(
        functools.partial(_dsa_proj_kernel, w_scale=w_scale),
        grid=(T // tm,),
        in_specs=[pl.BlockSpec((tm, D_MODEL), lambda i: (i, 0)),
                  pl.BlockSpec(w_ext.shape, lambda i: (0, 0)),
                  tab, tab, tab, tab, tab],
        out_specs=[wide, wide, wide, wide, narrow, narrow],
        out_shape=[jax.ShapeDtypeStruct((B, S, DSA_HD), BF16),
                   jax.ShapeDtypeStruct((B, S, DSA_HD), BF16),
                   jax.ShapeDtypeStruct((B, S, DSA_HD), BF16),
                   jax.ShapeDtypeStruct((B, S, IDX_HEADS * LANES), BF16),
                   jax.ShapeDtypeStruct((B, S, LANES), BF16),
                   jax.ShapeDtypeStruct((B, S, LANES), F32)],
        compiler_params=_cparams("parallel"),
        name="dsa_proj",
    )(xt, w_ext, ch, sh, ci, ck, sk)


DSA_KEY_CLASSES = 4


def _dsa_attn_block(q_ref, qi_ref, wi_ref, k_ref, v_ref, ki_ref, o_ref, blk, *, tq, L, topk, scale):
    ki = ki_ref[0, 0:L, :]
    qi = qi_ref[0]
    wi = wi_ref[0]
    score = jnp.zeros((tq, L), F32)
    for hd in range(IDX_HEADS):
        logit = _dot_nt(qi[:, hd * LANES:(hd + 1) * LANES], ki)
        score = score + wi[:, hd:hd + 1] * jnp.maximum(logit, 0.0)
    row = lax.broadcasted_iota(I32, (tq, L), 0)
    col = lax.broadcasted_iota(I32, (tq, L), 1)
    adm = col // CHUNK <= (blk * tq + row) // CHUNK
    score = jnp.where(adm, score + 0.0, NEG_INF)
    bits = lax.bitcast_convert_type(score, I32)
    u = jnp.where(bits < 0, bits ^ 0x7FFFFFFF, bits)

    def search(i, thr):
        cand = thr + lax.shift_left(jnp.int32(1), 31 - i)
        cnt = jnp.sum((u >= cand).astype(F32), axis=-1, keepdims=True)
        return jnp.where(cnt >= topk, cand, thr)
    thr = lax.fori_loop(0, 32, search, jnp.full((tq, 1), INT_MIN, I32), unroll=8)

    gt = u > thr
    eq = u == thr
    need = topk - jnp.sum(gt.astype(F32), axis=-1, keepdims=True)
    r_i = lax.broadcasted_iota(I32, (LANES, LANES), 0)
    c_i = lax.broadcasted_iota(I32, (LANES, LANES), 1)
    tri = (r_i < c_i).astype(BF16)
    eq_b = eq.astype(BF16)
    run = jnp.zeros((tq, 1), F32)
    ranks = []
    for j in range(L // LANES):
        e = eq_b[:, j * LANES:(j + 1) * LANES]
        ranks.append(_dot(e, tri) + run)
        run = run + jnp.sum(e.astype(F32), axis=-1, keepdims=True)
    rank = jnp.concatenate(ranks, axis=-1)
    sel = adm & (gt | (eq & (rank < need)))

    for hd in range(DSA_HEADS):
        c0 = hd * DSA_HEAD_DIM
        s = _dot_nt(q_ref[0, :, c0:c0 + LANES], k_ref[0, 0:L, c0:c0 + LANES]) * scale
        s = jnp.where(sel, s, NEG_INF)
        m = jnp.max(s, axis=-1, keepdims=True)
        p = jnp.exp(s - m)
        l = jnp.sum(p, axis=-1, keepdims=True)
        o = _dot(p.astype(BF16), v_ref[0, 0:L, c0:c0 + LANES])
        o_ref[0, :, c0:c0 + LANES] = (o / l).astype(BF16)


def _dsa_attn_kernel(q_ref, qi_ref, wi_ref, k_ref, v_ref, ki_ref, o_ref, *, tq, S, topk, scale):
    blk = pl.program_id(1)
    width = S // DSA_KEY_CLASSES
    cls = ((blk + 1) * tq - 1) // width
    for c in range(DSA_KEY_CLASSES):
        @pl.when(cls == c)
        def _(c=c):
            _dsa_attn_block(q_ref, qi_ref, wi_ref, k_ref, v_ref, ki_ref, o_ref, blk,
                            tq=tq, L=(c + 1) * width, topk=topk, scale=scale)


def _dsa_attn(q, k, v, qi, ki, wi, tq):
    B, S, _ = q.shape
    topk = min(DSA_TOPK_MAX, S // 4)
    scale = DSA_HEAD_DIM ** -0.5
    qmap = lambda b, i: (b, i, 0)
    kmap = lambda b, i: (b, 0, 0)
    return pl.pallas_call(
        functools.partial(_dsa_attn_kernel, tq=tq, S=S, topk=topk, scale=scale),
        grid=(B, S // tq),
        in_specs=[pl.BlockSpec((1, tq, DSA_HD), qmap),
                  pl.BlockSpec((1, tq, IDX_HEADS * LANES), qmap),
                  pl.BlockSpec((1, tq, LANES), qmap),
                  pl.BlockSpec((1, S, DSA_HD), kmap),
                  pl.BlockSpec((1, S, DSA_HD), kmap),
                  pl.BlockSpec((1, S, LANES), kmap)],
        out_specs=pl.BlockSpec((1, tq, DSA_HD), qmap),
        out_shape=jax.ShapeDtypeStruct((B, S, DSA_HD), BF16),
        compiler_params=_cparams("parallel", "arbitrary"),
        name="dsa_attn",
    )(q, qi, wi, k, v, ki)


_PEER_PAIRS = [(a, b) for a in range(PEER_TOPK) for b in range(PEER_TOPK // (a + 1))]
_PEER_CAND_ROWS = -(-len(_PEER_PAIRS) // SUBLANES) * SUBLANES


def _top_rows(s, n):
    vals = []
    cur = s
    for _ in range(n):
        m = jnp.max(cur, axis=0, keepdims=True)
        vals.append(m)
        cur = jnp.where(cur == m, NEG_INF, cur)
    return vals


def _peer_route_kernel(x_ref, wq_ref, keys_ref, lim_ref, e1_ref, r2_ref, e2_ref, cand_ref):
    tt = x_ref.shape[0]
    q = _dot(x_ref[...].astype(BF16), wq_ref[...]).astype(BF16)
    half = PEER_QDIM // 2
    pad0 = _PEER_CAND_ROWS - SUBLANES
    cand_ref[pad0:, :] = jnp.full((SUBLANES, tt), NEG_INF, F32)
    for hd in range(PEER_HEADS):
        c0 = hd * PEER_QDIM
        s1 = _dot_nt(keys_ref[0, hd], q[:, c0:c0 + half])
        s2 = _dot_nt(keys_ref[1, hd], q[:, c0 + half:c0 + PEER_QDIM])
        v1 = _top_rows(s1, PEER_TOPK)
        v2 = _top_rows(s2, PEER_TOPK)
        for k, (a, b) in enumerate(_PEER_PAIRS):
            cand_ref[k:k + 1, :] = v1[a] + v2[b]
        cand = cand_ref[...]
        cur = cand
        theta = None
        for _ in range(PEER_TOPK):
            theta = jnp.max(cur, axis=0, keepdims=True)
            cur = jnp.where(cur == theta, NEG_INF, cur)
        top = v1[0] + v2[0]
        z = jnp.sum(jnp.where(cand >= theta, jnp.exp(cand - top), 0.0), axis=0, keepdims=True)
        r2 = jnp.full(s2.shape, float(PEER_TOPK), F32)
        for b in reversed(range(PEER_TOPK)):
            r2 = jnp.where(s2 >= v2[b], float(b), r2)
        lim = jnp.zeros(s1.shape, F32)
        for b in range(PEER_TOPK):
            lim = jnp.where(s1 + v2[b] >= theta, float(b + 1), lim)
        lim_ref[hd] = lim
        r2_ref[hd] = r2.astype(BF16)
        e1_ref[hd] = jnp.exp(s1 - v1[0]) / z
        e2_ref[hd] = jnp.exp(s2 - v2[0]).astype(BF16)


def _peer_route(xt, w_q, sub_keys, tt):
    T = xt.shape[0]
    wq = w_q.astype(BF16)
    keys = sub_keys.astype(BF16)
    big = jax.ShapeDtypeStruct((PEER_HEADS, PEER_NKEYS, T), F32)
    big_b = jax.ShapeDtypeStruct((PEER_HEADS, PEER_NKEYS, T), BF16)
    big_spec = pl.BlockSpec((PEER_HEADS, PEER_NKEYS, tt), lambda i: (0, 0, i))
    return pl.pallas_call(
        _peer_route_kernel,
        grid=(T // tt,),
        in_specs=[pl.BlockSpec((tt, D_MODEL), lambda i: (i, 0)),
                  pl.BlockSpec(wq.shape, lambda i: (0, 0)),
                  pl.BlockSpec(keys.shape, lambda i: (0, 0, 0, 0))],
        out_specs=[big_spec, big_spec, big_spec, big_spec],
        out_shape=[big, big, big_b, big_b],
        scratch_shapes=[pltpu.VMEM((_PEER_CAND_ROWS, tt), F32)],
        compiler_params=_cparams("parallel"),
        name="peer_route",
    )(xt, wq, keys)


PEER_CHUNK = 2 * PEER_NKEYS


BF16_SUBLANES = 2 * SUBLANES


def _row_bf16(ref, i1, hd, ls):
    row = jnp.broadcast_to(ref[i1, hd:hd + 1, ls], (BF16_SUBLANES, LANES)).astype(BF16)
    return jnp.tile(row, (PEER_NKEYS // BF16_SUBLANES, 1))


def _peer_dense_kernel(x_ref, wd_ref, wu_ref, lim_ref, e1_ref, r2_ref, e2_ref, g_ref, b_ref,
                       o_ref, xb_ref, p_ref, acc_ref, limr_ref, e1r_ref, r2s_ref, e2s_ref, *, te, tt):
    e = pl.program_id(1)

    @pl.when(e == 0)
    def _():
        xb_ref[...] = x_ref[...].astype(BF16)
        acc_ref[...] = jnp.zeros_like(acc_ref)
        r2s_ref[...] = r2_ref[...]
        e2s_ref[...] = e2_ref[...]

        def relayout(g, carry):
            g0 = pl.multiple_of(g * SUBLANES, SUBLANES)
            for hd in range(PEER_HEADS):
                limg = lim_ref[hd, pl.ds(g0, SUBLANES), :]
                e1g = e1_ref[hd, pl.ds(g0, SUBLANES), :]
                for r in range(SUBLANES):
                    limr_ref[g0 + r, hd:hd + 1, :] = limg[r:r + 1, :]
                    e1r_ref[g0 + r, hd:hd + 1, :] = e1g[r:r + 1, :]
            return carry
        lax.fori_loop(0, PEER_NKEYS // SUBLANES, relayout, 0)

    xb = xb_ref[...]
    n_ch = te // PEER_CHUNK
    i1_0 = e * (te // PEER_NKEYS)
    a_next = _dot_nt(wd_ref[0:PEER_CHUNK, :], xb)
    for c in range(n_ch):
        a_cur = a_next
        if c + 1 < n_ch:
            a_next = _dot_nt(wd_ref[(c + 1) * PEER_CHUNK:(c + 2) * PEER_CHUNK, :], xb)
        for j in range(PEER_CHUNK // PEER_NKEYS):
            i1 = i1_0 + c * (PEER_CHUNK // PEER_NKEYS) + j
            r0 = c * PEER_CHUNK + j * PEER_NKEYS
            for lg in range(tt // LANES):
                ls = slice(lg * LANES, (lg + 1) * LANES)
                gate = jnp.zeros((PEER_NKEYS, LANES), BF16)
                zero = jnp.zeros((), BF16)
                for hd in range(PEER_HEADS):
                    limb = _row_bf16(limr_ref, i1, hd, ls)
                    e1b = _row_bf16(e1r_ref, i1, hd, ls)
                    w = jnp.minimum(e2s_ref[hd, :, ls], limb - r2s_ref[hd, :, ls])
                    gate = gate + jnp.maximum(w, zero) * e1b
                a = a_cur[j * PEER_NKEYS:(j + 1) * PEER_NKEYS, ls]
                act = 0.5 * a * (1.0 + lax.erf(a * (2.0 ** -0.5)))
                p_ref[r0:r0 + PEER_NKEYS, ls] = act.astype(BF16) * gate
        acc_ref[...] += _dot(wu_ref[c], p_ref[c * PEER_CHUNK:(c + 1) * PEER_CHUNK, :])

    @pl.when(e == pl.num_programs(1) - 1)
    def _():
        y = DN_ALPHA * x_ref[...] + acc_ref[...].T
        o_ref[...] = _layernorm_rows(y, g_ref[...], b_ref[...])


def _peer_dense(xt, w_down, w_up, route, gain, bias, tt, te):
    T = xt.shape[0]
    c1, e1, s2, e2 = route
    wd = w_down.astype(BF16)
    wu = w_up.astype(BF16).reshape(PEER_EXPERTS // PEER_CHUNK, PEER_CHUNK, D_MODEL).transpose(0, 2, 1)
    g = gain.reshape(1, -1)
    b = bias.reshape(1, -1)
    big_spec = pl.BlockSpec((PEER_HEADS, PEER_NKEYS, tt), lambda i, e: (0, 0, i))
    return pl.pallas_call(
        functools.partial(_peer_dense_kernel, te=te, tt=tt),
        grid=(T // tt, PEER_EXPERTS // te),
        in_specs=[pl.BlockSpec((tt, D_MODEL), lambda i, e: (i, 0)),
                  pl.BlockSpec((te, D_MODEL), lambda i, e: (e, 0)),
                  pl.BlockSpec((te // PEER_CHUNK, D_MODEL, PEER_CHUNK), lambda i, e: (e, 0, 0)),
                  big_spec, big_spec, big_spec, big_spec,
                  pl.BlockSpec(g.shape, lambda i, e: (0, 0)),
                  pl.BlockSpec(b.shape, lambda i, e: (0, 0))],
        out_specs=pl.BlockSpec((tt, D_MODEL), lambda i, e: (i, 0)),
        out_shape=jax.ShapeDtypeStruct((T, D_MODEL), F32),
        scratch_shapes=[pltpu.VMEM((tt, D_MODEL), BF16),
                        pltpu.VMEM((te, tt), BF16),
                        pltpu.VMEM((D_MODEL, tt), F32),
                        pltpu.VMEM((PEER_NKEYS, PEER_HEADS, tt), F32),
                        pltpu.VMEM((PEER_NKEYS, PEER_HEADS, tt), F32),
                        pltpu.VMEM((PEER_HEADS, PEER_NKEYS, tt), BF16),
                        pltpu.VMEM((PEER_HEADS, PEER_NKEYS, tt), BF16)],
        compiler_params=_cparams("parallel", "arbitrary"),
        name="peer_dense",
    )(xt, wd, wu, c1, e1, s2, e2, g, b)


def _rope_tables(seq, dim):
    inv = ROPE_THETA ** (-jnp.arange(0, dim, 2, dtype=F32) / dim)
    ang = jnp.arange(seq, dtype=F32)[:, None] * inv[None, :]
    return jnp.cos(ang), jnp.sin(ang)


def _tiles(S, T):
    row = min(256, S)
    tq_mla = min(256, S)
    tq_dsa = min(256, S)
    tt_route = min(256, T)
    tt_dense = min(512, T)
    te_dense = 2048
    return row, tq_mla, tq_dsa, tt_route, tt_dense, te_dense


def _peer_layer(xt, w_q, sub_keys, w_down, w_up, gain, bias, tt_route, tt_dense, te_dense):
    route = _peer_route(xt, w_q, sub_keys, tt_route)
    return _peer_dense(xt, w_down, w_up, route, gain, bias, tt_dense, te_dense)


def kernel(x, mla_w_in, mla_q_norm, mla_kv_norm, mla_w_uq, mla_w_ukv, mla_w_o,
           dsa_w_in, dsa_w_o, peer_w_q, peer_sub_keys, peer_w_down, peer_w_up,
           ln_gain, ln_bias):
    B, S, D = x.shape
    T = B * S
    row, tq_mla, tq_dsa, tt_route, tt_dense, te_dense = _tiles(S, T)
    xt = x.reshape(T, D)

    cos, sin = _rope_tables(S, MLA_ROPE)
    q, k, v = _mla_proj(xt, B, S, mla_w_in[0], mla_q_norm[0], mla_kv_norm[0],
                        mla_w_uq[0], mla_w_ukv[0], cos, sin, row)
    o = _mla_attn(q, k, v, tq_mla).reshape(T, MLA_HEADS * MLA_V)
    xt = _proj_ln(o, mla_w_o[0], xt, ln_gain[0, 0], ln_bias[0, 0], row)
    xt = _peer_layer(xt, peer_w_q[0], peer_sub_keys[0], peer_w_down[0], peer_w_up[0],
                     ln_gain[0, 1], ln_bias[0, 1], tt_route, tt_dense, te_dense)

    q, k, v, qi, ki, wi = _dsa_proj(xt, B, S, dsa_w_in[0], row)
    o = _dsa_attn(q, k, v, qi, ki, wi, tq_dsa).reshape(T, DSA_HD)
    xt = _proj_ln(o, dsa_w_o[0], xt, ln_gain[1, 0], ln_bias[1, 0], row)
    xt = _peer_layer(xt, peer_w_q[1], peer_sub_keys[1], peer_w_down[1], peer_w_up[1],
                     ln_gain[1, 1], ln_bias[1, 1], tt_route, tt_dense, te_dense)
    return xt.reshape(B, S, D)
```

```python
import functools

import jax
import jax.numpy as jnp
from jax import lax
from jax.experimental import pallas as pl
from jax.experimental.pallas import tpu as pltpu

F32 = jnp.float32
BF16 = jnp.bfloat16
I32 = jnp.int32

D_MODEL = 1024
DEPTH = 2
CHUNK = 64
ROPE_THETA = 10000.0
LN_EPS = 1e-5
RMS_EPS = 1e-6
DN_ALPHA = (2 * DEPTH) ** 0.25

MLA_HEADS = 8
MLA_NOPE = 128
MLA_ROPE = 64
MLA_V = 128
MLA_Q_RANK = 384
MLA_KV_RANK = 256
MLA_QK_PAD = 256

DSA_HEADS = 8
DSA_HEAD_DIM = 128
IDX_HEADS = 8
IDX_DIM = 64
DSA_TOPK_MAX = 256
DSA_HD = DSA_HEADS * DSA_HEAD_DIM

PEER_HEADS = 8
PEER_NKEYS = 128
PEER_EXPERTS = PEER_NKEYS * PEER_NKEYS
PEER_QDIM = 256
PEER_TOPK = 16

LANES = 128
SUBLANES = 8
VMEM_LIMIT = 56 * 1024 * 1024

NEG_INF = float("-inf")
INT_MIN = -2 ** 31


def _cparams(*sem):
    return pltpu.CompilerParams(dimension_semantics=sem, vmem_limit_bytes=VMEM_LIMIT)


def _dot(a, b):
    return jnp.dot(a, b, preferred_element_type=F32)


def _dot_nt(a, b):
    return lax.dot_general(a, b, (((1,), (1,)), ((), ())), preferred_element_type=F32)


def _layernorm_rows(y, g, b):
    mu = jnp.mean(y, axis=-1, keepdims=True)
    yc = y - mu
    var = jnp.mean(yc * yc, axis=-1, keepdims=True)
    return yc * lax.rsqrt(var + LN_EPS) * g + b


def _rms_rows(h, g):
    ms = jnp.mean(h * h, axis=-1, keepdims=True)
    return h * lax.rsqrt(ms + RMS_EPS) * g


def _mla_proj_kernel(x_ref, win_ref, qn_ref, kvn_ref, wuq_ref, wukv_ref, cc_ref, ss_ref,
                     q_ref, k_ref, v_ref):
    xb = x_ref[...].astype(BF16)
    h = _dot(xb, win_ref[...])
    cq = h[:, :MLA_Q_RANK]
    ckv = h[:, MLA_Q_RANK:MLA_Q_RANK + MLA_KV_RANK]
    o = MLA_Q_RANK + MLA_KV_RANK
    cc = cc_ref[...]
    ss = ss_ref[...]
    k_rope = h[:, o:o + LANES] * cc + h[:, o + LANES:o + 2 * LANES] * ss
    qall = _dot(_rms_rows(cq, qn_ref[...]).astype(BF16), wuq_ref[...])
    kvall = _dot(_rms_rows(ckv, kvn_ref[...]).astype(BF16), wukv_ref[...])
    k_rope_b = k_rope.astype(BF16)
    for hd in range(MLA_HEADS):
        b0 = hd * 3 * LANES
        q_ref[0, hd, :, 0:LANES] = qall[:, b0:b0 + LANES].astype(BF16)
        q_rope = qall[:, b0 + LANES:b0 + 2 * LANES] * cc + qall[:, b0 + 2 * LANES:b0 + 3 * LANES] * ss
        q_ref[0, hd, :, LANES:2 * LANES] = q_rope.astype(BF16)
        k_ref[0, hd, :, 0:LANES] = kvall[:, hd * LANES:(hd + 1) * LANES].astype(BF16)
        k_ref[0, hd, :, LANES:2 * LANES] = k_rope_b
        v0 = MLA_HEADS * MLA_NOPE + hd * MLA_V
        v_ref[0, hd] = kvall[:, v0:v0 + MLA_V].astype(BF16)


def _mla_proj(xt, B, S, w_in, q_norm, kv_norm, w_uq, w_ukv, cos, sin, tm):
    T = B * S
    n_s = S // tm
    half = MLA_ROPE // 2
    def swap_rope(w):
        return jnp.concatenate([w[..., half:], w[..., :half]], axis=-1)
    zpad = lambda w: jnp.concatenate([w, jnp.zeros_like(w)], axis=-1)
    w_kr = w_in[:, MLA_Q_RANK + MLA_KV_RANK:]
    win_ext = jnp.concatenate(
        [w_in[:, :MLA_Q_RANK + MLA_KV_RANK], zpad(w_kr), zpad(swap_rope(w_kr))], axis=-1).astype(BF16)
    uq_nope = w_uq[:, :, :MLA_NOPE]
    uq_rope = w_uq[:, :, MLA_NOPE:]
    wuq_ext = jnp.concatenate([uq_nope, zpad(uq_rope), zpad(swap_rope(uq_rope))], axis=-1)
    wuq_ext = wuq_ext.reshape(MLA_Q_RANK, MLA_HEADS * 3 * LANES).astype(BF16)
    wukv_ext = jnp.concatenate(
        [w_ukv[:, :, :MLA_NOPE].reshape(MLA_KV_RANK, -1), w_ukv[:, :, MLA_NOPE:].reshape(MLA_KV_RANK, -1)],
        axis=-1).astype(BF16)
    z = jnp.zeros((S, LANES - MLA_ROPE), F32)
    cc = jnp.concatenate([cos, cos, z], axis=-1)
    ss = jnp.concatenate([-sin, sin, z], axis=-1)
    full = lambda a: pl.BlockSpec(a.shape, lambda i: (0,) * a.ndim)
    qn = q_norm.reshape(1, -1)
    kvn = kv_norm.reshape(1, -1)
    head_map = lambda i: (i // n_s, 0, i % n_s, 0)
    return pl.pallas_call(
        _mla_proj_kernel,
        grid=(T // tm,),
        in_specs=[pl.BlockSpec((tm, D_MODEL), lambda i: (i, 0)),
                  full(win_ext), full(qn), full(kvn), full(wuq_ext), full(wukv_ext),
                  pl.BlockSpec((tm, LANES), lambda i: (i % n_s, 0)),
                  pl.BlockSpec((tm, LANES), lambda i: (i % n_s, 0))],
        out_specs=[pl.BlockSpec((1, MLA_HEADS, tm, MLA_QK_PAD), head_map),
                   pl.BlockSpec((1, MLA_HEADS, tm, MLA_QK_PAD), head_map),
                   pl.BlockSpec((1, MLA_HEADS, tm, MLA_V), head_map)],
        out_shape=[jax.ShapeDtypeStruct((B, MLA_HEADS, S, MLA_QK_PAD), BF16),
                   jax.ShapeDtypeStruct((B, MLA_HEADS, S, MLA_QK_PAD), BF16),
                   jax.ShapeDtypeStruct((B, MLA_HEADS, S, MLA_V), BF16)],
        compiler_params=_cparams("parallel"),
        name="mla_proj",
    )(xt, win_ext, qn, kvn, wuq_ext, wukv_ext, cc, ss)


MLA_HEADS_PER_STEP = 4
KEY_CLASSES = 4


def _mla_attn_block(q_ref, k_ref, v_ref, o_ref, blk, *, tq, L, scale):
    row = lax.broadcasted_iota(I32, (tq, L), 0)
    col = lax.broadcasted_iota(I32, (tq, L), 1)
    adm = col // CHUNK <= (blk * tq + row) // CHUNK
    for h in range(MLA_HEADS_PER_STEP):
        s = _dot_nt(q_ref[0, h], k_ref[0, h, 0:L, :]) * scale
        s = jnp.where(adm, s, NEG_INF)
        m = jnp.max(s, axis=-1, keepdims=True)
        p = jnp.exp(s - m)
        l = jnp.sum(p, axis=-1, keepdims=True)
        o = _dot(p.astype(BF16), v_ref[0, h, 0:L, :])
        o_ref[0, :, h * MLA_V:(h + 1) * MLA_V] = (o / l).astype(BF16)


def _mla_attn_kernel(q_ref, k_ref, v_ref, o_ref, *, tq, S, scale):
    blk = pl.program_id(2)
    width = S // KEY_CLASSES
    cls = ((blk + 1) * tq - 1) // width
    for c in range(KEY_CLASSES):
        @pl.when(cls == c)
        def _(c=c):
            _mla_attn_block(q_ref, k_ref, v_ref, o_ref, blk, tq=tq, L=(c + 1) * width, scale=scale)


def _mla_attn(q, k, v, tq):
    B, H, S, _ = q.shape
    hp = MLA_HEADS_PER_STEP
    scale = (MLA_NOPE + MLA_ROPE) ** -0.5
    return pl.pallas_call(
        functools.partial(_mla_attn_kernel, tq=tq, S=S, scale=scale),
        grid=(B, H // hp, S // tq),
        in_specs=[pl.BlockSpec((1, hp, tq, MLA_QK_PAD), lambda b, h, i: (b, h, i, 0)),
                  pl.BlockSpec((1, hp, S, MLA_QK_PAD), lambda b, h, i: (b, h, 0, 0)),
                  pl.BlockSpec((1, hp, S, MLA_V), lambda b, h, i: (b, h, 0, 0))],
        out_specs=pl.BlockSpec((1, tq, hp * MLA_V), lambda b, h, i: (b, i, h)),
        out_shape=jax.ShapeDtypeStruct((B, S, H * MLA_V), BF16),
        compiler_params=_cparams("parallel", "parallel", "arbitrary"),
        name="mla_attn",
    )(q, k, v)


def _proj_ln_kernel(a_ref, w_ref, x_ref, g_ref, b_ref, o_ref):
    m = _dot(a_ref[...], w_ref[...])
    y = DN_ALPHA * x_ref[...] + m
    o_ref[...] = _layernorm_rows(y, g_ref[...], b_ref[...])


def _proj_ln(a, w, xt, gain, bias, tm):
    T = xt.shape[0]
    wb = w.astype(BF16)
    g = gain.reshape(1, -1)
    b = bias.reshape(1, -1)
    return pl.pallas_call(
        _proj_ln_kernel,
        grid=(T // tm,),
        in_specs=[pl.BlockSpec((tm, a.shape[1]), lambda i: (i, 0)),
                  pl.BlockSpec(wb.shape, lambda i: (0, 0)),
                  pl.BlockSpec((tm, D_MODEL), lambda i: (i, 0)),
                  pl.BlockSpec(g.shape, lambda i: (0, 0)),
                  pl.BlockSpec(b.shape, lambda i: (0, 0))],
        out_specs=pl.BlockSpec((tm, D_MODEL), lambda i: (i, 0)),
        out_shape=jax.ShapeDtypeStruct((T, D_MODEL), F32),
        compiler_params=_cparams("parallel"),
        name="proj_ln",
    )(a, wb, xt, g, b)


def _dsa_proj_kernel(x_ref, w_ref, ch_ref, sh_ref, ci_ref, ck_ref, sk_ref,
                     q_ref, k_ref, v_ref, qi_ref, ki_ref, wi_ref, *, w_scale):
    xb = x_ref[...].astype(BF16)
    h = _dot(xb, w_ref[...])
    ch = ch_ref[...]
    sh = sh_ref[...]
    ci = ci_ref[...]
    half = LANES // 2
    for hd in range(DSA_HEADS):
        c0 = hd * DSA_HEAD_DIM
        qh = h[:, c0:c0 + LANES]
        q_ref[0, :, c0:c0 + LANES] = (qh * ch + pltpu.roll(qh, half, 1) * sh).astype(BF16)
        kh = h[:, DSA_HD + c0:DSA_HD + c0 + LANES]
        k_ref[0, :, c0:c0 + LANES] = (kh * ch + pltpu.roll(kh, half, 1) * sh).astype(BF16)
        t = h[:, 3 * DSA_HD + c0:3 * DSA_HD + c0 + LANES] * ci
        qi_ref[0, :, c0:c0 + LANES] = (t + pltpu.roll(t, half, 1)).astype(BF16)
    v_ref[0] = h[:, 2 * DSA_HD:3 * DSA_HD].astype(BF16)
    o = 4 * DSA_HD
    ki = h[:, o:o + LANES] * ck_ref[...] + h[:, o + LANES:o + 2 * LANES] * sk_ref[...]
    ki_ref[0] = ki.astype(BF16)
    wi_ref[0] = h[:, o + 2 * LANES:o + 3 * LANES] * w_scale


def _dsa_proj(xt, B, S, w_in, tm):
    T = B * S
    n_s = S // tm
    o1 = 3 * DSA_HD + IDX_HEADS * IDX_DIM
    ih = IDX_DIM // 2
    w_qi = w_in[:, 3 * DSA_HD:o1].reshape(D_MODEL, IDX_HEADS, IDX_DIM)
    w_qi_sw = jnp.concatenate([w_qi[..., ih:], w_qi[..., :ih]], axis=-1)
    w_qi_ext = jnp.concatenate([w_qi, w_qi_sw], axis=-1).reshape(D_MODEL, IDX_HEADS * LANES)
    w_ki = w_in[:, o1:o1 + IDX_DIM]
    w_ki_sw = jnp.concatenate([w_ki[:, ih:], w_ki[:, :ih]], axis=-1)
    zpad = lambda w: jnp.concatenate([w, jnp.zeros((w.shape[0], LANES - w.shape[1]), w.dtype)], axis=-1)
    w_ext = jnp.concatenate(
        [w_in[:, :3 * DSA_HD], w_qi_ext, zpad(w_ki), zpad(w_ki_sw), zpad(w_in[:, o1 + IDX_DIM:])],
        axis=-1).astype(BF16)
    cos_h, sin_h = _rope_tables(S, DSA_HEAD_DIM)
    cos_i, sin_i = _rope_tables(S, IDX_DIM)
    ch = jnp.concatenate([cos_h, cos_h], axis=-1)
    sh = jnp.concatenate([-sin_h, sin_h], axis=-1)
    ci = jnp.concatenate([cos_i, cos_i, -sin_i, sin_i], axis=-1)
    z = jnp.zeros((S, LANES - IDX_DIM), F32)
    ck = jnp.concatenate([cos_i, cos_i, z], axis=-1)
    sk = jnp.concatenate([-sin_i, sin_i, z], axis=-1)
    tab = pl.BlockSpec((tm, LANES), lambda i: (i % n_s, 0))
    row_map = lambda i: (i // n_s, i % n_s, 0)
    wide = pl.BlockSpec((1, tm, DSA_HD), row_map)
    narrow = pl.BlockSpec((1, tm, LANES), row_map)
    w_scale = IDX_HEADS ** -0.5 * IDX_DIM ** -0.5
    return pl.pallas_call(
        functools.partial(_dsa_proj_kernel, w_scale=w_scale),
        grid=(T // tm,),
        in_specs=[pl.BlockSpec((tm, D_MODEL), lambda i: (i, 0)),
                  pl.BlockSpec(w_ext.shape, lambda i: (0, 0)),
                  tab, tab, tab, tab, tab],
        out_specs=[wide, wide, wide, wide, narrow, narrow],
        out_shape=[jax.ShapeDtypeStruct((B, S, DSA_HD), BF16),
                   jax.ShapeDtypeStruct((B, S, DSA_HD), BF16),
                   jax.ShapeDtypeStruct((B, S, DSA_HD), BF16),
                   jax.ShapeDtypeStruct((B, S, IDX_HEADS * LANES), BF16),
                   jax.ShapeDtypeStruct((B, S, LANES), BF16),
                   jax.ShapeDtypeStruct((B, S, LANES), F32)],
        compiler_params=_cparams("parallel"),
        name="dsa_proj",
    )(xt, w_ext, ch, sh, ci, ck, sk)


def _dsa_attn_block(q_ref, qi_ref, wi_ref, k_ref, v_ref, ki_ref, o_ref, blk, *, tq, L, topk, scale):
    ki = ki_ref[0, 0:L, :]
    qi = qi_ref[0]
    wi = wi_ref[0]
    score = jnp.zeros((tq, L), F32)
    for hd in range(IDX_HEADS):
        logit = _dot_nt(qi[:, hd * LANES:(hd + 1) * LANES], ki)
        score = score + wi[:, hd:hd + 1] * jnp.maximum(logit, 0.0)
    row = lax.broadcasted_iota(I32, (tq, L), 0)
    col = lax.broadcasted_iota(I32, (tq, L), 1)
    adm = col // CHUNK <= (blk * tq + row) // CHUNK
    score = jnp.where(adm, score + 0.0, NEG_INF)
    bits = lax.bitcast_convert_type(score, I32)
    u = jnp.where(bits < 0, bits ^ 0x7FFFFFFF, bits)

    def search(i, thr):
        cand = thr + lax.shift_left(jnp.int32(1), 31 - i)
        cnt = jnp.sum((u >= cand).astype(F32), axis=-1, keepdims=True)
        return jnp.where(cnt >= topk, cand, thr)
    thr = lax.fori_loop(0, 32, search, jnp.full((tq, 1), INT_MIN, I32), unroll=8)

    gt = u > thr
    eq = u == thr
    need = topk - jnp.sum(gt.astype(F32), axis=-1, keepdims=True)
    r_i = lax.broadcasted_iota(I32, (LANES, LANES), 0)
    c_i = lax.broadcasted_iota(I32, (LANES, LANES), 1)
    tri = (r_i < c_i).astype(BF16)
    eq_b = eq.astype(BF16)
    run = jnp.zeros((tq, 1), F32)
    ranks = []
    for j in range(L // LANES):
        e = eq_b[:, j * LANES:(j + 1) * LANES]
        ranks.append(_dot(e, tri) + run)
        run = run + jnp.sum(e.astype(F32), axis=-1, keepdims=True)
    rank = jnp.concatenate(ranks, axis=-1)
    sel = adm & (gt | (eq & (rank < need)))

    for hd in range(DSA_HEADS):
        c0 = hd * DSA_HEAD_DIM
        s = _dot_nt(q_ref[0, :, c0:c0 + LANES], k_ref[0, 0:L, c0:c0 + LANES]) * scale
        s = jnp.where(sel, s, NEG_INF)
        m = jnp.max(s, axis=-1, keepdims=True)
        p = jnp.exp(s - m)
        l = jnp.sum(p, axis=-1, keepdims=True)
        o = _dot(p.astype(BF16), v_ref[0, 0:L, c0:c0 + LANES])
        o_ref[0, :, c0:c0 + LANES] = (o / l).astype(BF16)


def _dsa_attn_kernel(q_ref, qi_ref, wi_ref, k_ref, v_ref, ki_ref, o_ref, *, tq, S, topk, scale):
    blk = pl.program_id(1)
    width = S // KEY_CLASSES
    cls = ((blk + 1) * tq - 1) // width
    for c in range(KEY_CLASSES):
        @pl.when(cls == c)
        def _(c=c):
            _dsa_attn_block(q_ref, qi_ref, wi_ref, k_ref, v_ref, ki_ref, o_ref, blk,
                            tq=tq, L=(c + 1) * width, topk=topk, scale=scale)


def _dsa_attn(q, k, v, qi, ki, wi, tq):
    B, S, _ = q.shape
    topk = min(DSA_TOPK_MAX, S // 4)
    scale = DSA_HEAD_DIM ** -0.5
    qmap = lambda b, i: (b, i, 0)
    kmap = lambda b, i: (b, 0, 0)
    return pl.pallas_call(
        functools.partial(_dsa_attn_kernel, tq=tq, S=S, topk=topk, scale=scale),
        grid=(B, S // tq),
        in_specs=[pl.BlockSpec((1, tq, DSA_HD), qmap),
                  pl.BlockSpec((1, tq, IDX_HEADS * LANES), qmap),
                  pl.BlockSpec((1, tq, LANES), qmap),
                  pl.BlockSpec((1, S, DSA_HD), kmap),
                  pl.BlockSpec((1, S, DSA_HD), kmap),
                  pl.BlockSpec((1, S, LANES), kmap)],
        out_specs=pl.BlockSpec((1, tq, DSA_HD), qmap),
        out_shape=jax.ShapeDtypeStruct((B, S, DSA_HD), BF16),
        compiler_params=_cparams("parallel", "arbitrary"),
        name="dsa_attn",
    )(q, qi, wi, k, v, ki)


_PEER_PAIRS = [(a, b) for a in range(PEER_TOPK) for b in range(PEER_TOPK // (a + 1))]
_PEER_CAND_ROWS = -(-len(_PEER_PAIRS) // SUBLANES) * SUBLANES


def _sort_network(n):
    pairs = []
    p = 1
    while p < n:
        k = p
        while k >= 1:
            for j in range(k % p, n - k, 2 * k):
                for i in range(min(k, n - j - k)):
                    if (i + j) // (2 * p) == (i + j + k) // (2 * p):
                        pairs.append((i + j, i + j + k))
            k //= 2
        p *= 2
    return pairs


def _top_rows(s, n):
    groups = PEER_NKEYS // SUBLANES
    v = [s[g * SUBLANES:(g + 1) * SUBLANES, :] for g in range(groups)]
    for i, j in _sort_network(groups):
        v[i], v[j] = jnp.maximum(v[i], v[j]), jnp.minimum(v[i], v[j])
    vals = []
    for k in range(n):
        m = jnp.max(v[0], axis=0, keepdims=True)
        vals.append(m)
        won = v[0] == m
        for d in range(groups - 1 - k):
            v[d] = jnp.where(won, v[d + 1], v[d])
    return vals


def _peer_route_kernel(x_ref, wq_ref, keys_ref, lim_ref, e1_ref, r2_ref, e2_ref, cand_ref):
    tt = x_ref.shape[0]
    q = _dot(x_ref[...].astype(BF16), wq_ref[...]).astype(BF16)
    half = PEER_QDIM // 2
    pad0 = _PEER_CAND_ROWS - SUBLANES
    cand_ref[pad0:, :] = jnp.full((SUBLANES, tt), NEG_INF, F32)
    for hd in range(PEER_HEADS):
        c0 = hd * PEER_QDIM
        s1 = _dot_nt(keys_ref[0, hd], q[:, c0:c0 + half])
        s2 = _dot_nt(keys_ref[1, hd], q[:, c0 + half:c0 + PEER_QDIM])
        v1 = _top_rows(s1, PEER_TOPK)
        v2 = _top_rows(s2, PEER_TOPK)
        for k, (a, b) in enumerate(_PEER_PAIRS):
            cand_ref[k:k + 1, :] = v1[a] + v2[b]
        cand = cand_ref[...]
        cur = cand
        theta = None
        for _ in range(PEER_TOPK):
            theta = jnp.max(cur, axis=0, keepdims=True)
            cur = jnp.where(cur == theta, NEG_INF, cur)
        top = v1[0] + v2[0]
        z = jnp.sum(jnp.where(cand >= theta, jnp.exp(cand - top), 0.0), axis=0, keepdims=True)
        r2 = jnp.full(s2.shape, float(PEER_TOPK), F32)
        for b in reversed(range(PEER_TOPK)):
            r2 = jnp.where(s2 >= v2[b], float(b), r2)
        lim = jnp.zeros(s1.shape, F32)
        for b in range(PEER_TOPK):
            lim = jnp.where(s1 + v2[b] >= theta, float(b + 1), lim)
        lim_ref[hd] = lim
        r2_ref[hd] = r2.astype(BF16)
        e1_ref[hd] = jnp.exp(s1 - v1[0]) / z
        e2_ref[hd] = jnp.exp(s2 - v2[0]).astype(BF16)


def _peer_route(xt, w_q, sub_keys, tt):
    T = xt.shape[0]
    wq = w_q.astype(BF16)
    keys = sub_keys.astype(BF16)
    big = jax.ShapeDtypeStruct((PEER_HEADS, PEER_NKEYS, T), F32)
    big_b = jax.ShapeDtypeStruct((PEER_HEADS, PEER_NKEYS, T), BF16)
    big_spec = pl.BlockSpec((PEER_HEADS, PEER_NKEYS, tt), lambda i: (0, 0, i))
    return pl.pallas_call(
        _peer_route_kernel,
        grid=(T // tt,),
        in_specs=[pl.BlockSpec((tt, D_MODEL), lambda i: (i, 0)),
                  pl.BlockSpec(wq.shape, lambda i: (0, 0)),
                  pl.BlockSpec(keys.shape, lambda i: (0, 0, 0, 0))],
        out_specs=[big_spec, big_spec, big_spec, big_spec],
        out_shape=[big, big, big_b, big_b],
        scratch_shapes=[pltpu.VMEM((_PEER_CAND_ROWS, tt), F32)],
        compiler_params=_cparams("parallel"),
        name="peer_route",
    )(xt, wq, keys)


PEER_CHUNK = 2 * PEER_NKEYS


BF16_SUBLANES = 2 * SUBLANES


def _row_bf16(ref, i1, hd, ls):
    row = jnp.broadcast_to(ref[i1, hd:hd + 1, ls], (BF16_SUBLANES, LANES)).astype(BF16)
    return jnp.tile(row, (PEER_NKEYS // BF16_SUBLANES, 1))


def _peer_dense_kernel(x_ref, wd_ref, wu_ref, lim_ref, e1_ref, r2_ref, e2_ref, g_ref, b_ref,
                       o_ref, xb_ref, p_ref, acc_ref, limr_ref, e1r_ref, r2s_ref, e2s_ref, *, te, tt):
    e = pl.program_id(1)

    @pl.when(e == 0)
    def _():
        xb_ref[...] = x_ref[...].astype(BF16)
        acc_ref[...] = jnp.zeros_like(acc_ref)
        r2s_ref[...] = r2_ref[...]
        e2s_ref[...] = e2_ref[...]

        def relayout(g, carry):
            g0 = pl.multiple_of(g * SUBLANES, SUBLANES)
            for hd in range(PEER_HEADS):
                limg = lim_ref[hd, pl.ds(g0, SUBLANES), :]
                e1g = e1_ref[hd, pl.ds(g0, SUBLANES), :]
                for r in range(SUBLANES):
                    limr_ref[g0 + r, hd:hd + 1, :] = limg[r:r + 1, :]
                    e1r_ref[g0 + r, hd:hd + 1, :] = e1g[r:r + 1, :]
            return carry
        lax.fori_loop(0, PEER_NKEYS // SUBLANES, relayout, 0)

    xb = xb_ref[...]
    n_ch = te // PEER_CHUNK
    i1_0 = e * (te // PEER_NKEYS)
    a_next = _dot_nt(wd_ref[0:PEER_CHUNK, :], xb)
    for c in range(n_ch):
        a_cur = a_next
        if c + 1 < n_ch:
            a_next = _dot_nt(wd_ref[(c + 1) * PEER_CHUNK:(c + 2) * PEER_CHUNK, :], xb)
        for j in range(PEER_CHUNK // PEER_NKEYS):
            i1 = i1_0 + c * (PEER_CHUNK // PEER_NKEYS) + j
            r0 = c * PEER_CHUNK + j * PEER_NKEYS
            for lg in range(tt // LANES):
                ls = slice(lg * LANES, (lg + 1) * LANES)
                gate = jnp.zeros((PEER_NKEYS, LANES), BF16)
                zero = jnp.zeros((), BF16)
                for hd in range(PEER_HEADS):
                    limb = _row_bf16(limr_ref, i1, hd, ls)
                    e1b = _row_bf16(e1r_ref, i1, hd, ls)
                    gate = gate + jnp.where(r2s_ref[hd, :, ls] < limb, e2s_ref[hd, :, ls] * e1b, zero)
                a = a_cur[j * PEER_NKEYS:(j + 1) * PEER_NKEYS, ls]
                act = 0.5 * a * (1.0 + lax.erf(a * (2.0 ** -0.5)))
                p_ref[r0:r0 + PEER_NKEYS, ls] = act.astype(BF16) * gate
        acc_ref[...] += _dot(wu_ref[c], p_ref[c * PEER_CHUNK:(c + 1) * PEER_CHUNK, :])

    @pl.when(e == pl.num_programs(1) - 1)
    def _():
        y = DN_ALPHA * x_ref[...] + acc_ref[...].T
        o_ref[...] = _layernorm_rows(y, g_ref[...], b_ref[...])


def _peer_dense(xt, w_down, w_up, route, gain, bias, tt, te):
    T = xt.shape[0]
    c1, e1, s2, e2 = route
    wd = w_down.astype(BF16)
    wu = w_up.astype(BF16).reshape(PEER_EXPERTS // PEER_CHUNK, PEER_CHUNK, D_MODEL).transpose(0, 2, 1)
    g = gain.reshape(1, -1)
    b = bias.reshape(1, -1)
    big_spec = pl.BlockSpec((PEER_HEADS, PEER_NKEYS, tt), lambda i, e: (0, 0, i))
    return pl.pallas_call(
        functools.partial(_peer_dense_kernel, te=te, tt=tt),
        grid=(T // tt, PEER_EXPERTS // te),
        in_specs=[pl.BlockSpec((tt, D_MODEL), lambda i, e: (i, 0)),
                  pl.BlockSpec((te, D_MODEL), lambda i, e: (e, 0)),
                  pl.BlockSpec((te // PEER_CHUNK, D_MODEL, PEER_CHUNK), lambda i, e: (e, 0, 0)),
                  big_spec, big_spec, big_spec, big_spec,
                  pl.BlockSpec(g.shape, lambda i, e: (0, 0)),
                  pl.BlockSpec(b.shape, lambda i, e: (0, 0))],
        out_specs=pl.BlockSpec((tt, D_MODEL), lambda i, e: (i, 0)),
        out_shape=jax.ShapeDtypeStruct((T, D_MODEL), F32),
        scratch_shapes=[pltpu.VMEM((tt, D_MODEL), BF16),
                        pltpu.VMEM((te, tt), BF16),
                        pltpu.VMEM((D_MODEL, tt), F32),
                        pltpu.VMEM((PEER_NKEYS, PEER_HEADS, tt), F32),
                        pltpu.VMEM((PEER_NKEYS, PEER_HEADS, tt), F32),
                        pltpu.VMEM((PEER_HEADS, PEER_NKEYS, tt), BF16),
                        pltpu.VMEM((PEER_HEADS, PEER_NKEYS, tt), BF16)],
        compiler_params=_cparams("parallel", "arbitrary"),
        name="peer_dense",
    )(xt, wd, wu, c1, e1, s2, e2, g, b)


def _rope_tables(seq, dim):
    inv = ROPE_THETA ** (-jnp.arange(0, dim, 2, dtype=F32) / dim)
    ang = jnp.arange(seq, dtype=F32)[:, None] * inv[None, :]
    return jnp.cos(ang), jnp.sin(ang)


def _tiles(S, T):
    row = min(256, S)
    tq_mla = min(256, S)
    tq_dsa = min(256, S)
    tt_route = min(256, T)
    tt_dense = min(512, T)
    te_dense = 2048
    return row, tq_mla, tq_dsa, tt_route, tt_dense, te_dense


def _peer_layer(xt, w_q, sub_keys, w_down, w_up, gain, bias, tt_route, tt_dense, te_dense):
    route = _peer_route(xt, w_q, sub_keys, tt_route)
    return _peer_dense(xt, w_down, w_up, route, gain, bias, tt_dense, te_dense)


def kernel(x, mla_w_in, mla_q_norm, mla_kv_norm, mla_w_uq, mla_w_ukv, mla_w_o,
           dsa_w_in, dsa_w_o, peer_w_q, peer_sub_keys, peer_w_down, peer_w_up,
           ln_gain, ln_bias):
    B, S, D = x.shape
    T = B * S
    row, tq_mla, tq_dsa, tt_route, tt_dense, te_dense = _tiles(S, T)
    xt = x.reshape(T, D)

    cos, sin = _rope_tables(S, MLA_ROPE)
    q, k, v = _mla_proj(xt, B, S, mla_w_in[0], mla_q_norm[0], mla_kv_norm[0],
                        mla_w_uq[0], mla_w_ukv[0], cos, sin, row)
    o = _mla_attn(q, k, v, tq_mla).reshape(T, MLA_HEADS * MLA_V)
    xt = _proj_ln(o, mla_w_o[0], xt, ln_gain[0, 0], ln_bias[0, 0], row)
    xt = _peer_layer(xt, peer_w_q[0], peer_sub_keys[0], peer_w_down[0], peer_w_up[0],
                     ln_gain[0, 1], ln_bias[0, 1], tt_route, tt_dense, te_dense)

    q, k, v, qi, ki, wi = _dsa_proj(xt, B, S, dsa_w_in[0], row)
    o = _dsa_attn(q, k, v, qi, ki, wi, tq_dsa).reshape(T, DSA_HD)
    xt = _proj_ln(o, dsa_w_o[0], xt, ln_gain[1, 0], ln_bias[1, 0], row)
    xt = _peer_layer(xt, peer_w_q[1], peer_sub_keys[1], peer_w_down[1], peer_w_up[1],
                     ln_gain[1, 1], ln_bias[1, 1], tt_route, tt_dense, te_dense)
    return xt.reshape(B, S, D)
```

```python
import functools

import jax
import jax.numpy as jnp
from jax import lax
from jax.experimental import pallas as pl
from jax.experimental.pallas import tpu as pltpu

F32 = jnp.float32
BF16 = jnp.bfloat16
I32 = jnp.int32

D_MODEL = 1024
DEPTH = 2
CHUNK = 64
ROPE_THETA = 10000.0
LN_EPS = 1e-5
RMS_EPS = 1e-6
DN_ALPHA = (2 * DEPTH) ** 0.25

MLA_HEADS = 8
MLA_NOPE = 128
MLA_ROPE = 64
MLA_V = 128
MLA_Q_RANK = 384
MLA_KV_RANK = 256
MLA_QK_PAD = 256

DSA_HEADS = 8
DSA_HEAD_DIM = 128
IDX_HEADS = 8
IDX_DIM = 64
DSA_TOPK_MAX = 256
DSA_HD = DSA_HEADS * DSA_HEAD_DIM

PEER_HEADS = 8
PEER_NKEYS = 128
PEER_EXPERTS = PEER_NKEYS * PEER_NKEYS
PEER_QDIM = 256
PEER_TOPK = 16

LANES = 128
SUBLANES = 8
VMEM_LIMIT = 56 * 1024 * 1024

NEG_INF = float("-inf")
INT_MIN = -2 ** 31


def _cparams(*sem):
    return pltpu.CompilerParams(dimension_semantics=sem, vmem_limit_bytes=VMEM_LIMIT)


def _dot(a, b):
    return jnp.dot(a, b, preferred_element_type=F32)


def _dot_nt(a, b):
    return lax.dot_general(a, b, (((1,), (1,)), ((), ())), preferred_element_type=F32)


def _layernorm_rows(y, g, b):
    mu = jnp.mean(y, axis=-1, keepdims=True)
    yc = y - mu
    var = jnp.mean(yc * yc, axis=-1, keepdims=True)
    return yc * lax.rsqrt(var + LN_EPS) * g + b


def _rms_rows(h, g):
    ms = jnp.mean(h * h, axis=-1, keepdims=True)
    return h * lax.rsqrt(ms + RMS_EPS) * g


def _mla_proj_kernel(x_ref, win_ref, qn_ref, kvn_ref, wuq_ref, wukv_ref, cc_ref, ss_ref,
                     q_ref, k_ref, v_ref):
    xb = x_ref[...].astype(BF16)
    h = _dot(xb, win_ref[...])
    cq = h[:, :MLA_Q_RANK]
    ckv = h[:, MLA_Q_RANK:MLA_Q_RANK + MLA_KV_RANK]
    o = MLA_Q_RANK + MLA_KV_RANK
    cc = cc_ref[...]
    ss = ss_ref[...]
    k_rope = h[:, o:o + LANES] * cc + h[:, o + LANES:o + 2 * LANES] * ss
    qall = _dot(_rms_rows(cq, qn_ref[...]).astype(BF16), wuq_ref[...])
    kvall = _dot(_rms_rows(ckv, kvn_ref[...]).astype(BF16), wukv_ref[...])
    k_rope_b = k_rope.astype(BF16)
    for hd in range(MLA_HEADS):
        b0 = hd * 3 * LANES
        q_ref[0, hd, :, 0:LANES] = qall[:, b0:b0 + LANES].astype(BF16)
        q_rope = qall[:, b0 + LANES:b0 + 2 * LANES] * cc + qall[:, b0 + 2 * LANES:b0 + 3 * LANES] * ss
        q_ref[0, hd, :, LANES:2 * LANES] = q_rope.astype(BF16)
        k_ref[0, hd, :, 0:LANES] = kvall[:, hd * LANES:(hd + 1) * LANES].astype(BF16)
        k_ref[0, hd, :, LANES:2 * LANES] = k_rope_b
        v0 = MLA_HEADS * MLA_NOPE + hd * MLA_V
        v_ref[0, hd] = kvall[:, v0:v0 + MLA_V].astype(BF16)


def _mla_proj(xt, B, S, w_in, q_norm, kv_norm, w_uq, w_ukv, cos, sin, tm):
    T = B * S
    n_s = S // tm
    half = MLA_ROPE // 2
    def swap_rope(w):
        return jnp.concatenate([w[..., half:], w[..., :half]], axis=-1)
    zpad = lambda w: jnp.concatenate([w, jnp.zeros_like(w)], axis=-1)
    w_kr = w_in[:, MLA_Q_RANK + MLA_KV_RANK:]
    win_ext = jnp.concatenate(
        [w_in[:, :MLA_Q_RANK + MLA_KV_RANK], zpad(w_kr), zpad(swap_rope(w_kr))], axis=-1).astype(BF16)
    uq_nope = w_uq[:, :, :MLA_NOPE]
    uq_rope = w_uq[:, :, MLA_NOPE:]
    wuq_ext = jnp.concatenate([uq_nope, zpad(uq_rope), zpad(swap_rope(uq_rope))], axis=-1)
    wuq_ext = wuq_ext.reshape(MLA_Q_RANK, MLA_HEADS * 3 * LANES).astype(BF16)
    wukv_ext = jnp.concatenate(
        [w_ukv[:, :, :MLA_NOPE].reshape(MLA_KV_RANK, -1), w_ukv[:, :, MLA_NOPE:].reshape(MLA_KV_RANK, -1)],
        axis=-1).astype(BF16)
    z = jnp.zeros((S, LANES - MLA_ROPE), F32)
    cc = jnp.concatenate([cos, cos, z], axis=-1)
    ss = jnp.concatenate([-sin, sin, z], axis=-1)
    full = lambda a: pl.BlockSpec(a.shape, lambda i: (0,) * a.ndim)
    qn = q_norm.reshape(1, -1)
    kvn = kv_norm.reshape(1, -1)
    head_map = lambda i: (i // n_s, 0, i % n_s, 0)
    return pl.pallas_call(
        _mla_proj_kernel,
        grid=(T // tm,),
        in_specs=[pl.BlockSpec((tm, D_MODEL), lambda i: (i, 0)),
                  full(win_ext), full(qn), full(kvn), full(wuq_ext), full(wukv_ext),
                  pl.BlockSpec((tm, LANES), lambda i: (i % n_s, 0)),
                  pl.BlockSpec((tm, LANES), lambda i: (i % n_s, 0))],
        out_specs=[pl.BlockSpec((1, MLA_HEADS, tm, MLA_QK_PAD), head_map),
                   pl.BlockSpec((1, MLA_HEADS, tm, MLA_QK_PAD), head_map),
                   pl.BlockSpec((1, MLA_HEADS, tm, MLA_V), head_map)],
        out_shape=[jax.ShapeDtypeStruct((B, MLA_HEADS, S, MLA_QK_PAD), BF16),
                   jax.ShapeDtypeStruct((B, MLA_HEADS, S, MLA_QK_PAD), BF16),
                   jax.ShapeDtypeStruct((B, MLA_HEADS, S, MLA_V), BF16)],
        compiler_params=_cparams("parallel"),
        name="mla_proj",
    )(xt, win_ext, qn, kvn, wuq_ext, wukv_ext, cc, ss)


MLA_HEADS_PER_STEP = 4
KEY_CLASSES = 4


def _mla_attn_block(q_ref, k_ref, v_ref, o_ref, blk, *, tq, L, scale):
    row = lax.broadcasted_iota(I32, (tq, L), 0)
    col = lax.broadcasted_iota(I32, (tq, L), 1)
    adm = col // CHUNK <= (blk * tq + row) // CHUNK
    for h in range(MLA_HEADS_PER_STEP):
        s = _dot_nt(q_ref[0, h], k_ref[0, h, 0:L, :]) * scale
        s = jnp.where(adm, s, NEG_INF)
        m = jnp.max(s, axis=-1, keepdims=True)
        p = jnp.exp(s - m)
        l = jnp.sum(p, axis=-1, keepdims=True)
        o = _dot(p.astype(BF16), v_ref[0, h, 0:L, :])
        o_ref[0, :, h * MLA_V:(h + 1) * MLA_V] = (o / l).astype(BF16)


def _mla_attn_kernel(q_ref, k_ref, v_ref, o_ref, *, tq, S, scale):
    blk = pl.program_id(2)
    width = S // KEY_CLASSES
    cls = ((blk + 1) * tq - 1) // width
    for c in range(KEY_CLASSES):
        @pl.when(cls == c)
        def _(c=c):
            _mla_attn_block(q_ref, k_ref, v_ref, o_ref, blk, tq=tq, L=(c + 1) * width, scale=scale)


def _mla_attn(q, k, v, tq):
    B, H, S, _ = q.shape
    hp = MLA_HEADS_PER_STEP
    scale = (MLA_NOPE + MLA_ROPE) ** -0.5
    return pl.pallas_call(
        functools.partial(_mla_attn_kernel, tq=tq, S=S, scale=scale),
        grid=(B, H // hp, S // tq),
        in_specs=[pl.BlockSpec((1, hp, tq, MLA_QK_PAD), lambda b, h, i: (b, h, i, 0)),
                  pl.BlockSpec((1, hp, S, MLA_QK_PAD), lambda b, h, i: (b, h, 0, 0)),
                  pl.BlockSpec((1, hp, S, MLA_V), lambda b, h, i: (b, h, 0, 0))],
        out_specs=pl.BlockSpec((1, tq, hp * MLA_V), lambda b, h, i: (b, i, h)),
        out_shape=jax.ShapeDtypeStruct((B, S, H * MLA_V), BF16),
        compiler_params=_cparams("parallel", "parallel", "arbitrary"),
        name="mla_attn",
    )(q, k, v)


def _proj_ln_kernel(a_ref, w_ref, x_ref, g_ref, b_ref, o_ref):
    m = _dot(a_ref[...], w_ref[...])
    y = DN_ALPHA * x_ref[...] + m
    o_ref[...] = _layernorm_rows(y, g_ref[...], b_ref[...])


def _proj_ln(a, w, xt, gain, bias, tm):
    T = xt.shape[0]
    wb = w.astype(BF16)
    g = gain.reshape(1, -1)
    b = bias.reshape(1, -1)
    return pl.pallas_call(
        _proj_ln_kernel,
        grid=(T // tm,),
        in_specs=[pl.BlockSpec((tm, a.shape[1]), lambda i: (i, 0)),
                  pl.BlockSpec(wb.shape, lambda i: (0, 0)),
                  pl.BlockSpec((tm, D_MODEL), lambda i: (i, 0)),
                  pl.BlockSpec(g.shape, lambda i: (0, 0)),
                  pl.BlockSpec(b.shape, lambda i: (0, 0))],
        out_specs=pl.BlockSpec((tm, D_MODEL), lambda i: (i, 0)),
        out_shape=jax.ShapeDtypeStruct((T, D_MODEL), F32),
        compiler_params=_cparams("parallel"),
        name="proj_ln",
    )(a, wb, xt, g, b)


def _dsa_proj_kernel(x_ref, w_ref, ch_ref, sh_ref, ci_ref, ck_ref, sk_ref,
                     q_ref, k_ref, v_ref, qi_ref, ki_ref, wi_ref, *, w_scale):
    xb = x_ref[...].astype(BF16)
    h = _dot(xb, w_ref[...])
    ch = ch_ref[...]
    sh = sh_ref[...]
    ci = ci_ref[...]
    half = LANES // 2
    for hd in range(DSA_HEADS):
        c0 = hd * DSA_HEAD_DIM
        qh = h[:, c0:c0 + LANES]
        q_ref[0, :, c0:c0 + LANES] = (qh * ch + pltpu.roll(qh, half, 1) * sh).astype(BF16)
        kh = h[:, DSA_HD + c0:DSA_HD + c0 + LANES]
        k_ref[0, :, c0:c0 + LANES] = (kh * ch + pltpu.roll(kh, half, 1) * sh).astype(BF16)
        t = h[:, 3 * DSA_HD + c0:3 * DSA_HD + c0 + LANES] * ci
        qi_ref[0, :, c0:c0 + LANES] = (t + pltpu.roll(t, half, 1)).astype(BF16)
    v_ref[0] = h[:, 2 * DSA_HD:3 * DSA_HD].astype(BF16)
    o = 4 * DSA_HD
    ki = h[:, o:o + LANES] * ck_ref[...] + h[:, o + LANES:o + 2 * LANES] * sk_ref[...]
    ki_ref[0] = ki.astype(BF16)
    wi_ref[0] = h[:, o + 2 * LANES:o + 3 * LANES] * w_scale


def _dsa_proj(xt, B, S, w_in, tm):
    T = B * S
    n_s = S // tm
    o1 = 3 * DSA_HD + IDX_HEADS * IDX_DIM
    ih = IDX_DIM // 2
    w_qi = w_in[:, 3 * DSA_HD:o1].reshape(D_MODEL, IDX_HEADS, IDX_DIM)
    w_qi_sw = jnp.concatenate([w_qi[..., ih:], w_qi[..., :ih]], axis=-1)
    w_qi_ext = jnp.concatenate([w_qi, w_qi_sw], axis=-1).reshape(D_MODEL, IDX_HEADS * LANES)
    w_ki = w_in[:, o1:o1 + IDX_DIM]
    w_ki_sw = jnp.concatenate([w_ki[:, ih:], w_ki[:, :ih]], axis=-1)
    zpad = lambda w: jnp.concatenate([w, jnp.zeros((w.shape[0], LANES - w.shape[1]), w.dtype)], axis=-1)
    w_ext = jnp.concatenate(
        [w_in[:, :3 * DSA_HD], w_qi_ext, zpad(w_ki), zpad(w_ki_sw), zpad(w_in[:, o1 + IDX_DIM:])],
        axis=-1).astype(BF16)
    cos_h, sin_h = _rope_tables(S, DSA_HEAD_DIM)
    cos_i, sin_i = _rope_tables(S, IDX_DIM)
    ch = jnp.concatenate([cos_h, cos_h], axis=-1)
    sh = jnp.concatenate([-sin_h, sin_h], axis=-1)
    ci = jnp.concatenate([cos_i, cos_i, -sin_i, sin_i], axis=-1)
    z = jnp.zeros((S, LANES - IDX_DIM), F32)
    ck = jnp.concatenate([cos_i, cos_i, z], axis=-1)
    sk = jnp.concatenate([-sin_i, sin_i, z], axis=-1)
    tab = pl.BlockSpec((tm, LANES), lambda i: (i % n_s, 0))
    row_map = lambda i: (i // n_s, i % n_s, 0)
    wide = pl.BlockSpec((1, tm, DSA_HD), row_map)
    narrow = pl.BlockSpec((1, tm, LANES), row_map)
    w_scale = IDX_HEADS ** -0.5 * IDX_DIM ** -0.5
    return pl.pallas_call(
        functools.partial(_dsa_proj_kernel, w_scale=w_scale),
        grid=(T // tm,),
        in_specs=[pl.BlockSpec((tm, D_MODEL), lambda i: (i, 0)),
                  pl.BlockSpec(w_ext.shape, lambda i: (0, 0)),
                  tab, tab, tab, tab, tab],
        out_specs=[wide, wide, wide, wide, narrow, narrow],
        out_shape=[jax.ShapeDtypeStruct((B, S, DSA_HD), BF16),
                   jax.ShapeDtypeStruct((B, S, DSA_HD), BF16),
                   jax.ShapeDtypeStruct((B, S, DSA_HD), BF16),
                   jax.ShapeDtypeStruct((B, S, IDX_HEADS * LANES), BF16),
                   jax.ShapeDtypeStruct((B, S, LANES), BF16),
                   jax.ShapeDtypeStruct((B, S, LANES), F32)],
        compiler_params=_cparams("parallel"),
        name="dsa_proj",
    )(xt, w_ext, ch, sh, ci, ck, sk)


def _dsa_attn_block(q_ref, qi_ref, wi_ref, k_ref, v_ref, ki_ref, o_ref, blk, *, tq, L, topk, scale):
    ki = ki_ref[0, 0:L, :]
    qi = qi_ref[0]
    wi = wi_ref[0]
    score = jnp.zeros((tq, L), F32)
    for hd in range(IDX_HEADS):
        logit = _dot_nt(qi[:, hd * LANES:(hd + 1) * LANES], ki)
        score = score + wi[:, hd:hd + 1] * jnp.maximum(logit, 0.0)
    row = lax.broadcasted_iota(I32, (tq, L), 0)
    col = lax.broadcasted_iota(I32, (tq, L), 1)
    adm = col // CHUNK <= (blk * tq + row) // CHUNK
    score = jnp.where(adm, score + 0.0, NEG_INF)
    bits = lax.bitcast_convert_type(score, I32)
    u = jnp.where(bits < 0, bits ^ 0x7FFFFFFF, bits)

    def search(i, thr):
        cand = thr + lax.shift_left(jnp.int32(1), 31 - i)
        cnt = jnp.sum((u >= cand).astype(F32), axis=-1, keepdims=True)
        return jnp.where(cnt >= topk, cand, thr)
    thr = lax.fori_loop(0, 32, search, jnp.full((tq, 1), INT_MIN, I32), unroll=8)

    gt = u > thr
    eq = u == thr
    need = topk - jnp.sum(gt.astype(F32), axis=-1, keepdims=True)
    r_i = lax.broadcasted_iota(I32, (LANES, LANES), 0)
    c_i = lax.broadcasted_iota(I32, (LANES, LANES), 1)
    tri = (r_i < c_i).astype(BF16)
    eq_b = eq.astype(BF16)
    run = jnp.zeros((tq, 1), F32)
    ranks = []
    for j in range(L // LANES):
        e = eq_b[:, j * LANES:(j + 1) * LANES]
        ranks.append(_dot(e, tri) + run)
        run = run + jnp.sum(e.astype(F32), axis=-1, keepdims=True)
    rank = jnp.concatenate(ranks, axis=-1)
    sel = adm & (gt | (eq & (rank < need)))

    for hd in range(DSA_HEADS):
        c0 = hd * DSA_HEAD_DIM
        s = _dot_nt(q_ref[0, :, c0:c0 + LANES], k_ref[0, 0:L, c0:c0 + LANES]) * scale
        s = jnp.where(sel, s, NEG_INF)
        m = jnp.max(s, axis=-1, keepdims=True)
        p = jnp.exp(s - m)
        l = jnp.sum(p, axis=-1, keepdims=True)
        o = _dot(p.astype(BF16), v_ref[0, 0:L, c0:c0 + LANES])
        o_ref[0, :, c0:c0 + LANES] = (o / l).astype(BF16)


def _dsa_attn_kernel(q_ref, qi_ref, wi_ref, k_ref, v_ref, ki_ref, o_ref, *, tq, S, topk, scale):
    blk = pl.program_id(1)
    width = S // KEY_CLASSES
    cls = ((blk + 1) * tq - 1) // width
    for c in range(KEY_CLASSES):
        @pl.when(cls == c)
        def _(c=c):
            _dsa_attn_block(q_ref, qi_ref, wi_ref, k_ref, v_ref, ki_ref, o_ref, blk,
                            tq=tq, L=(c + 1) * width, topk=topk, scale=scale)


def _dsa_attn(q, k, v, qi, ki, wi, tq):
    B, S, _ = q.shape
    topk = min(DSA_TOPK_MAX, S // 4)
    scale = DSA_HEAD_DIM ** -0.5
    qmap = lambda b, i: (b, i, 0)
    kmap = lambda b, i: (b, 0, 0)
    return pl.pallas_call(
        functools.partial(_dsa_attn_kernel, tq=tq, S=S, topk=topk, scale=scale),
        grid=(B, S // tq),
        in_specs=[pl.BlockSpec((1, tq, DSA_HD), qmap),
                  pl.BlockSpec((1, tq, IDX_HEADS * LANES), qmap),
                  pl.BlockSpec((1, tq, LANES), qmap),
                  pl.BlockSpec((1, S, DSA_HD), kmap),
                  pl.BlockSpec((1, S, DSA_HD), kmap),
                  pl.BlockSpec((1, S, LANES), kmap)],
        out_specs=pl.BlockSpec((1, tq, DSA_HD), qmap),
        out_shape=jax.ShapeDtypeStruct((B, S, DSA_HD), BF16),
        compiler_params=_cparams("parallel", "arbitrary"),
        name="dsa_attn",
    )(q, qi, wi, k, v, ki)


_PEER_PAIRS = [(a, b) for a in range(PEER_TOPK) for b in range(PEER_TOPK // (a + 1))]
_PEER_CAND_ROWS = -(-len(_PEER_PAIRS) // SUBLANES) * SUBLANES


def _sort_network(n):
    pairs = []
    p = 1
    while p < n:
        k = p
        while k >= 1:
            for j in range(k % p, n - k, 2 * k):
                for i in range(min(k, n - j - k)):
                    if (i + j) // (2 * p) == (i + j + k) // (2 * p):
                        pairs.append((i + j, i + j + k))
            k //= 2
        p *= 2
    return pairs


def _top_rows(s, n):
    groups = PEER_NKEYS // SUBLANES
    v = [s[g * SUBLANES:(g + 1) * SUBLANES, :] for g in range(groups)]
    for i, j in _sort_network(groups):
        v[i], v[j] = jnp.maximum(v[i], v[j]), jnp.minimum(v[i], v[j])
    vals = []
    for k in range(n):
        m = jnp.max(v[0], axis=0, keepdims=True)
        vals.append(m)
        won = v[0] == m
        for d in range(groups - 1 - k):
            v[d] = jnp.where(won, v[d + 1], v[d])
    return vals


def _peer_route_kernel(x_ref, wq_ref, keys_ref, lim_ref, e1_ref, r2_ref, e2_ref, cand_ref):
    tt = x_ref.shape[0]
    q = _dot(x_ref[...].astype(BF16), wq_ref[...]).astype(BF16)
    half = PEER_QDIM // 2
    pad0 = _PEER_CAND_ROWS - SUBLANES
    cand_ref[pad0:, :] = jnp.full((SUBLANES, tt), NEG_INF, F32)
    for hd in range(PEER_HEADS):
        c0 = hd * PEER_QDIM
        s1 = _dot_nt(keys_ref[0, hd], q[:, c0:c0 + half])
        s2 = _dot_nt(keys_ref[1, hd], q[:, c0 + half:c0 + PEER_QDIM])
        v1 = _top_rows(s1, PEER_TOPK)
        v2 = _top_rows(s2, PEER_TOPK)
        for k, (a, b) in enumerate(_PEER_PAIRS):
            cand_ref[k:k + 1, :] = v1[a] + v2[b]
        cand = cand_ref[...]
        cur = cand
        theta = None
        for _ in range(PEER_TOPK):
            theta = jnp.max(cur, axis=0, keepdims=True)
            cur = jnp.where(cur == theta, NEG_INF, cur)
        top = v1[0] + v2[0]
        z = jnp.sum(jnp.where(cand >= theta, jnp.exp(cand - top), 0.0), axis=0, keepdims=True)
        r2 = jnp.full(s2.shape, float(PEER_TOPK), F32)
        for b in reversed(range(PEER_TOPK)):
            r2 = jnp.where(s2 >= v2[b], float(b), r2)
        lim = jnp.zeros(s1.shape, F32)
        for b in range(PEER_TOPK):
            lim = jnp.where(s1 + v2[b] >= theta, float(b + 1), lim)
        lim_ref[hd] = lim
        r2_ref[hd] = r2.astype(BF16)
        e1_ref[hd] = jnp.exp(s1 - v1[0]) / z
        e2_ref[hd] = jnp.exp(s2 - v2[0]).astype(BF16)


def _peer_route(xt, w_q, sub_keys, tt):
    T = xt.shape[0]
    wq = w_q.astype(BF16)
    keys = sub_keys.astype(BF16)
    big = jax.ShapeDtypeStruct((PEER_HEADS, PEER_NKEYS, T), F32)
    big_b = jax.ShapeDtypeStruct((PEER_HEADS, PEER_NKEYS, T), BF16)
    big_spec = pl.BlockSpec((PEER_HEADS, PEER_NKEYS, tt), lambda i: (0, 0, i))
    return pl.pallas_call(
        _peer_route_kernel,
        grid=(T // tt,),
        in_specs=[pl.BlockSpec((tt, D_MODEL), lambda i: (i, 0)),
                  pl.BlockSpec(wq.shape, lambda i: (0, 0)),
                  pl.BlockSpec(keys.shape, lambda i: (0, 0, 0, 0))],
        out_specs=[big_spec, big_spec, big_spec, big_spec],
        out_shape=[big, big, big_b, big_b],
        scratch_shapes=[pltpu.VMEM((_PEER_CAND_ROWS, tt), F32)],
        compiler_params=_cparams("parallel"),
        name="peer_route",
    )(xt, wq, keys)


PEER_CHUNK = 2 * PEER_NKEYS


BF16_SUBLANES = 2 * SUBLANES


def _dup_bf16_words(v):
    u = lax.bitcast_convert_type(v.astype(BF16).astype(F32), jnp.uint32)
    return u | (u >> 16)


def _row_bf16(ref, i1, hd, ls):
    words = jnp.broadcast_to(ref[i1, hd:hd + 1, ls], (SUBLANES, LANES))
    row = pltpu.bitcast(words, BF16)
    return jnp.tile(row, (PEER_NKEYS // BF16_SUBLANES, 1))


def _peer_dense_kernel(x_ref, wd_ref, wu_ref, lim_ref, e1_ref, r2_ref, e2_ref, g_ref, b_ref,
                       o_ref, xb_ref, p_ref, acc_ref, limr_ref, e1r_ref, r2s_ref, e2s_ref, *, te, tt):
    e = pl.program_id(1)

    @pl.when(e == 0)
    def _():
        xb_ref[...] = x_ref[...].astype(BF16)
        acc_ref[...] = jnp.zeros_like(acc_ref)
        r2s_ref[...] = r2_ref[...]
        e2s_ref[...] = e2_ref[...]

        def relayout(g, carry):
            g0 = pl.multiple_of(g * SUBLANES, SUBLANES)
            for hd in range(PEER_HEADS):
                limg = _dup_bf16_words(lim_ref[hd, pl.ds(g0, SUBLANES), :])
                e1g = _dup_bf16_words(e1_ref[hd, pl.ds(g0, SUBLANES), :])
                for r in range(SUBLANES):
                    limr_ref[g0 + r, hd:hd + 1, :] = limg[r:r + 1, :]
                    e1r_ref[g0 + r, hd:hd + 1, :] = e1g[r:r + 1, :]
            return carry
        lax.fori_loop(0, PEER_NKEYS // SUBLANES, relayout, 0)

    xb = xb_ref[...]
    n_ch = te // PEER_CHUNK
    i1_0 = e * (te // PEER_NKEYS)
    a_next = _dot_nt(wd_ref[0:PEER_CHUNK, :], xb)
    for c in range(n_ch):
        a_cur = a_next
        if c + 1 < n_ch:
            a_next = _dot_nt(wd_ref[(c + 1) * PEER_CHUNK:(c + 2) * PEER_CHUNK, :], xb)
        for j in range(PEER_CHUNK // PEER_NKEYS):
            i1 = i1_0 + c * (PEER_CHUNK // PEER_NKEYS) + j
            r0 = c * PEER_CHUNK + j * PEER_NKEYS
            for lg in range(tt // LANES):
                ls = slice(lg * LANES, (lg + 1) * LANES)
                gate = jnp.zeros((PEER_NKEYS, LANES), BF16)
                zero = jnp.zeros((), BF16)
                for hd in range(PEER_HEADS):
                    limb = _row_bf16(limr_ref, i1, hd, ls)
                    e1b = _row_bf16(e1r_ref, i1, hd, ls)
                    gate = gate + jnp.where(r2s_ref[hd, :, ls] < limb, e2s_ref[hd, :, ls] * e1b, zero)
                a = a_cur[j * PEER_NKEYS:(j + 1) * PEER_NKEYS, ls]
                act = 0.5 * a * (1.0 + lax.erf(a * (2.0 ** -0.5)))
                p_ref[r0:r0 + PEER_NKEYS, ls] = (act * gate.astype(F32)).astype(BF16)
        acc_ref[...] += _dot(wu_ref[c], p_ref[c * PEER_CHUNK:(c + 1) * PEER_CHUNK, :])

    @pl.when(e == pl.num_programs(1) - 1)
    def _():
        y = DN_ALPHA * x_ref[...] + acc_ref[...].T
        o_ref[...] = _layernorm_rows(y, g_ref[...], b_ref[...])


def _peer_dense(xt, w_down, w_up, route, gain, bias, tt, te):
    T = xt.shape[0]
    c1, e1, s2, e2 = route
    wd = w_down.astype(BF16)
    wu = w_up.astype(BF16).reshape(PEER_EXPERTS // PEER_CHUNK, PEER_CHUNK, D_MODEL).transpose(0, 2, 1)
    g = gain.reshape(1, -1)
    b = bias.reshape(1, -1)
    big_spec = pl.BlockSpec((PEER_HEADS, PEER_NKEYS, tt), lambda i, e: (0, 0, i))
    return pl.pallas_call(
        functools.partial(_peer_dense_kernel, te=te, tt=tt),
        grid=(T // tt, PEER_EXPERTS // te),
        in_specs=[pl.BlockSpec((tt, D_MODEL), lambda i, e: (i, 0)),
                  pl.BlockSpec((te, D_MODEL), lambda i, e: (e, 0)),
                  pl.BlockSpec((te // PEER_CHUNK, D_MODEL, PEER_CHUNK), lambda i, e: (e, 0, 0)),
                  big_spec, big_spec, big_spec, big_spec,
                  pl.BlockSpec(g.shape, lambda i, e: (0, 0)),
                  pl.BlockSpec(b.shape, lambda i, e: (0, 0))],
        out_specs=pl.BlockSpec((tt, D_MODEL), lambda i, e: (i, 0)),
        out_shape=jax.ShapeDtypeStruct((T, D_MODEL), F32),
        scratch_shapes=[pltpu.VMEM((tt, D_MODEL), BF16),
                        pltpu.VMEM((te, tt), BF16),
                        pltpu.VMEM((D_MODEL, tt), F32),
                        pltpu.VMEM((PEER_NKEYS, PEER_HEADS, tt), jnp.uint32),
                        pltpu.VMEM((PEER_NKEYS, PEER_HEADS, tt), jnp.uint32),
                        pltpu.VMEM((PEER_HEADS, PEER_NKEYS, tt), BF16),
                        pltpu.VMEM((PEER_HEADS, PEER_NKEYS, tt), BF16)],
        compiler_params=_cparams("parallel", "arbitrary"),
        name="peer_dense",
    )(xt, wd, wu, c1, e1, s2, e2, g, b)


def _rope_tables(seq, dim):
    inv = ROPE_THETA ** (-jnp.arange(0, dim, 2, dtype=F32) / dim)
    ang = jnp.arange(seq, dtype=F32)[:, None] * inv[None, :]
    return jnp.cos(ang), jnp.sin(ang)


def _tiles(S, T):
    row = min(256, S)
    tq_mla = min(256, S)
    tq_dsa = min(256, S)
    tt_route = min(256, T)
    tt_dense = min(512, T)
    te_dense = 2048
    return row, tq_mla, tq_dsa, tt_route, tt_dense, te_dense


def _peer_layer(xt, w_q, sub_keys, w_down, w_up, gain, bias, tt_route, tt_dense, te_dense):
    route = _peer_route(xt, w_q, sub_keys, tt_route)
    return _peer_dense(xt, w_down, w_up, route, gain, bias, tt_dense, te_dense)


def kernel(x, mla_w_in, mla_q_norm, mla_kv_norm, mla_w_uq, mla_w_ukv, mla_w_o,
           dsa_w_in, dsa_w_o, peer_w_q, peer_sub_keys, peer_w_down, peer_w_up,
           ln_gain, ln_bias):
    B, S, D = x.shape
    T = B * S
    row, tq_mla, tq_dsa, tt_route, tt_dense, te_dense = _tiles(S, T)
    xt = x.reshape(T, D)

    cos, sin = _rope_tables(S, MLA_ROPE)
    q, k, v = _mla_proj(xt, B, S, mla_w_in[0], mla_q_norm[0], mla_kv_norm[0],
                        mla_w_uq[0], mla_w_ukv[0], cos, sin, row)
    o = _mla_attn(q, k, v, tq_mla).reshape(T, MLA_HEADS * MLA_V)
    xt = _proj_ln(o, mla_w_o[0], xt, ln_gain[0, 0], ln_bias[0, 0], row)
    xt = _peer_layer(xt, peer_w_q[0], peer_sub_keys[0], peer_w_down[0], peer_w_up[0],
                     ln_gain[0, 1], ln_bias[0, 1], tt_route, tt_dense, te_dense)

    q, k, v, qi, ki, wi = _dsa_proj(xt, B, S, dsa_w_in[0], row)
    o = _dsa_attn(q, k, v, qi, ki, wi, tq_dsa).reshape(T, DSA_HD)
    xt = _proj_ln(o, dsa_w_o[0], xt, ln_gain[1, 0], ln_bias[1, 0], row)
    xt = _peer_layer(xt, peer_w_q[1], peer_sub_keys[1], peer_w_down[1], peer_w_up[1],
                     ln_gain[1, 1], ln_bias[1, 1], tt_route, tt_dense, te_dense)
    return xt.reshape(B, S, D)
```

```python
import functools

import jax
import jax.numpy as jnp
from jax import lax
from jax.experimental import pallas as pl
from jax.experimental.pallas import tpu as pltpu

F32 = jnp.float32
BF16 = jnp.bfloat16
I32 = jnp.int32

D_MODEL = 1024
DEPTH = 2
CHUNK = 64
ROPE_THETA = 10000.0
LN_EPS = 1e-5
RMS_EPS = 1e-6
DN_ALPHA = (2 * DEPTH) ** 0.25

MLA_HEADS = 8
MLA_NOPE = 128
MLA_ROPE = 64
MLA_V = 128
MLA_Q_RANK = 384
MLA_KV_RANK = 256
MLA_QK_PAD = 256

DSA_HEADS = 8
DSA_HEAD_DIM = 128
IDX_HEADS = 8
IDX_DIM = 64
DSA_TOPK_MAX = 256
DSA_HD = DSA_HEADS * DSA_HEAD_DIM

PEER_HEADS = 8
PEER_NKEYS = 128
PEER_EXPERTS = PEER_NKEYS * PEER_NKEYS
PEER_QDIM = 256
PEER_TOPK = 16

LANES = 128
SUBLANES = 8
VMEM_LIMIT = 56 * 1024 * 1024

NEG_INF = float("-inf")
INT_MIN = -2 ** 31


def _cparams(*sem):
    return pltpu.CompilerParams(dimension_semantics=sem, vmem_limit_bytes=VMEM_LIMIT)


def _dot(a, b):
    return jnp.dot(a, b, preferred_element_type=F32)


def _dot_nt(a, b):
    return lax.dot_general(a, b, (((1,), (1,)), ((), ())), preferred_element_type=F32)


def _layernorm_rows(y, g, b):
    mu = jnp.mean(y, axis=-1, keepdims=True)
    yc = y - mu
    var = jnp.mean(yc * yc, axis=-1, keepdims=True)
    return yc * lax.rsqrt(var + LN_EPS) * g + b


def _rms_rows(h, g):
    ms = jnp.mean(h * h, axis=-1, keepdims=True)
    return h * lax.rsqrt(ms + RMS_EPS) * g


def _mla_proj_kernel(x_ref, win_ref, qn_ref, kvn_ref, wuq_ref, wukv_ref, cc_ref, ss_ref,
                     q_ref, k_ref, v_ref):
    xb = x_ref[...].astype(BF16)
    h = _dot(xb, win_ref[...])
    cq = h[:, :MLA_Q_RANK]
    ckv = h[:, MLA_Q_RANK:MLA_Q_RANK + MLA_KV_RANK]
    o = MLA_Q_RANK + MLA_KV_RANK
    cc = cc_ref[...]
    ss = ss_ref[...]
    k_rope = h[:, o:o + LANES] * cc + h[:, o + LANES:o + 2 * LANES] * ss
    qall = _dot(_rms_rows(cq, qn_ref[...]).astype(BF16), wuq_ref[...])
    kvall = _dot(_rms_rows(ckv, kvn_ref[...]).astype(BF16), wukv_ref[...])
    k_rope_b = k_rope.astype(BF16)
    for hd in range(MLA_HEADS):
        b0 = hd * 3 * LANES
        q_ref[0, hd, :, 0:LANES] = qall[:, b0:b0 + LANES].astype(BF16)
        q_rope = qall[:, b0 + LANES:b0 + 2 * LANES] * cc + qall[:, b0 + 2 * LANES:b0 + 3 * LANES] * ss
        q_ref[0, hd, :, LANES:2 * LANES] = q_rope.astype(BF16)
        k_ref[0, hd, :, 0:LANES] = kvall[:, hd * LANES:(hd + 1) * LANES].astype(BF16)
        k_ref[0, hd, :, LANES:2 * LANES] = k_rope_b
        v0 = MLA_HEADS * MLA_NOPE + hd * MLA_V
        v_ref[0, hd] = kvall[:, v0:v0 + MLA_V].astype(BF16)


def _mla_proj(xt, B, S, w_in, q_norm, kv_norm, w_uq, w_ukv, cos, sin, tm):
    T = B * S
    n_s = S // tm
    half = MLA_ROPE // 2
    def swap_rope(w):
        return jnp.concatenate([w[..., half:], w[..., :half]], axis=-1)
    zpad = lambda w: jnp.concatenate([w, jnp.zeros_like(w)], axis=-1)
    w_kr = w_in[:, MLA_Q_RANK + MLA_KV_RANK:]
    win_ext = jnp.concatenate(
        [w_in[:, :MLA_Q_RANK + MLA_KV_RANK], zpad(w_kr), zpad(swap_rope(w_kr))], axis=-1).astype(BF16)
    uq_nope = w_uq[:, :, :MLA_NOPE]
    uq_rope = w_uq[:, :, MLA_NOPE:]
    wuq_ext = jnp.concatenate([uq_nope, zpad(uq_rope), zpad(swap_rope(uq_rope))], axis=-1)
    wuq_ext = wuq_ext.reshape(MLA_Q_RANK, MLA_HEADS * 3 * LANES).astype(BF16)
    wukv_ext = jnp.concatenate(
        [w_ukv[:, :, :MLA_NOPE].reshape(MLA_KV_RANK, -1), w_ukv[:, :, MLA_NOPE:].reshape(MLA_KV_RANK, -1)],
        axis=-1).astype(BF16)
    z = jnp.zeros((S, LANES - MLA_ROPE), F32)
    cc = jnp.concatenate([cos, cos, z], axis=-1)
    ss = jnp.concatenate([-sin, sin, z], axis=-1)
    full = lambda a: pl.BlockSpec(a.shape, lambda i: (0,) * a.ndim)
    qn = q_norm.reshape(1, -1)
    kvn = kv_norm.reshape(1, -1)
    head_map = lambda i: (i // n_s, 0, i % n_s, 0)
    return pl.pallas_call(
        _mla_proj_kernel,
        grid=(T // tm,),
        in_specs=[pl.BlockSpec((tm, D_MODEL), lambda i: (i, 0)),
                  full(win_ext), full(qn), full(kvn), full(wuq_ext), full(wukv_ext),
                  pl.BlockSpec((tm, LANES), lambda i: (i % n_s, 0)),
                  pl.BlockSpec((tm, LANES), lambda i: (i % n_s, 0))],
        out_specs=[pl.BlockSpec((1, MLA_HEADS, tm, MLA_QK_PAD), head_map),
                   pl.BlockSpec((1, MLA_HEADS, tm, MLA_QK_PAD), head_map),
                   pl.BlockSpec((1, MLA_HEADS, tm, MLA_V), head_map)],
        out_shape=[jax.ShapeDtypeStruct((B, MLA_HEADS, S, MLA_QK_PAD), BF16),
                   jax.ShapeDtypeStruct((B, MLA_HEADS, S, MLA_QK_PAD), BF16),
                   jax.ShapeDtypeStruct((B, MLA_HEADS, S, MLA_V), BF16)],
        compiler_params=_cparams("parallel"),
        name="mla_proj",
    )(xt, win_ext, qn, kvn, wuq_ext, wukv_ext, cc, ss)


MLA_HEADS_PER_STEP = 4
KEY_CLASSES = 4


def _mla_attn_block(q_ref, k_ref, v_ref, o_ref, blk, *, tq, L, scale):
    row = lax.broadcasted_iota(I32, (tq, L), 0)
    col = lax.broadcasted_iota(I32, (tq, L), 1)
    adm = col // CHUNK <= (blk * tq + row) // CHUNK
    for h in range(MLA_HEADS_PER_STEP):
        s = _dot_nt(q_ref[0, h], k_ref[0, h, 0:L, :]) * scale
        s = jnp.where(adm, s, NEG_INF)
        m = jnp.max(s, axis=-1, keepdims=True)
        p = jnp.exp(s - m)
        l = jnp.sum(p, axis=-1, keepdims=True)
        o = _dot(p.astype(BF16), v_ref[0, h, 0:L, :])
        o_ref[0, :, h * MLA_V:(h + 1) * MLA_V] = (o / l).astype(BF16)


def _mla_attn_kernel(q_ref, k_ref, v_ref, o_ref, *, tq, S, scale):
    blk = pl.program_id(2)
    width = S // KEY_CLASSES
    cls = ((blk + 1) * tq - 1) // width
    for c in range(KEY_CLASSES):
        @pl.when(cls == c)
        def _(c=c):
            _mla_attn_block(q_ref, k_ref, v_ref, o_ref, blk, tq=tq, L=(c + 1) * width, scale=scale)


def _mla_attn(q, k, v, tq):
    B, H, S, _ = q.shape
    hp = MLA_HEADS_PER_STEP
    scale = (MLA_NOPE + MLA_ROPE) ** -0.5
    return pl.pallas_call(
        functools.partial(_mla_attn_kernel, tq=tq, S=S, scale=scale),
        grid=(B, H // hp, S // tq),
        in_specs=[pl.BlockSpec((1, hp, tq, MLA_QK_PAD), lambda b, h, i: (b, h, i, 0)),
                  pl.BlockSpec((1, hp, S, MLA_QK_PAD), lambda b, h, i: (b, h, 0, 0)),
                  pl.BlockSpec((1, hp, S, MLA_V), lambda b, h, i: (b, h, 0, 0))],
        out_specs=pl.BlockSpec((1, tq, hp * MLA_V), lambda b, h, i: (b, i, h)),
        out_shape=jax.ShapeDtypeStruct((B, S, H * MLA_V), BF16),
        compiler_params=_cparams("parallel", "parallel", "arbitrary"),
        name="mla_attn",
    )(q, k, v)


def _proj_ln_kernel(a_ref, w_ref, x_ref, g_ref, b_ref, o_ref):
    m = _dot(a_ref[...], w_ref[...])
    y = DN_ALPHA * x_ref[...] + m
    o_ref[...] = _layernorm_rows(y, g_ref[...], b_ref[...])


def _proj_ln(a, w, xt, gain, bias, tm):
    T = xt.shape[0]
    wb = w.astype(BF16)
    g = gain.reshape(1, -1)
    b = bias.reshape(1, -1)
    return pl.pallas_call(
        _proj_ln_kernel,
        grid=(T // tm,),
        in_specs=[pl.BlockSpec((tm, a.shape[1]), lambda i: (i, 0)),
                  pl.BlockSpec(wb.shape, lambda i: (0, 0)),
                  pl.BlockSpec((tm, D_MODEL), lambda i: (i, 0)),
                  pl.BlockSpec(g.shape, lambda i: (0, 0)),
                  pl.BlockSpec(b.shape, lambda i: (0, 0))],
        out_specs=pl.BlockSpec((tm, D_MODEL), lambda i: (i, 0)),
        out_shape=jax.ShapeDtypeStruct((T, D_MODEL), F32),
        compiler_params=_cparams("parallel"),
        name="proj_ln",
    )(a, wb, xt, g, b)


def _dsa_proj_kernel(x_ref, w_ref, ch_ref, sh_ref, ci_ref, ck_ref, sk_ref,
                     q_ref, k_ref, v_ref, qi_ref, ki_ref, wi_ref, *, w_scale):
    xb = x_ref[...].astype(BF16)
    h = _dot(xb, w_ref[...])
    ch = ch_ref[...]
    sh = sh_ref[...]
    ci = ci_ref[...]
    half = LANES // 2
    for hd in range(DSA_HEADS):
        c0 = hd * DSA_HEAD_DIM
        qh = h[:, c0:c0 + LANES]
        q_ref[0, :, c0:c0 + LANES] = (qh * ch + pltpu.roll(qh, half, 1) * sh).astype(BF16)
        kh = h[:, DSA_HD + c0:DSA_HD + c0 + LANES]
        k_ref[0, :, c0:c0 + LANES] = (kh * ch + pltpu.roll(kh, half, 1) * sh).astype(BF16)
        t = h[:, 3 * DSA_HD + c0:3 * DSA_HD + c0 + LANES] * ci
        qi_ref[0, :, c0:c0 + LANES] = (t + pltpu.roll(t, half, 1)).astype(BF16)
    v_ref[0] = h[:, 2 * DSA_HD:3 * DSA_HD].astype(BF16)
    o = 4 * DSA_HD
    ki = h[:, o:o + LANES] * ck_ref[...] + h[:, o + LANES:o + 2 * LANES] * sk_ref[...]
    ki_ref[0] = ki.astype(BF16)
    wi_ref[0] = h[:, o + 2 * LANES:o + 3 * LANES] * w_scale


def _dsa_proj(xt, B, S, w_in, tm):
    T = B * S
    n_s = S // tm
    o1 = 3 * DSA_HD + IDX_HEADS * IDX_DIM
    ih = IDX_DIM // 2
    w_qi = w_in[:, 3 * DSA_HD:o1].reshape(D_MODEL, IDX_HEADS, IDX_DIM)
    w_qi_sw = jnp.concatenate([w_qi[..., ih:], w_qi[..., :ih]], axis=-1)
    w_qi_ext = jnp.concatenate([w_qi, w_qi_sw], axis=-1).reshape(D_MODEL, IDX_HEADS * LANES)
    w_ki = w_in[:, o1:o1 + IDX_DIM]
    w_ki_sw = jnp.concatenate([w_ki[:, ih:], w_ki[:, :ih]], axis=-1)
    zpad = lambda w: jnp.concatenate([w, jnp.zeros((w.shape[0], LANES - w.shape[1]), w.dtype)], axis=-1)
    w_ext = jnp.concatenate(
        [w_in[:, :3 * DSA_HD], w_qi_ext, zpad(w_ki), zpad(w_ki_sw), zpad(w_in[:, o1 + IDX_DIM:])],
        axis=-1).astype(BF16)
    cos_h, sin_h = _rope_tables(S, DSA_HEAD_DIM)
    cos_i, sin_i = _rope_tables(S, IDX_DIM)
    ch = jnp.concatenate([cos_h, cos_h], axis=-1)
    sh = jnp.concatenate([-sin_h, sin_h], axis=-1)
    ci = jnp.concatenate([cos_i, cos_i, -sin_i, sin_i], axis=-1)
    z = jnp.zeros((S, LANES - IDX_DIM), F32)
    ck = jnp.concatenate([cos_i, cos_i, z], axis=-1)
    sk = jnp.concatenate([-sin_i, sin_i, z], axis=-1)
    tab = pl.BlockSpec((tm, LANES), lambda i: (i % n_s, 0))
    row_map = lambda i: (i // n_s, i % n_s, 0)
    wide = pl.BlockSpec((1, tm, DSA_HD), row_map)
    narrow = pl.BlockSpec((1, tm, LANES), row_map)
    w_scale = IDX_HEADS ** -0.5 * IDX_DIM ** -0.5
    return pl.pallas_call(
        functools.partial(_dsa_proj_kernel, w_scale=w_scale),
        grid=(T // tm,),
        in_specs=[pl.BlockSpec((tm, D_MODEL), lambda i: (i, 0)),
                  pl.BlockSpec(w_ext.shape, lambda i: (0, 0)),
                  tab, tab, tab, tab, tab],
        out_specs=[wide, wide, wide, wide, narrow, narrow],
        out_shape=[jax.ShapeDtypeStruct((B, S, DSA_HD), BF16),
                   jax.ShapeDtypeStruct((B, S, DSA_HD), BF16),
                   jax.ShapeDtypeStruct((B, S, DSA_HD), BF16),
                   jax.ShapeDtypeStruct((B, S, IDX_HEADS * LANES), BF16),
                   jax.ShapeDtypeStruct((B, S, LANES), BF16),
                   jax.ShapeDtypeStruct((B, S, LANES), F32)],
        compiler_params=_cparams("parallel"),
        name="dsa_proj",
    )(xt, w_ext, ch, sh, ci, ck, sk)


def _dsa_attn_block(q_ref, qi_ref, wi_ref, k_ref, v_ref, ki_ref, o_ref, blk, *, tq, L, topk, scale):
    ki = ki_ref[0, 0:L, :]
    qi = qi_ref[0]
    wi = wi_ref[0]
    score = jnp.zeros((tq, L), F32)
    for hd in range(IDX_HEADS):
        logit = _dot_nt(qi[:, hd * LANES:(hd + 1) * LANES], ki)
        score = score + wi[:, hd:hd + 1] * jnp.maximum(logit, 0.0)
    row = lax.broadcasted_iota(I32, (tq, L), 0)
    col = lax.broadcasted_iota(I32, (tq, L), 1)
    adm = col // CHUNK <= (blk * tq + row) // CHUNK
    score = jnp.where(adm, score + 0.0, NEG_INF)
    bits = lax.bitcast_convert_type(score, I32)
    u = jnp.where(bits < 0, bits ^ 0x7FFFFFFF, bits)

    def search(i, thr):
        cand = thr + lax.shift_left(jnp.int32(1), 31 - i)
        cnt = jnp.sum((u >= cand).astype(F32), axis=-1, keepdims=True)
        return jnp.where(cnt >= topk, cand, thr)
    thr = lax.fori_loop(0, 32, search, jnp.full((tq, 1), INT_MIN, I32), unroll=8)

    gt = u > thr
    eq = u == thr
    need = topk - jnp.sum(gt.astype(F32), axis=-1, keepdims=True)
    r_i = lax.broadcasted_iota(I32, (LANES, LANES), 0)
    c_i = lax.broadcasted_iota(I32, (LANES, LANES), 1)
    tri = (r_i < c_i).astype(BF16)
    eq_b = eq.astype(BF16)
    run = jnp.zeros((tq, 1), F32)
    ranks = []
    for j in range(L // LANES):
        e = eq_b[:, j * LANES:(j + 1) * LANES]
        ranks.append(_dot(e, tri) + run)
        run = run + jnp.sum(e.astype(F32), axis=-1, keepdims=True)
    rank = jnp.concatenate(ranks, axis=-1)
    sel = adm & (gt | (eq & (rank < need)))

    for hd in range(DSA_HEADS):
        c0 = hd * DSA_HEAD_DIM
        s = _dot_nt(q_ref[0, :, c0:c0 + LANES], k_ref[0, 0:L, c0:c0 + LANES]) * scale
        s = jnp.where(sel, s, NEG_INF)
        m = jnp.max(s, axis=-1, keepdims=True)
        p = jnp.exp(s - m)
        l = jnp.sum(p, axis=-1, keepdims=True)
        o = _dot(p.astype(BF16), v_ref[0, 0:L, c0:c0 + LANES])
        o_ref[0, :, c0:c0 + LANES] = (o / l).astype(BF16)


def _dsa_attn_kernel(q_ref, qi_ref, wi_ref, k_ref, v_ref, ki_ref, o_ref, *, tq, S, topk, scale):
    blk = pl.program_id(1)
    width = S // KEY_CLASSES
    cls = ((blk + 1) * tq - 1) // width
    for c in range(KEY_CLASSES):
        @pl.when(cls == c)
        def _(c=c):
            _dsa_attn_block(q_ref, qi_ref, wi_ref, k_ref, v_ref, ki_ref, o_ref, blk,
                            tq=tq, L=(c + 1) * width, topk=topk, scale=scale)


def _dsa_attn(q, k, v, qi, ki, wi, tq):
    B, S, _ = q.shape
    topk = min(DSA_TOPK_MAX, S // 4)
    scale = DSA_HEAD_DIM ** -0.5
    qmap = lambda b, i: (b, i, 0)
    kmap = lambda b, i: (b, 0, 0)
    return pl.pallas_call(
        functools.partial(_dsa_attn_kernel, tq=tq, S=S, topk=topk, scale=scale),
        grid=(B, S // tq),
        in_specs=[pl.BlockSpec((1, tq, DSA_HD), qmap),
                  pl.BlockSpec((1, tq, IDX_HEADS * LANES), qmap),
                  pl.BlockSpec((1, tq, LANES), qmap),
                  pl.BlockSpec((1, S, DSA_HD), kmap),
                  pl.BlockSpec((1, S, DSA_HD), kmap),
                  pl.BlockSpec((1, S, LANES), kmap)],
        out_specs=pl.BlockSpec((1, tq, DSA_HD), qmap),
        out_shape=jax.ShapeDtypeStruct((B, S, DSA_HD), BF16),
        compiler_params=_cparams("parallel", "arbitrary"),
        name="dsa_attn",
    )(q, qi, wi, k, v, ki)


_PEER_PAIRS = [(a, b) for a in range(PEER_TOPK) for b in range(PEER_TOPK // (a + 1))]
_PEER_CAND_ROWS = -(-len(_PEER_PAIRS) // SUBLANES) * SUBLANES


def _sort_network(n):
    pairs = []
    p = 1
    while p < n:
        k = p
        while k >= 1:
            for j in range(k % p, n - k, 2 * k):
                for i in range(min(k, n - j - k)):
                    if (i + j) // (2 * p) == (i + j + k) // (2 * p):
                        pairs.append((i + j, i + j + k))
            k //= 2
        p *= 2
    return pairs


def _top_rows(s, n):
    groups = PEER_NKEYS // SUBLANES
    v = [s[g * SUBLANES:(g + 1) * SUBLANES, :] for g in range(groups)]
    for i, j in _sort_network(groups):
        v[i], v[j] = jnp.maximum(v[i], v[j]), jnp.minimum(v[i], v[j])
    vals = []
    for k in range(n):
        m = jnp.max(v[0], axis=0, keepdims=True)
        vals.append(m)
        won = v[0] == m
        for d in range(groups - 1 - k):
            v[d] = jnp.where(won, v[d + 1], v[d])
    return vals


def _peer_route_kernel(x_ref, wq_ref, keys_ref, lim_ref, e1_ref, r2_ref, e2_ref, cand_ref):
    tt = x_ref.shape[0]
    q = _dot(x_ref[...].astype(BF16), wq_ref[...]).astype(BF16)
    half = PEER_QDIM // 2
    pad0 = _PEER_CAND_ROWS - SUBLANES
    cand_ref[pad0:, :] = jnp.full((SUBLANES, tt), NEG_INF, F32)
    for hd in range(PEER_HEADS):
        c0 = hd * PEER_QDIM
        s1 = _dot_nt(keys_ref[0, hd], q[:, c0:c0 + half])
        s2 = _dot_nt(keys_ref[1, hd], q[:, c0 + half:c0 + PEER_QDIM])
        v1 = _top_rows(s1, PEER_TOPK)
        v2 = _top_rows(s2, PEER_TOPK)
        for k, (a, b) in enumerate(_PEER_PAIRS):
            cand_ref[k:k + 1, :] = v1[a] + v2[b]
        cand = cand_ref[...]
        cur = cand
        theta = None
        for _ in range(PEER_TOPK):
            theta = jnp.max(cur, axis=0, keepdims=True)
            cur = jnp.where(cur == theta, NEG_INF, cur)
        top = v1[0] + v2[0]
        z = jnp.sum(jnp.where(cand >= theta, jnp.exp(cand - top), 0.0), axis=0, keepdims=True)
        r2 = jnp.full(s2.shape, float(PEER_TOPK), F32)
        for b in reversed(range(PEER_TOPK)):
            r2 = jnp.where(s2 >= v2[b], float(b), r2)
        lim = jnp.zeros(s1.shape, F32)
        for b in range(PEER_TOPK):
            lim = jnp.where(s1 + v2[b] >= theta, float(b + 1), lim)
        lim_ref[hd] = lim
        r2_ref[hd] = r2.astype(BF16)
        e1_ref[hd] = jnp.exp(s1 - v1[0]) / z
        e2_ref[hd] = jnp.exp(s2 - v2[0]).astype(BF16)


def _peer_route(xt, w_q, sub_keys, tt):
    T = xt.shape[0]
    wq = w_q.astype(BF16)
    keys = sub_keys.astype(BF16)
    big = jax.ShapeDtypeStruct((PEER_HEADS, PEER_NKEYS, T), F32)
    big_b = jax.ShapeDtypeStruct((PEER_HEADS, PEER_NKEYS, T), BF16)
    big_spec = pl.BlockSpec((PEER_HEADS, PEER_NKEYS, tt), lambda i: (0, 0, i))
    return pl.pallas_call(
        _peer_route_kernel,
        grid=(T // tt,),
        in_specs=[pl.BlockSpec((tt, D_MODEL), lambda i: (i, 0)),
                  pl.BlockSpec(wq.shape, lambda i: (0, 0)),
                  pl.BlockSpec(keys.shape, lambda i: (0, 0, 0, 0))],
        out_specs=[big_spec, big_spec, big_spec, big_spec],
        out_shape=[big, big, big_b, big_b],
        scratch_shapes=[pltpu.VMEM((_PEER_CAND_ROWS, tt), F32)],
        compiler_params=_cparams("parallel"),
        name="peer_route",
    )(xt, wq, keys)


PEER_CHUNK = 2 * PEER_NKEYS


BF16_SUBLANES = 2 * SUBLANES


def _dup_bf16_words(v):
    u = lax.bitcast_convert_type(v.astype(BF16).astype(F32), jnp.uint32)
    return u | (u >> 16)


def _row_bf16(ref, i1, hd, ls):
    words = jnp.broadcast_to(ref[i1, hd:hd + 1, ls], (SUBLANES, LANES))
    row = pltpu.bitcast(words, BF16)
    return jnp.tile(row, (PEER_NKEYS // BF16_SUBLANES, 1))


def _peer_dense_kernel(x_ref, wd_ref, wu_ref, lim_ref, e1_ref, r2_ref, e2_ref, g_ref, b_ref,
                       o_ref, xb_ref, p_ref, acc_ref, limr_ref, e1r_ref, r2s_ref, e2s_ref, *, te, tt):
    e = pl.program_id(1)

    @pl.when(e == 0)
    def _():
        xb_ref[...] = x_ref[...].astype(BF16)
        acc_ref[...] = jnp.zeros_like(acc_ref)
        r2s_ref[...] = r2_ref[...]
        e2s_ref[...] = e2_ref[...]

        def relayout(g, carry):
            g0 = pl.multiple_of(g * SUBLANES, SUBLANES)
            for hd in range(PEER_HEADS):
                limg = _dup_bf16_words(lim_ref[hd, pl.ds(g0, SUBLANES), :])
                e1g = _dup_bf16_words(e1_ref[hd, pl.ds(g0, SUBLANES), :])
                for r in range(SUBLANES):
                    limr_ref[g0 + r, hd:hd + 1, :] = limg[r:r + 1, :]
                    e1r_ref[g0 + r, hd:hd + 1, :] = e1g[r:r + 1, :]
            return carry
        lax.fori_loop(0, PEER_NKEYS // SUBLANES, relayout, 0)

    xb = xb_ref[...]
    n_ch = te // PEER_CHUNK
    i1_0 = e * (te // PEER_NKEYS)
    a_next = _dot_nt(wd_ref[0:PEER_CHUNK, :], xb)
    for c in range(n_ch):
        a_cur = a_next
        if c + 1 < n_ch:
            a_next = _dot_nt(wd_ref[(c + 1) * PEER_CHUNK:(c + 2) * PEER_CHUNK, :], xb)
        for j in range(PEER_CHUNK // PEER_NKEYS):
            i1 = i1_0 + c * (PEER_CHUNK // PEER_NKEYS) + j
            r0 = c * PEER_CHUNK + j * PEER_NKEYS
            for lg in range(tt // LANES):
                ls = slice(lg * LANES, (lg + 1) * LANES)
                gate = jnp.zeros((PEER_NKEYS, LANES), BF16)
                zero = jnp.zeros((), BF16)
                for hd in range(PEER_HEADS):
                    limb = _row_bf16(limr_ref, i1, hd, ls)
                    e1b = _row_bf16(e1r_ref, i1, hd, ls)
                    gate = gate + jnp.where(r2s_ref[hd, :, ls] < limb, e2s_ref[hd, :, ls] * e1b, zero)
                a = a_cur[j * PEER_NKEYS:(j + 1) * PEER_NKEYS, ls]
                act = 0.5 * a * (1.0 + lax.erf(a * (2.0 ** -0.5)))
                p_ref[r0:r0 + PEER_NKEYS, ls] = act.astype(BF16) * gate
        acc_ref[...] += _dot(wu_ref[c], p_ref[c * PEER_CHUNK:(c + 1) * PEER_CHUNK, :])

    @pl.when(e == pl.num_programs(1) - 1)
    def _():
        y = DN_ALPHA * x_ref[...] + acc_ref[...].T
        o_ref[...] = _layernorm_rows(y, g_ref[...], b_ref[...])


def _peer_dense(xt, w_down, w_up, route, gain, bias, tt, te):
    T = xt.shape[0]
    c1, e1, s2, e2 = route
    wd = w_down.astype(BF16)
    wu = w_up.astype(BF16).reshape(PEER_EXPERTS // PEER_CHUNK, PEER_CHUNK, D_MODEL).transpose(0, 2, 1)
    g = gain.reshape(1, -1)
    b = bias.reshape(1, -1)
    big_spec = pl.BlockSpec((PEER_HEADS, PEER_NKEYS, tt), lambda i, e: (0, 0, i))
    return pl.pallas_call(
        functools.partial(_peer_dense_kernel, te=te, tt=tt),
        grid=(T // tt, PEER_EXPERTS // te),
        in_specs=[pl.BlockSpec((tt, D_MODEL), lambda i, e: (i, 0)),
                  pl.BlockSpec((te, D_MODEL), lambda i, e: (e, 0)),
                  pl.BlockSpec((te // PEER_CHUNK, D_MODEL, PEER_CHUNK), lambda i, e: (e, 0, 0)),
                  big_spec, big_spec, big_spec, big_spec,
                  pl.BlockSpec(g.shape, lambda i, e: (0, 0)),
                  pl.BlockSpec(b.shape, lambda i, e: (0, 0))],
        out_specs=pl.BlockSpec((tt, D_MODEL), lambda i, e: (i, 0)),
        out_shape=jax.ShapeDtypeStruct((T, D_MODEL), F32),
        scratch_shapes=[pltpu.VMEM((tt, D_MODEL), BF16),
                        pltpu.VMEM((te, tt), BF16),
                        pltpu.VMEM((D_MODEL, tt), F32),
                        pltpu.VMEM((PEER_NKEYS, PEER_HEADS, tt), jnp.uint32),
                        pltpu.VMEM((PEER_NKEYS, PEER_HEADS, tt), jnp.uint32),
                        pltpu.VMEM((PEER_HEADS, PEER_NKEYS, tt), BF16),
                        pltpu.VMEM((PEER_HEADS, PEER_NKEYS, tt), BF16)],
        compiler_params=_cparams("parallel", "arbitrary"),
        name="peer_dense",
    )(xt, wd, wu, c1, e1, s2, e2, g, b)


def _rope_tables(seq, dim):
    inv = ROPE_THETA ** (-jnp.arange(0, dim, 2, dtype=F32) / dim)
    ang = jnp.arange(seq, dtype=F32)[:, None] * inv[None, :]
    return jnp.cos(ang), jnp.sin(ang)


def _tiles(S, T):
    row = min(256, S)
    tq_mla = min(256, S)
    tq_dsa = min(256, S)
    tt_route = min(256, T)
    tt_dense = min(512, T)
    te_dense = 2048
    return row, tq_mla, tq_dsa, tt_route, tt_dense, te_dense


def _peer_layer(xt, w_q, sub_keys, w_down, w_up, gain, bias, tt_route, tt_dense, te_dense):
    route = _peer_route(xt, w_q, sub_keys, tt_route)
    return _peer_dense(xt, w_down, w_up, route, gain, bias, tt_dense, te_dense)


def kernel(x, mla_w_in, mla_q_norm, mla_kv_norm, mla_w_uq, mla_w_ukv, mla_w_o,
           dsa_w_in, dsa_w_o, peer_w_q, peer_sub_keys, peer_w_down, peer_w_up,
           ln_gain, ln_bias):
    B, S, D = x.shape
    T = B * S
    row, tq_mla, tq_dsa, tt_route, tt_dense, te_dense = _tiles(S, T)
    xt = x.reshape(T, D)

    cos, sin = _rope_tables(S, MLA_ROPE)
    q, k, v = _mla_proj(xt, B, S, mla_w_in[0], mla_q_norm[0], mla_kv_norm[0],
                        mla_w_uq[0], mla_w_ukv[0], cos, sin, row)
    o = _mla_attn(q, k, v, tq_mla).reshape(T, MLA_HEADS * MLA_V)
    xt = _proj_ln(o, mla_w_o[0], xt, ln_gain[0, 0], ln_bias[0, 0], row)
    xt = _peer_layer(xt, peer_w_q[0], peer_sub_keys[0], peer_w_down[0], peer_w_up[0],
                     ln_gain[0, 1], ln_bias[0, 1], tt_route, tt_dense, te_dense)

    q, k, v, qi, ki, wi = _dsa_proj(xt, B, S, dsa_w_in[0], row)
    o = _dsa_attn(q, k, v, qi, ki, wi, tq_dsa).reshape(T, DSA_HD)
    xt = _proj_ln(o, dsa_w_o[0], xt, ln_gain[1, 0], ln_bias[1, 0], row)
    xt = _peer_layer(xt, peer_w_q[1], peer_sub_keys[1], peer_w_down[1], peer_w_up[1],
                     ln_gain[1, 1], ln_bias[1, 1], tt_route, tt_dense, te_dense)
    return xt.reshape(B, S, D)
```

```python
import functools

import jax
import jax.numpy as jnp
from jax import lax
from jax.experimental import pallas as pl
from jax.experimental.pallas import tpu as pltpu

F32 = jnp.float32
BF16 = jnp.bfloat16
I32 = jnp.int32

D_MODEL = 1024
DEPTH = 2
CHUNK = 64
ROPE_THETA = 10000.0
LN_EPS = 1e-5
RMS_EPS = 1e-6
DN_ALPHA = (2 * DEPTH) ** 0.25

MLA_HEADS = 8
MLA_NOPE = 128
MLA_ROPE = 64
MLA_V = 128
MLA_Q_RANK = 384
MLA_KV_RANK = 256
MLA_QK_PAD = 256

DSA_HEADS = 8
DSA_HEAD_DIM = 128
IDX_HEADS = 8
IDX_DIM = 64
DSA_TOPK_MAX = 256
DSA_HD = DSA_HEADS * DSA_HEAD_DIM

PEER_HEADS = 8
PEER_NKEYS = 128
PEER_EXPERTS = PEER_NKEYS * PEER_NKEYS
PEER_QDIM = 256
PEER_TOPK = 16

LANES = 128
SUBLANES = 8
VMEM_LIMIT = 56 * 1024 * 1024

NEG_INF = float("-inf")
INT_MIN = -2 ** 31


def _cparams(*sem):
    return pltpu.CompilerParams(dimension_semantics=sem, vmem_limit_bytes=VMEM_LIMIT)


def _dot(a, b):
    return jnp.dot(a, b, preferred_element_type=F32)


def _dot_nt(a, b):
    return lax.dot_general(a, b, (((1,), (1,)), ((), ())), preferred_element_type=F32)


def _layernorm_rows(y, g, b):
    mu = jnp.mean(y, axis=-1, keepdims=True)
    yc = y - mu
    var = jnp.mean(yc * yc, axis=-1, keepdims=True)
    return yc * lax.rsqrt(var + LN_EPS) * g + b


def _rms_rows(h, g):
    ms = jnp.mean(h * h, axis=-1, keepdims=True)
    return h * lax.rsqrt(ms + RMS_EPS) * g


def _mla_proj_kernel(x_ref, win_ref, qn_ref, kvn_ref, wuq_ref, wukv_ref, cc_ref, ss_ref,
                     q_ref, k_ref, v_ref):
    xb = x_ref[...].astype(BF16)
    h = _dot(xb, win_ref[...])
    cq = h[:, :MLA_Q_RANK]
    ckv = h[:, MLA_Q_RANK:MLA_Q_RANK + MLA_KV_RANK]
    o = MLA_Q_RANK + MLA_KV_RANK
    cc = cc_ref[...]
    ss = ss_ref[...]
    k_rope = h[:, o:o + LANES] * cc + h[:, o + LANES:o + 2 * LANES] * ss
    qall = _dot(_rms_rows(cq, qn_ref[...]).astype(BF16), wuq_ref[...])
    kvall = _dot(_rms_rows(ckv, kvn_ref[...]).astype(BF16), wukv_ref[...])
    k_rope_b = k_rope.astype(BF16)
    for hd in range(MLA_HEADS):
        b0 = hd * 3 * LANES
        q_ref[0, hd, :, 0:LANES] = qall[:, b0:b0 + LANES].astype(BF16)
        q_rope = qall[:, b0 + LANES:b0 + 2 * LANES] * cc + qall[:, b0 + 2 * LANES:b0 + 3 * LANES] * ss
        q_ref[0, hd, :, LANES:2 * LANES] = q_rope.astype(BF16)
        k_ref[0, hd, :, 0:LANES] = kvall[:, hd * LANES:(hd + 1) * LANES].astype(BF16)
        k_ref[0, hd, :, LANES:2 * LANES] = k_rope_b
        v0 = MLA_HEADS * MLA_NOPE + hd * MLA_V
        v_ref[0, hd] = kvall[:, v0:v0 + MLA_V].astype(BF16)


def _mla_proj(xt, B, S, w_in, q_norm, kv_norm, w_uq, w_ukv, cos, sin, tm):
    T = B * S
    n_s = S // tm
    half = MLA_ROPE // 2
    def swap_rope(w):
        return jnp.concatenate([w[..., half:], w[..., :half]], axis=-1)
    zpad = lambda w: jnp.concatenate([w, jnp.zeros_like(w)], axis=-1)
    w_kr = w_in[:, MLA_Q_RANK + MLA_KV_RANK:]
    win_ext = jnp.concatenate(
        [w_in[:, :MLA_Q_RANK + MLA_KV_RANK], zpad(w_kr), zpad(swap_rope(w_kr))], axis=-1).astype(BF16)
    uq_nope = w_uq[:, :, :MLA_NOPE]
    uq_rope = w_uq[:, :, MLA_NOPE:]
    wuq_ext = jnp.concatenate([uq_nope, zpad(uq_rope), zpad(swap_rope(uq_rope))], axis=-1)
    wuq_ext = wuq_ext.reshape(MLA_Q_RANK, MLA_HEADS * 3 * LANES).astype(BF16)
    wukv_ext = jnp.concatenate(
        [w_ukv[:, :, :MLA_NOPE].reshape(MLA_KV_RANK, -1), w_ukv[:, :, MLA_NOPE:].reshape(MLA_KV_RANK, -1)],
        axis=-1).astype(BF16)
    z = jnp.zeros((S, LANES - MLA_ROPE), F32)
    cc = jnp.concatenate([cos, cos, z], axis=-1)
    ss = jnp.concatenate([-sin, sin, z], axis=-1)
    full = lambda a: pl.BlockSpec(a.shape, lambda i: (0,) * a.ndim)
    qn = q_norm.reshape(1, -1)
    kvn = kv_norm.reshape(1, -1)
    head_map = lambda i: (i // n_s, 0, i % n_s, 0)
    return pl.pallas_call(
        _mla_proj_kernel,
        grid=(T // tm,),
        in_specs=[pl.BlockSpec((tm, D_MODEL), lambda i: (i, 0)),
                  full(win_ext), full(qn), full(kvn), full(wuq_ext), full(wukv_ext),
                  pl.BlockSpec((tm, LANES), lambda i: (i % n_s, 0)),
                  pl.BlockSpec((tm, LANES), lambda i: (i % n_s, 0))],
        out_specs=[pl.BlockSpec((1, MLA_HEADS, tm, MLA_QK_PAD), head_map),
                   pl.BlockSpec((1, MLA_HEADS, tm, MLA_QK_PAD), head_map),
                   pl.BlockSpec((1, MLA_HEADS, tm, MLA_V), head_map)],
        out_shape=[jax.ShapeDtypeStruct((B, MLA_HEADS, S, MLA_QK_PAD), BF16),
                   jax.ShapeDtypeStruct((B, MLA_HEADS, S, MLA_QK_PAD), BF16),
                   jax.ShapeDtypeStruct((B, MLA_HEADS, S, MLA_V), BF16)],
        compiler_params=_cparams("parallel"),
        name="mla_proj",
    )(xt, win_ext, qn, kvn, wuq_ext, wukv_ext, cc, ss)


MLA_HEADS_PER_STEP = 4
KEY_CLASSES = 4


def _mla_attn_block(q_ref, k_ref, v_ref, o_ref, blk, *, tq, L, scale):
    row = lax.broadcasted_iota(I32, (tq, L), 0)
    col = lax.broadcasted_iota(I32, (tq, L), 1)
    adm = col // CHUNK <= (blk * tq + row) // CHUNK
    for h in range(MLA_HEADS_PER_STEP):
        s = _dot_nt(q_ref[0, h], k_ref[0, h, 0:L, :]) * scale
        s = jnp.where(adm, s, NEG_INF)
        m = jnp.max(s, axis=-1, keepdims=True)
        p = jnp.exp(s - m)
        l = jnp.sum(p, axis=-1, keepdims=True)
        o = _dot(p.astype(BF16), v_ref[0, h, 0:L, :])
        o_ref[0, :, h * MLA_V:(h + 1) * MLA_V] = (o / l).astype(BF16)


def _mla_attn_kernel(q_ref, k_ref, v_ref, o_ref, *, tq, S, scale):
    blk = pl.program_id(2)
    width = S // KEY_CLASSES
    cls = ((blk + 1) * tq - 1) // width
    for c in range(KEY_CLASSES):
        @pl.when(cls == c)
        def _(c=c):
            _mla_attn_block(q_ref, k_ref, v_ref, o_ref, blk, tq=tq, L=(c + 1) * width, scale=scale)


def _mla_attn(q, k, v, tq):
    B, H, S, _ = q.shape
    hp = MLA_HEADS_PER_STEP
    scale = (MLA_NOPE + MLA_ROPE) ** -0.5
    return pl.pallas_call(
        functools.partial(_mla_attn_kernel, tq=tq, S=S, scale=scale),
        grid=(B, H // hp, S // tq),
        in_specs=[pl.BlockSpec((1, hp, tq, MLA_QK_PAD), lambda b, h, i: (b, h, i, 0)),
                  pl.BlockSpec((1, hp, S, MLA_QK_PAD), lambda b, h, i: (b, h, 0, 0)),
                  pl.BlockSpec((1, hp, S, MLA_V), lambda b, h, i: (b, h, 0, 0))],
        out_specs=pl.BlockSpec((1, tq, hp * MLA_V), lambda b, h, i: (b, i, h)),
        out_shape=jax.ShapeDtypeStruct((B, S, H * MLA_V), BF16),
        compiler_params=_cparams("parallel", "parallel", "arbitrary"),
        name="mla_attn",
    )(q, k, v)


def _proj_ln_kernel(a_ref, w_ref, x_ref, g_ref, b_ref, o_ref):
    m = _dot(a_ref[...], w_ref[...])
    y = DN_ALPHA * x_ref[...] + m
    o_ref[...] = _layernorm_rows(y, g_ref[...], b_ref[...])


def _proj_ln(a, w, xt, gain, bias, tm):
    T = xt.shape[0]
    wb = w.astype(BF16)
    g = gain.reshape(1, -1)
    b = bias.reshape(1, -1)
    return pl.pallas_call(
        _proj_ln_kernel,
        grid=(T // tm,),
        in_specs=[pl.BlockSpec((tm, a.shape[1]), lambda i: (i, 0)),
                  pl.BlockSpec(wb.shape, lambda i: (0, 0)),
                  pl.BlockSpec((tm, D_MODEL), lambda i: (i, 0)),
                  pl.BlockSpec(g.shape, lambda i: (0, 0)),
                  pl.BlockSpec(b.shape, lambda i: (0, 0))],
        out_specs=pl.BlockSpec((tm, D_MODEL), lambda i: (i, 0)),
        out_shape=jax.ShapeDtypeStruct((T, D_MODEL), F32),
        compiler_params=_cparams("parallel"),
        name="proj_ln",
    )(a, wb, xt, g, b)


def _dsa_proj_kernel(x_ref, w_ref, ch_ref, sh_ref, ci_ref, ck_ref, sk_ref,
                     q_ref, k_ref, v_ref, qi_ref, ki_ref, wi_ref, *, w_scale):
    xb = x_ref[...].astype(BF16)
    h = _dot(xb, w_ref[...])
    ch = ch_ref[...]
    sh = sh_ref[...]
    ci = ci_ref[...]
    half = LANES // 2
    for hd in range(DSA_HEADS):
        c0 = hd * DSA_HEAD_DIM
        qh = h[:, c0:c0 + LANES]
        q_ref[0, :, c0:c0 + LANES] = (qh * ch + pltpu.roll(qh, half, 1) * sh).astype(BF16)
        kh = h[:, DSA_HD + c0:DSA_HD + c0 + LANES]
        k_ref[0, :, c0:c0 + LANES] = (kh * ch + pltpu.roll(kh, half, 1) * sh).astype(BF16)
        t = h[:, 3 * DSA_HD + c0:3 * DSA_HD + c0 + LANES] * ci
        qi_ref[0, :, c0:c0 + LANES] = (t + pltpu.roll(t, half, 1)).astype(BF16)
    v_ref[0] = h[:, 2 * DSA_HD:3 * DSA_HD].astype(BF16)
    o = 4 * DSA_HD
    ki = h[:, o:o + LANES] * ck_ref[...] + h[:, o + LANES:o + 2 * LANES] * sk_ref[...]
    ki_ref[0] = ki.astype(BF16)
    wi_ref[0] = h[:, o + 2 * LANES:o + 3 * LANES] * w_scale


def _dsa_proj(xt, B, S, w_in, tm):
    T = B * S
    n_s = S // tm
    o1 = 3 * DSA_HD + IDX_HEADS * IDX_DIM
    ih = IDX_DIM // 2
    w_qi = w_in[:, 3 * DSA_HD:o1].reshape(D_MODEL, IDX_HEADS, IDX_DIM)
    w_qi_sw = jnp.concatenate([w_qi[..., ih:], w_qi[..., :ih]], axis=-1)
    w_qi_ext = jnp.concatenate([w_qi, w_qi_sw], axis=-1).reshape(D_MODEL, IDX_HEADS * LANES)
    w_ki = w_in[:, o1:o1 + IDX_DIM]
    w_ki_sw = jnp.concatenate([w_ki[:, ih:], w_ki[:, :ih]], axis=-1)
    zpad = lambda w: jnp.concatenate([w, jnp.zeros((w.shape[0], LANES - w.shape[1]), w.dtype)], axis=-1)
    w_ext = jnp.concatenate(
        [w_in[:, :3 * DSA_HD], w_qi_ext, zpad(w_ki), zpad(w_ki_sw), zpad(w_in[:, o1 + IDX_DIM:])],
        axis=-1).astype(BF16)
    cos_h, sin_h = _rope_tables(S, DSA_HEAD_DIM)
    cos_i, sin_i = _rope_tables(S, IDX_DIM)
    ch = jnp.concatenate([cos_h, cos_h], axis=-1)
    sh = jnp.concatenate([-sin_h, sin_h], axis=-1)
    ci = jnp.concatenate([cos_i, cos_i, -sin_i, sin_i], axis=-1)
    z = jnp.zeros((S, LANES - IDX_DIM), F32)
    ck = jnp.concatenate([cos_i, cos_i, z], axis=-1)
    sk = jnp.concatenate([-sin_i, sin_i, z], axis=-1)
    tab = pl.BlockSpec((tm, LANES), lambda i: (i % n_s, 0))
    row_map = lambda i: (i // n_s, i % n_s, 0)
    wide = pl.BlockSpec((1, tm, DSA_HD), row_map)
    narrow = pl.BlockSpec((1, tm, LANES), row_map)
    w_scale = IDX_HEADS ** -0.5 * IDX_DIM ** -0.5
    return pl.pallas_call(
        functools.partial(_dsa_proj_kernel, w_scale=w_scale),
        grid=(T // tm,),
        in_specs=[pl.BlockSpec((tm, D_MODEL), lambda i: (i, 0)),
                  pl.BlockSpec(w_ext.shape, lambda i: (0, 0)),
                  tab, tab, tab, tab, tab],
        out_specs=[wide, wide, wide, wide, narrow, narrow],
        out_shape=[jax.ShapeDtypeStruct((B, S, DSA_HD), BF16),
                   jax.ShapeDtypeStruct((B, S, DSA_HD), BF16),
                   jax.ShapeDtypeStruct((B, S, DSA_HD), BF16),
                   jax.ShapeDtypeStruct((B, S, IDX_HEADS * LANES), BF16),
                   jax.ShapeDtypeStruct((B, S, LANES), BF16),
                   jax.ShapeDtypeStruct((B, S, LANES), F32)],
        compiler_params=_cparams("parallel"),
        name="dsa_proj",
    )(xt, w_ext, ch, sh, ci, ck, sk)


def _dsa_attn_block(q_ref, qi_ref, wi_ref, k_ref, v_ref, ki_ref, o_ref, blk, *, tq, L, topk, scale):
    ki = ki_ref[0, 0:L, :]
    qi = qi_ref[0]
    wi = wi_ref[0]
    score = jnp.zeros((tq, L), F32)
    for hd in range(IDX_HEADS):
        logit = _dot_nt(qi[:, hd * LANES:(hd + 1) * LANES], ki)
        score = score + wi[:, hd:hd + 1] * jnp.maximum(logit, 0.0)
    row = lax.broadcasted_iota(I32, (tq, L), 0)
    col = lax.broadcasted_iota(I32, (tq, L), 1)
    adm = col // CHUNK <= (blk * tq + row) // CHUNK
    score = jnp.where(adm, score + 0.0, NEG_INF)
    bits = lax.bitcast_convert_type(score, I32)
    u = jnp.where(bits < 0, bits ^ 0x7FFFFFFF, bits)

    def search(i, thr):
        cand = thr + lax.shift_left(jnp.int32(1), 31 - i)
        cnt = jnp.sum((u >= cand).astype(F32), axis=-1, keepdims=True)
        return jnp.where(cnt >= topk, cand, thr)
    thr = lax.fori_loop(0, 32, search, jnp.full((tq, 1), INT_MIN, I32), unroll=8)

    gt = u > thr
    eq = u == thr
    need = topk - jnp.sum(gt.astype(F32), axis=-1, keepdims=True)
    r_i = lax.broadcasted_iota(I32, (LANES, LANES), 0)
    c_i = lax.broadcasted_iota(I32, (LANES, LANES), 1)
    tri = (r_i < c_i).astype(BF16)
    eq_b = eq.astype(BF16)
    run = jnp.zeros((tq, 1), F32)
    ranks = []
    for j in range(L // LANES):
        e = eq_b[:, j * LANES:(j + 1) * LANES]
        ranks.append(_dot(e, tri) + run)
        run = run + jnp.sum(e.astype(F32), axis=-1, keepdims=True)
    rank = jnp.concatenate(ranks, axis=-1)
    sel = adm & (gt | (eq & (rank < need)))

    for hd in range(DSA_HEADS):
        c0 = hd * DSA_HEAD_DIM
        s = _dot_nt(q_ref[0, :, c0:c0 + LANES], k_ref[0, 0:L, c0:c0 + LANES]) * scale
        s = jnp.where(sel, s, NEG_INF)
        m = jnp.max(s, axis=-1, keepdims=True)
        p = jnp.exp(s - m)
        l = jnp.sum(p, axis=-1, keepdims=True)
        o = _dot(p.astype(BF16), v_ref[0, 0:L, c0:c0 + LANES])
        o_ref[0, :, c0:c0 + LANES] = (o / l).astype(BF16)


def _dsa_attn_kernel(q_ref, qi_ref, wi_ref, k_ref, v_ref, ki_ref, o_ref, *, tq, S, topk, scale):
    blk = pl.program_id(1)
    width = S // KEY_CLASSES
    cls = ((blk + 1) * tq - 1) // width
    for c in range(KEY_CLASSES):
        @pl.when(cls == c)
        def _(c=c):
            _dsa_attn_block(q_ref, qi_ref, wi_ref, k_ref, v_ref, ki_ref, o_ref, blk,
                            tq=tq, L=(c + 1) * width, topk=topk, scale=scale)


def _dsa_attn(q, k, v, qi, ki, wi, tq):
    B, S, _ = q.shape
    topk = min(DSA_TOPK_MAX, S // 4)
    scale = DSA_HEAD_DIM ** -0.5
    qmap = lambda b, i: (b, i, 0)
    kmap = lambda b, i: (b, 0, 0)
    return pl.pallas_call(
        functools.partial(_dsa_attn_kernel, tq=tq, S=S, topk=topk, scale=scale),
        grid=(B, S // tq),
        in_specs=[pl.BlockSpec((1, tq, DSA_HD), qmap),
                  pl.BlockSpec((1, tq, IDX_HEADS * LANES), qmap),
                  pl.BlockSpec((1, tq, LANES), qmap),
                  pl.BlockSpec((1, S, DSA_HD), kmap),
                  pl.BlockSpec((1, S, DSA_HD), kmap),
                  pl.BlockSpec((1, S, LANES), kmap)],
        out_specs=pl.BlockSpec((1, tq, DSA_HD), qmap),
        out_shape=jax.ShapeDtypeStruct((B, S, DSA_HD), BF16),
        compiler_params=_cparams("parallel", "arbitrary"),
        name="dsa_attn",
    )(q, qi, wi, k, v, ki)


_PEER_PAIRS = [(a, b) for a in range(PEER_TOPK) for b in range(PEER_TOPK // (a + 1))]
_PEER_CAND_ROWS = -(-len(_PEER_PAIRS) // SUBLANES) * SUBLANES


def _sort_network(n):
    pairs = []
    p = 1
    while p < n:
        k = p
        while k >= 1:
            for j in range(k % p, n - k, 2 * k):
                for i in range(min(k, n - j - k)):
                    if (i + j) // (2 * p) == (i + j + k) // (2 * p):
                        pairs.append((i + j, i + j + k))
            k //= 2
        p *= 2
    return pairs


def _top_rows(s, n):
    groups = PEER_NKEYS // SUBLANES
    v = [s[g * SUBLANES:(g + 1) * SUBLANES, :] for g in range(groups)]
    for i, j in _sort_network(groups):
        v[i], v[j] = jnp.maximum(v[i], v[j]), jnp.minimum(v[i], v[j])
    vals = []
    for k in range(n):
        m = jnp.max(v[0], axis=0, keepdims=True)
        vals.append(m)
        won = v[0] == m
        for d in range(groups - 1 - k):
            v[d] = jnp.where(won, v[d + 1], v[d])
    return vals


def _peer_route_kernel(x_ref, wq_ref, keys_ref, lim_ref, e1_ref, r2_ref, e2_ref, cand_ref):
    tt = x_ref.shape[0]
    q = _dot(x_ref[...].astype(BF16), wq_ref[...]).astype(BF16)
    half = PEER_QDIM // 2
    pad0 = _PEER_CAND_ROWS - SUBLANES
    cand_ref[pad0:, :] = jnp.full((SUBLANES, tt), NEG_INF, F32)
    for hd in range(PEER_HEADS):
        c0 = hd * PEER_QDIM
        s1 = _dot_nt(keys_ref[0, hd], q[:, c0:c0 + half])
        s2 = _dot_nt(keys_ref[1, hd], q[:, c0 + half:c0 + PEER_QDIM])
        v1 = _top_rows(s1, PEER_TOPK)
        v2 = _top_rows(s2, PEER_TOPK)
        for k, (a, b) in enumerate(_PEER_PAIRS):
            cand_ref[k:k + 1, :] = v1[a] + v2[b]
        cand = cand_ref[...]
        cur = cand
        theta = None
        for _ in range(PEER_TOPK):
            theta = jnp.max(cur, axis=0, keepdims=True)
            cur = jnp.where(cur == theta, NEG_INF, cur)
        top = v1[0] + v2[0]
        z = jnp.sum(jnp.where(cand >= theta, jnp.exp(cand - top), 0.0), axis=0, keepdims=True)
        r2 = jnp.full(s2.shape, float(PEER_TOPK), F32)
        for b in reversed(range(PEER_TOPK)):
            r2 = jnp.where(s2 >= v2[b], float(b), r2)
        lim = jnp.zeros(s1.shape, F32)
        for b in range(PEER_TOPK):
            lim = jnp.where(s1 + v2[b] >= theta, float(b + 1), lim)
        lim_ref[hd] = lim
        r2_ref[hd] = r2.astype(BF16)
        e1_ref[hd] = jnp.exp(s1 - v1[0]) / z
        e2_ref[hd] = jnp.exp(s2 - v2[0]).astype(BF16)


def _peer_route(xt, w_q, sub_keys, tt):
    T = xt.shape[0]
    wq = w_q.astype(BF16)
    keys = sub_keys.astype(BF16)
    big = jax.ShapeDtypeStruct((PEER_HEADS, PEER_NKEYS, T), F32)
    big_b = jax.ShapeDtypeStruct((PEER_HEADS, PEER_NKEYS, T), BF16)
    big_spec = pl.BlockSpec((PEER_HEADS, PEER_NKEYS, tt), lambda i: (0, 0, i))
    return pl.pallas_call(
        _peer_route_kernel,
        grid=(T // tt,),
        in_specs=[pl.BlockSpec((tt, D_MODEL), lambda i: (i, 0)),
                  pl.BlockSpec(wq.shape, lambda i: (0, 0)),
                  pl.BlockSpec(keys.shape, lambda i: (0, 0, 0, 0))],
        out_specs=[big_spec, big_spec, big_spec, big_spec],
        out_shape=[big, big, big_b, big_b],
        scratch_shapes=[pltpu.VMEM((_PEER_CAND_ROWS, tt), F32)],
        compiler_params=_cparams("parallel"),
        name="peer_route",
    )(xt, wq, keys)


PEER_CHUNK = 8 * PEER_NKEYS


BF16_SUBLANES = 2 * SUBLANES


def _row_bf16(ref, i1, hd, ls):
    row = jnp.broadcast_to(ref[i1, hd:hd + 1, ls], (BF16_SUBLANES, LANES)).astype(BF16)
    return jnp.tile(row, (PEER_NKEYS // BF16_SUBLANES, 1))


def _peer_dense_kernel(x_ref, wd_ref, wu_ref, lim_ref, e1_ref, r2_ref, e2_ref, g_ref, b_ref,
                       o_ref, xb_ref, p_ref, acc_ref, limr_ref, e1r_ref, r2s_ref, e2s_ref, *, te, tt):
    e = pl.program_id(1)

    @pl.when(e == 0)
    def _():
        xb_ref[...] = x_ref[...].astype(BF16)
        acc_ref[...] = jnp.zeros_like(acc_ref)
        r2s_ref[...] = r2_ref[...]
        e2s_ref[...] = e2_ref[...]

        def relayout(g, carry):
            g0 = pl.multiple_of(g * SUBLANES, SUBLANES)
            for hd in range(PEER_HEADS):
                limg = lim_ref[hd, pl.ds(g0, SUBLANES), :]
                e1g = e1_ref[hd, pl.ds(g0, SUBLANES), :]
                for r in range(SUBLANES):
                    limr_ref[g0 + r, hd:hd + 1, :] = limg[r:r + 1, :]
                    e1r_ref[g0 + r, hd:hd + 1, :] = e1g[r:r + 1, :]
            return carry
        lax.fori_loop(0, PEER_NKEYS // SUBLANES, relayout, 0)

    xb = xb_ref[...]
    n_ch = te // PEER_CHUNK
    i1_0 = e * (te // PEER_NKEYS)
    a_next = _dot_nt(wd_ref[0:PEER_CHUNK, :], xb)
    for c in range(n_ch):
        a_cur = a_next
        if c + 1 < n_ch:
            a_next = _dot_nt(wd_ref[(c + 1) * PEER_CHUNK:(c + 2) * PEER_CHUNK, :], xb)
        for j in range(PEER_CHUNK // PEER_NKEYS):
            i1 = i1_0 + c * (PEER_CHUNK // PEER_NKEYS) + j
            r0 = c * PEER_CHUNK + j * PEER_NKEYS
            for lg in range(tt // LANES):
                ls = slice(lg * LANES, (lg + 1) * LANES)
                gate = jnp.zeros((PEER_NKEYS, LANES), BF16)
                zero = jnp.zeros((), BF16)
                for hd in range(PEER_HEADS):
                    limb = _row_bf16(limr_ref, i1, hd, ls)
                    e1b = _row_bf16(e1r_ref, i1, hd, ls)
                    gate = gate + jnp.where(r2s_ref[hd, :, ls] < limb, e2s_ref[hd, :, ls] * e1b, zero)
                a = a_cur[j * PEER_NKEYS:(j + 1) * PEER_NKEYS, ls]
                act = 0.5 * a * (1.0 + lax.erf(a * (2.0 ** -0.5)))
                p_ref[r0:r0 + PEER_NKEYS, ls] = act.astype(BF16) * gate
        acc_ref[...] += _dot(wu_ref[c], p_ref[c * PEER_CHUNK:(c + 1) * PEER_CHUNK, :])

    @pl.when(e == pl.num_programs(1) - 1)
    def _():
        y = DN_ALPHA * x_ref[...] + acc_ref[...].T
        o_ref[...] = _layernorm_rows(y, g_ref[...], b_ref[...])


def _peer_dense(xt, w_down, w_up, route, gain, bias, tt, te):
    T = xt.shape[0]
    c1, e1, s2, e2 = route
    wd = w_down.astype(BF16)
    wu = w_up.astype(BF16).reshape(PEER_EXPERTS // PEER_CHUNK, PEER_CHUNK, D_MODEL).transpose(0, 2, 1)
    g = gain.reshape(1, -1)
    b = bias.reshape(1, -1)
    big_spec = pl.BlockSpec((PEER_HEADS, PEER_NKEYS, tt), lambda i, e: (0, 0, i))
    return pl.pallas_call(
        functools.partial(_peer_dense_kernel, te=te, tt=tt),
        grid=(T // tt, PEER_EXPERTS // te),
        in_specs=[pl.BlockSpec((tt, D_MODEL), lambda i, e: (i, 0)),
                  pl.BlockSpec((te, D_MODEL), lambda i, e: (e, 0)),
                  pl.BlockSpec((te // PEER_CHUNK, D_MODEL, PEER_CHUNK), lambda i, e: (e, 0, 0)),
                  big_spec, big_spec, big_spec, big_spec,
                  pl.BlockSpec(g.shape, lambda i, e: (0, 0)),
                  pl.BlockSpec(b.shape, lambda i, e: (0, 0))],
        out_specs=pl.BlockSpec((tt, D_MODEL), lambda i, e: (i, 0)),
        out_shape=jax.ShapeDtypeStruct((T, D_MODEL), F32),
        scratch_shapes=[pltpu.VMEM((tt, D_MODEL), BF16),
                        pltpu.VMEM((te, tt), BF16),
                        pltpu.VMEM((D_MODEL, tt), F32),
                        pltpu.VMEM((PEER_NKEYS, PEER_HEADS, tt), F32),
                        pltpu.VMEM((PEER_NKEYS, PEER_HEADS, tt), F32),
                        pltpu.VMEM((PEER_HEADS, PEER_NKEYS, tt), BF16),
                        pltpu.VMEM((PEER_HEADS, PEER_NKEYS, tt), BF16)],
        compiler_params=_cparams("parallel", "arbitrary"),
        name="peer_dense",
    )(xt, wd, wu, c1, e1, s2, e2, g, b)


def _rope_tables(seq, dim):
    inv = ROPE_THETA ** (-jnp.arange(0, dim, 2, dtype=F32) / dim)
    ang = jnp.arange(seq, dtype=F32)[:, None] * inv[None, :]
    return jnp.cos(ang), jnp.sin(ang)


def _tiles(S, T):
    row = min(256, S)
    tq_mla = min(256, S)
    tq_dsa = min(256, S)
    tt_route = min(256, T)
    tt_dense = min(512, T)
    te_dense = 2048
    return row, tq_mla, tq_dsa, tt_route, tt_dense, te_dense


def _peer_layer(xt, w_q, sub_keys, w_down, w_up, gain, bias, tt_route, tt_dense, te_dense):
    route = _peer_route(xt, w_q, sub_keys, tt_route)
    return _peer_dense(xt, w_down, w_up, route, gain, bias, tt_dense, te_dense)


def kernel(x, mla_w_in, mla_q_norm, mla_kv_norm, mla_w_uq, mla_w_ukv, mla_w_o,
           dsa_w_in, dsa_w_o, peer_w_q, peer_sub_keys, peer_w_down, peer_w_up,
           ln_gain, ln_bias):
    B, S, D = x.shape
    T = B * S
    row, tq_mla, tq_dsa, tt_route, tt_dense, te_dense = _tiles(S, T)
    xt = x.reshape(T, D)

    cos, sin = _rope_tables(S, MLA_ROPE)
    q, k, v = _mla_proj(xt, B, S, mla_w_in[0], mla_q_norm[0], mla_kv_norm[0],
                        mla_w_uq[0], mla_w_ukv[0], cos, sin, row)
    o = _mla_attn(q, k, v, tq_mla).reshape(T, MLA_HEADS * MLA_V)
    xt = _proj_ln(o, mla_w_o[0], xt, ln_gain[0, 0], ln_bias[0, 0], row)
    xt = _peer_layer(xt, peer_w_q[0], peer_sub_keys[0], peer_w_down[0], peer_w_up[0],
                     ln_gain[0, 1], ln_bias[0, 1], tt_route, tt_dense, te_dense)

    q, k, v, qi, ki, wi = _dsa_proj(xt, B, S, dsa_w_in[0], row)
    o = _dsa_attn(q, k, v, qi, ki, wi, tq_dsa).reshape(T, DSA_HD)
    xt = _proj_ln(o, dsa_w_o[0], xt, ln_gain[1, 0], ln_bias[1, 0], row)
    xt = _peer_layer(xt, peer_w_q[1], peer_sub_keys[1], peer_w_down[1], peer_w_up[1],
                     ln_gain[1, 1], ln_bias[1, 1], tt_route, tt_dense, te_dense)
    return xt.reshape(B, S, D)
```

```python
import functools

import jax
import jax.numpy as jnp
from jax import lax
from jax.experimental import pallas as pl
from jax.experimental.pallas import tpu as pltpu

F32 = jnp.float32
BF16 = jnp.bfloat16
I32 = jnp.int32

D_MODEL = 1024
DEPTH = 2
CHUNK = 64
ROPE_THETA = 10000.0
LN_EPS = 1e-5
RMS_EPS = 1e-6
DN_ALPHA = (2 * DEPTH) ** 0.25

MLA_HEADS = 8
MLA_NOPE = 128
MLA_ROPE = 64
MLA_V = 128
MLA_Q_RANK = 384
MLA_KV_RANK = 256
MLA_QK_PAD = 256

DSA_HEADS = 8
DSA_HEAD_DIM = 128
IDX_HEADS = 8
IDX_DIM = 64
DSA_TOPK_MAX = 256
DSA_HD = DSA_HEADS * DSA_HEAD_DIM

PEER_HEADS = 8
PEER_NKEYS = 128
PEER_EXPERTS = PEER_NKEYS * PEER_NKEYS
PEER_QDIM = 256
PEER_TOPK = 16

LANES = 128
SUBLANES = 8
VMEM_LIMIT = 56 * 1024 * 1024

NEG_INF = float("-inf")
INT_MIN = -2 ** 31


def _cparams(*sem):
    return pltpu.CompilerParams(dimension_semantics=sem, vmem_limit_bytes=VMEM_LIMIT)


def _dot(a, b):
    return jnp.dot(a, b, preferred_element_type=F32)


def _dot_nt(a, b):
    return lax.dot_general(a, b, (((1,), (1,)), ((), ())), preferred_element_type=F32)


def _layernorm_rows(y, g, b):
    mu = jnp.mean(y, axis=-1, keepdims=True)
    yc = y - mu
    var = jnp.mean(yc * yc, axis=-1, keepdims=True)
    return yc * lax.rsqrt(var + LN_EPS) * g + b


def _rms_rows(h, g):
    ms = jnp.mean(h * h, axis=-1, keepdims=True)
    return h * lax.rsqrt(ms + RMS_EPS) * g


def _mla_proj_kernel(x_ref, win_ref, qn_ref, kvn_ref, wuq_ref, wukv_ref, cc_ref, ss_ref,
                     q_ref, k_ref, v_ref):
    xb = x_ref[...].astype(BF16)
    h = _dot(xb, win_ref[...])
    cq = h[:, :MLA_Q_RANK]
    ckv = h[:, MLA_Q_RANK:MLA_Q_RANK + MLA_KV_RANK]
    o = MLA_Q_RANK + MLA_KV_RANK
    cc = cc_ref[...]
    ss = ss_ref[...]
    k_rope = h[:, o:o + LANES] * cc + h[:, o + LANES:o + 2 * LANES] * ss
    qall = _dot(_rms_rows(cq, qn_ref[...]).astype(BF16), wuq_ref[...])
    kvall = _dot(_rms_rows(ckv, kvn_ref[...]).astype(BF16), wukv_ref[...])
    k_rope_b = k_rope.astype(BF16)
    for hd in range(MLA_HEADS):
        b0 = hd * 3 * LANES
        q_ref[0, hd, :, 0:LANES] = qall[:, b0:b0 + LANES].astype(BF16)
        q_rope = qall[:, b0 + LANES:b0 + 2 * LANES] * cc + qall[:, b0 + 2 * LANES:b0 + 3 * LANES] * ss
        q_ref[0, hd, :, LANES:2 * LANES] = q_rope.astype(BF16)
        k_ref[0, hd, :, 0:LANES] = kvall[:, hd * LANES:(hd + 1) * LANES].astype(BF16)
        k_ref[0, hd, :, LANES:2 * LANES] = k_rope_b
        v0 = MLA_HEADS * MLA_NOPE + hd * MLA_V
        v_ref[0, hd] = kvall[:, v0:v0 + MLA_V].astype(BF16)


def _mla_proj(xt, B, S, w_in, q_norm, kv_norm, w_uq, w_ukv, cos, sin, tm):
    T = B * S
    n_s = S // tm
    half = MLA_ROPE // 2
    def swap_rope(w):
        return jnp.concatenate([w[..., half:], w[..., :half]], axis=-1)
    zpad = lambda w: jnp.concatenate([w, jnp.zeros_like(w)], axis=-1)
    w_kr = w_in[:, MLA_Q_RANK + MLA_KV_RANK:]
    win_ext = jnp.concatenate(
        [w_in[:, :MLA_Q_RANK + MLA_KV_RANK], zpad(w_kr), zpad(swap_rope(w_kr))], axis=-1).astype(BF16)
    uq_nope = w_uq[:, :, :MLA_NOPE]
    uq_rope = w_uq[:, :, MLA_NOPE:]
    wuq_ext = jnp.concatenate([uq_nope, zpad(uq_rope), zpad(swap_rope(uq_rope))], axis=-1)
    wuq_ext = wuq_ext.reshape(MLA_Q_RANK, MLA_HEADS * 3 * LANES).astype(BF16)
    wukv_ext = jnp.concatenate(
        [w_ukv[:, :, :MLA_NOPE].reshape(MLA_KV_RANK, -1), w_ukv[:, :, MLA_NOPE:].reshape(MLA_KV_RANK, -1)],
        axis=-1).astype(BF16)
    z = jnp.zeros((S, LANES - MLA_ROPE), F32)
    cc = jnp.concatenate([cos, cos, z], axis=-1)
    ss = jnp.concatenate([-sin, sin, z], axis=-1)
    full = lambda a: pl.BlockSpec(a.shape, lambda i: (0,) * a.ndim)
    qn = q_norm.reshape(1, -1)
    kvn = kv_norm.reshape(1, -1)
    head_map = lambda i: (i // n_s, 0, i % n_s, 0)
    return pl.pallas_call(
        _mla_proj_kernel,
        grid=(T // tm,),
        in_specs=[pl.BlockSpec((tm, D_MODEL), lambda i: (i, 0)),
                  full(win_ext), full(qn), full(kvn), full(wuq_ext), full(wukv_ext),
                  pl.BlockSpec((tm, LANES), lambda i: (i % n_s, 0)),
                  pl.BlockSpec((tm, LANES), lambda i: (i % n_s, 0))],
        out_specs=[pl.BlockSpec((1, MLA_HEADS, tm, MLA_QK_PAD), head_map),
                   pl.BlockSpec((1, MLA_HEADS, tm, MLA_QK_PAD), head_map),
                   pl.BlockSpec((1, MLA_HEADS, tm, MLA_V), head_map)],
        out_shape=[jax.ShapeDtypeStruct((B, MLA_HEADS, S, MLA_QK_PAD), BF16),
                   jax.ShapeDtypeStruct((B, MLA_HEADS, S, MLA_QK_PAD), BF16),
                   jax.ShapeDtypeStruct((B, MLA_HEADS, S, MLA_V), BF16)],
        compiler_params=_cparams("parallel"),
        name="mla_proj",
    )(xt, win_ext, qn, kvn, wuq_ext, wukv_ext, cc, ss)


MLA_HEADS_PER_STEP = 4
MLA_KEY_CLASSES = 8
DSA_KEY_CLASSES = 4


def _mla_attn_block(q_ref, k_ref, v_ref, o_ref, blk, *, tq, L, scale):
    row = lax.broadcasted_iota(I32, (tq, L), 0)
    col = lax.broadcasted_iota(I32, (tq, L), 1)
    adm = col // CHUNK <= (blk * tq + row) // CHUNK
    for h in range(MLA_HEADS_PER_STEP):
        s = _dot_nt(q_ref[0, h], k_ref[0, h, 0:L, :]) * scale
        s = jnp.where(adm, s, NEG_INF)
        m = jnp.max(s, axis=-1, keepdims=True)
        p = jnp.exp(s - m)
        l = jnp.sum(p, axis=-1, keepdims=True)
        o = _dot(p.astype(BF16), v_ref[0, h, 0:L, :])
        o_ref[0, :, h * MLA_V:(h + 1) * MLA_V] = (o * (1.0 / l)).astype(BF16)


def _mla_attn_kernel(q_ref, k_ref, v_ref, o_ref, *, tq, S, scale):
    blk = pl.program_id(2)
    n_cls = min(MLA_KEY_CLASSES, S // tq)
    width = S // n_cls
    cls = ((blk + 1) * tq - 1) // width
    for c in range(n_cls):
        @pl.when(cls == c)
        def _(c=c):
            _mla_attn_block(q_ref, k_ref, v_ref, o_ref, blk, tq=tq, L=(c + 1) * width, scale=scale)


def _mla_attn(q, k, v, tq):
    B, H, S, _ = q.shape
    hp = MLA_HEADS_PER_STEP
    scale = (MLA_NOPE + MLA_ROPE) ** -0.5
    return pl.pallas_call(
        functools.partial(_mla_attn_kernel, tq=tq, S=S, scale=scale),
        grid=(B, H // hp, S // tq),
        in_specs=[pl.BlockSpec((1, hp, tq, MLA_QK_PAD), lambda b, h, i: (b, h, i, 0)),
                  pl.BlockSpec((1, hp, S, MLA_QK_PAD), lambda b, h, i: (b, h, 0, 0)),
                  pl.BlockSpec((1, hp, S, MLA_V), lambda b, h, i: (b, h, 0, 0))],
        out_specs=pl.BlockSpec((1, tq, hp * MLA_V), lambda b, h, i: (b, i, h)),
        out_shape=jax.ShapeDtypeStruct((B, S, H * MLA_V), BF16),
        compiler_params=_cparams("parallel", "parallel", "arbitrary"),
        name="mla_attn",
    )(q, k, v)


def _proj_ln_kernel(a_ref, w_ref, x_ref, g_ref, b_ref, o_ref):
    m = _dot(a_ref[...], w_ref[...])
    y = DN_ALPHA * x_ref[...] + m
    o_ref[...] = _layernorm_rows(y, g_ref[...], b_ref[...])


def _proj_ln(a, w, xt, gain, bias, tm):
    T = xt.shape[0]
    wb = w.astype(BF16)
    g = gain.reshape(1, -1)
    b = bias.reshape(1, -1)
    return pl.pallas_call(
        _proj_ln_kernel,
        grid=(T // tm,),
        in_specs=[pl.BlockSpec((tm, a.shape[1]), lambda i: (i, 0)),
                  pl.BlockSpec(wb.shape, lambda i: (0, 0)),
                  pl.BlockSpec((tm, D_MODEL), lambda i: (i, 0)),
                  pl.BlockSpec(g.shape, lambda i: (0, 0)),
                  pl.BlockSpec(b.shape, lambda i: (0, 0))],
        out_specs=pl.BlockSpec((tm, D_MODEL), lambda i: (i, 0)),
        out_shape=jax.ShapeDtypeStruct((T, D_MODEL), F32),
        compiler_params=_cparams("parallel"),
        name="proj_ln",
    )(a, wb, xt, g, b)


def _dsa_proj_kernel(x_ref, w_ref, ch_ref, sh_ref, ci_ref, ck_ref, sk_ref,
                     q_ref, k_ref, v_ref, qi_ref, ki_ref, wi_ref, *, w_scale):
    xb = x_ref[...].astype(BF16)
    h = _dot(xb, w_ref[...])
    ch = ch_ref[...]
    sh = sh_ref[...]
    ci = ci_ref[...]
    half = LANES // 2
    for hd in range(DSA_HEADS):
        c0 = hd * DSA_HEAD_DIM
        qh = h[:, c0:c0 + LANES]
        q_ref[0, :, c0:c0 + LANES] = (qh * ch + pltpu.roll(qh, half, 1) * sh).astype(BF16)
        kh = h[:, DSA_HD + c0:DSA_HD + c0 + LANES]
        k_ref[0, :, c0:c0 + LANES] = (kh * ch + pltpu.roll(kh, half, 1) * sh).astype(BF16)
        t = h[:, 3 * DSA_HD + c0:3 * DSA_HD + c0 + LANES] * ci
        qi_ref[0, :, c0:c0 + LANES] = (t + pltpu.roll(t, half, 1)).astype(BF16)
    v_ref[0] = h[:, 2 * DSA_HD:3 * DSA_HD].astype(BF16)
    o = 4 * DSA_HD
    ki = h[:, o:o + LANES] * ck_ref[...] + h[:, o + LANES:o + 2 * LANES] * sk_ref[...]
    ki_ref[0] = ki.astype(BF16)
    wi_ref[0] = h[:, o + 2 * LANES:o + 3 * LANES] * w_scale


def _dsa_proj(xt, B, S, w_in, tm):
    T = B * S
    n_s = S // tm
    o1 = 3 * DSA_HD + IDX_HEADS * IDX_DIM
    ih = IDX_DIM // 2
    w_qi = w_in[:, 3 * DSA_HD:o1].reshape(D_MODEL, IDX_HEADS, IDX_DIM)
    w_qi_sw = jnp.concatenate([w_qi[..., ih:], w_qi[..., :ih]], axis=-1)
    w_qi_ext = jnp.concatenate([w_qi, w_qi_sw], axis=-1).reshape(D_MODEL, IDX_HEADS * LANES)
    w_ki = w_in[:, o1:o1 + IDX_DIM]
    w_ki_sw = jnp.concatenate([w_ki[:, ih:], w_ki[:, :ih]], axis=-1)
    zpad = lambda w: jnp.concatenate([w, jnp.zeros((w.shape[0], LANES - w.shape[1]), w.dtype)], axis=-1)
    w_ext = jnp.concatenate(
        [w_in[:, :3 * DSA_HD], w_qi_ext, zpad(w_ki), zpad(w_ki_sw), zpad(w_in[:, o1 + IDX_DIM:])],
        axis=-1).astype(BF16)
    cos_h, sin_h = _rope_tables(S, DSA_HEAD_DIM)
    cos_i, sin_i = _rope_tables(S, IDX_DIM)
    ch = jnp.concatenate([cos_h, cos_h], axis=-1)
    sh = jnp.concatenate([-sin_h, sin_h], axis=-1)
    ci = jnp.concatenate([cos_i, cos_i, -sin_i, sin_i], axis=-1)
    z = jnp.zeros((S, LANES - IDX_DIM), F32)
    ck = jnp.concatenate([cos_i, cos_i, z], axis=-1)
    sk = jnp.concatenate([-sin_i, sin_i, z], axis=-1)
    tab = pl.BlockSpec((tm, LANES), lambda i: (i % n_s, 0))
    row_map = lambda i: (i // n_s, i % n_s, 0)
    wide = pl.BlockSpec((1, tm, DSA_HD), row_map)
    narrow = pl.BlockSpec((1, tm, LANES), row_map)
    w_scale = IDX_HEADS ** -0.5 * IDX_DIM ** -0.5
    return pl.pallas_call(
        functools.partial(_dsa_proj_kernel, w_scale=w_scale),
        grid=(T // tm,),
        in_specs=[pl.BlockSpec((tm, D_MODEL), lambda i: (i, 0)),
                  pl.BlockSpec(w_ext.shape, lambda i: (0, 0)),
                  tab, tab, tab, tab, tab],
        out_specs=[wide, wide, wide, wide, narrow, narrow],
        out_shape=[jax.ShapeDtypeStruct((B, S, DSA_HD), BF16),
                   jax.ShapeDtypeStruct((B, S, DSA_HD), BF16),
                   jax.ShapeDtypeStruct((B, S, DSA_HD), BF16),
                   jax.ShapeDtypeStruct((B, S, IDX_HEADS * LANES), BF16),
                   jax.ShapeDtypeStruct((B, S, LANES), BF16),
                   jax.ShapeDtypeStruct((B, S, LANES), F32)],
        compiler_params=_cparams("parallel"),
        name="dsa_proj",
    )(xt, w_ext, ch, sh, ci, ck, sk)


def _dsa_attn_block(q_ref, qi_ref, wi_ref, k_ref, v_ref, ki_ref, o_ref, blk, *, tq, L, topk, scale):
    ki = ki_ref[0, 0:L, :]
    qi = qi_ref[0]
    wi = wi_ref[0]
    score = jnp.zeros((tq, L), F32)
    for hd in range(IDX_HEADS):
        logit = _dot_nt(qi[:, hd * LANES:(hd + 1) * LANES], ki)
        score = score + wi[:, hd:hd + 1] * jnp.maximum(logit, 0.0)
    row = lax.broadcasted_iota(I32, (tq, L), 0)
    col = lax.broadcasted_iota(I32, (tq, L), 1)
    adm = col // CHUNK <= (blk * tq + row) // CHUNK
    score = jnp.where(adm, score + 0.0, NEG_INF)
    bits = lax.bitcast_convert_type(score, I32)
    u = jnp.where(bits < 0, bits ^ 0x7FFFFFFF, bits)

    def search(i, thr):
        cand = thr + lax.shift_left(jnp.int32(1), 31 - i)
        cnt = jnp.sum((u >= cand).astype(F32), axis=-1, keepdims=True)
        return jnp.where(cnt >= topk, cand, thr)
    thr = lax.fori_loop(0, 32, search, jnp.full((tq, 1), INT_MIN, I32), unroll=8)

    gt = u > thr
    eq = u == thr
    need = topk - jnp.sum(gt.astype(F32), axis=-1, keepdims=True)
    r_i = lax.broadcasted_iota(I32, (LANES, LANES), 0)
    c_i = lax.broadcasted_iota(I32, (LANES, LANES), 1)
    tri = (r_i < c_i).astype(BF16)
    eq_b = eq.astype(BF16)
    run = jnp.zeros((tq, 1), F32)
    ranks = []
    for j in range(L // LANES):
        e = eq_b[:, j * LANES:(j + 1) * LANES]
        ranks.append(_dot(e, tri) + run)
        run = run + jnp.sum(e.astype(F32), axis=-1, keepdims=True)
    rank = jnp.concatenate(ranks, axis=-1)
    sel = adm & (gt | (eq & (rank < need)))

    for hd in range(DSA_HEADS):
        c0 = hd * DSA_HEAD_DIM
        s = _dot_nt(q_ref[0, :, c0:c0 + LANES], k_ref[0, 0:L, c0:c0 + LANES]) * scale
        s = jnp.where(sel, s, NEG_INF)
        m = jnp.max(s, axis=-1, keepdims=True)
        p = jnp.exp(s - m)
        l = jnp.sum(p, axis=-1, keepdims=True)
        o = _dot(p.astype(BF16), v_ref[0, 0:L, c0:c0 + LANES])
        o_ref[0, :, c0:c0 + LANES] = (o * (1.0 / l)).astype(BF16)


def _dsa_attn_kernel(q_ref, qi_ref, wi_ref, k_ref, v_ref, ki_ref, o_ref, *, tq, S, topk, scale):
    blk = pl.program_id(1)
    n_cls = min(DSA_KEY_CLASSES, S // tq)
    width = S // n_cls
    cls = ((blk + 1) * tq - 1) // width
    for c in range(n_cls):
        @pl.when(cls == c)
        def _(c=c):
            _dsa_attn_block(q_ref, qi_ref, wi_ref, k_ref, v_ref, ki_ref, o_ref, blk,
                            tq=tq, L=(c + 1) * width, topk=topk, scale=scale)


def _dsa_attn(q, k, v, qi, ki, wi, tq):
    B, S, _ = q.shape
    topk = min(DSA_TOPK_MAX, S // 4)
    scale = DSA_HEAD_DIM ** -0.5
    qmap = lambda b, i: (b, i, 0)
    kmap = lambda b, i: (b, 0, 0)
    return pl.pallas_call(
        functools.partial(_dsa_attn_kernel, tq=tq, S=S, topk=topk, scale=scale),
        grid=(B, S // tq),
        in_specs=[pl.BlockSpec((1, tq, DSA_HD), qmap),
                  pl.BlockSpec((1, tq, IDX_HEADS * LANES), qmap),
                  pl.BlockSpec((1, tq, LANES), qmap),
                  pl.BlockSpec((1, S, DSA_HD), kmap),
                  pl.BlockSpec((1, S, DSA_HD), kmap),
                  pl.BlockSpec((1, S, LANES), kmap)],
        out_specs=pl.BlockSpec((1, tq, DSA_HD), qmap),
        out_shape=jax.ShapeDtypeStruct((B, S, DSA_HD), BF16),
        compiler_params=_cparams("parallel", "arbitrary"),
        name="dsa_attn",
    )(q, qi, wi, k, v, ki)


_PEER_PAIRS = [(a, b) for a in range(PEER_TOPK) for b in range(PEER_TOPK // (a + 1))]
_PEER_CAND_ROWS = -(-len(_PEER_PAIRS) // SUBLANES) * SUBLANES


def _sort_network(n):
    pairs = []
    p = 1
    while p < n:
        k = p
        while k >= 1:
            for j in range(k % p, n - k, 2 * k):
                for i in range(min(k, n - j - k)):
                    if (i + j) // (2 * p) == (i + j + k) // (2 * p):
                        pairs.append((i + j, i + j + k))
            k //= 2
        p *= 2
    return pairs


def _top_rows(s, n):
    groups = PEER_NKEYS // SUBLANES
    v = [s[g * SUBLANES:(g + 1) * SUBLANES, :] for g in range(groups)]
    for i, j in _sort_network(groups):
        v[i], v[j] = jnp.maximum(v[i], v[j]), jnp.minimum(v[i], v[j])
    vals = []
    for k in range(n):
        m = jnp.max(v[0], axis=0, keepdims=True)
        vals.append(m)
        won = v[0] == m
        for d in range(groups - 1 - k):
            v[d] = jnp.where(won, v[d + 1], v[d])
    return vals


def _peer_route_kernel(x_ref, wq_ref, keys_ref, lim_ref, e1_ref, r2_ref, e2_ref, cand_ref):
    tt = x_ref.shape[0]
    q = _dot(x_ref[...].astype(BF16), wq_ref[...]).astype(BF16)
    half = PEER_QDIM // 2
    pad0 = _PEER_CAND_ROWS - SUBLANES
    cand_ref[pad0:, :] = jnp.full((SUBLANES, tt), NEG_INF, F32)
    for hd in range(PEER_HEADS):
        c0 = hd * PEER_QDIM
        s1 = _dot_nt(keys_ref[0, hd], q[:, c0:c0 + half])
        s2 = _dot_nt(keys_ref[1, hd], q[:, c0 + half:c0 + PEER_QDIM])
        v1 = _top_rows(s1, PEER_TOPK)
        v2 = _top_rows(s2, PEER_TOPK)
        for k, (a, b) in enumerate(_PEER_PAIRS):
            cand_ref[k:k + 1, :] = v1[a] + v2[b]
        cand = cand_ref[...]
        cur = cand
        theta = None
        for _ in range(PEER_TOPK):
            theta = jnp.max(cur, axis=0, keepdims=True)
            cur = jnp.where(cur == theta, NEG_INF, cur)
        top = v1[0] + v2[0]
        z = jnp.sum(jnp.where(cand >= theta, jnp.exp(cand - top), 0.0), axis=0, keepdims=True)
        r2 = jnp.full(s2.shape, float(PEER_TOPK), F32)
        for b in reversed(range(PEER_TOPK)):
            r2 = jnp.where(s2 >= v2[b], float(b), r2)
        lim = jnp.zeros(s1.shape, F32)
        for b in range(PEER_TOPK):
            lim = jnp.where(s1 + v2[b] >= theta, float(b + 1), lim)
        lim_ref[hd] = lim
        r2_ref[hd] = r2.astype(BF16)
        e1_ref[hd] = jnp.exp(s1 - v1[0]) * (1.0 / z)
        e2_ref[hd] = jnp.exp(s2 - v2[0]).astype(BF16)


def _peer_route(xt, w_q, sub_keys, tt):
    T = xt.shape[0]
    wq = w_q.astype(BF16)
    keys = sub_keys.astype(BF16)
    big = jax.ShapeDtypeStruct((PEER_HEADS, PEER_NKEYS, T), F32)
    big_b = jax.ShapeDtypeStruct((PEER_HEADS, PEER_NKEYS, T), BF16)
    big_spec = pl.BlockSpec((PEER_HEADS, PEER_NKEYS, tt), lambda i: (0, 0, i))
    return pl.pallas_call(
        _peer_route_kernel,
        grid=(T // tt,),
        in_specs=[pl.BlockSpec((tt, D_MODEL), lambda i: (i, 0)),
                  pl.BlockSpec(wq.shape, lambda i: (0, 0)),
                  pl.BlockSpec(keys.shape, lambda i: (0, 0, 0, 0))],
        out_specs=[big_spec, big_spec, big_spec, big_spec],
        out_shape=[big, big, big_b, big_b],
        scratch_shapes=[pltpu.VMEM((_PEER_CAND_ROWS, tt), F32)],
        compiler_params=_cparams("parallel"),
        name="peer_route",
    )(xt, wq, keys)


PEER_CHUNK = 8 * PEER_NKEYS


BF16_SUBLANES = 2 * SUBLANES


def _row_bf16(ref, i1, hd, ls):
    row = jnp.broadcast_to(ref[i1, hd:hd + 1, ls], (BF16_SUBLANES, LANES)).astype(BF16)
    return jnp.tile(row, (PEER_NKEYS // BF16_SUBLANES, 1))


def _peer_dense_kernel(x_ref, wd_ref, wu_ref, lim_ref, e1_ref, r2_ref, e2_ref, g_ref, b_ref,
                       o_ref, xb_ref, p_ref, acc_ref, limr_ref, e1r_ref, r2s_ref, e2s_ref, *, te, tt):
    e = pl.program_id(1)

    @pl.when(e == 0)
    def _():
        xb_ref[...] = x_ref[...].astype(BF16)
        acc_ref[...] = jnp.zeros_like(acc_ref)
        r2s_ref[...] = r2_ref[...]
        e2s_ref[...] = e2_ref[...]

        def relayout(g, carry):
            g0 = pl.multiple_of(g * SUBLANES, SUBLANES)
            for hd in range(PEER_HEADS):
                limg = lim_ref[hd, pl.ds(g0, SUBLANES), :]
                e1g = e1_ref[hd, pl.ds(g0, SUBLANES), :]
                for r in range(SUBLANES):
                    limr_ref[g0 + r, hd:hd + 1, :] = limg[r:r + 1, :]
                    e1r_ref[g0 + r, hd:hd + 1, :] = e1g[r:r + 1, :]
            return carry
        lax.fori_loop(0, PEER_NKEYS // SUBLANES, relayout, 0)

    xb = xb_ref[...]
    n_ch = te // PEER_CHUNK
    i1_0 = e * (te // PEER_NKEYS)
    a_next = _dot_nt(wd_ref[0:PEER_CHUNK, :], xb)
    for c in range(n_ch):
        a_cur = a_next
        if c + 1 < n_ch:
            a_next = _dot_nt(wd_ref[(c + 1) * PEER_CHUNK:(c + 2) * PEER_CHUNK, :], xb)
        for j in range(PEER_CHUNK // PEER_NKEYS):
            i1 = i1_0 + c * (PEER_CHUNK // PEER_NKEYS) + j
            r0 = c * PEER_CHUNK + j * PEER_NKEYS
            for lg in range(tt // LANES):
                ls = slice(lg * LANES, (lg + 1) * LANES)
                gate = jnp.zeros((PEER_NKEYS, LANES), BF16)
                zero = jnp.zeros((), BF16)
                for hd in range(PEER_HEADS):
                    limb = _row_bf16(limr_ref, i1, hd, ls)
                    e1b = _row_bf16(e1r_ref, i1, hd, ls)
                    gate = gate + jnp.where(r2s_ref[hd, :, ls] < limb, e2s_ref[hd, :, ls] * e1b, zero)
                a = a_cur[j * PEER_NKEYS:(j + 1) * PEER_NKEYS, ls]
                act = 0.5 * a * (1.0 + lax.erf(a * (2.0 ** -0.5)))
                p_ref[r0:r0 + PEER_NKEYS, ls] = act.astype(BF16) * gate
        acc_ref[...] += _dot(wu_ref[c], p_ref[c * PEER_CHUNK:(c + 1) * PEER_CHUNK, :])

    @pl.when(e == pl.num_programs(1) - 1)
    def _():
        y = DN_ALPHA * x_ref[...] + acc_ref[...].T
        o_ref[...] = _layernorm_rows(y, g_ref[...], b_ref[...])


def _peer_dense(xt, w_down, w_up, route, gain, bias, tt, te):
    T = xt.shape[0]
    c1, e1, s2, e2 = route
    wd = w_down.astype(BF16)
    wu = w_up.astype(BF16).reshape(PEER_EXPERTS // PEER_CHUNK, PEER_CHUNK, D_MODEL).transpose(0, 2, 1)
    g = gain.reshape(1, -1)
    b = bias.reshape(1, -1)
    big_spec = pl.BlockSpec((PEER_HEADS, PEER_NKEYS, tt), lambda i, e: (0, 0, i))
    return pl.pallas_call(
        functools.partial(_peer_dense_kernel, te=te, tt=tt),
        grid=(T // tt, PEER_EXPERTS // te),
        in_specs=[pl.BlockSpec((tt, D_MODEL), lambda i, e: (i, 0)),
                  pl.BlockSpec((te, D_MODEL), lambda i, e: (e, 0)),
                  pl.BlockSpec((te // PEER_CHUNK, D_MODEL, PEER_CHUNK), lambda i, e: (e, 0, 0)),
                  big_spec, big_spec, big_spec, big_spec,
                  pl.BlockSpec(g.shape, lambda i, e: (0, 0)),
                  pl.BlockSpec(b.shape, lambda i, e: (0, 0))],
        out_specs=pl.BlockSpec((tt, D_MODEL), lambda i, e: (i, 0)),
        out_shape=jax.ShapeDtypeStruct((T, D_MODEL), F32),
        scratch_shapes=[pltpu.VMEM((tt, D_MODEL), BF16),
                        pltpu.VMEM((te, tt), BF16),
                        pltpu.VMEM((D_MODEL, tt), F32),
                        pltpu.VMEM((PEER_NKEYS, PEER_HEADS, tt), F32),
                        pltpu.VMEM((PEER_NKEYS, PEER_HEADS, tt), F32),
                        pltpu.VMEM((PEER_HEADS, PEER_NKEYS, tt), BF16),
                        pltpu.VMEM((PEER_HEADS, PEER_NKEYS, tt), BF16)],
        compiler_params=_cparams("parallel", "arbitrary"),
        name="peer_dense",
    )(xt, wd, wu, c1, e1, s2, e2, g, b)


def _rope_tables(seq, dim):
    inv = ROPE_THETA ** (-jnp.arange(0, dim, 2, dtype=F32) / dim)
    ang = jnp.arange(seq, dtype=F32)[:, None] * inv[None, :]
    return jnp.cos(ang), jnp.sin(ang)


def _tiles(S, T):
    row = min(512, S)
    tq_mla = min(256, S)
    tq_dsa = min(256, S)
    tt_route = min(256, T)
    tt_dense = min(512, T)
    te_dense = 2048
    return row, tq_mla, tq_dsa, tt_route, tt_dense, te_dense


def _peer_layer(xt, w_q, sub_keys, w_down, w_up, gain, bias, tt_route, tt_dense, te_dense):
    route = _peer_route(xt, w_q, sub_keys, tt_route)
    return _peer_dense(xt, w_down, w_up, route, gain, bias, tt_dense, te_dense)


def kernel(x, mla_w_in, mla_q_norm, mla_kv_norm, mla_w_uq, mla_w_ukv, mla_w_o,
           dsa_w_in, dsa_w_o, peer_w_q, peer_sub_keys, peer_w_down, peer_w_up,
           ln_gain, ln_bias):
    B, S, D = x.shape
    T = B * S
    row, tq_mla, tq_dsa, tt_route, tt_dense, te_dense = _tiles(S, T)
    xt = x.reshape(T, D)

    cos, sin = _rope_tables(S, MLA_ROPE)
    q, k, v = _mla_proj(xt, B, S, mla_w_in[0], mla_q_norm[0], mla_kv_norm[0],
                        mla_w_uq[0], mla_w_ukv[0], cos, sin, row)
    o = _mla_attn(q, k, v, tq_mla).reshape(T, MLA_HEADS * MLA_V)
    xt = _proj_ln(o, mla_w_o[0], xt, ln_gain[0, 0], ln_bias[0, 0], row)
    xt = _peer_layer(xt, peer_w_q[0], peer_sub_keys[0], peer_w_down[0], peer_w_up[0],
                     ln_gain[0, 1], ln_bias[0, 1], tt_route, tt_dense, te_dense)

    q, k, v, qi, ki, wi = _dsa_proj(xt, B, S, dsa_w_in[0], row)
    o = _dsa_attn(q, k, v, qi, ki, wi, tq_dsa).reshape(T, DSA_HD)
    xt = _proj_ln(o, dsa_w_o[0], xt, ln_gain[1, 0], ln_bias[1, 0], row)
    xt = _peer_layer(xt, peer_w_q[1], peer_sub_keys[1], peer_w_down[1], peer_w_up[1],
                     ln_gain[1, 1], ln_bias[1, 1], tt_route, tt_dense, te_dense)
    return xt.reshape(B, S, D)
```

```python
import functools

import jax
import jax.numpy as jnp
from jax import lax
from jax.experimental import pallas as pl
from jax.experimental.pallas import tpu as pltpu

F32 = jnp.float32
BF16 = jnp.bfloat16
I32 = jnp.int32

D_MODEL = 1024
DEPTH = 2
CHUNK = 64
ROPE_THETA = 10000.0
LN_EPS = 1e-5
RMS_EPS = 1e-6
DN_ALPHA = (2 * DEPTH) ** 0.25

MLA_HEADS = 8
MLA_NOPE = 128
MLA_ROPE = 64
MLA_V = 128
MLA_Q_RANK = 384
MLA_KV_RANK = 256
MLA_QK_PAD = 256

DSA_HEADS = 8
DSA_HEAD_DIM = 128
IDX_HEADS = 8
IDX_DIM = 64
DSA_TOPK_MAX = 256
DSA_HD = DSA_HEADS * DSA_HEAD_DIM

PEER_HEADS = 8
PEER_NKEYS = 128
PEER_EXPERTS = PEER_NKEYS * PEER_NKEYS
PEER_QDIM = 256
PEER_TOPK = 16

LANES = 128
SUBLANES = 8
VMEM_LIMIT = 56 * 1024 * 1024

NEG_INF = float("-inf")
INT_MIN = -2 ** 31


def _cparams(*sem):
    return pltpu.CompilerParams(dimension_semantics=sem, vmem_limit_bytes=VMEM_LIMIT)


def _dot(a, b):
    return jnp.dot(a, b, preferred_element_type=F32)


def _dot_nt(a, b):
    return lax.dot_general(a, b, (((1,), (1,)), ((), ())), preferred_element_type=F32)


def _layernorm_rows(y, g, b):
    mu = jnp.mean(y, axis=-1, keepdims=True)
    yc = y - mu
    var = jnp.mean(yc * yc, axis=-1, keepdims=True)
    return yc * lax.rsqrt(var + LN_EPS) * g + b


def _rms_rows(h, g):
    ms = jnp.mean(h * h, axis=-1, keepdims=True)
    return h * lax.rsqrt(ms + RMS_EPS) * g


def _mla_proj_kernel(x_ref, win_ref, qn_ref, kvn_ref, wuq_ref, wukv_ref, cc_ref, ss_ref,
                     q_ref, k_ref, v_ref):
    xb = x_ref[...].astype(BF16)
    h = _dot(xb, win_ref[...])
    cq = h[:, :MLA_Q_RANK]
    ckv = h[:, MLA_Q_RANK:MLA_Q_RANK + MLA_KV_RANK]
    o = MLA_Q_RANK + MLA_KV_RANK
    cc = cc_ref[...]
    ss = ss_ref[...]
    k_rope = h[:, o:o + LANES] * cc + h[:, o + LANES:o + 2 * LANES] * ss
    qall = _dot(_rms_rows(cq, qn_ref[...]).astype(BF16), wuq_ref[...])
    kvall = _dot(_rms_rows(ckv, kvn_ref[...]).astype(BF16), wukv_ref[...])
    k_rope_b = k_rope.astype(BF16)
    for hd in range(MLA_HEADS):
        b0 = hd * 3 * LANES
        q_ref[0, hd, :, 0:LANES] = qall[:, b0:b0 + LANES].astype(BF16)
        q_rope = qall[:, b0 + LANES:b0 + 2 * LANES] * cc + qall[:, b0 + 2 * LANES:b0 + 3 * LANES] * ss
        q_ref[0, hd, :, LANES:2 * LANES] = q_rope.astype(BF16)
        k_ref[0, hd, :, 0:LANES] = kvall[:, hd * LANES:(hd + 1) * LANES].astype(BF16)
        k_ref[0, hd, :, LANES:2 * LANES] = k_rope_b
        v0 = MLA_HEADS * MLA_NOPE + hd * MLA_V
        v_ref[0, hd] = kvall[:, v0:v0 + MLA_V].astype(BF16)


def _mla_proj(xt, B, S, w_in, q_norm, kv_norm, w_uq, w_ukv, cos, sin, tm):
    T = B * S
    n_s = S // tm
    half = MLA_ROPE // 2
    def swap_rope(w):
        return jnp.concatenate([w[..., half:], w[..., :half]], axis=-1)
    zpad = lambda w: jnp.concatenate([w, jnp.zeros_like(w)], axis=-1)
    w_kr = w_in[:, MLA_Q_RANK + MLA_KV_RANK:]
    win_ext = jnp.concatenate(
        [w_in[:, :MLA_Q_RANK + MLA_KV_RANK], zpad(w_kr), zpad(swap_rope(w_kr))], axis=-1).astype(BF16)
    uq_nope = w_uq[:, :, :MLA_NOPE]
    uq_rope = w_uq[:, :, MLA_NOPE:]
    wuq_ext = jnp.concatenate([uq_nope, zpad(uq_rope), zpad(swap_rope(uq_rope))], axis=-1)
    wuq_ext = wuq_ext.reshape(MLA_Q_RANK, MLA_HEADS * 3 * LANES).astype(BF16)
    wukv_ext = jnp.concatenate(
        [w_ukv[:, :, :MLA_NOPE].reshape(MLA_KV_RANK, -1), w_ukv[:, :, MLA_NOPE:].reshape(MLA_KV_RANK, -1)],
        axis=-1).astype(BF16)
    z = jnp.zeros((S, LANES - MLA_ROPE), F32)
    cc = jnp.concatenate([cos, cos, z], axis=-1)
    ss = jnp.concatenate([-sin, sin, z], axis=-1)
    full = lambda a: pl.BlockSpec(a.shape, lambda i: (0,) * a.ndim)
    qn = q_norm.reshape(1, -1)
    kvn = kv_norm.reshape(1, -1)
    head_map = lambda i: (i // n_s, 0, i % n_s, 0)
    return pl.pallas_call(
        _mla_proj_kernel,
        grid=(T // tm,),
        in_specs=[pl.BlockSpec((tm, D_MODEL), lambda i: (i, 0)),
                  full(win_ext), full(qn), full(kvn), full(wuq_ext), full(wukv_ext),
                  pl.BlockSpec((tm, LANES), lambda i: (i % n_s, 0)),
                  pl.BlockSpec((tm, LANES), lambda i: (i % n_s, 0))],
        out_specs=[pl.BlockSpec((1, MLA_HEADS, tm, MLA_QK_PAD), head_map),
                   pl.BlockSpec((1, MLA_HEADS, tm, MLA_QK_PAD), head_map),
                   pl.BlockSpec((1, MLA_HEADS, tm, MLA_V), head_map)],
        out_shape=[jax.ShapeDtypeStruct((B, MLA_HEADS, S, MLA_QK_PAD), BF16),
                   jax.ShapeDtypeStruct((B, MLA_HEADS, S, MLA_QK_PAD), BF16),
                   jax.ShapeDtypeStruct((B, MLA_HEADS, S, MLA_V), BF16)],
        compiler_params=_cparams("parallel"),
        name="mla_proj",
    )(xt, win_ext, qn, kvn, wuq_ext, wukv_ext, cc, ss)


MLA_HEADS_PER_STEP = 4
MLA_KEY_CLASSES = 8
DSA_KEY_CLASSES = 8


def _mla_attn_block(q_ref, k_ref, v_ref, o_ref, blk, *, tq, L, scale):
    row = lax.broadcasted_iota(I32, (tq, L), 0)
    col = lax.broadcasted_iota(I32, (tq, L), 1)
    adm = col // CHUNK <= (blk * tq + row) // CHUNK
    for h in range(MLA_HEADS_PER_STEP):
        s = _dot_nt(q_ref[0, h], k_ref[0, h, 0:L, :]) * scale
        s = jnp.where(adm, s, NEG_INF)
        m = jnp.max(s, axis=-1, keepdims=True)
        p = jnp.exp(s - m)
        l = jnp.sum(p, axis=-1, keepdims=True)
        o = _dot(p.astype(BF16), v_ref[0, h, 0:L, :])
        o_ref[0, :, h * MLA_V:(h + 1) * MLA_V] = (o * (1.0 / l)).astype(BF16)


def _mla_attn_kernel(q_ref, k_ref, v_ref, o_ref, *, tq, S, scale):
    blk = pl.program_id(2)
    n_cls = min(MLA_KEY_CLASSES, S // tq)
    width = S // n_cls
    cls = ((blk + 1) * tq - 1) // width
    for c in range(n_cls):
        @pl.when(cls == c)
        def _(c=c):
            _mla_attn_block(q_ref, k_ref, v_ref, o_ref, blk, tq=tq, L=(c + 1) * width, scale=scale)


def _mla_attn(q, k, v, tq):
    B, H, S, _ = q.shape
    hp = MLA_HEADS_PER_STEP
    scale = (MLA_NOPE + MLA_ROPE) ** -0.5
    return pl.pallas_call(
        functools.partial(_mla_attn_kernel, tq=tq, S=S, scale=scale),
        grid=(B, H // hp, S // tq),
        in_specs=[pl.BlockSpec((1, hp, tq, MLA_QK_PAD), lambda b, h, i: (b, h, i, 0)),
                  pl.BlockSpec((1, hp, S, MLA_QK_PAD), lambda b, h, i: (b, h, 0, 0)),
                  pl.BlockSpec((1, hp, S, MLA_V), lambda b, h, i: (b, h, 0, 0))],
        out_specs=pl.BlockSpec((1, tq, hp * MLA_V), lambda b, h, i: (b, i, h)),
        out_shape=jax.ShapeDtypeStruct((B, S, H * MLA_V), BF16),
        compiler_params=_cparams("parallel", "parallel", "arbitrary"),
        name="mla_attn",
    )(q, k, v)


def _proj_ln_kernel(a_ref, w_ref, x_ref, g_ref, b_ref, o_ref):
    m = _dot(a_ref[...], w_ref[...])
    y = DN_ALPHA * x_ref[...] + m
    o_ref[...] = _layernorm_rows(y, g_ref[...], b_ref[...])


def _proj_ln(a, w, xt, gain, bias, tm):
    T = xt.shape[0]
    wb = w.astype(BF16)
    g = gain.reshape(1, -1)
    b = bias.reshape(1, -1)
    return pl.pallas_call(
        _proj_ln_kernel,
        grid=(T // tm,),
        in_specs=[pl.BlockSpec((tm, a.shape[1]), lambda i: (i, 0)),
                  pl.BlockSpec(wb.shape, lambda i: (0, 0)),
                  pl.BlockSpec((tm, D_MODEL), lambda i: (i, 0)),
                  pl.BlockSpec(g.shape, lambda i: (0, 0)),
                  pl.BlockSpec(b.shape, lambda i: (0, 0))],
        out_specs=pl.BlockSpec((tm, D_MODEL), lambda i: (i, 0)),
        out_shape=jax.ShapeDtypeStruct((T, D_MODEL), F32),
        compiler_params=_cparams("parallel"),
        name="proj_ln",
    )(a, wb, xt, g, b)


def _dsa_proj_kernel(x_ref, w_ref, ch_ref, sh_ref, ci_ref, ck_ref, sk_ref,
                     q_ref, k_ref, v_ref, qi_ref, ki_ref, wi_ref, *, w_scale):
    xb = x_ref[...].astype(BF16)
    h = _dot(xb, w_ref[...])
    ch = ch_ref[...]
    sh = sh_ref[...]
    ci = ci_ref[...]
    half = LANES // 2
    for hd in range(DSA_HEADS):
        c0 = hd * DSA_HEAD_DIM
        qh = h[:, c0:c0 + LANES]
        q_ref[0, :, c0:c0 + LANES] = (qh * ch + pltpu.roll(qh, half, 1) * sh).astype(BF16)
        kh = h[:, DSA_HD + c0:DSA_HD + c0 + LANES]
        k_ref[0, :, c0:c0 + LANES] = (kh * ch + pltpu.roll(kh, half, 1) * sh).astype(BF16)
        t = h[:, 3 * DSA_HD + c0:3 * DSA_HD + c0 + LANES] * ci
        qi_ref[0, :, c0:c0 + LANES] = (t + pltpu.roll(t, half, 1)).astype(BF16)
    v_ref[0] = h[:, 2 * DSA_HD:3 * DSA_HD].astype(BF16)
    o = 4 * DSA_HD
    ki = h[:, o:o + LANES] * ck_ref[...] + h[:, o + LANES:o + 2 * LANES] * sk_ref[...]
    ki_ref[0] = ki.astype(BF16)
    wi_ref[0] = h[:, o + 2 * LANES:o + 3 * LANES] * w_scale


def _dsa_proj(xt, B, S, w_in, tm):
    T = B * S
    n_s = S // tm
    o1 = 3 * DSA_HD + IDX_HEADS * IDX_DIM
    ih = IDX_DIM // 2
    w_qi = w_in[:, 3 * DSA_HD:o1].reshape(D_MODEL, IDX_HEADS, IDX_DIM)
    w_qi_sw = jnp.concatenate([w_qi[..., ih:], w_qi[..., :ih]], axis=-1)
    w_qi_ext = jnp.concatenate([w_qi, w_qi_sw], axis=-1).reshape(D_MODEL, IDX_HEADS * LANES)
    w_ki = w_in[:, o1:o1 + IDX_DIM]
    w_ki_sw = jnp.concatenate([w_ki[:, ih:], w_ki[:, :ih]], axis=-1)
    zpad = lambda w: jnp.concatenate([w, jnp.zeros((w.shape[0], LANES - w.shape[1]), w.dtype)], axis=-1)
    w_ext = jnp.concatenate(
        [w_in[:, :3 * DSA_HD], w_qi_ext, zpad(w_ki), zpad(w_ki_sw), zpad(w_in[:, o1 + IDX_DIM:])],
        axis=-1).astype(BF16)
    cos_h, sin_h = _rope_tables(S, DSA_HEAD_DIM)
    cos_i, sin_i = _rope_tables(S, IDX_DIM)
    ch = jnp.concatenate([cos_h, cos_h], axis=-1)
    sh = jnp.concatenate([-sin_h, sin_h], axis=-1)
    ci = jnp.concatenate([cos_i, cos_i, -sin_i, sin_i], axis=-1)
    z = jnp.zeros((S, LANES - IDX_DIM), F32)
    ck = jnp.concatenate([cos_i, cos_i, z], axis=-1)
    sk = jnp.concatenate([-sin_i, sin_i, z], axis=-1)
    tab = pl.BlockSpec((tm, LANES), lambda i: (i % n_s, 0))
    row_map = lambda i: (i // n_s, i % n_s, 0)
    wide = pl.BlockSpec((1, tm, DSA_HD), row_map)
    narrow = pl.BlockSpec((1, tm, LANES), row_map)
    w_scale = IDX_HEADS ** -0.5 * IDX_DIM ** -0.5
    return pl.pallas_call(
        functools.partial(_dsa_proj_kernel, w_scale=w_scale),
        grid=(T // tm,),
        in_specs=[pl.BlockSpec((tm, D_MODEL), lambda i: (i, 0)),
                  pl.BlockSpec(w_ext.shape, lambda i: (0, 0)),
                  tab, tab, tab, tab, tab],
        out_specs=[wide, wide, wide, wide, narrow, narrow],
        out_shape=[jax.ShapeDtypeStruct((B, S, DSA_HD), BF16),
                   jax.ShapeDtypeStruct((B, S, DSA_HD), BF16),
                   jax.ShapeDtypeStruct((B, S, DSA_HD), BF16),
                   jax.ShapeDtypeStruct((B, S, IDX_HEADS * LANES), BF16),
                   jax.ShapeDtypeStruct((B, S, LANES), BF16),
                   jax.ShapeDtypeStruct((B, S, LANES), F32)],
        compiler_params=_cparams("parallel"),
        name="dsa_proj",
    )(xt, w_ext, ch, sh, ci, ck, sk)


def _dsa_attn_block(q_ref, qi_ref, wi_ref, k_ref, v_ref, ki_ref, o_ref, blk, *, tq, L, topk, scale):
    ki = ki_ref[0, 0:L, :]
    qi = qi_ref[0]
    wi = wi_ref[0]
    score = jnp.zeros((tq, L), F32)
    for hd in range(IDX_HEADS):
        logit = _dot_nt(qi[:, hd * LANES:(hd + 1) * LANES], ki)
        score = score + wi[:, hd:hd + 1] * jnp.maximum(logit, 0.0)
    row = lax.broadcasted_iota(I32, (tq, L), 0)
    col = lax.broadcasted_iota(I32, (tq, L), 1)
    adm = col // CHUNK <= (blk * tq + row) // CHUNK
    score = jnp.where(adm, score + 0.0, NEG_INF)
    bits = lax.bitcast_convert_type(score, I32)
    u = jnp.where(bits < 0, bits ^ 0x7FFFFFFF, bits)

    def search(i, thr):
        cand = thr + lax.shift_left(jnp.int32(1), 31 - i)
        cnt = jnp.sum((u >= cand).astype(F32), axis=-1, keepdims=True)
        return jnp.where(cnt >= topk, cand, thr)
    thr = lax.fori_loop(0, 32, search, jnp.full((tq, 1), INT_MIN, I32), unroll=8)

    gt = u > thr
    eq = u == thr
    need = topk - jnp.sum(gt.astype(F32), axis=-1, keepdims=True)
    r_i = lax.broadcasted_iota(I32, (LANES, LANES), 0)
    c_i = lax.broadcasted_iota(I32, (LANES, LANES), 1)
    tri = (r_i < c_i).astype(BF16)
    eq_b = eq.astype(BF16)
    run = jnp.zeros((tq, 1), F32)
    ranks = []
    for j in range(L // LANES):
        e = eq_b[:, j * LANES:(j + 1) * LANES]
        ranks.append(_dot(e, tri) + run)
        run = run + jnp.sum(e.astype(F32), axis=-1, keepdims=True)
    rank = jnp.concatenate(ranks, axis=-1)
    sel = adm & (gt | (eq & (rank < need)))

    for hd in range(DSA_HEADS):
        c0 = hd * DSA_HEAD_DIM
        s = _dot_nt(q_ref[0, :, c0:c0 + LANES], k_ref[0, 0:L, c0:c0 + LANES]) * scale
        s = jnp.where(sel, s, NEG_INF)
        m = jnp.max(s, axis=-1, keepdims=True)
        p = jnp.exp(s - m)
        l = jnp.sum(p, axis=-1, keepdims=True)
        o = _dot(p.astype(BF16), v_ref[0, 0:L, c0:c0 + LANES])
        o_ref[0, :, c0:c0 + LANES] = (o * (1.0 / l)).astype(BF16)


def _dsa_attn_kernel(q_ref, qi_ref, wi_ref, k_ref, v_ref, ki_ref, o_ref, *, tq, S, topk, scale):
    blk = pl.program_id(1)
    n_cls = min(DSA_KEY_CLASSES, S // tq)
    width = S // n_cls
    cls = ((blk + 1) * tq - 1) // width
    for c in range(n_cls):
        @pl.when(cls == c)
        def _(c=c):
            _dsa_attn_block(q_ref, qi_ref, wi_ref, k_ref, v_ref, ki_ref, o_ref, blk,
                            tq=tq, L=(c + 1) * width, topk=topk, scale=scale)


def _dsa_attn(q, k, v, qi, ki, wi, tq):
    B, S, _ = q.shape
    topk = min(DSA_TOPK_MAX, S // 4)
    scale = DSA_HEAD_DIM ** -0.5
    qmap = lambda b, i: (b, i, 0)
    kmap = lambda b, i: (b, 0, 0)
    return pl.pallas_call(
        functools.partial(_dsa_attn_kernel, tq=tq, S=S, topk=topk, scale=scale),
        grid=(B, S // tq),
        in_specs=[pl.BlockSpec((1, tq, DSA_HD), qmap),
                  pl.BlockSpec((1, tq, IDX_HEADS * LANES), qmap),
                  pl.BlockSpec((1, tq, LANES), qmap),
                  pl.BlockSpec((1, S, DSA_HD), kmap),
                  pl.BlockSpec((1, S, DSA_HD), kmap),
                  pl.BlockSpec((1, S, LANES), kmap)],
        out_specs=pl.BlockSpec((1, tq, DSA_HD), qmap),
        out_shape=jax.ShapeDtypeStruct((B, S, DSA_HD), BF16),
        compiler_params=_cparams("parallel", "arbitrary"),
        name="dsa_attn",
    )(q, qi, wi, k, v, ki)


_PEER_PAIRS = [(a, b) for a in range(PEER_TOPK) for b in range(PEER_TOPK // (a + 1))]
_PEER_CAND_ROWS = -(-len(_PEER_PAIRS) // SUBLANES) * SUBLANES


def _sort_network(n):
    pairs = []
    p = 1
    while p < n:
        k = p
        while k >= 1:
            for j in range(k % p, n - k, 2 * k):
                for i in range(min(k, n - j - k)):
                    if (i + j) // (2 * p) == (i + j + k) // (2 * p):
                        pairs.append((i + j, i + j + k))
            k //= 2
        p *= 2
    return pairs


def _top_rows(s, n):
    groups = PEER_NKEYS // SUBLANES
    v = [s[g * SUBLANES:(g + 1) * SUBLANES, :] for g in range(groups)]
    for i, j in _sort_network(groups):
        v[i], v[j] = jnp.maximum(v[i], v[j]), jnp.minimum(v[i], v[j])
    vals = []
    for k in range(n):
        m = jnp.max(v[0], axis=0, keepdims=True)
        vals.append(m)
        won = v[0] == m
        for d in range(groups - 1 - k):
            v[d] = jnp.where(won, v[d + 1], v[d])
    return vals


def _peer_route_kernel(x_ref, wq_ref, keys_ref, lim_ref, e1_ref, r2_ref, e2_ref, cand_ref):
    tt = x_ref.shape[0]
    q = _dot(x_ref[...].astype(BF16), wq_ref[...]).astype(BF16)
    half = PEER_QDIM // 2
    pad0 = _PEER_CAND_ROWS - SUBLANES
    cand_ref[pad0:, :] = jnp.full((SUBLANES, tt), NEG_INF, F32)
    for hd in range(PEER_HEADS):
        c0 = hd * PEER_QDIM
        s1 = _dot_nt(keys_ref[0, hd], q[:, c0:c0 + half])
        s2 = _dot_nt(keys_ref[1, hd], q[:, c0 + half:c0 + PEER_QDIM])
        v1 = _top_rows(s1, PEER_TOPK)
        v2 = _top_rows(s2, PEER_TOPK)
        for k, (a, b) in enumerate(_PEER_PAIRS):
            cand_ref[k:k + 1, :] = v1[a] + v2[b]
        cand = cand_ref[...]
        cur = cand
        theta = None
        for _ in range(PEER_TOPK):
            theta = jnp.max(cur, axis=0, keepdims=True)
            cur = jnp.where(cur == theta, NEG_INF, cur)
        top = v1[0] + v2[0]
        z = jnp.sum(jnp.where(cand >= theta, jnp.exp(cand - top), 0.0), axis=0, keepdims=True)
        r2 = jnp.full(s2.shape, float(PEER_TOPK), F32)
        for b in reversed(range(PEER_TOPK)):
            r2 = jnp.where(s2 >= v2[b], float(b), r2)
        lim = jnp.zeros(s1.shape, F32)
        for b in range(PEER_TOPK):
            lim = jnp.where(s1 + v2[b] >= theta, float(b + 1), lim)
        lim_ref[hd] = lim
        r2_ref[hd] = r2.astype(BF16)
        e1_ref[hd] = jnp.exp(s1 - v1[0]) * (1.0 / z)
        e2_ref[hd] = jnp.exp(s2 - v2[0]).astype(BF16)


def _peer_route(xt, w_q, sub_keys, tt):
    T = xt.shape[0]
    wq = w_q.astype(BF16)
    keys = sub_keys.astype(BF16)
    big = jax.ShapeDtypeStruct((PEER_HEADS, PEER_NKEYS, T), F32)
    big_b = jax.ShapeDtypeStruct((PEER_HEADS, PEER_NKEYS, T), BF16)
    big_spec = pl.BlockSpec((PEER_HEADS, PEER_NKEYS, tt), lambda i: (0, 0, i))
    return pl.pallas_call(
        _peer_route_kernel,
        grid=(T // tt,),
        in_specs=[pl.BlockSpec((tt, D_MODEL), lambda i: (i, 0)),
                  pl.BlockSpec(wq.shape, lambda i: (0, 0)),
                  pl.BlockSpec(keys.shape, lambda i: (0, 0, 0, 0))],
        out_specs=[big_spec, big_spec, big_spec, big_spec],
        out_shape=[big, big, big_b, big_b],
        scratch_shapes=[pltpu.VMEM((_PEER_CAND_ROWS, tt), F32)],
        compiler_params=_cparams("parallel"),
        name="peer_route",
    )(xt, wq, keys)


PEER_CHUNK = 8 * PEER_NKEYS


BF16_SUBLANES = 2 * SUBLANES


def _row_bf16(ref, i1, hd, ls):
    row = jnp.broadcast_to(ref[i1, hd:hd + 1, ls], (BF16_SUBLANES, LANES)).astype(BF16)
    return jnp.tile(row, (PEER_NKEYS // BF16_SUBLANES, 1))


def _peer_dense_kernel(x_ref, wd_ref, wu_ref, lim_ref, e1_ref, r2_ref, e2_ref, g_ref, b_ref,
                       o_ref, xb_ref, p_ref, acc_ref, limr_ref, e1r_ref, r2s_ref, e2s_ref, *, te, tt):
    e = pl.program_id(1)

    @pl.when(e == 0)
    def _():
        xb_ref[...] = x_ref[...].astype(BF16)
        acc_ref[...] = jnp.zeros_like(acc_ref)
        r2s_ref[...] = r2_ref[...]
        e2s_ref[...] = e2_ref[...]

        def relayout(g, carry):
            g0 = pl.multiple_of(g * SUBLANES, SUBLANES)
            for hd in range(PEER_HEADS):
                limg = lim_ref[hd, pl.ds(g0, SUBLANES), :]
                e1g = e1_ref[hd, pl.ds(g0, SUBLANES), :]
                for r in range(SUBLANES):
                    limr_ref[g0 + r, hd:hd + 1, :] = limg[r:r + 1, :]
                    e1r_ref[g0 + r, hd:hd + 1, :] = e1g[r:r + 1, :]
            return carry
        lax.fori_loop(0, PEER_NKEYS // SUBLANES, relayout, 0)

    xb = xb_ref[...]
    n_ch = te // PEER_CHUNK
    i1_0 = e * (te // PEER_NKEYS)
    a_next = _dot_nt(wd_ref[0:PEER_CHUNK, :], xb)
    for c in range(n_ch):
        a_cur = a_next
        if c + 1 < n_ch:
            a_next = _dot_nt(wd_ref[(c + 1) * PEER_CHUNK:(c + 2) * PEER_CHUNK, :], xb)
        for j in range(PEER_CHUNK // PEER_NKEYS):
            i1 = i1_0 + c * (PEER_CHUNK // PEER_NKEYS) + j
            r0 = c * PEER_CHUNK + j * PEER_NKEYS
            for lg in range(tt // LANES):
                ls = slice(lg * LANES, (lg + 1) * LANES)
                gate = jnp.zeros((PEER_NKEYS, LANES), BF16)
                zero = jnp.zeros((), BF16)
                for hd in range(PEER_HEADS):
                    limb = _row_bf16(limr_ref, i1, hd, ls)
                    e1b = _row_bf16(e1r_ref, i1, hd, ls)
                    gate = gate + jnp.where(r2s_ref[hd, :, ls] < limb, e2s_ref[hd, :, ls] * e1b, zero)
                a = a_cur[j * PEER_NKEYS:(j + 1) * PEER_NKEYS, ls]
                act = 0.5 * a * (1.0 + lax.erf(a * (2.0 ** -0.5)))
                p_ref[r0:r0 + PEER_NKEYS, ls] = act.astype(BF16) * gate
        acc_ref[...] += _dot(wu_ref[c], p_ref[c * PEER_CHUNK:(c + 1) * PEER_CHUNK, :])

    @pl.when(e == pl.num_programs(1) - 1)
    def _():
        y = DN_ALPHA * x_ref[...] + acc_ref[...].T
        o_ref[...] = _layernorm_rows(y, g_ref[...], b_ref[...])


def _peer_dense(xt, w_down, w_up, route, gain, bias, tt, te):
    T = xt.shape[0]
    c1, e1, s2, e2 = route
    wd = w_down.astype(BF16)
    wu = w_up.astype(BF16).reshape(PEER_EXPERTS // PEER_CHUNK, PEER_CHUNK, D_MODEL).transpose(0, 2, 1)
    g = gain.reshape(1, -1)
    b = bias.reshape(1, -1)
    big_spec = pl.BlockSpec((PEER_HEADS, PEER_NKEYS, tt), lambda i, e: (0, 0, i))
    return pl.pallas_call(
        functools.partial(_peer_dense_kernel, te=te, tt=tt),
        grid=(T // tt, PEER_EXPERTS // te),
        in_specs=[pl.BlockSpec((tt, D_MODEL), lambda i, e: (i, 0)),
                  pl.BlockSpec((te, D_MODEL), lambda i, e: (e, 0)),
                  pl.BlockSpec((te // PEER_CHUNK, D_MODEL, PEER_CHUNK), lambda i, e: (e, 0, 0)),
                  big_spec, big_spec, big_spec, big_spec,
                  pl.BlockSpec(g.shape, lambda i, e: (0, 0)),
                  pl.BlockSpec(b.shape, lambda i, e: (0, 0))],
        out_specs=pl.BlockSpec((tt, D_MODEL), lambda i, e: (i, 0)),
        out_shape=jax.ShapeDtypeStruct((T, D_MODEL), F32),
        scratch_shapes=[pltpu.VMEM((tt, D_MODEL), BF16),
                        pltpu.VMEM((te, tt), BF16),
                        pltpu.VMEM((D_MODEL, tt), F32),
                        pltpu.VMEM((PEER_NKEYS, PEER_HEADS, tt), F32),
                        pltpu.VMEM((PEER_NKEYS, PEER_HEADS, tt), F32),
                        pltpu.VMEM((PEER_HEADS, PEER_NKEYS, tt), BF16),
                        pltpu.VMEM((PEER_HEADS, PEER_NKEYS, tt), BF16)],
        compiler_params=_cparams("parallel", "arbitrary"),
        name="peer_dense",
    )(xt, wd, wu, c1, e1, s2, e2, g, b)


def _rope_tables(seq, dim):
    inv = ROPE_THETA ** (-jnp.arange(0, dim, 2, dtype=F32) / dim)
    ang = jnp.arange(seq, dtype=F32)[:, None] * inv[None, :]
    return jnp.cos(ang), jnp.sin(ang)


def _tiles(S, T):
    row = min(512, S)
    tq_mla = min(256, S)
    tq_dsa = min(256, S)
    tt_route = min(256, T)
    tt_dense = min(512, T)
    te_dense = 2048
    return row, tq_mla, tq_dsa, tt_route, tt_dense, te_dense


def _peer_layer(xt, w_q, sub_keys, w_down, w_up, gain, bias, tt_route, tt_dense, te_dense):
    route = _peer_route(xt, w_q, sub_keys, tt_route)
    return _peer_dense(xt, w_down, w_up, route, gain, bias, tt_dense, te_dense)


def kernel(x, mla_w_in, mla_q_norm, mla_kv_norm, mla_w_uq, mla_w_ukv, mla_w_o,
           dsa_w_in, dsa_w_o, peer_w_q, peer_sub_keys, peer_w_down, peer_w_up,
           ln_gain, ln_bias):
    B, S, D = x.shape
    T = B * S
    row, tq_mla, tq_dsa, tt_route, tt_dense, te_dense = _tiles(S, T)
    xt = x.reshape(T, D)

    cos, sin = _rope_tables(S, MLA_ROPE)
    q, k, v = _mla_proj(xt, B, S, mla_w_in[0], mla_q_norm[0], mla_kv_norm[0],
                        mla_w_uq[0], mla_w_ukv[0], cos, sin, row)
    o = _mla_attn(q, k, v, tq_mla).reshape(T, MLA_HEADS * MLA_V)
    xt = _proj_ln(o, mla_w_o[0], xt, ln_gain[0, 0], ln_bias[0, 0], row)
    xt = _peer_layer(xt, peer_w_q[0], peer_sub_keys[0], peer_w_down[0], peer_w_up[0],
                     ln_gain[0, 1], ln_bias[0, 1], tt_route, tt_dense, te_dense)

    q, k, v, qi, ki, wi = _dsa_proj(xt, B, S, dsa_w_in[0], row)
    o = _dsa_attn(q, k, v, qi, ki, wi, tq_dsa).reshape(T, DSA_HD)
    xt = _proj_ln(o, dsa_w_o[0], xt, ln_gain[1, 0], ln_bias[1, 0], row)
    xt = _peer_layer(xt, peer_w_q[1], peer_sub_keys[1], peer_w_down[1], peer_w_up[1],
                     ln_gain[1, 1], ln_bias[1, 1], tt_route, tt_dense, te_dense)
    return xt.reshape(B, S, D)
```

```python
import functools

import jax
import jax.numpy as jnp
from jax import lax
from jax.experimental import pallas as pl
from jax.experimental.pallas import tpu as pltpu

F32 = jnp.float32
BF16 = jnp.bfloat16
I32 = jnp.int32

D_MODEL = 1024
DEPTH = 2
CHUNK = 64
ROPE_THETA = 10000.0
LN_EPS = 1e-5
RMS_EPS = 1e-6
DN_ALPHA = (2 * DEPTH) ** 0.25

MLA_HEADS = 8
MLA_NOPE = 128
MLA_ROPE = 64
MLA_V = 128
MLA_Q_RANK = 384
MLA_KV_RANK = 256
MLA_QK_PAD = 256

DSA_HEADS = 8
DSA_HEAD_DIM = 128
IDX_HEADS = 8
IDX_DIM = 64
DSA_TOPK_MAX = 256
DSA_HD = DSA_HEADS * DSA_HEAD_DIM

PEER_HEADS = 8
PEER_NKEYS = 128
PEER_EXPERTS = PEER_NKEYS * PEER_NKEYS
PEER_QDIM = 256
PEER_TOPK = 16

LANES = 128
SUBLANES = 8
VMEM_LIMIT = 56 * 1024 * 1024

NEG_INF = float("-inf")
INT_MIN = -2 ** 31


def _cparams(*sem):
    return pltpu.CompilerParams(dimension_semantics=sem, vmem_limit_bytes=VMEM_LIMIT)


def _dot(a, b):
    return jnp.dot(a, b, preferred_element_type=F32)


def _dot_nt(a, b):
    return lax.dot_general(a, b, (((1,), (1,)), ((), ())), preferred_element_type=F32)


def _layernorm_rows(y, g, b):
    mu = jnp.mean(y, axis=-1, keepdims=True)
    yc = y - mu
    var = jnp.mean(yc * yc, axis=-1, keepdims=True)
    return yc * lax.rsqrt(var + LN_EPS) * g + b


def _rms_rows(h, g):
    ms = jnp.mean(h * h, axis=-1, keepdims=True)
    return h * lax.rsqrt(ms + RMS_EPS) * g


def _mla_proj_kernel(x_ref, win_ref, qn_ref, kvn_ref, wuq_ref, wukv_ref, cc_ref, ss_ref,
                     q_ref, k_ref, v_ref):
    xb = x_ref[...].astype(BF16)
    h = _dot(xb, win_ref[...])
    cq = h[:, :MLA_Q_RANK]
    ckv = h[:, MLA_Q_RANK:MLA_Q_RANK + MLA_KV_RANK]
    o = MLA_Q_RANK + MLA_KV_RANK
    cc = cc_ref[...]
    ss = ss_ref[...]
    k_rope = h[:, o:o + LANES] * cc + h[:, o + LANES:o + 2 * LANES] * ss
    qall = _dot(_rms_rows(cq, qn_ref[...]).astype(BF16), wuq_ref[...])
    kvall = _dot(_rms_rows(ckv, kvn_ref[...]).astype(BF16), wukv_ref[...])
    k_rope_b = k_rope.astype(BF16)
    for hd in range(MLA_HEADS):
        b0 = hd * 3 * LANES
        q_ref[0, hd, :, 0:LANES] = qall[:, b0:b0 + LANES].astype(BF16)
        q_rope = qall[:, b0 + LANES:b0 + 2 * LANES] * cc + qall[:, b0 + 2 * LANES:b0 + 3 * LANES] * ss
        q_ref[0, hd, :, LANES:2 * LANES] = q_rope.astype(BF16)
        k_ref[0, hd, :, 0:LANES] = kvall[:, hd * LANES:(hd + 1) * LANES].astype(BF16)
        k_ref[0, hd, :, LANES:2 * LANES] = k_rope_b
        v0 = MLA_HEADS * MLA_NOPE + hd * MLA_V
        v_ref[0, hd] = kvall[:, v0:v0 + MLA_V].astype(BF16)


def _mla_proj(xt, B, S, w_in, q_norm, kv_norm, w_uq, w_ukv, cos, sin, tm):
    T = B * S
    n_s = S // tm
    half = MLA_ROPE // 2
    def swap_rope(w):
        return jnp.concatenate([w[..., half:], w[..., :half]], axis=-1)
    zpad = lambda w: jnp.concatenate([w, jnp.zeros_like(w)], axis=-1)
    w_kr = w_in[:, MLA_Q_RANK + MLA_KV_RANK:]
    win_ext = jnp.concatenate(
        [w_in[:, :MLA_Q_RANK + MLA_KV_RANK], zpad(w_kr), zpad(swap_rope(w_kr))], axis=-1).astype(BF16)
    uq_nope = w_uq[:, :, :MLA_NOPE]
    uq_rope = w_uq[:, :, MLA_NOPE:]
    wuq_ext = jnp.concatenate([uq_nope, zpad(uq_rope), zpad(swap_rope(uq_rope))], axis=-1)
    wuq_ext = wuq_ext.reshape(MLA_Q_RANK, MLA_HEADS * 3 * LANES).astype(BF16)
    wukv_ext = jnp.concatenate(
        [w_ukv[:, :, :MLA_NOPE].reshape(MLA_KV_RANK, -1), w_ukv[:, :, MLA_NOPE:].reshape(MLA_KV_RANK, -1)],
        axis=-1).astype(BF16)
    z = jnp.zeros((S, LANES - MLA_ROPE), F32)
    cc = jnp.concatenate([cos, cos, z], axis=-1)
    ss = jnp.concatenate([-sin, sin, z], axis=-1)
    full = lambda a: pl.BlockSpec(a.shape, lambda i: (0,) * a.ndim)
    qn = q_norm.reshape(1, -1)
    kvn = kv_norm.reshape(1, -1)
    head_map = lambda i: (i // n_s, 0, i % n_s, 0)
    return pl.pallas_call(
        _mla_proj_kernel,
        grid=(T // tm,),
        in_specs=[pl.BlockSpec((tm, D_MODEL), lambda i: (i, 0)),
                  full(win_ext), full(qn), full(kvn), full(wuq_ext), full(wukv_ext),
                  pl.BlockSpec((tm, LANES), lambda i: (i % n_s, 0)),
                  pl.BlockSpec((tm, LANES), lambda i: (i % n_s, 0))],
        out_specs=[pl.BlockSpec((1, MLA_HEADS, tm, MLA_QK_PAD), head_map),
                   pl.BlockSpec((1, MLA_HEADS, tm, MLA_QK_PAD), head_map),
                   pl.BlockSpec((1, MLA_HEADS, tm, MLA_V), head_map)],
        out_shape=[jax.ShapeDtypeStruct((B, MLA_HEADS, S, MLA_QK_PAD), BF16),
                   jax.ShapeDtypeStruct((B, MLA_HEADS, S, MLA_QK_PAD), BF16),
                   jax.ShapeDtypeStruct((B, MLA_HEADS, S, MLA_V), BF16)],
        compiler_params=_cparams("parallel"),
        name="mla_proj",
    )(xt, win_ext, qn, kvn, wuq_ext, wukv_ext, cc, ss)


MLA_HEADS_PER_STEP = 4
MLA_KEY_CLASSES = 8
DSA_KEY_CLASSES = 4


def _mla_attn_block(q_ref, k_ref, v_ref, o_ref, blk, *, tq, L, scale):
    row = lax.broadcasted_iota(I32, (tq, L), 0)
    col = lax.broadcasted_iota(I32, (tq, L), 1)
    adm = col // CHUNK <= (blk * tq + row) // CHUNK
    for h in range(MLA_HEADS_PER_STEP):
        s = _dot_nt(q_ref[0, h], k_ref[0, h, 0:L, :]) * scale
        s = jnp.where(adm, s, NEG_INF)
        m = jnp.max(s, axis=-1, keepdims=True)
        p = jnp.exp(s - m)
        l = jnp.sum(p, axis=-1, keepdims=True)
        o = _dot(p.astype(BF16), v_ref[0, h, 0:L, :])
        o_ref[0, :, h * MLA_V:(h + 1) * MLA_V] = (o * (1.0 / l)).astype(BF16)


def _mla_attn_kernel(q_ref, k_ref, v_ref, o_ref, *, tq, S, scale):
    blk = pl.program_id(2)
    n_cls = min(MLA_KEY_CLASSES, S // tq)
    width = S // n_cls
    cls = ((blk + 1) * tq - 1) // width
    for c in range(n_cls):
        @pl.when(cls == c)
        def _(c=c):
            _mla_attn_block(q_ref, k_ref, v_ref, o_ref, blk, tq=tq, L=(c + 1) * width, scale=scale)


def _mla_attn(q, k, v, tq):
    B, H, S, _ = q.shape
    hp = MLA_HEADS_PER_STEP
    scale = (MLA_NOPE + MLA_ROPE) ** -0.5
    return pl.pallas_call(
        functools.partial(_mla_attn_kernel, tq=tq, S=S, scale=scale),
        grid=(B, H // hp, S // tq),
        in_specs=[pl.BlockSpec((1, hp, tq, MLA_QK_PAD), lambda b, h, i: (b, h, i, 0)),
                  pl.BlockSpec((1, hp, S, MLA_QK_PAD), lambda b, h, i: (b, h, 0, 0)),
                  pl.BlockSpec((1, hp, S, MLA_V), lambda b, h, i: (b, h, 0, 0))],
        out_specs=pl.BlockSpec((1, tq, hp * MLA_V), lambda b, h, i: (b, i, h)),
        out_shape=jax.ShapeDtypeStruct((B, S, H * MLA_V), BF16),
        compiler_params=_cparams("parallel", "parallel", "arbitrary"),
        name="mla_attn",
    )(q, k, v)


def _proj_ln_kernel(a_ref, w_ref, x_ref, g_ref, b_ref, o_ref):
    m = _dot(a_ref[...], w_ref[...])
    y = DN_ALPHA * x_ref[...] + m
    o_ref[...] = _layernorm_rows(y, g_ref[...], b_ref[...])


def _proj_ln(a, w, xt, gain, bias, tm):
    T = xt.shape[0]
    wb = w.astype(BF16)
    g = gain.reshape(1, -1)
    b = bias.reshape(1, -1)
    return pl.pallas_call(
        _proj_ln_kernel,
        grid=(T // tm,),
        in_specs=[pl.BlockSpec((tm, a.shape[1]), lambda i: (i, 0)),
                  pl.BlockSpec(wb.shape, lambda i: (0, 0)),
                  pl.BlockSpec((tm, D_MODEL), lambda i: (i, 0)),
                  pl.BlockSpec(g.shape, lambda i: (0, 0)),
                  pl.BlockSpec(b.shape, lambda i: (0, 0))],
        out_specs=pl.BlockSpec((tm, D_MODEL), lambda i: (i, 0)),
        out_shape=jax.ShapeDtypeStruct((T, D_MODEL), F32),
        compiler_params=_cparams("parallel"),
        name="proj_ln",
    )(a, wb, xt, g, b)


def _dsa_proj_kernel(x_ref, w_ref, ch_ref, sh_ref, ci_ref, ck_ref, sk_ref,
                     q_ref, k_ref, v_ref, qi_ref, ki_ref, wi_ref, *, w_scale):
    xb = x_ref[...].astype(BF16)
    h = _dot(xb, w_ref[...])
    ch = ch_ref[...]
    sh = sh_ref[...]
    ci = ci_ref[...]
    half = LANES // 2
    for hd in range(DSA_HEADS):
        c0 = hd * DSA_HEAD_DIM
        qh = h[:, c0:c0 + LANES]
        q_ref[0, :, c0:c0 + LANES] = (qh * ch + pltpu.roll(qh, half, 1) * sh).astype(BF16)
        kh = h[:, DSA_HD + c0:DSA_HD + c0 + LANES]
        k_ref[0, :, c0:c0 + LANES] = (kh * ch + pltpu.roll(kh, half, 1) * sh).astype(BF16)
        t = h[:, 3 * DSA_HD + c0:3 * DSA_HD + c0 + LANES] * ci
        qi_ref[0, :, c0:c0 + LANES] = (t + pltpu.roll(t, half, 1)).astype(BF16)
    v_ref[0] = h[:, 2 * DSA_HD:3 * DSA_HD].astype(BF16)
    o = 4 * DSA_HD
    ki = h[:, o:o + LANES] * ck_ref[...] + h[:, o + LANES:o + 2 * LANES] * sk_ref[...]
    ki_ref[0] = ki.astype(BF16)
    wi_ref[0] = h[:, o + 2 * LANES:o + 3 * LANES] * w_scale


def _dsa_proj(xt, B, S, w_in, tm):
    T = B * S
    n_s = S // tm
    o1 = 3 * DSA_HD + IDX_HEADS * IDX_DIM
    ih = IDX_DIM // 2
    w_qi = w_in[:, 3 * DSA_HD:o1].reshape(D_MODEL, IDX_HEADS, IDX_DIM)
    w_qi_sw = jnp.concatenate([w_qi[..., ih:], w_qi[..., :ih]], axis=-1)
    w_qi_ext = jnp.concatenate([w_qi, w_qi_sw], axis=-1).reshape(D_MODEL, IDX_HEADS * LANES)
    w_ki = w_in[:, o1:o1 + IDX_DIM]
    w_ki_sw = jnp.concatenate([w_ki[:, ih:], w_ki[:, :ih]], axis=-1)
    zpad = lambda w: jnp.concatenate([w, jnp.zeros((w.shape[0], LANES - w.shape[1]), w.dtype)], axis=-1)
    w_ext = jnp.concatenate(
        [w_in[:, :3 * DSA_HD], w_qi_ext, zpad(w_ki), zpad(w_ki_sw), zpad(w_in[:, o1 + IDX_DIM:])],
        axis=-1).astype(BF16)
    cos_h, sin_h = _rope_tables(S, DSA_HEAD_DIM)
    cos_i, sin_i = _rope_tables(S, IDX_DIM)
    ch = jnp.concatenate([cos_h, cos_h], axis=-1)
    sh = jnp.concatenate([-sin_h, sin_h], axis=-1)
    ci = jnp.concatenate([cos_i, cos_i, -sin_i, sin_i], axis=-1)
    z = jnp.zeros((S, LANES - IDX_DIM), F32)
    ck = jnp.concatenate([cos_i, cos_i, z], axis=-1)
    sk = jnp.concatenate([-sin_i, sin_i, z], axis=-1)
    tab = pl.BlockSpec((tm, LANES), lambda i: (i % n_s, 0))
    row_map = lambda i: (i // n_s, i % n_s, 0)
    wide = pl.BlockSpec((1, tm, DSA_HD), row_map)
    narrow = pl.BlockSpec((1, tm, LANES), row_map)
    w_scale = IDX_HEADS ** -0.5 * IDX_DIM ** -0.5
    return pl.pallas_call(
        functools.partial(_dsa_proj_kernel, w_scale=w_scale),
        grid=(T // tm,),
        in_specs=[pl.BlockSpec((tm, D_MODEL), lambda i: (i, 0)),
                  pl.BlockSpec(w_ext.shape, lambda i: (0, 0)),
                  tab, tab, tab, tab, tab],
        out_specs=[wide, wide, wide, wide, narrow, narrow],
        out_shape=[jax.ShapeDtypeStruct((B, S, DSA_HD), BF16),
                   jax.ShapeDtypeStruct((B, S, DSA_HD), BF16),
                   jax.ShapeDtypeStruct((B, S, DSA_HD), BF16),
                   jax.ShapeDtypeStruct((B, S, IDX_HEADS * LANES), BF16),
                   jax.ShapeDtypeStruct((B, S, LANES), BF16),
                   jax.ShapeDtypeStruct((B, S, LANES), F32)],
        compiler_params=_cparams("parallel"),
        name="dsa_proj",
    )(xt, w_ext, ch, sh, ci, ck, sk)


def _dsa_attn_block(q_ref, qi_ref, wi_ref, k_ref, v_ref, ki_ref, o_ref, blk, *, tq, L, topk, scale):
    ki = ki_ref[0, 0:L, :]
    qi = qi_ref[0]
    wi = wi_ref[0]
    score = jnp.zeros((tq, L), F32)
    for hd in range(IDX_HEADS):
        logit = _dot_nt(qi[:, hd * LANES:(hd + 1) * LANES], ki)
        score = score + wi[:, hd:hd + 1] * jnp.maximum(logit, 0.0)
    row = lax.broadcasted_iota(I32, (tq, L), 0)
    col = lax.broadcasted_iota(I32, (tq, L), 1)
    adm = col // CHUNK <= (blk * tq + row) // CHUNK
    score = jnp.where(adm, score + 0.0, NEG_INF)
    bits = lax.bitcast_convert_type(score, I32)
    u = jnp.where(bits < 0, bits ^ 0x7FFFFFFF, bits)

    def search(i, thr):
        cand = thr + lax.shift_left(jnp.int32(1), 31 - i)
        cnt = jnp.sum((u >= cand).astype(F32), axis=-1, keepdims=True)
        return jnp.where(cnt >= topk, cand, thr)
    thr = lax.fori_loop(0, 32, search, jnp.full((tq, 1), INT_MIN, I32), unroll=8)

    gt = u > thr
    eq = u == thr
    need = topk - jnp.sum(gt.astype(F32), axis=-1, keepdims=True)
    r_i = lax.broadcasted_iota(I32, (LANES, LANES), 0)
    c_i = lax.broadcasted_iota(I32, (LANES, LANES), 1)
    tri = (r_i < c_i).astype(BF16)
    eq_b = eq.astype(BF16)
    run = jnp.zeros((tq, 1), F32)
    ranks = []
    for j in range(L // LANES):
        e = eq_b[:, j * LANES:(j + 1) * LANES]
        ranks.append(_dot(e, tri) + run)
        run = run + jnp.sum(e.astype(F32), axis=-1, keepdims=True)
    rank = jnp.concatenate(ranks, axis=-1)
    sel = adm & (gt | (eq & (rank < need)))

    for hd in range(DSA_HEADS):
        c0 = hd * DSA_HEAD_DIM
        s = _dot_nt(q_ref[0, :, c0:c0 + LANES], k_ref[0, 0:L, c0:c0 + LANES]) * scale
        s = jnp.where(sel, s, NEG_INF)
        m = jnp.max(s, axis=-1, keepdims=True)
        p = jnp.exp(s - m)
        l = jnp.sum(p, axis=-1, keepdims=True)
        o = _dot(p.astype(BF16), v_ref[0, 0:L, c0:c0 + LANES])
        o_ref[0, :, c0:c0 + LANES] = (o * (1.0 / l)).astype(BF16)


def _dsa_attn_kernel(q_ref, qi_ref, wi_ref, k_ref, v_ref, ki_ref, o_ref, *, tq, S, topk, scale):
    blk = pl.program_id(1)
    n_cls = min(DSA_KEY_CLASSES, S // tq)
    width = S // n_cls
    cls = ((blk + 1) * tq - 1) // width
    for c in range(n_cls):
        @pl.when(cls == c)
        def _(c=c):
            _dsa_attn_block(q_ref, qi_ref, wi_ref, k_ref, v_ref, ki_ref, o_ref, blk,
                            tq=tq, L=(c + 1) * width, topk=topk, scale=scale)


def _dsa_attn(q, k, v, qi, ki, wi, tq):
    B, S, _ = q.shape
    topk = min(DSA_TOPK_MAX, S // 4)
    scale = DSA_HEAD_DIM ** -0.5
    qmap = lambda b, i: (b, i, 0)
    kmap = lambda b, i: (b, 0, 0)
    return pl.pallas_call(
        functools.partial(_dsa_attn_kernel, tq=tq, S=S, topk=topk, scale=scale),
        grid=(B, S // tq),
        in_specs=[pl.BlockSpec((1, tq, DSA_HD), qmap),
                  pl.BlockSpec((1, tq, IDX_HEADS * LANES), qmap),
                  pl.BlockSpec((1, tq, LANES), qmap),
                  pl.BlockSpec((1, S, DSA_HD), kmap),
                  pl.BlockSpec((1, S, DSA_HD), kmap),
                  pl.BlockSpec((1, S, LANES), kmap)],
        out_specs=pl.BlockSpec((1, tq, DSA_HD), qmap),
        out_shape=jax.ShapeDtypeStruct((B, S, DSA_HD), BF16),
        compiler_params=_cparams("parallel", "arbitrary"),
        name="dsa_attn",
    )(q, qi, wi, k, v, ki)


_PEER_PAIRS = [(a, b) for a in range(PEER_TOPK) for b in range(PEER_TOPK // (a + 1))]
_PEER_CAND_ROWS = -(-len(_PEER_PAIRS) // SUBLANES) * SUBLANES


def _sort_network(n):
    pairs = []
    p = 1
    while p < n:
        k = p
        while k >= 1:
            for j in range(k % p, n - k, 2 * k):
                for i in range(min(k, n - j - k)):
                    if (i + j) // (2 * p) == (i + j + k) // (2 * p):
                        pairs.append((i + j, i + j + k))
            k //= 2
        p *= 2
    return pairs


def _top_rows(s, n):
    groups = PEER_NKEYS // SUBLANES
    v = [s[g * SUBLANES:(g + 1) * SUBLANES, :] for g in range(groups)]
    for i, j in _sort_network(groups):
        v[i], v[j] = jnp.maximum(v[i], v[j]), jnp.minimum(v[i], v[j])
    vals = []
    for k in range(n):
        m = jnp.max(v[0], axis=0, keepdims=True)
        vals.append(m)
        won = v[0] == m
        for d in range(groups - 1 - k):
            v[d] = jnp.where(won, v[d + 1], v[d])
    return vals


def _peer_route_kernel(x_ref, wq_ref, keys_ref, lim_ref, e1_ref, r2_ref, e2_ref, cand_ref):
    tt = x_ref.shape[0]
    q = _dot(x_ref[...].astype(BF16), wq_ref[...]).astype(BF16)
    half = PEER_QDIM // 2
    pad0 = _PEER_CAND_ROWS - SUBLANES
    cand_ref[pad0:, :] = jnp.full((SUBLANES, tt), NEG_INF, F32)
    for hd in range(PEER_HEADS):
        c0 = hd * PEER_QDIM
        s1 = _dot_nt(keys_ref[0, hd], q[:, c0:c0 + half])
        s2 = _dot_nt(keys_ref[1, hd], q[:, c0 + half:c0 + PEER_QDIM])
        v1 = _top_rows(s1, PEER_TOPK)
        v2 = _top_rows(s2, PEER_TOPK)
        for k, (a, b) in enumerate(_PEER_PAIRS):
            cand_ref[k:k + 1, :] = v1[a] + v2[b]
        cand = cand_ref[...]
        cur = cand
        theta = None
        for _ in range(PEER_TOPK):
            theta = jnp.max(cur, axis=0, keepdims=True)
            cur = jnp.where(cur == theta, NEG_INF, cur)
        top = v1[0] + v2[0]
        z = jnp.sum(jnp.where(cand >= theta, jnp.exp(cand - top), 0.0), axis=0, keepdims=True)
        r2 = jnp.full(s2.shape, float(PEER_TOPK), F32)
        for b in reversed(range(PEER_TOPK)):
            r2 = jnp.where(s2 >= v2[b], float(b), r2)
        lim = jnp.zeros(s1.shape, F32)
        for b in range(PEER_TOPK):
            lim = jnp.where(s1 + v2[b] >= theta, float(b + 1), lim)
        lim_ref[hd] = lim
        r2_ref[hd] = r2.astype(BF16)
        e1_ref[hd] = jnp.exp(s1 - v1[0]) * (1.0 / z)
        e2_ref[hd] = jnp.exp(s2 - v2[0]).astype(BF16)


def _peer_route(xt, w_q, sub_keys, tt):
    T = xt.shape[0]
    wq = w_q.astype(BF16)
    keys = sub_keys.astype(BF16)
    big = jax.ShapeDtypeStruct((PEER_HEADS, PEER_NKEYS, T), F32)
    big_b = jax.ShapeDtypeStruct((PEER_HEADS, PEER_NKEYS, T), BF16)
    big_spec = pl.BlockSpec((PEER_HEADS, PEER_NKEYS, tt), lambda i: (0, 0, i))
    return pl.pallas_call(
        _peer_route_kernel,
        grid=(T // tt,),
        in_specs=[pl.BlockSpec((tt, D_MODEL), lambda i: (i, 0)),
                  pl.BlockSpec(wq.shape, lambda i: (0, 0)),
                  pl.BlockSpec(keys.shape, lambda i: (0, 0, 0, 0))],
        out_specs=[big_spec, big_spec, big_spec, big_spec],
        out_shape=[big, big, big_b, big_b],
        scratch_shapes=[pltpu.VMEM((_PEER_CAND_ROWS, tt), F32)],
        compiler_params=_cparams("parallel"),
        name="peer_route",
    )(xt, wq, keys)


PEER_CHUNK = 8 * PEER_NKEYS


BF16_SUBLANES = 2 * SUBLANES


def _row_bf16(tile8, r):
    row = jnp.broadcast_to(tile8[r:r + 1, :], (BF16_SUBLANES, LANES)).astype(BF16)
    return jnp.tile(row, (PEER_NKEYS // BF16_SUBLANES, 1))


def _peer_dense_kernel(x_ref, wd_ref, wu_ref, lim_ref, e1_ref, r2_ref, e2_ref, g_ref, b_ref,
                       o_ref, xb_ref, p_ref, acc_ref, r2s_ref, e2s_ref, *, te, tt):
    e = pl.program_id(1)

    @pl.when(e == 0)
    def _():
        xb_ref[...] = x_ref[...].astype(BF16)
        acc_ref[...] = jnp.zeros_like(acc_ref)
        r2s_ref[...] = r2_ref[...]
        e2s_ref[...] = e2_ref[...]

    xb = xb_ref[...]
    n_ch = te // PEER_CHUNK
    i1_0 = e * (te // PEER_NKEYS)
    a_next = _dot_nt(wd_ref[0:PEER_CHUNK, :], xb)
    for c in range(n_ch):
        a_cur = a_next
        if c + 1 < n_ch:
            a_next = _dot_nt(wd_ref[(c + 1) * PEER_CHUNK:(c + 2) * PEER_CHUNK, :], xb)
        n_j = PEER_CHUNK // PEER_NKEYS
        for g in range(n_j // SUBLANES):
            gb = pl.multiple_of(i1_0 + c * n_j + g * SUBLANES, SUBLANES)
            for lg in range(tt // LANES):
                ls = slice(lg * LANES, (lg + 1) * LANES)
                lim8 = [lim_ref[hd, pl.ds(gb, SUBLANES), ls] for hd in range(PEER_HEADS)]
                e18 = [e1_ref[hd, pl.ds(gb, SUBLANES), ls] for hd in range(PEER_HEADS)]
                for r in range(SUBLANES):
                    a0 = (g * SUBLANES + r) * PEER_NKEYS
                    gate = jnp.zeros((PEER_NKEYS, LANES), BF16)
                    zero = jnp.zeros((), BF16)
                    for hd in range(PEER_HEADS):
                        limb = _row_bf16(lim8[hd], r)
                        e1b = _row_bf16(e18[hd], r)
                        gate = gate + jnp.where(r2s_ref[hd, :, ls] < limb, e2s_ref[hd, :, ls] * e1b, zero)
                    a = a_cur[a0:a0 + PEER_NKEYS, ls]
                    act = 0.5 * a * (1.0 + lax.erf(a * (2.0 ** -0.5)))
                    r0 = c * PEER_CHUNK + a0
                    p_ref[r0:r0 + PEER_NKEYS, ls] = act.astype(BF16) * gate
        acc_ref[...] += _dot(wu_ref[c], p_ref[c * PEER_CHUNK:(c + 1) * PEER_CHUNK, :])

    @pl.when(e == pl.num_programs(1) - 1)
    def _():
        y = DN_ALPHA * x_ref[...] + acc_ref[...].T
        o_ref[...] = _layernorm_rows(y, g_ref[...], b_ref[...])


def _peer_dense(xt, w_down, w_up, route, gain, bias, tt, te):
    T = xt.shape[0]
    c1, e1, s2, e2 = route
    wd = w_down.astype(BF16)
    wu = w_up.astype(BF16).reshape(PEER_EXPERTS // PEER_CHUNK, PEER_CHUNK, D_MODEL).transpose(0, 2, 1)
    g = gain.reshape(1, -1)
    b = bias.reshape(1, -1)
    big_spec = pl.BlockSpec((PEER_HEADS, PEER_NKEYS, tt), lambda i, e: (0, 0, i))
    return pl.pallas_call(
        functools.partial(_peer_dense_kernel, te=te, tt=tt),
        grid=(T // tt, PEER_EXPERTS // te),
        in_specs=[pl.BlockSpec((tt, D_MODEL), lambda i, e: (i, 0)),
                  pl.BlockSpec((te, D_MODEL), lambda i, e: (e, 0)),
                  pl.BlockSpec((te // PEER_CHUNK, D_MODEL, PEER_CHUNK), lambda i, e: (e, 0, 0)),
                  big_spec, big_spec, big_spec, big_spec,
                  pl.BlockSpec(g.shape, lambda i, e: (0, 0)),
                  pl.BlockSpec(b.shape, lambda i, e: (0, 0))],
        out_specs=pl.BlockSpec((tt, D_MODEL), lambda i, e: (i, 0)),
        out_shape=jax.ShapeDtypeStruct((T, D_MODEL), F32),
        scratch_shapes=[pltpu.VMEM((tt, D_MODEL), BF16),
                        pltpu.VMEM((te, tt), BF16),
                        pltpu.VMEM((D_MODEL, tt), F32),
                        pltpu.VMEM((PEER_HEADS, PEER_NKEYS, tt), BF16),
                        pltpu.VMEM((PEER_HEADS, PEER_NKEYS, tt), BF16)],
        compiler_params=_cparams("parallel", "arbitrary"),
        name="peer_dense",
    )(xt, wd, wu, c1, e1, s2, e2, g, b)


def _rope_tables(seq, dim):
    inv = ROPE_THETA ** (-jnp.arange(0, dim, 2, dtype=F32) / dim)
    ang = jnp.arange(seq, dtype=F32)[:, None] * inv[None, :]
    return jnp.cos(ang), jnp.sin(ang)


def _tiles(S, T):
    row = min(512, S)
    tq_mla = min(256, S)
    tq_dsa = min(256, S)
    tt_route = min(256, T)
    tt_dense = min(512, T)
    te_dense = 2048
    return row, tq_mla, tq_dsa, tt_route, tt_dense, te_dense


def _peer_layer(xt, w_q, sub_keys, w_down, w_up, gain, bias, tt_route, tt_dense, te_dense):
    route = _peer_route(xt, w_q, sub_keys, tt_route)
    return _peer_dense(xt, w_down, w_up, route, gain, bias, tt_dense, te_dense)


def kernel(x, mla_w_in, mla_q_norm, mla_kv_norm, mla_w_uq, mla_w_ukv, mla_w_o,
           dsa_w_in, dsa_w_o, peer_w_q, peer_sub_keys, peer_w_down, peer_w_up,
           ln_gain, ln_bias):
    B, S, D = x.shape
    T = B * S
    row, tq_mla, tq_dsa, tt_route, tt_dense, te_dense = _tiles(S, T)
    xt = x.reshape(T, D)

    cos, sin = _rope_tables(S, MLA_ROPE)
    q, k, v = _mla_proj(xt, B, S, mla_w_in[0], mla_q_norm[0], mla_kv_norm[0],
                        mla_w_uq[0], mla_w_ukv[0], cos, sin, row)
    o = _mla_attn(q, k, v, tq_mla).reshape(T, MLA_HEADS * MLA_V)
    xt = _proj_ln(o, mla_w_o[0], xt, ln_gain[0, 0], ln_bias[0, 0], row)
    xt = _peer_layer(xt, peer_w_q[0], peer_sub_keys[0], peer_w_down[0], peer_w_up[0],
                     ln_gain[0, 1], ln_bias[0, 1], tt_route, tt_dense, te_dense)

    q, k, v, qi, ki, wi = _dsa_proj(xt, B, S, dsa_w_in[0], row)
    o = _dsa_attn(q, k, v, qi, ki, wi, tq_dsa).reshape(T, DSA_HD)
    xt = _proj_ln(o, dsa_w_o[0], xt, ln_gain[1, 0], ln_bias[1, 0], row)
    xt = _peer_layer(xt, peer_w_q[1], peer_sub_keys[1], peer_w_down[1], peer_w_up[1],
                     ln_gain[1, 1], ln_bias[1, 1], tt_route, tt_dense, te_dense)
    return xt.reshape(B, S, D)
```

```python
import functools

import jax
import jax.numpy as jnp
from jax import lax
from jax.experimental import pallas as pl
from jax.experimental.pallas import tpu as pltpu

F32 = jnp.float32
BF16 = jnp.bfloat16
I32 = jnp.int32

D_MODEL = 1024
DEPTH = 2
CHUNK = 64
ROPE_THETA = 10000.0
LN_EPS = 1e-5
RMS_EPS = 1e-6
DN_ALPHA = (2 * DEPTH) ** 0.25

MLA_HEADS = 8
MLA_NOPE = 128
MLA_ROPE = 64
MLA_V = 128
MLA_Q_RANK = 384
MLA_KV_RANK = 256
MLA_QK_PAD = 256

DSA_HEADS = 8
DSA_HEAD_DIM = 128
IDX_HEADS = 8
IDX_DIM = 64
DSA_TOPK_MAX = 256
DSA_HD = DSA_HEADS * DSA_HEAD_DIM

PEER_HEADS = 8
PEER_NKEYS = 128
PEER_EXPERTS = PEER_NKEYS * PEER_NKEYS
PEER_QDIM = 256
PEER_TOPK = 16

LANES = 128
SUBLANES = 8
VMEM_LIMIT = 56 * 1024 * 1024

NEG_INF = float("-inf")
INT_MIN = -2 ** 31


def _cparams(*sem):
    return pltpu.CompilerParams(dimension_semantics=sem, vmem_limit_bytes=VMEM_LIMIT)


def _dot(a, b):
    return jnp.dot(a, b, preferred_element_type=F32)


def _dot_nt(a, b):
    return lax.dot_general(a, b, (((1,), (1,)), ((), ())), preferred_element_type=F32)


def _layernorm_rows(y, g, b):
    mu = jnp.mean(y, axis=-1, keepdims=True)
    yc = y - mu
    var = jnp.mean(yc * yc, axis=-1, keepdims=True)
    return yc * lax.rsqrt(var + LN_EPS) * g + b


def _rms_rows(h, g):
    ms = jnp.mean(h * h, axis=-1, keepdims=True)
    return h * lax.rsqrt(ms + RMS_EPS) * g


def _mla_proj_kernel(x_ref, win_ref, qn_ref, kvn_ref, wuq_ref, wukv_ref, cc_ref, ss_ref,
                     q_ref, k_ref, v_ref):
    xb = x_ref[...].astype(BF16)
    h = _dot(xb, win_ref[...])
    cq = h[:, :MLA_Q_RANK]
    ckv = h[:, MLA_Q_RANK:MLA_Q_RANK + MLA_KV_RANK]
    o = MLA_Q_RANK + MLA_KV_RANK
    cc = cc_ref[...]
    ss = ss_ref[...]
    k_rope = h[:, o:o + LANES] * cc + h[:, o + LANES:o + 2 * LANES] * ss
    qall = _dot(_rms_rows(cq, qn_ref[...]).astype(BF16), wuq_ref[...])
    kvall = _dot(_rms_rows(ckv, kvn_ref[...]).astype(BF16), wukv_ref[...])
    k_rope_b = k_rope.astype(BF16)
    for hd in range(MLA_HEADS):
        b0 = hd * 3 * LANES
        q_ref[0, hd, :, 0:LANES] = qall[:, b0:b0 + LANES].astype(BF16)
        q_rope = qall[:, b0 + LANES:b0 + 2 * LANES] * cc + qall[:, b0 + 2 * LANES:b0 + 3 * LANES] * ss
        q_ref[0, hd, :, LANES:2 * LANES] = q_rope.astype(BF16)
        k_ref[0, hd, :, 0:LANES] = kvall[:, hd * LANES:(hd + 1) * LANES].astype(BF16)
        k_ref[0, hd, :, LANES:2 * LANES] = k_rope_b
        v0 = MLA_HEADS * MLA_NOPE + hd * MLA_V
        v_ref[0, hd] = kvall[:, v0:v0 + MLA_V].astype(BF16)


def _mla_proj(xt, B, S, w_in, q_norm, kv_norm, w_uq, w_ukv, cos, sin, tm):
    T = B * S
    n_s = S // tm
    half = MLA_ROPE // 2
    def swap_rope(w):
        return jnp.concatenate([w[..., half:], w[..., :half]], axis=-1)
    zpad = lambda w: jnp.concatenate([w, jnp.zeros_like(w)], axis=-1)
    w_kr = w_in[:, MLA_Q_RANK + MLA_KV_RANK:]
    win_ext = jnp.concatenate(
        [w_in[:, :MLA_Q_RANK + MLA_KV_RANK], zpad(w_kr), zpad(swap_rope(w_kr))], axis=-1).astype(BF16)
    uq_nope = w_uq[:, :, :MLA_NOPE]
    uq_rope = w_uq[:, :, MLA_NOPE:]
    wuq_ext = jnp.concatenate([uq_nope, zpad(uq_rope), zpad(swap_rope(uq_rope))], axis=-1)
    wuq_ext = wuq_ext.reshape(MLA_Q_RANK, MLA_HEADS * 3 * LANES).astype(BF16)
    wukv_ext = jnp.concatenate(
        [w_ukv[:, :, :MLA_NOPE].reshape(MLA_KV_RANK, -1), w_ukv[:, :, MLA_NOPE:].reshape(MLA_KV_RANK, -1)],
        axis=-1).astype(BF16)
    z = jnp.zeros((S, LANES - MLA_ROPE), F32)
    cc = jnp.concatenate([cos, cos, z], axis=-1)
    ss = jnp.concatenate([-sin, sin, z], axis=-1)
    full = lambda a: pl.BlockSpec(a.shape, lambda i: (0,) * a.ndim)
    qn = q_norm.reshape(1, -1)
    kvn = kv_norm.reshape(1, -1)
    head_map = lambda i: (i // n_s, 0, i % n_s, 0)
    return pl.pallas_call(
        _mla_proj_kernel,
        grid=(T // tm,),
        in_specs=[pl.BlockSpec((tm, D_MODEL), lambda i: (i, 0)),
                  full(win_ext), full(qn), full(kvn), full(wuq_ext), full(wukv_ext),
                  pl.BlockSpec((tm, LANES), lambda i: (i % n_s, 0)),
                  pl.BlockSpec((tm, LANES), lambda i: (i % n_s, 0))],
        out_specs=[pl.BlockSpec((1, MLA_HEADS, tm, MLA_QK_PAD), head_map),
                   pl.BlockSpec((1, MLA_HEADS, tm, MLA_QK_PAD), head_map),
                   pl.BlockSpec((1, MLA_HEADS, tm, MLA_V), head_map)],
        out_shape=[jax.ShapeDtypeStruct((B, MLA_HEADS, S, MLA_QK_PAD), BF16),
                   jax.ShapeDtypeStruct((B, MLA_HEADS, S, MLA_QK_PAD), BF16),
                   jax.ShapeDtypeStruct((B, MLA_HEADS, S, MLA_V), BF16)],
        compiler_params=_cparams("parallel"),
        name="mla_proj",
    )(xt, win_ext, qn, kvn, wuq_ext, wukv_ext, cc, ss)


MLA_HEADS_PER_STEP = 4
MLA_KEY_CLASSES = 8
DSA_KEY_CLASSES = 4


def _mla_attn_block(q_ref, k_ref, v_ref, o_ref, blk, *, tq, L, scale):
    row = lax.broadcasted_iota(I32, (tq, L), 0)
    col = lax.broadcasted_iota(I32, (tq, L), 1)
    adm = col // CHUNK <= (blk * tq + row) // CHUNK
    for h in range(MLA_HEADS_PER_STEP):
        s = _dot_nt(q_ref[0, h], k_ref[0, h, 0:L, :]) * scale
        s = jnp.where(adm, s, NEG_INF)
        m = jnp.max(s, axis=-1, keepdims=True)
        p = jnp.exp(s - m)
        l = jnp.sum(p, axis=-1, keepdims=True)
        o = _dot(p.astype(BF16), v_ref[0, h, 0:L, :])
        o_ref[0, :, h * MLA_V:(h + 1) * MLA_V] = (o * (1.0 / l)).astype(BF16)


def _mla_attn_kernel(q_ref, k_ref, v_ref, o_ref, *, tq, S, scale):
    blk = pl.program_id(2)
    n_cls = min(MLA_KEY_CLASSES, S // tq)
    width = S // n_cls
    cls = ((blk + 1) * tq - 1) // width
    for c in range(n_cls):
        @pl.when(cls == c)
        def _(c=c):
            _mla_attn_block(q_ref, k_ref, v_ref, o_ref, blk, tq=tq, L=(c + 1) * width, scale=scale)


def _mla_attn(q, k, v, tq):
    B, H, S, _ = q.shape
    hp = MLA_HEADS_PER_STEP
    scale = (MLA_NOPE + MLA_ROPE) ** -0.5
    return pl.pallas_call(
        functools.partial(_mla_attn_kernel, tq=tq, S=S, scale=scale),
        grid=(B, H // hp, S // tq),
        in_specs=[pl.BlockSpec((1, hp, tq, MLA_QK_PAD), lambda b, h, i: (b, h, i, 0)),
                  pl.BlockSpec((1, hp, S, MLA_QK_PAD), lambda b, h, i: (b, h, 0, 0)),
                  pl.BlockSpec((1, hp, S, MLA_V), lambda b, h, i: (b, h, 0, 0))],
        out_specs=pl.BlockSpec((1, tq, hp * MLA_V), lambda b, h, i: (b, i, h)),
        out_shape=jax.ShapeDtypeStruct((B, S, H * MLA_V), BF16),
        compiler_params=_cparams("parallel", "parallel", "arbitrary"),
        name="mla_attn",
    )(q, k, v)


def _proj_ln_kernel(a_ref, w_ref, x_ref, g_ref, b_ref, o_ref):
    m = _dot(a_ref[...], w_ref[...])
    y = DN_ALPHA * x_ref[...] + m
    o_ref[...] = _layernorm_rows(y, g_ref[...], b_ref[...])


def _proj_ln(a, w, xt, gain, bias, tm):
    T = xt.shape[0]
    wb = w.astype(BF16)
    g = gain.reshape(1, -1)
    b = bias.reshape(1, -1)
    return pl.pallas_call(
        _proj_ln_kernel,
        grid=(T // tm,),
        in_specs=[pl.BlockSpec((tm, a.shape[1]), lambda i: (i, 0)),
                  pl.BlockSpec(wb.shape, lambda i: (0, 0)),
                  pl.BlockSpec((tm, D_MODEL), lambda i: (i, 0)),
                  pl.BlockSpec(g.shape, lambda i: (0, 0)),
                  pl.BlockSpec(b.shape, lambda i: (0, 0))],
        out_specs=pl.BlockSpec((tm, D_MODEL), lambda i: (i, 0)),
        out_shape=jax.ShapeDtypeStruct((T, D_MODEL), F32),
        compiler_params=_cparams("parallel"),
        name="proj_ln",
    )(a, wb, xt, g, b)


def _dsa_proj_kernel(x_ref, w_ref, ch_ref, sh_ref, ci_ref, ck_ref, sk_ref,
                     q_ref, k_ref, v_ref, qi_ref, ki_ref, wi_ref, *, w_scale):
    xb = x_ref[...].astype(BF16)
    h = _dot(xb, w_ref[...])
    ch = ch_ref[...]
    sh = sh_ref[...]
    ci = ci_ref[...]
    half = LANES // 2
    for hd in range(DSA_HEADS):
        c0 = hd * DSA_HEAD_DIM
        qh = h[:, c0:c0 + LANES]
        q_ref[0, :, c0:c0 + LANES] = (qh * ch + pltpu.roll(qh, half, 1) * sh).astype(BF16)
        kh = h[:, DSA_HD + c0:DSA_HD + c0 + LANES]
        k_ref[0, :, c0:c0 + LANES] = (kh * ch + pltpu.roll(kh, half, 1) * sh).astype(BF16)
        t = h[:, 3 * DSA_HD + c0:3 * DSA_HD + c0 + LANES] * ci
        qi_ref[0, :, c0:c0 + LANES] = (t + pltpu.roll(t, half, 1)).astype(BF16)
    v_ref[0] = h[:, 2 * DSA_HD:3 * DSA_HD].astype(BF16)
    o = 4 * DSA_HD
    ki = h[:, o:o + LANES] * ck_ref[...] + h[:, o + LANES:o + 2 * LANES] * sk_ref[...]
    ki_ref[0] = ki.astype(BF16)
    wi_ref[0] = h[:, o + 2 * LANES:o + 3 * LANES] * w_scale


def _dsa_proj(xt, B, S, w_in, tm):
    T = B * S
    n_s = S // tm
    o1 = 3 * DSA_HD + IDX_HEADS * IDX_DIM
    ih = IDX_DIM // 2
    w_qi = w_in[:, 3 * DSA_HD:o1].reshape(D_MODEL, IDX_HEADS, IDX_DIM)
    w_qi_sw = jnp.concatenate([w_qi[..., ih:], w_qi[..., :ih]], axis=-1)
    w_qi_ext = jnp.concatenate([w_qi, w_qi_sw], axis=-1).reshape(D_MODEL, IDX_HEADS * LANES)
    w_ki = w_in[:, o1:o1 + IDX_DIM]
    w_ki_sw = jnp.concatenate([w_ki[:, ih:], w_ki[:, :ih]], axis=-1)
    zpad = lambda w: jnp.concatenate([w, jnp.zeros((w.shape[0], LANES - w.shape[1]), w.dtype)], axis=-1)
    w_ext = jnp.concatenate(
        [w_in[:, :3 * DSA_HD], w_qi_ext, zpad(w_ki), zpad(w_ki_sw), zpad(w_in[:, o1 + IDX_DIM:])],
        axis=-1).astype(BF16)
    cos_h, sin_h = _rope_tables(S, DSA_HEAD_DIM)
    cos_i, sin_i = _rope_tables(S, IDX_DIM)
    ch = jnp.concatenate([cos_h, cos_h], axis=-1)
    sh = jnp.concatenate([-sin_h, sin_h], axis=-1)
    ci = jnp.concatenate([cos_i, cos_i, -sin_i, sin_i], axis=-1)
    z = jnp.zeros((S, LANES - IDX_DIM), F32)
    ck = jnp.concatenate([cos_i, cos_i, z], axis=-1)
    sk = jnp.concatenate([-sin_i, sin_i, z], axis=-1)
    tab = pl.BlockSpec((tm, LANES), lambda i: (i % n_s, 0))
    row_map = lambda i: (i // n_s, i % n_s, 0)
    wide = pl.BlockSpec((1, tm, DSA_HD), row_map)
    narrow = pl.BlockSpec((1, tm, LANES), row_map)
    w_scale = IDX_HEADS ** -0.5 * IDX_DIM ** -0.5
    return pl.pallas_call(
        functools.partial(_dsa_proj_kernel, w_scale=w_scale),
        grid=(T // tm,),
        in_specs=[pl.BlockSpec((tm, D_MODEL), lambda i: (i, 0)),
                  pl.BlockSpec(w_ext.shape, lambda i: (0, 0)),
                  tab, tab, tab, tab, tab],
        out_specs=[wide, wide, wide, wide, narrow, narrow],
        out_shape=[jax.ShapeDtypeStruct((B, S, DSA_HD), BF16),
                   jax.ShapeDtypeStruct((B, S, DSA_HD), BF16),
                   jax.ShapeDtypeStruct((B, S, DSA_HD), BF16),
                   jax.ShapeDtypeStruct((B, S, IDX_HEADS * LANES), BF16),
                   jax.ShapeDtypeStruct((B, S, LANES), BF16),
                   jax.ShapeDtypeStruct((B, S, LANES), F32)],
        compiler_params=_cparams("parallel"),
        name="dsa_proj",
    )(xt, w_ext, ch, sh, ci, ck, sk)


def _dsa_attn_block(q_ref, qi_ref, wi_ref, k_ref, v_ref, ki_ref, o_ref, blk, *, tq, L, topk, scale):
    ki = ki_ref[0, 0:L, :]
    qi = qi_ref[0]
    wi = wi_ref[0]
    score = jnp.zeros((tq, L), F32)
    for hd in range(IDX_HEADS):
        logit = _dot_nt(qi[:, hd * LANES:(hd + 1) * LANES], ki)
        score = score + wi[:, hd:hd + 1] * jnp.maximum(logit, 0.0)
    row = lax.broadcasted_iota(I32, (tq, L), 0)
    col = lax.broadcasted_iota(I32, (tq, L), 1)
    adm = col // CHUNK <= (blk * tq + row) // CHUNK
    score = jnp.where(adm, score + 0.0, NEG_INF)
    bits = lax.bitcast_convert_type(score, I32)
    u = jnp.where(bits < 0, bits ^ 0x7FFFFFFF, bits)

    def search(i, thr):
        cand = thr + lax.shift_left(jnp.int32(1), 31 - i)
        cnt = jnp.sum((u >= cand).astype(F32), axis=-1, keepdims=True)
        return jnp.where(cnt >= topk, cand, thr)
    thr = lax.fori_loop(0, 32, search, jnp.full((tq, 1), INT_MIN, I32), unroll=8)

    gt = u > thr
    eq = u == thr
    need = topk - jnp.sum(gt.astype(F32), axis=-1, keepdims=True)
    r_i = lax.broadcasted_iota(I32, (LANES, LANES), 0)
    c_i = lax.broadcasted_iota(I32, (LANES, LANES), 1)
    tri = (r_i < c_i).astype(BF16)
    eq_b = eq.astype(BF16)
    run = jnp.zeros((tq, 1), F32)
    ranks = []
    for j in range(L // LANES):
        e = eq_b[:, j * LANES:(j + 1) * LANES]
        ranks.append(_dot(e, tri) + run)
        run = run + jnp.sum(e.astype(F32), axis=-1, keepdims=True)
    rank = jnp.concatenate(ranks, axis=-1)
    sel = adm & (gt | (eq & (rank < need)))

    for hd in range(DSA_HEADS):
        c0 = hd * DSA_HEAD_DIM
        s = _dot_nt(q_ref[0, :, c0:c0 + LANES], k_ref[0, 0:L, c0:c0 + LANES]) * scale
        s = jnp.where(sel, s, NEG_INF)
        m = jnp.max(s, axis=-1, keepdims=True)
        p = jnp.exp(s - m)
        l = jnp.sum(p, axis=-1, keepdims=True)
        o = _dot(p.astype(BF16), v_ref[0, 0:L, c0:c0 + LANES])
        o_ref[0, :, c0:c0 + LANES] = (o * (1.0 / l)).astype(BF16)


def _dsa_attn_kernel(q_ref, qi_ref, wi_ref, k_ref, v_ref, ki_ref, o_ref, *, tq, S, topk, scale):
    blk = pl.program_id(1)
    n_cls = min(DSA_KEY_CLASSES, S // tq)
    width = S // n_cls
    cls = ((blk + 1) * tq - 1) // width
    for c in range(n_cls):
        @pl.when(cls == c)
        def _(c=c):
            _dsa_attn_block(q_ref, qi_ref, wi_ref, k_ref, v_ref, ki_ref, o_ref, blk,
                            tq=tq, L=(c + 1) * width, topk=topk, scale=scale)


def _dsa_attn(q, k, v, qi, ki, wi, tq):
    B, S, _ = q.shape
    topk = min(DSA_TOPK_MAX, S // 4)
    scale = DSA_HEAD_DIM ** -0.5
    qmap = lambda b, i: (b, i, 0)
    kmap = lambda b, i: (b, 0, 0)
    return pl.pallas_call(
        functools.partial(_dsa_attn_kernel, tq=tq, S=S, topk=topk, scale=scale),
        grid=(B, S // tq),
        in_specs=[pl.BlockSpec((1, tq, DSA_HD), qmap),
                  pl.BlockSpec((1, tq, IDX_HEADS * LANES), qmap),
                  pl.BlockSpec((1, tq, LANES), qmap),
                  pl.BlockSpec((1, S, DSA_HD), kmap),
                  pl.BlockSpec((1, S, DSA_HD), kmap),
                  pl.BlockSpec((1, S, LANES), kmap)],
        out_specs=pl.BlockSpec((1, tq, DSA_HD), qmap),
        out_shape=jax.ShapeDtypeStruct((B, S, DSA_HD), BF16),
        compiler_params=_cparams("parallel", "arbitrary"),
        name="dsa_attn",
    )(q, qi, wi, k, v, ki)


_PEER_PAIRS = [(a, b) for a in range(PEER_TOPK) for b in range(PEER_TOPK // (a + 1))]
_PEER_CAND_ROWS = -(-len(_PEER_PAIRS) // SUBLANES) * SUBLANES


def _sort_network(n):
    pairs = []
    p = 1
    while p < n:
        k = p
        while k >= 1:
            for j in range(k % p, n - k, 2 * k):
                for i in range(min(k, n - j - k)):
                    if (i + j) // (2 * p) == (i + j + k) // (2 * p):
                        pairs.append((i + j, i + j + k))
            k //= 2
        p *= 2
    return pairs


def _top_rows(s, n):
    groups = PEER_NKEYS // SUBLANES
    v = [s[g * SUBLANES:(g + 1) * SUBLANES, :] for g in range(groups)]
    for i, j in _sort_network(groups):
        v[i], v[j] = jnp.maximum(v[i], v[j]), jnp.minimum(v[i], v[j])
    vals = []
    for k in range(n):
        m = jnp.max(v[0], axis=0, keepdims=True)
        vals.append(m)
        won = v[0] == m
        for d in range(groups - 1 - k):
            v[d] = jnp.where(won, v[d + 1], v[d])
    return vals


def _peer_route_kernel(x_ref, wq_ref, keys_ref, lim_ref, e1_ref, r2_ref, e2_ref, cand_ref):
    tt = x_ref.shape[0]
    q = _dot(x_ref[...].astype(BF16), wq_ref[...]).astype(BF16)
    half = PEER_QDIM // 2
    pad0 = _PEER_CAND_ROWS - SUBLANES
    cand_ref[pad0:, :] = jnp.full((SUBLANES, tt), NEG_INF, F32)
    for hd in range(PEER_HEADS):
        c0 = hd * PEER_QDIM
        s1 = _dot_nt(keys_ref[0, hd], q[:, c0:c0 + half])
        s2 = _dot_nt(keys_ref[1, hd], q[:, c0 + half:c0 + PEER_QDIM])
        v1 = _top_rows(s1, PEER_TOPK)
        v2 = _top_rows(s2, PEER_TOPK)
        for k, (a, b) in enumerate(_PEER_PAIRS):
            cand_ref[k:k + 1, :] = v1[a] + v2[b]
        cand = cand_ref[...]
        cur = cand
        theta = None
        for _ in range(PEER_TOPK):
            theta = jnp.max(cur, axis=0, keepdims=True)
            cur = jnp.where(cur == theta, NEG_INF, cur)
        top = v1[0] + v2[0]
        z = jnp.sum(jnp.where(cand >= theta, jnp.exp(cand - top), 0.0), axis=0, keepdims=True)
        r2 = jnp.full(s2.shape, float(PEER_TOPK), F32)
        for b in reversed(range(PEER_TOPK)):
            r2 = jnp.where(s2 >= v2[b], float(b), r2)
        lim = jnp.zeros(s1.shape, F32)
        for b in range(PEER_TOPK):
            lim = jnp.where(s1 + v2[b] >= theta, float(b + 1), lim)
        lim_ref[hd] = lim
        r2_ref[hd] = r2.astype(BF16)
        e1_ref[hd] = jnp.exp(s1 - v1[0]) * (1.0 / z)
        e2_ref[hd] = jnp.exp(s2 - v2[0]).astype(BF16)


def _peer_route(xt, w_q, sub_keys, tt):
    T = xt.shape[0]
    wq = w_q.astype(BF16)
    keys = sub_keys.astype(BF16)
    big = jax.ShapeDtypeStruct((PEER_HEADS, PEER_NKEYS, T), F32)
    big_b = jax.ShapeDtypeStruct((PEER_HEADS, PEER_NKEYS, T), BF16)
    big_spec = pl.BlockSpec((PEER_HEADS, PEER_NKEYS, tt), lambda i: (0, 0, i))
    return pl.pallas_call(
        _peer_route_kernel,
        grid=(T // tt,),
        in_specs=[pl.BlockSpec((tt, D_MODEL), lambda i: (i, 0)),
                  pl.BlockSpec(wq.shape, lambda i: (0, 0)),
                  pl.BlockSpec(keys.shape, lambda i: (0, 0, 0, 0))],
        out_specs=[big_spec, big_spec, big_spec, big_spec],
        out_shape=[big, big, big_b, big_b],
        scratch_shapes=[pltpu.VMEM((_PEER_CAND_ROWS, tt), F32)],
        compiler_params=_cparams("parallel"),
        name="peer_route",
    )(xt, wq, keys)


PEER_CHUNK = 8 * PEER_NKEYS


BF16_SUBLANES = 2 * SUBLANES


def _row_bf16(tile8, r):
    row = jnp.broadcast_to(tile8[r:r + 1, :], (BF16_SUBLANES, LANES)).astype(BF16)
    return jnp.tile(row, (PEER_NKEYS // BF16_SUBLANES, 1))


def _peer_dense_kernel(x_ref, wd_ref, wdn_ref, wu_ref, lim_ref, e1_ref, r2_ref, e2_ref, g_ref, b_ref,
                       o_ref, xb_ref, p_ref, acc_ref, r2s_ref, e2s_ref, a0_ref, *, te, tt):
    e = pl.program_id(1)

    @pl.when(e == 0)
    def _():
        xb_ref[...] = x_ref[...].astype(BF16)
        acc_ref[...] = jnp.zeros_like(acc_ref)
        r2s_ref[...] = r2_ref[...]
        e2s_ref[...] = e2_ref[...]

        a0_ref[...] = _dot_nt(wd_ref[0:PEER_CHUNK, :], xb_ref[...])

    xb = xb_ref[...]
    n_ch = te // PEER_CHUNK
    i1_0 = e * (te // PEER_NKEYS)
    a_next = None
    for c in range(n_ch):
        a_cur = a0_ref if c == 0 else a_next
        if c + 1 < n_ch:
            a_next = _dot_nt(wd_ref[(c + 1) * PEER_CHUNK:(c + 2) * PEER_CHUNK, :], xb)
        else:
            a0_new = _dot_nt(wdn_ref[...], xb)
            if c > 0:
                a0_ref[...] = a0_new
        n_j = PEER_CHUNK // PEER_NKEYS
        for g in range(n_j // SUBLANES):
            gb = pl.multiple_of(i1_0 + c * n_j + g * SUBLANES, SUBLANES)
            for lg in range(tt // LANES):
                ls = slice(lg * LANES, (lg + 1) * LANES)
                lim8 = [lim_ref[hd, pl.ds(gb, SUBLANES), ls] for hd in range(PEER_HEADS)]
                e18 = [e1_ref[hd, pl.ds(gb, SUBLANES), ls] for hd in range(PEER_HEADS)]
                for r in range(SUBLANES):
                    a0 = (g * SUBLANES + r) * PEER_NKEYS
                    gate = jnp.zeros((PEER_NKEYS, LANES), BF16)
                    zero = jnp.zeros((), BF16)
                    for hd in range(PEER_HEADS):
                        limb = _row_bf16(lim8[hd], r)
                        e1b = _row_bf16(e18[hd], r)
                        gate = gate + jnp.where(r2s_ref[hd, :, ls] < limb, e2s_ref[hd, :, ls] * e1b, zero)
                    a = a_cur[a0:a0 + PEER_NKEYS, ls]
                    act = 0.5 * a * (1.0 + lax.erf(a * (2.0 ** -0.5)))
                    r0 = c * PEER_CHUNK + a0
                    p_ref[r0:r0 + PEER_NKEYS, ls] = act.astype(BF16) * gate
        acc_ref[...] += _dot(wu_ref[c], p_ref[c * PEER_CHUNK:(c + 1) * PEER_CHUNK, :])
    if n_ch == 1:
        a0_ref[...] = a0_new

    @pl.when(e == pl.num_programs(1) - 1)
    def _():
        y = DN_ALPHA * x_ref[...] + acc_ref[...].T
        o_ref[...] = _layernorm_rows(y, g_ref[...], b_ref[...])


def _peer_dense(xt, w_down, w_up, route, gain, bias, tt, te):
    T = xt.shape[0]
    c1, e1, s2, e2 = route
    wd = w_down.astype(BF16)
    wu = w_up.astype(BF16).reshape(PEER_EXPERTS // PEER_CHUNK, PEER_CHUNK, D_MODEL).transpose(0, 2, 1)
    g = gain.reshape(1, -1)
    b = bias.reshape(1, -1)
    big_spec = pl.BlockSpec((PEER_HEADS, PEER_NKEYS, tt), lambda i, e: (0, 0, i))
    n_e = PEER_EXPERTS // te
    per = te // PEER_CHUNK
    return pl.pallas_call(
        functools.partial(_peer_dense_kernel, te=te, tt=tt),
        grid=(T // tt, n_e),
        in_specs=[pl.BlockSpec((tt, D_MODEL), lambda i, e: (i, 0)),
                  pl.BlockSpec((te, D_MODEL), lambda i, e: (e, 0)),
                  pl.BlockSpec((PEER_CHUNK, D_MODEL), lambda i, e: (jnp.minimum(e + 1, n_e - 1) * per, 0)),
                  pl.BlockSpec((te // PEER_CHUNK, D_MODEL, PEER_CHUNK), lambda i, e: (e, 0, 0)),
                  big_spec, big_spec, big_spec, big_spec,
                  pl.BlockSpec(g.shape, lambda i, e: (0, 0)),
                  pl.BlockSpec(b.shape, lambda i, e: (0, 0))],
        out_specs=pl.BlockSpec((tt, D_MODEL), lambda i, e: (i, 0)),
        out_shape=jax.ShapeDtypeStruct((T, D_MODEL), F32),
        scratch_shapes=[pltpu.VMEM((tt, D_MODEL), BF16),
                        pltpu.VMEM((te, tt), BF16),
                        pltpu.VMEM((D_MODEL, tt), F32),
                        pltpu.VMEM((PEER_HEADS, PEER_NKEYS, tt), BF16),
                        pltpu.VMEM((PEER_HEADS, PEER_NKEYS, tt), BF16),
                        pltpu.VMEM((PEER_CHUNK, tt), F32)],
        compiler_params=_cparams("parallel", "arbitrary"),
        name="peer_dense",
    )(xt, wd, wd, wu, c1, e1, s2, e2, g, b)


def _rope_tables(seq, dim):
    inv = ROPE_THETA ** (-jnp.arange(0, dim, 2, dtype=F32) / dim)
    ang = jnp.arange(seq, dtype=F32)[:, None] * inv[None, :]
    return jnp.cos(ang), jnp.sin(ang)


def _tiles(S, T):
    row = min(512, S)
    tq_mla = min(256, S)
    tq_dsa = min(256, S)
    tt_route = min(256, T)
    tt_dense = min(512, T)
    te_dense = 2048
    return row, tq_mla, tq_dsa, tt_route, tt_dense, te_dense


def _peer_layer(xt, w_q, sub_keys, w_down, w_up, gain, bias, tt_route, tt_dense, te_dense):
    route = _peer_route(xt, w_q, sub_keys, tt_route)
    return _peer_dense(xt, w_down, w_up, route, gain, bias, tt_dense, te_dense)


def kernel(x, mla_w_in, mla_q_norm, mla_kv_norm, mla_w_uq, mla_w_ukv, mla_w_o,
           dsa_w_in, dsa_w_o, peer_w_q, peer_sub_keys, peer_w_down, peer_w_up,
           ln_gain, ln_bias):
    B, S, D = x.shape
    T = B * S
    row, tq_mla, tq_dsa, tt_route, tt_dense, te_dense = _tiles(S, T)
    xt = x.reshape(T, D)

    cos, sin = _rope_tables(S, MLA_ROPE)
    q, k, v = _mla_proj(xt, B, S, mla_w_in[0], mla_q_norm[0], mla_kv_norm[0],
                        mla_w_uq[0], mla_w_ukv[0], cos, sin, row)
    o = _mla_attn(q, k, v, tq_mla).reshape(T, MLA_HEADS * MLA_V)
    xt = _proj_ln(o, mla_w_o[0], xt, ln_gain[0, 0], ln_bias[0, 0], row)
    xt = _peer_layer(xt, peer_w_q[0], peer_sub_keys[0], peer_w_down[0], peer_w_up[0],
                     ln_gain[0, 1], ln_bias[0, 1], tt_route, tt_dense, te_dense)

    q, k, v, qi, ki, wi = _dsa_proj(xt, B, S, dsa_w_in[0], row)
    o = _dsa_attn(q, k, v, qi, ki, wi, tq_dsa).reshape(T, DSA_HD)
    xt = _proj_ln(o, dsa_w_o[0], xt, ln_gain[1, 0], ln_bias[1, 0], row)
    xt = _peer_layer(xt, peer_w_q[1], peer_sub_keys[1], peer_w_down[1], peer_w_up[1],
                     ln_gain[1, 1], ln_bias[1, 1], tt_route, tt_dense, te_dense)
    return xt.reshape(B, S, D)
```

```python
import functools

import jax
import jax.numpy as jnp
from jax import lax
from jax.experimental import pallas as pl
from jax.experimental.pallas import tpu as pltpu

F32 = jnp.float32
BF16 = jnp.bfloat16
I32 = jnp.int32

D_MODEL = 1024
DEPTH = 2
CHUNK = 64
ROPE_THETA = 10000.0
LN_EPS = 1e-5
RMS_EPS = 1e-6
DN_ALPHA = (2 * DEPTH) ** 0.25

MLA_HEADS = 8
MLA_NOPE = 128
MLA_ROPE = 64
MLA_V = 128
MLA_Q_RANK = 384
MLA_KV_RANK = 256
MLA_QK_PAD = 256

DSA_HEADS = 8
DSA_HEAD_DIM = 128
IDX_HEADS = 8
IDX_DIM = 64
DSA_TOPK_MAX = 256
DSA_HD = DSA_HEADS * DSA_HEAD_DIM

PEER_HEADS = 8
PEER_NKEYS = 128
PEER_EXPERTS = PEER_NKEYS * PEER_NKEYS
PEER_QDIM = 256
PEER_TOPK = 16

LANES = 128
SUBLANES = 8
VMEM_LIMIT = 56 * 1024 * 1024

NEG_INF = float("-inf")
INT_MIN = -2 ** 31


def _cparams(*sem):
    return pltpu.CompilerParams(dimension_semantics=sem, vmem_limit_bytes=VMEM_LIMIT)


def _dot(a, b):
    return jnp.dot(a, b, preferred_element_type=F32)


def _dot_nt(a, b):
    return lax.dot_general(a, b, (((1,), (1,)), ((), ())), preferred_element_type=F32)


def _layernorm_rows(y, g, b):
    mu = jnp.mean(y, axis=-1, keepdims=True)
    yc = y - mu
    var = jnp.mean(yc * yc, axis=-1, keepdims=True)
    return yc * lax.rsqrt(var + LN_EPS) * g + b


def _rms_rows(h, g):
    ms = jnp.mean(h * h, axis=-1, keepdims=True)
    return h * lax.rsqrt(ms + RMS_EPS) * g


def _mla_proj_kernel(x_ref, win_ref, qn_ref, kvn_ref, wuq_ref, wukv_ref, cc_ref, ss_ref,
                     q_ref, k_ref, v_ref):
    xb = x_ref[...].astype(BF16)
    h = _dot(xb, win_ref[...])
    cq = h[:, :MLA_Q_RANK]
    ckv = h[:, MLA_Q_RANK:MLA_Q_RANK + MLA_KV_RANK]
    o = MLA_Q_RANK + MLA_KV_RANK
    cc = cc_ref[...]
    ss = ss_ref[...]
    k_rope = h[:, o:o + LANES] * cc + h[:, o + LANES:o + 2 * LANES] * ss
    qall = _dot(_rms_rows(cq, qn_ref[...]).astype(BF16), wuq_ref[...])
    kvall = _dot(_rms_rows(ckv, kvn_ref[...]).astype(BF16), wukv_ref[...])
    k_rope_b = k_rope.astype(BF16)
    for hd in range(MLA_HEADS):
        b0 = hd * 3 * LANES
        q_ref[0, hd, :, 0:LANES] = qall[:, b0:b0 + LANES].astype(BF16)
        q_rope = qall[:, b0 + LANES:b0 + 2 * LANES] * cc + qall[:, b0 + 2 * LANES:b0 + 3 * LANES] * ss
        q_ref[0, hd, :, LANES:2 * LANES] = q_rope.astype(BF16)
        k_ref[0, hd, :, 0:LANES] = kvall[:, hd * LANES:(hd + 1) * LANES].astype(BF16)
        k_ref[0, hd, :, LANES:2 * LANES] = k_rope_b
        v0 = MLA_HEADS * MLA_NOPE + hd * MLA_V
        v_ref[0, hd] = kvall[:, v0:v0 + MLA_V].astype(BF16)


def _mla_proj(xt, B, S, w_in, q_norm, kv_norm, w_uq, w_ukv, cos, sin, tm):
    T = B * S
    n_s = S // tm
    half = MLA_ROPE // 2
    def swap_rope(w):
        return jnp.concatenate([w[..., half:], w[..., :half]], axis=-1)
    zpad = lambda w: jnp.concatenate([w, jnp.zeros_like(w)], axis=-1)
    w_kr = w_in[:, MLA_Q_RANK + MLA_KV_RANK:]
    win_ext = jnp.concatenate(
        [w_in[:, :MLA_Q_RANK + MLA_KV_RANK], zpad(w_kr), zpad(swap_rope(w_kr))], axis=-1).astype(BF16)
    uq_nope = w_uq[:, :, :MLA_NOPE]
    uq_rope = w_uq[:, :, MLA_NOPE:]
    wuq_ext = jnp.concatenate([uq_nope, zpad(uq_rope), zpad(swap_rope(uq_rope))], axis=-1)
    wuq_ext = wuq_ext.reshape(MLA_Q_RANK, MLA_HEADS * 3 * LANES).astype(BF16)
    wukv_ext = jnp.concatenate(
        [w_ukv[:, :, :MLA_NOPE].reshape(MLA_KV_RANK, -1), w_ukv[:, :, MLA_NOPE:].reshape(MLA_KV_RANK, -1)],
        axis=-1).astype(BF16)
    z = jnp.zeros((S, LANES - MLA_ROPE), F32)
    cc = jnp.concatenate([cos, cos, z], axis=-1)
    ss = jnp.concatenate([-sin, sin, z], axis=-1)
    full = lambda a: pl.BlockSpec(a.shape, lambda i: (0,) * a.ndim)
    qn = q_norm.reshape(1, -1)
    kvn = kv_norm.reshape(1, -1)
    head_map = lambda i: (i // n_s, 0, i % n_s, 0)
    return pl.pallas_call(
        _mla_proj_kernel,
        grid=(T // tm,),
        in_specs=[pl.BlockSpec((tm, D_MODEL), lambda i: (i, 0)),
                  full(win_ext), full(qn), full(kvn), full(wuq_ext), full(wukv_ext),
                  pl.BlockSpec((tm, LANES), lambda i: (i % n_s, 0)),
                  pl.BlockSpec((tm, LANES), lambda i: (i % n_s, 0))],
        out_specs=[pl.BlockSpec((1, MLA_HEADS, tm, MLA_QK_PAD), head_map),
                   pl.BlockSpec((1, MLA_HEADS, tm, MLA_QK_PAD), head_map),
                   pl.BlockSpec((1, MLA_HEADS, tm, MLA_V), head_map)],
        out_shape=[jax.ShapeDtypeStruct((B, MLA_HEADS, S, MLA_QK_PAD), BF16),
                   jax.ShapeDtypeStruct((B, MLA_HEADS, S, MLA_QK_PAD), BF16),
                   jax.ShapeDtypeStruct((B, MLA_HEADS, S, MLA_V), BF16)],
        compiler_params=_cparams("parallel"),
        name="mla_proj",
    )(xt, win_ext, qn, kvn, wuq_ext, wukv_ext, cc, ss)


MLA_HEADS_PER_STEP = 4
MLA_KEY_CLASSES = 8
DSA_KEY_CLASSES = 4


def _mla_attn_block(q_ref, k_ref, v_ref, o_ref, blk, *, tq, L, scale):
    row = lax.broadcasted_iota(I32, (tq, L), 0)
    col = lax.broadcasted_iota(I32, (tq, L), 1)
    adm = col // CHUNK <= (blk * tq + row) // CHUNK
    for h in range(MLA_HEADS_PER_STEP):
        s = _dot_nt(q_ref[0, h], k_ref[0, h, 0:L, :]) * scale
        s = jnp.where(adm, s, NEG_INF)
        m = jnp.max(s, axis=-1, keepdims=True)
        p = jnp.exp(s - m)
        l = jnp.sum(p, axis=-1, keepdims=True)
        o = _dot(p.astype(BF16), v_ref[0, h, 0:L, :])
        o_ref[0, :, h * MLA_V:(h + 1) * MLA_V] = (o * (1.0 / l)).astype(BF16)


def _mla_attn_kernel(q_ref, k_ref, v_ref, o_ref, *, tq, S, scale):
    blk = pl.program_id(2)
    n_cls = min(MLA_KEY_CLASSES, S // tq)
    width = S // n_cls
    cls = ((blk + 1) * tq - 1) // width
    for c in range(n_cls):
        @pl.when(cls == c)
        def _(c=c):
            _mla_attn_block(q_ref, k_ref, v_ref, o_ref, blk, tq=tq, L=(c + 1) * width, scale=scale)


def _mla_attn(q, k, v, tq):
    B, H, S, _ = q.shape
    hp = MLA_HEADS_PER_STEP
    scale = (MLA_NOPE + MLA_ROPE) ** -0.5
    return pl.pallas_call(
        functools.partial(_mla_attn_kernel, tq=tq, S=S, scale=scale),
        grid=(B, H // hp, S // tq),
        in_specs=[pl.BlockSpec((1, hp, tq, MLA_QK_PAD), lambda b, h, i: (b, h, i, 0)),
                  pl.BlockSpec((1, hp, S, MLA_QK_PAD), lambda b, h, i: (b, h, 0, 0)),
                  pl.BlockSpec((1, hp, S, MLA_V), lambda b, h, i: (b, h, 0, 0))],
        out_specs=pl.BlockSpec((1, tq, hp * MLA_V), lambda b, h, i: (b, i, h)),
        out_shape=jax.ShapeDtypeStruct((B, S, H * MLA_V), BF16),
        compiler_params=_cparams("parallel", "parallel", "arbitrary"),
        name="mla_attn",
    )(q, k, v)


def _proj_ln_kernel(a_ref, w_ref, x_ref, g_ref, b_ref, o_ref):
    m = _dot(a_ref[...], w_ref[...])
    y = DN_ALPHA * x_ref[...] + m
    o_ref[...] = _layernorm_rows(y, g_ref[...], b_ref[...])


def _proj_ln(a, w, xt, gain, bias, tm):
    T = xt.shape[0]
    wb = w.astype(BF16)
    g = gain.reshape(1, -1)
    b = bias.reshape(1, -1)
    return pl.pallas_call(
        _proj_ln_kernel,
        grid=(T // tm,),
        in_specs=[pl.BlockSpec((tm, a.shape[1]), lambda i: (i, 0)),
                  pl.BlockSpec(wb.shape, lambda i: (0, 0)),
                  pl.BlockSpec((tm, D_MODEL), lambda i: (i, 0)),
                  pl.BlockSpec(g.shape, lambda i: (0, 0)),
                  pl.BlockSpec(b.shape, lambda i: (0, 0))],
        out_specs=pl.BlockSpec((tm, D_MODEL), lambda i: (i, 0)),
        out_shape=jax.ShapeDtypeStruct((T, D_MODEL), F32),
        compiler_params=_cparams("parallel"),
        name="proj_ln",
    )(a, wb, xt, g, b)


def _dsa_proj_kernel(x_ref, w_ref, ch_ref, sh_ref, ci_ref, ck_ref, sk_ref,
                     q_ref, k_ref, v_ref, qi_ref, ki_ref, wi_ref, *, w_scale):
    xb = x_ref[...].astype(BF16)
    h = _dot(xb, w_ref[...])
    ch = ch_ref[...]
    sh = sh_ref[...]
    ci = ci_ref[...]
    half = LANES // 2
    for hd in range(DSA_HEADS):
        c0 = hd * DSA_HEAD_DIM
        qh = h[:, c0:c0 + LANES]
        q_ref[0, :, c0:c0 + LANES] = (qh * ch + pltpu.roll(qh, half, 1) * sh).astype(BF16)
        kh = h[:, DSA_HD + c0:DSA_HD + c0 + LANES]
        k_ref[0, :, c0:c0 + LANES] = (kh * ch + pltpu.roll(kh, half, 1) * sh).astype(BF16)
        t = h[:, 3 * DSA_HD + c0:3 * DSA_HD + c0 + LANES] * ci
        qi_ref[0, :, c0:c0 + LANES] = (t + pltpu.roll(t, half, 1)).astype(BF16)
    v_ref[0] = h[:, 2 * DSA_HD:3 * DSA_HD].astype(BF16)
    o = 4 * DSA_HD
    ki = h[:, o:o + LANES] * ck_ref[...] + h[:, o + LANES:o + 2 * LANES] * sk_ref[...]
    ki_ref[0] = ki.astype(BF16)
    wi_ref[0] = h[:, o + 2 * LANES:o + 3 * LANES] * w_scale


def _dsa_proj(xt, B, S, w_in, tm):
    T = B * S
    n_s = S // tm
    o1 = 3 * DSA_HD + IDX_HEADS * IDX_DIM
    ih = IDX_DIM // 2
    w_qi = w_in[:, 3 * DSA_HD:o1].reshape(D_MODEL, IDX_HEADS, IDX_DIM)
    w_qi_sw = jnp.concatenate([w_qi[..., ih:], w_qi[..., :ih]], axis=-1)
    w_qi_ext = jnp.concatenate([w_qi, w_qi_sw], axis=-1).reshape(D_MODEL, IDX_HEADS * LANES)
    w_ki = w_in[:, o1:o1 + IDX_DIM]
    w_ki_sw = jnp.concatenate([w_ki[:, ih:], w_ki[:, :ih]], axis=-1)
    zpad = lambda w: jnp.concatenate([w, jnp.zeros((w.shape[0], LANES - w.shape[1]), w.dtype)], axis=-1)
    w_ext = jnp.concatenate(
        [w_in[:, :3 * DSA_HD], w_qi_ext, zpad(w_ki), zpad(w_ki_sw), zpad(w_in[:, o1 + IDX_DIM:])],
        axis=-1).astype(BF16)
    cos_h, sin_h = _rope_tables(S, DSA_HEAD_DIM)
    cos_i, sin_i = _rope_tables(S, IDX_DIM)
    ch = jnp.concatenate([cos_h, cos_h], axis=-1)
    sh = jnp.concatenate([-sin_h, sin_h], axis=-1)
    ci = jnp.concatenate([cos_i, cos_i, -sin_i, sin_i], axis=-1)
    z = jnp.zeros((S, LANES - IDX_DIM), F32)
    ck = jnp.concatenate([cos_i, cos_i, z], axis=-1)
    sk = jnp.concatenate([-sin_i, sin_i, z], axis=-1)
    tab = pl.BlockSpec((tm, LANES), lambda i: (i % n_s, 0))
    row_map = lambda i: (i // n_s, i % n_s, 0)
    wide = pl.BlockSpec((1, tm, DSA_HD), row_map)
    narrow = pl.BlockSpec((1, tm, LANES), row_map)
    w_scale = IDX_HEADS ** -0.5 * IDX_DIM ** -0.5
    return pl.pallas_call(
        functools.partial(_dsa_proj_kernel, w_scale=w_scale),
        grid=(T // tm,),
        in_specs=[pl.BlockSpec((tm, D_MODEL), lambda i: (i, 0)),
                  pl.BlockSpec(w_ext.shape, lambda i: (0, 0)),
                  tab, tab, tab, tab, tab],
        out_specs=[wide, wide, wide, wide, narrow, narrow],
        out_shape=[jax.ShapeDtypeStruct((B, S, DSA_HD), BF16),
                   jax.ShapeDtypeStruct((B, S, DSA_HD), BF16),
                   jax.ShapeDtypeStruct((B, S, DSA_HD), BF16),
                   jax.ShapeDtypeStruct((B, S, IDX_HEADS * LANES), BF16),
                   jax.ShapeDtypeStruct((B, S, LANES), BF16),
                   jax.ShapeDtypeStruct((B, S, LANES), F32)],
        compiler_params=_cparams("parallel"),
        name="dsa_proj",
    )(xt, w_ext, ch, sh, ci, ck, sk)


def _dsa_attn_block(q_ref, qi_ref, wi_ref, k_ref, v_ref, ki_ref, o_ref, blk, *, tq, L, topk, scale):
    ki = ki_ref[0, 0:L, :]
    qi = qi_ref[0]
    wi = wi_ref[0]
    score = jnp.zeros((tq, L), F32)
    for hd in range(IDX_HEADS):
        logit = _dot_nt(qi[:, hd * LANES:(hd + 1) * LANES], ki)
        score = score + wi[:, hd:hd + 1] * jnp.maximum(logit, 0.0)
    row = lax.broadcasted_iota(I32, (tq, L), 0)
    col = lax.broadcasted_iota(I32, (tq, L), 1)
    adm = col // CHUNK <= (blk * tq + row) // CHUNK
    score = jnp.where(adm, score + 0.0, NEG_INF)
    bits = lax.bitcast_convert_type(score, I32)
    u = jnp.where(bits < 0, bits ^ 0x7FFFFFFF, bits)

    def search(i, thr):
        cand = thr + lax.shift_left(jnp.int32(1), 31 - i)
        cnt = jnp.sum((u >= cand).astype(F32), axis=-1, keepdims=True)
        return jnp.where(cnt >= topk, cand, thr)
    thr = lax.fori_loop(0, 32, search, jnp.full((tq, 1), INT_MIN, I32), unroll=8)

    gt = u > thr
    eq = u == thr
    need = topk - jnp.sum(gt.astype(F32), axis=-1, keepdims=True)
    r_i = lax.broadcasted_iota(I32, (LANES, LANES), 0)
    c_i = lax.broadcasted_iota(I32, (LANES, LANES), 1)
    tri = (r_i < c_i).astype(BF16)
    eq_b = eq.astype(BF16)
    run = jnp.zeros((tq, 1), F32)
    ranks = []
    for j in range(L // LANES):
        e = eq_b[:, j * LANES:(j + 1) * LANES]
        ranks.append(_dot(e, tri) + run)
        run = run + jnp.sum(e.astype(F32), axis=-1, keepdims=True)
    rank = jnp.concatenate(ranks, axis=-1)
    sel = adm & (gt | (eq & (rank < need)))

    for hd in range(DSA_HEADS):
        c0 = hd * DSA_HEAD_DIM
        s = _dot_nt(q_ref[0, :, c0:c0 + LANES], k_ref[0, 0:L, c0:c0 + LANES]) * scale
        s = jnp.where(sel, s, NEG_INF)
        m = jnp.max(s, axis=-1, keepdims=True)
        p = jnp.exp(s - m)
        l = jnp.sum(p, axis=-1, keepdims=True)
        o = _dot(p.astype(BF16), v_ref[0, 0:L, c0:c0 + LANES])
        o_ref[0, :, c0:c0 + LANES] = (o * (1.0 / l)).astype(BF16)


def _dsa_attn_kernel(q_ref, qi_ref, wi_ref, k_ref, v_ref, ki_ref, o_ref, *, tq, S, topk, scale):
    blk = pl.program_id(1)
    n_cls = min(DSA_KEY_CLASSES, S // tq)
    width = S // n_cls
    cls = ((blk + 1) * tq - 1) // width
    for c in range(n_cls):
        @pl.when(cls == c)
        def _(c=c):
            _dsa_attn_block(q_ref, qi_ref, wi_ref, k_ref, v_ref, ki_ref, o_ref, blk,
                            tq=tq, L=(c + 1) * width, topk=topk, scale=scale)


def _dsa_attn(q, k, v, qi, ki, wi, tq):
    B, S, _ = q.shape
    topk = min(DSA_TOPK_MAX, S // 4)
    scale = DSA_HEAD_DIM ** -0.5
    qmap = lambda b, i: (b, i, 0)
    kmap = lambda b, i: (b, 0, 0)
    return pl.pallas_call(
        functools.partial(_dsa_attn_kernel, tq=tq, S=S, topk=topk, scale=scale),
        grid=(B, S // tq),
        in_specs=[pl.BlockSpec((1, tq, DSA_HD), qmap),
                  pl.BlockSpec((1, tq, IDX_HEADS * LANES), qmap),
                  pl.BlockSpec((1, tq, LANES), qmap),
                  pl.BlockSpec((1, S, DSA_HD), kmap),
                  pl.BlockSpec((1, S, DSA_HD), kmap),
                  pl.BlockSpec((1, S, LANES), kmap)],
        out_specs=pl.BlockSpec((1, tq, DSA_HD), qmap),
        out_shape=jax.ShapeDtypeStruct((B, S, DSA_HD), BF16),
        compiler_params=_cparams("parallel", "arbitrary"),
        name="dsa_attn",
    )(q, qi, wi, k, v, ki)


_PEER_PAIRS = [(a, b) for a in range(PEER_TOPK) for b in range(PEER_TOPK // (a + 1))]
_PEER_CAND_ROWS = -(-len(_PEER_PAIRS) // SUBLANES) * SUBLANES


def _sort_network(n):
    pairs = []
    p = 1
    while p < n:
        k = p
        while k >= 1:
            for j in range(k % p, n - k, 2 * k):
                for i in range(min(k, n - j - k)):
                    if (i + j) // (2 * p) == (i + j + k) // (2 * p):
                        pairs.append((i + j, i + j + k))
            k //= 2
        p *= 2
    return pairs


def _top_rows(s, n):
    groups = PEER_NKEYS // SUBLANES
    v = [s[g * SUBLANES:(g + 1) * SUBLANES, :] for g in range(groups)]
    for i, j in _sort_network(groups):
        v[i], v[j] = jnp.maximum(v[i], v[j]), jnp.minimum(v[i], v[j])
    vals = []
    for k in range(n):
        m = jnp.max(v[0], axis=0, keepdims=True)
        vals.append(m)
        won = v[0] == m
        for d in range(groups - 1 - k):
            v[d] = jnp.where(won, v[d + 1], v[d])
    return vals


def _peer_route_kernel(x_ref, wq_ref, keys_ref, lim_ref, e1_ref, r2_ref, e2_ref, cand_ref):
    tt = x_ref.shape[0]
    q = _dot(x_ref[...].astype(BF16), wq_ref[...]).astype(BF16)
    half = PEER_QDIM // 2
    pad0 = _PEER_CAND_ROWS - SUBLANES
    cand_ref[pad0:, :] = jnp.full((SUBLANES, tt), NEG_INF, F32)
    for hd in range(PEER_HEADS):
        c0 = hd * PEER_QDIM
        s1 = _dot_nt(keys_ref[0, hd], q[:, c0:c0 + half])
        s2 = _dot_nt(keys_ref[1, hd], q[:, c0 + half:c0 + PEER_QDIM])
        v1 = _top_rows(s1, PEER_TOPK)
        v2 = _top_rows(s2, PEER_TOPK)
        for k, (a, b) in enumerate(_PEER_PAIRS):
            cand_ref[k:k + 1, :] = v1[a] + v2[b]
        cand = cand_ref[...]
        cur = cand
        theta = None
        for _ in range(PEER_TOPK):
            theta = jnp.max(cur, axis=0, keepdims=True)
            cur = jnp.where(cur == theta, NEG_INF, cur)
        top = v1[0] + v2[0]
        z = jnp.sum(jnp.where(cand >= theta, jnp.exp(cand - top), 0.0), axis=0, keepdims=True)
        r2 = jnp.full(s2.shape, float(PEER_TOPK), F32)
        for b in reversed(range(PEER_TOPK)):
            r2 = jnp.where(s2 >= v2[b], float(b), r2)
        lim = jnp.zeros(s1.shape, F32)
        for b in range(PEER_TOPK):
            lim = jnp.where(s1 + v2[b] >= theta, float(b + 1), lim)
        lim_ref[hd] = lim
        r2_ref[hd] = r2.astype(BF16)
        e1_ref[hd] = jnp.exp(s1 - v1[0]) * (1.0 / z)
        e2_ref[hd] = jnp.exp(s2 - v2[0]).astype(BF16)


def _peer_route(xt, w_q, sub_keys, tt):
    T = xt.shape[0]
    wq = w_q.astype(BF16)
    keys = sub_keys.astype(BF16)
    big = jax.ShapeDtypeStruct((PEER_HEADS, PEER_NKEYS, T), F32)
    big_b = jax.ShapeDtypeStruct((PEER_HEADS, PEER_NKEYS, T), BF16)
    big_spec = pl.BlockSpec((PEER_HEADS, PEER_NKEYS, tt), lambda i: (0, 0, i))
    return pl.pallas_call(
        _peer_route_kernel,
        grid=(T // tt,),
        in_specs=[pl.BlockSpec((tt, D_MODEL), lambda i: (i, 0)),
                  pl.BlockSpec(wq.shape, lambda i: (0, 0)),
                  pl.BlockSpec(keys.shape, lambda i: (0, 0, 0, 0))],
        out_specs=[big_spec, big_spec, big_spec, big_spec],
        out_shape=[big, big, big_b, big_b],
        scratch_shapes=[pltpu.VMEM((_PEER_CAND_ROWS, tt), F32)],
        compiler_params=_cparams("parallel"),
        name="peer_route",
    )(xt, wq, keys)


def _peer_chunks(te):
    q = te // 4
    return [(0, q), (q, 2 * q), (3 * q, q)]


BF16_SUBLANES = 2 * SUBLANES


def _row_bf16(ref, i1, hd, ls):
    row = jnp.broadcast_to(ref[i1, hd:hd + 1, ls], (BF16_SUBLANES, LANES)).astype(BF16)
    return jnp.tile(row, (PEER_NKEYS // BF16_SUBLANES, 1))


def _peer_dense_kernel(x_ref, wd_ref, wu_ref, lim_ref, e1_ref, r2_ref, e2_ref, g_ref, b_ref,
                       o_ref, xb_ref, p_ref, acc_ref, limr_ref, e1r_ref, r2s_ref, e2s_ref, *, te, tt):
    e = pl.program_id(1)

    @pl.when(e == 0)
    def _():
        xb_ref[...] = x_ref[...].astype(BF16)
        acc_ref[...] = jnp.zeros_like(acc_ref)
        r2s_ref[...] = r2_ref[...]
        e2s_ref[...] = e2_ref[...]

        def relayout(g, carry):
            g0 = pl.multiple_of(g * SUBLANES, SUBLANES)
            for hd in range(PEER_HEADS):
                limg = lim_ref[hd, pl.ds(g0, SUBLANES), :]
                e1g = e1_ref[hd, pl.ds(g0, SUBLANES), :]
                for r in range(SUBLANES):
                    limr_ref[g0 + r, hd:hd + 1, :] = limg[r:r + 1, :]
                    e1r_ref[g0 + r, hd:hd + 1, :] = e1g[r:r + 1, :]
            return carry
        lax.fori_loop(0, PEER_NKEYS // SUBLANES, relayout, 0)

    xb = xb_ref[...]
    i1_0 = e * (te // PEER_NKEYS)
    chunks = _peer_chunks(te)
    a_next = _dot_nt(wd_ref[chunks[0][0]:chunks[0][0] + chunks[0][1], :], xb)
    for c, (off, size) in enumerate(chunks):
        a_cur = a_next
        if c + 1 < len(chunks):
            noff, nsize = chunks[c + 1]
            a_next = _dot_nt(wd_ref[noff:noff + nsize, :], xb)
        for j in range(size // PEER_NKEYS):
            i1 = i1_0 + off // PEER_NKEYS + j
            r0 = off + j * PEER_NKEYS
            for lg in range(tt // LANES):
                ls = slice(lg * LANES, (lg + 1) * LANES)
                gate = jnp.zeros((PEER_NKEYS, LANES), BF16)
                zero = jnp.zeros((), BF16)
                for hd in range(PEER_HEADS):
                    limb = _row_bf16(limr_ref, i1, hd, ls)
                    e1b = _row_bf16(e1r_ref, i1, hd, ls)
                    gate = gate + jnp.where(r2s_ref[hd, :, ls] < limb, e2s_ref[hd, :, ls] * e1b, zero)
                a = a_cur[j * PEER_NKEYS:(j + 1) * PEER_NKEYS, ls]
                act = 0.5 * a * (1.0 + lax.erf(a * (2.0 ** -0.5)))
                p_ref[r0:r0 + PEER_NKEYS, ls] = act.astype(BF16) * gate
        acc_ref[...] += _dot(wu_ref[:, off:off + size], p_ref[off:off + size, :])

    @pl.when(e == pl.num_programs(1) - 1)
    def _():
        y = DN_ALPHA * x_ref[...] + acc_ref[...].T
        o_ref[...] = _layernorm_rows(y, g_ref[...], b_ref[...])


def _peer_dense(xt, w_down, w_up, route, gain, bias, tt, te):
    T = xt.shape[0]
    lim, e1, r2, e2 = route
    wd = w_down.astype(BF16)
    wu = w_up.astype(BF16).T
    g = gain.reshape(1, -1)
    b = bias.reshape(1, -1)
    big_spec = pl.BlockSpec((PEER_HEADS, PEER_NKEYS, tt), lambda i, e: (0, 0, i))
    return pl.pallas_call(
        functools.partial(_peer_dense_kernel, te=te, tt=tt),
        grid=(T // tt, PEER_EXPERTS // te),
        in_specs=[pl.BlockSpec((tt, D_MODEL), lambda i, e: (i, 0)),
                  pl.BlockSpec((te, D_MODEL), lambda i, e: (e, 0)),
                  pl.BlockSpec((D_MODEL, te), lambda i, e: (0, e)),
                  big_spec, big_spec, big_spec, big_spec,
                  pl.BlockSpec(g.shape, lambda i, e: (0, 0)),
                  pl.BlockSpec(b.shape, lambda i, e: (0, 0))],
        out_specs=pl.BlockSpec((tt, D_MODEL), lambda i, e: (i, 0)),
        out_shape=jax.ShapeDtypeStruct((T, D_MODEL), F32),
        scratch_shapes=[pltpu.VMEM((tt, D_MODEL), BF16),
                        pltpu.VMEM((te, tt), BF16),
                        pltpu.VMEM((D_MODEL, tt), F32),
                        pltpu.VMEM((PEER_NKEYS, PEER_HEADS, tt), F32),
                        pltpu.VMEM((PEER_NKEYS, PEER_HEADS, tt), F32),
                        pltpu.VMEM((PEER_HEADS, PEER_NKEYS, tt), BF16),
                        pltpu.VMEM((PEER_HEADS, PEER_NKEYS, tt), BF16)],
        compiler_params=_cparams("parallel", "arbitrary"),
        name="peer_dense",
    )(xt, wd, wu, lim, e1, r2, e2, g, b)


def _rope_tables(seq, dim):
    inv = ROPE_THETA ** (-jnp.arange(0, dim, 2, dtype=F32) / dim)
    ang = jnp.arange(seq, dtype=F32)[:, None] * inv[None, :]
    return jnp.cos(ang), jnp.sin(ang)


def _tiles(S, T):
    row = min(512, S)
    tq_mla = min(256, S)
    tq_dsa = min(256, S)
    tt_route = min(256, T)
    tt_dense = min(512, T)
    te_dense = 2048
    return row, tq_mla, tq_dsa, tt_route, tt_dense, te_dense


def _peer_layer(xt, w_q, sub_keys, w_down, w_up, gain, bias, tt_route, tt_dense, te_dense):
    route = _peer_route(xt, w_q, sub_keys, tt_route)
    return _peer_dense(xt, w_down, w_up, route, gain, bias, tt_dense, te_dense)


def kernel(x, mla_w_in, mla_q_norm, mla_kv_norm, mla_w_uq, mla_w_ukv, mla_w_o,
           dsa_w_in, dsa_w_o, peer_w_q, peer_sub_keys, peer_w_down, peer_w_up,
           ln_gain, ln_bias):
    B, S, D = x.shape
    T = B * S
    row, tq_mla, tq_dsa, tt_route, tt_dense, te_dense = _tiles(S, T)
    xt = x.reshape(T, D)

    cos, sin = _rope_tables(S, MLA_ROPE)
    q, k, v = _mla_proj(xt, B, S, mla_w_in[0], mla_q_norm[0], mla_kv_norm[0],
                        mla_w_uq[0], mla_w_ukv[0], cos, sin, row)
    o = _mla_attn(q, k, v, tq_mla).reshape(T, MLA_HEADS * MLA_V)
    xt = _proj_ln(o, mla_w_o[0], xt, ln_gain[0, 0], ln_bias[0, 0], row)
    xt = _peer_layer(xt, peer_w_q[0], peer_sub_keys[0], peer_w_down[0], peer_w_up[0],
                     ln_gain[0, 1], ln_bias[0, 1], tt_route, tt_dense, te_dense)

    q, k, v, qi, ki, wi = _dsa_proj(xt, B, S, dsa_w_in[0], row)
    o = _dsa_attn(q, k, v, qi, ki, wi, tq_dsa).reshape(T, DSA_HD)
    xt = _proj_ln(o, dsa_w_o[0], xt, ln_gain[1, 0], ln_bias[1, 0], row)
    xt = _peer_layer(xt, peer_w_q[1], peer_sub_keys[1], peer_w_down[1], peer_w_up[1],
                     ln_gain[1, 1], ln_bias[1, 1], tt_route, tt_dense, te_dense)
    return xt.reshape(B, S, D)
```

```python
import functools

import jax
import jax.numpy as jnp
from jax import lax
from jax.experimental import pallas as pl
from jax.experimental.pallas import tpu as pltpu

F32 = jnp.float32
BF16 = jnp.bfloat16
I32 = jnp.int32

D_MODEL = 1024
DEPTH = 2
CHUNK = 64
ROPE_THETA = 10000.0
LN_EPS = 1e-5
RMS_EPS = 1e-6
DN_ALPHA = (2 * DEPTH) ** 0.25

MLA_HEADS = 8
MLA_NOPE = 128
MLA_ROPE = 64
MLA_V = 128
MLA_Q_RANK = 384
MLA_KV_RANK = 256
MLA_QK_PAD = 256

DSA_HEADS = 8
DSA_HEAD_DIM = 128
IDX_HEADS = 8
IDX_DIM = 64
DSA_TOPK_MAX = 256
DSA_HD = DSA_HEADS * DSA_HEAD_DIM

PEER_HEADS = 8
PEER_NKEYS = 128
PEER_EXPERTS = PEER_NKEYS * PEER_NKEYS
PEER_QDIM = 256
PEER_TOPK = 16

LANES = 128
SUBLANES = 8
VMEM_LIMIT = 56 * 1024 * 1024

NEG_INF = float("-inf")
INT_MIN = -2 ** 31


def _cparams(*sem):
    return pltpu.CompilerParams(dimension_semantics=sem, vmem_limit_bytes=VMEM_LIMIT)


def _dot(a, b):
    return jnp.dot(a, b, preferred_element_type=F32)


def _dot_nt(a, b):
    return lax.dot_general(a, b, (((1,), (1,)), ((), ())), preferred_element_type=F32)


def _layernorm_rows(y, g, b):
    mu = jnp.mean(y, axis=-1, keepdims=True)
    yc = y - mu
    var = jnp.mean(yc * yc, axis=-1, keepdims=True)
    return yc * lax.rsqrt(var + LN_EPS) * g + b


def _rms_rows(h, g):
    ms = jnp.mean(h * h, axis=-1, keepdims=True)
    return h * lax.rsqrt(ms + RMS_EPS) * g


def _mla_proj_kernel(x_ref, win_ref, qn_ref, kvn_ref, wuq_ref, wukv_ref, cc_ref, ss_ref,
                     q_ref, k_ref, v_ref):
    xb = x_ref[...].astype(BF16)
    h = _dot(xb, win_ref[...])
    cq = h[:, :MLA_Q_RANK]
    ckv = h[:, MLA_Q_RANK:MLA_Q_RANK + MLA_KV_RANK]
    o = MLA_Q_RANK + MLA_KV_RANK
    cc = cc_ref[...]
    ss = ss_ref[...]
    k_rope = h[:, o:o + LANES] * cc + h[:, o + LANES:o + 2 * LANES] * ss
    qall = _dot(_rms_rows(cq, qn_ref[...]).astype(BF16), wuq_ref[...])
    kvall = _dot(_rms_rows(ckv, kvn_ref[...]).astype(BF16), wukv_ref[...])
    k_rope_b = k_rope.astype(BF16)
    for hd in range(MLA_HEADS):
        b0 = hd * 3 * LANES
        q_ref[0, hd, :, 0:LANES] = qall[:, b0:b0 + LANES].astype(BF16)
        q_rope = qall[:, b0 + LANES:b0 + 2 * LANES] * cc + qall[:, b0 + 2 * LANES:b0 + 3 * LANES] * ss
        q_ref[0, hd, :, LANES:2 * LANES] = q_rope.astype(BF16)
        k_ref[0, hd, :, 0:LANES] = kvall[:, hd * LANES:(hd + 1) * LANES].astype(BF16)
        k_ref[0, hd, :, LANES:2 * LANES] = k_rope_b
        v0 = MLA_HEADS * MLA_NOPE + hd * MLA_V
        v_ref[0, hd] = kvall[:, v0:v0 + MLA_V].astype(BF16)


def _mla_proj(xt, B, S, w_in, q_norm, kv_norm, w_uq, w_ukv, cos, sin, tm):
    T = B * S
    n_s = S // tm
    half = MLA_ROPE // 2
    def swap_rope(w):
        return jnp.concatenate([w[..., half:], w[..., :half]], axis=-1)
    zpad = lambda w: jnp.concatenate([w, jnp.zeros_like(w)], axis=-1)
    w_kr = w_in[:, MLA_Q_RANK + MLA_KV_RANK:]
    win_ext = jnp.concatenate(
        [w_in[:, :MLA_Q_RANK + MLA_KV_RANK], zpad(w_kr), zpad(swap_rope(w_kr))], axis=-1).astype(BF16)
    uq_nope = w_uq[:, :, :MLA_NOPE]
    uq_rope = w_uq[:, :, MLA_NOPE:]
    wuq_ext = jnp.concatenate([uq_nope, zpad(uq_rope), zpad(swap_rope(uq_rope))], axis=-1)
    wuq_ext = wuq_ext.reshape(MLA_Q_RANK, MLA_HEADS * 3 * LANES).astype(BF16)
    wukv_ext = jnp.concatenate(
        [w_ukv[:, :, :MLA_NOPE].reshape(MLA_KV_RANK, -1), w_ukv[:, :, MLA_NOPE:].reshape(MLA_KV_RANK, -1)],
        axis=-1).astype(BF16)
    z = jnp.zeros((S, LANES - MLA_ROPE), F32)
    cc = jnp.concatenate([cos, cos, z], axis=-1)
    ss = jnp.concatenate([-sin, sin, z], axis=-1)
    full = lambda a: pl.BlockSpec(a.shape, lambda i: (0,) * a.ndim)
    qn = q_norm.reshape(1, -1)
    kvn = kv_norm.reshape(1, -1)
    head_map = lambda i: (i // n_s, 0, i % n_s, 0)
    return pl.pallas_call(
        _mla_proj_kernel,
        grid=(T // tm,),
        in_specs=[pl.BlockSpec((tm, D_MODEL), lambda i: (i, 0)),
                  full(win_ext), full(qn), full(kvn), full(wuq_ext), full(wukv_ext),
                  pl.BlockSpec((tm, LANES), lambda i: (i % n_s, 0)),
                  pl.BlockSpec((tm, LANES), lambda i: (i % n_s, 0))],
        out_specs=[pl.BlockSpec((1, MLA_HEADS, tm, MLA_QK_PAD), head_map),
                   pl.BlockSpec((1, MLA_HEADS, tm, MLA_QK_PAD), head_map),
                   pl.BlockSpec((1, MLA_HEADS, tm, MLA_V), head_map)],
        out_shape=[jax.ShapeDtypeStruct((B, MLA_HEADS, S, MLA_QK_PAD), BF16),
                   jax.ShapeDtypeStruct((B, MLA_HEADS, S, MLA_QK_PAD), BF16),
                   jax.ShapeDtypeStruct((B, MLA_HEADS, S, MLA_V), BF16)],
        compiler_params=_cparams("parallel"),
        name="mla_proj",
    )(xt, win_ext, qn, kvn, wuq_ext, wukv_ext, cc, ss)


MLA_HEADS_PER_STEP = 4
MLA_KEY_CLASSES = 8
DSA_KEY_CLASSES = 4


def _mla_attn_block(q_ref, k_ref, v_ref, o_ref, blk, *, tq, L, scale):
    row = lax.broadcasted_iota(I32, (tq, L), 0)
    col = lax.broadcasted_iota(I32, (tq, L), 1)
    adm = col // CHUNK <= (blk * tq + row) // CHUNK
    for h in range(MLA_HEADS_PER_STEP):
        s = _dot_nt(q_ref[0, h], k_ref[0, h, 0:L, :]) * scale
        s = jnp.where(adm, s, NEG_INF)
        m = jnp.max(s, axis=-1, keepdims=True)
        p = jnp.exp(s - m)
        l = jnp.sum(p, axis=-1, keepdims=True)
        o = _dot(p.astype(BF16), v_ref[0, h, 0:L, :])
        o_ref[0, :, h * MLA_V:(h + 1) * MLA_V] = (o * (1.0 / l)).astype(BF16)


def _mla_attn_kernel(q_ref, k_ref, v_ref, o_ref, *, tq, S, scale):
    blk = pl.program_id(2)
    n_cls = min(MLA_KEY_CLASSES, S // tq)
    width = S // n_cls
    cls = ((blk + 1) * tq - 1) // width
    for c in range(n_cls):
        @pl.when(cls == c)
        def _(c=c):
            _mla_attn_block(q_ref, k_ref, v_ref, o_ref, blk, tq=tq, L=(c + 1) * width, scale=scale)


def _mla_attn(q, k, v, tq):
    B, H, S, _ = q.shape
    hp = MLA_HEADS_PER_STEP
    scale = (MLA_NOPE + MLA_ROPE) ** -0.5
    return pl.pallas_call(
        functools.partial(_mla_attn_kernel, tq=tq, S=S, scale=scale),
        grid=(B, H // hp, S // tq),
        in_specs=[pl.BlockSpec((1, hp, tq, MLA_QK_PAD), lambda b, h, i: (b, h, i, 0)),
                  pl.BlockSpec((1, hp, S, MLA_QK_PAD), lambda b, h, i: (b, h, 0, 0)),
                  pl.BlockSpec((1, hp, S, MLA_V), lambda b, h, i: (b, h, 0, 0))],
        out_specs=pl.BlockSpec((1, tq, hp * MLA_V), lambda b, h, i: (b, i, h)),
        out_shape=jax.ShapeDtypeStruct((B, S, H * MLA_V), BF16),
        compiler_params=_cparams("parallel", "parallel", "arbitrary"),
        name="mla_attn",
    )(q, k, v)


def _proj_ln_kernel(a_ref, w_ref, x_ref, g_ref, b_ref, o_ref):
    m = _dot(a_ref[...], w_ref[...])
    y = DN_ALPHA * x_ref[...] + m
    o_ref[...] = _layernorm_rows(y, g_ref[...], b_ref[...])


def _proj_ln(a, w, xt, gain, bias, tm):
    T = xt.shape[0]
    wb = w.astype(BF16)
    g = gain.reshape(1, -1)
    b = bias.reshape(1, -1)
    return pl.pallas_call(
        _proj_ln_kernel,
        grid=(T // tm,),
        in_specs=[pl.BlockSpec((tm, a.shape[1]), lambda i: (i, 0)),
                  pl.BlockSpec(wb.shape, lambda i: (0, 0)),
                  pl.BlockSpec((tm, D_MODEL), lambda i: (i, 0)),
                  pl.BlockSpec(g.shape, lambda i: (0, 0)),
                  pl.BlockSpec(b.shape, lambda i: (0, 0))],
        out_specs=pl.BlockSpec((tm, D_MODEL), lambda i: (i, 0)),
        out_shape=jax.ShapeDtypeStruct((T, D_MODEL), F32),
        compiler_params=_cparams("parallel"),
        name="proj_ln",
    )(a, wb, xt, g, b)


def _dsa_proj_kernel(x_ref, w_ref, ch_ref, sh_ref, ci_ref, ck_ref, sk_ref,
                     q_ref, k_ref, v_ref, qi_ref, ki_ref, wi_ref, *, w_scale):
    xb = x_ref[...].astype(BF16)
    h = _dot(xb, w_ref[...])
    ch = ch_ref[...]
    sh = sh_ref[...]
    ci = ci_ref[...]
    half = LANES // 2
    for hd in range(DSA_HEADS):
        c0 = hd * DSA_HEAD_DIM
        qh = h[:, c0:c0 + LANES]
        q_ref[0, :, c0:c0 + LANES] = (qh * ch + pltpu.roll(qh, half, 1) * sh).astype(BF16)
        kh = h[:, DSA_HD + c0:DSA_HD + c0 + LANES]
        k_ref[0, :, c0:c0 + LANES] = (kh * ch + pltpu.roll(kh, half, 1) * sh).astype(BF16)
        t = h[:, 3 * DSA_HD + c0:3 * DSA_HD + c0 + LANES] * ci
        qi_ref[0, :, c0:c0 + LANES] = (t + pltpu.roll(t, half, 1)).astype(BF16)
    v_ref[0] = h[:, 2 * DSA_HD:3 * DSA_HD].astype(BF16)
    o = 4 * DSA_HD
    ki = h[:, o:o + LANES] * ck_ref[...] + h[:, o + LANES:o + 2 * LANES] * sk_ref[...]
    ki_ref[0] = ki.astype(BF16)
    wi_ref[0] = h[:, o + 2 * LANES:o + 3 * LANES] * w_scale


def _dsa_proj(xt, B, S, w_in, tm):
    T = B * S
    n_s = S // tm
    o1 = 3 * DSA_HD + IDX_HEADS * IDX_DIM
    ih = IDX_DIM // 2
    w_qi = w_in[:, 3 * DSA_HD:o1].reshape(D_MODEL, IDX_HEADS, IDX_DIM)
    w_qi_sw = jnp.concatenate([w_qi[..., ih:], w_qi[..., :ih]], axis=-1)
    w_qi_ext = jnp.concatenate([w_qi, w_qi_sw], axis=-1).reshape(D_MODEL, IDX_HEADS * LANES)
    w_ki = w_in[:, o1:o1 + IDX_DIM]
    w_ki_sw = jnp.concatenate([w_ki[:, ih:], w_ki[:, :ih]], axis=-1)
    zpad = lambda w: jnp.concatenate([w, jnp.zeros((w.shape[0], LANES - w.shape[1]), w.dtype)], axis=-1)
    w_ext = jnp.concatenate(
        [w_in[:, :3 * DSA_HD], w_qi_ext, zpad(w_ki), zpad(w_ki_sw), zpad(w_in[:, o1 + IDX_DIM:])],
        axis=-1).astype(BF16)
    cos_h, sin_h = _rope_tables(S, DSA_HEAD_DIM)
    cos_i, sin_i = _rope_tables(S, IDX_DIM)
    ch = jnp.concatenate([cos_h, cos_h], axis=-1)
    sh = jnp.concatenate([-sin_h, sin_h], axis=-1)
    ci = jnp.concatenate([cos_i, cos_i, -sin_i, sin_i], axis=-1)
    z = jnp.zeros((S, LANES - IDX_DIM), F32)
    ck = jnp.concatenate([cos_i, cos_i, z], axis=-1)
    sk = jnp.concatenate([-sin_i, sin_i, z], axis=-1)
    tab = pl.BlockSpec((tm, LANES), lambda i: (i % n_s, 0))
    row_map = lambda i: (i // n_s, i % n_s, 0)
    wide = pl.BlockSpec((1, tm, DSA_HD), row_map)
    narrow = pl.BlockSpec((1, tm, LANES), row_map)
    w_scale = IDX_HEADS ** -0.5 * IDX_DIM ** -0.5
    return pl.pallas_call(
        functools.partial(_dsa_proj_kernel, w_scale=w_scale),
        grid=(T // tm,),
        in_specs=[pl.BlockSpec((tm, D_MODEL), lambda i: (i, 0)),
                  pl.BlockSpec(w_ext.shape, lambda i: (0, 0)),
                  tab, tab, tab, tab, tab],
        out_specs=[wide, wide, wide, wide, narrow, narrow],
        out_shape=[jax.ShapeDtypeStruct((B, S, DSA_HD), BF16),
                   jax.ShapeDtypeStruct((B, S, DSA_HD), BF16),
                   jax.ShapeDtypeStruct((B, S, DSA_HD), BF16),
                   jax.ShapeDtypeStruct((B, S, IDX_HEADS * LANES), BF16),
                   jax.ShapeDtypeStruct((B, S, LANES), BF16),
                   jax.ShapeDtypeStruct((B, S, LANES), F32)],
        compiler_params=_cparams("parallel"),
        name="dsa_proj",
    )(xt, w_ext, ch, sh, ci, ck, sk)


def _dsa_attn_block(q_ref, qi_ref, wi_ref, k_ref, v_ref, ki_ref, o_ref, blk, *, tq, L, topk, scale):
    ki = ki_ref[0, 0:L, :]
    qi = qi_ref[0]
    wi = wi_ref[0]
    score = jnp.zeros((tq, L), F32)
    for hd in range(IDX_HEADS):
        logit = _dot_nt(qi[:, hd * LANES:(hd + 1) * LANES], ki)
        score = score + wi[:, hd:hd + 1] * jnp.maximum(logit, 0.0)
    row = lax.broadcasted_iota(I32, (tq, L), 0)
    col = lax.broadcasted_iota(I32, (tq, L), 1)
    adm = col // CHUNK <= (blk * tq + row) // CHUNK
    score = jnp.where(adm, score + 0.0, NEG_INF)
    bits = lax.bitcast_convert_type(score, I32)
    u = jnp.where(bits < 0, bits ^ 0x7FFFFFFF, bits)

    def search(i, thr):
        cand = thr + lax.shift_left(jnp.int32(1), 31 - i)
        cnt = jnp.sum((u >= cand).astype(F32), axis=-1, keepdims=True)
        return jnp.where(cnt >= topk, cand, thr)
    thr = lax.fori_loop(0, 32, search, jnp.full((tq, 1), INT_MIN, I32), unroll=8)

    gt = u > thr
    eq = u == thr
    need = topk - jnp.sum(gt.astype(F32), axis=-1, keepdims=True)
    r_i = lax.broadcasted_iota(I32, (LANES, LANES), 0)
    c_i = lax.broadcasted_iota(I32, (LANES, LANES), 1)
    tri = (r_i < c_i).astype(BF16)
    eq_b = eq.astype(BF16)
    run = jnp.zeros((tq, 1), F32)
    ranks = []
    for j in range(L // LANES):
        e = eq_b[:, j * LANES:(j + 1) * LANES]
        ranks.append(_dot(e, tri) + run)
        run = run + jnp.sum(e.astype(F32), axis=-1, keepdims=True)
    rank = jnp.concatenate(ranks, axis=-1)
    sel = adm & (gt | (eq & (rank < need)))

    for hd in range(DSA_HEADS):
        c0 = hd * DSA_HEAD_DIM
        s = _dot_nt(q_ref[0, :, c0:c0 + LANES], k_ref[0, 0:L, c0:c0 + LANES]) * scale
        s = jnp.where(sel, s, NEG_INF)
        m = jnp.max(s, axis=-1, keepdims=True)
        p = jnp.exp(s - m)
        l = jnp.sum(p, axis=-1, keepdims=True)
        o = _dot(p.astype(BF16), v_ref[0, 0:L, c0:c0 + LANES])
        o_ref[0, :, c0:c0 + LANES] = (o * (1.0 / l)).astype(BF16)


def _dsa_attn_kernel(q_ref, qi_ref, wi_ref, k_ref, v_ref, ki_ref, o_ref, *, tq, S, topk, scale):
    blk = pl.program_id(1)
    n_cls = min(DSA_KEY_CLASSES, S // tq)
    width = S // n_cls
    cls = ((blk + 1) * tq - 1) // width
    for c in range(n_cls):
        @pl.when(cls == c)
        def _(c=c):
            _dsa_attn_block(q_ref, qi_ref, wi_ref, k_ref, v_ref, ki_ref, o_ref, blk,
                            tq=tq, L=(c + 1) * width, topk=topk, scale=scale)


def _dsa_attn(q, k, v, qi, ki, wi, tq):
    B, S, _ = q.shape
    topk = min(DSA_TOPK_MAX, S // 4)
    scale = DSA_HEAD_DIM ** -0.5
    qmap = lambda b, i: (b, i, 0)
    kmap = lambda b, i: (b, 0, 0)
    return pl.pallas_call(
        functools.partial(_dsa_attn_kernel, tq=tq, S=S, topk=topk, scale=scale),
        grid=(B, S // tq),
        in_specs=[pl.BlockSpec((1, tq, DSA_HD), qmap),
                  pl.BlockSpec((1, tq, IDX_HEADS * LANES), qmap),
                  pl.BlockSpec((1, tq, LANES), qmap),
                  pl.BlockSpec((1, S, DSA_HD), kmap),
                  pl.BlockSpec((1, S, DSA_HD), kmap),
                  pl.BlockSpec((1, S, LANES), kmap)],
        out_specs=pl.BlockSpec((1, tq, DSA_HD), qmap),
        out_shape=jax.ShapeDtypeStruct((B, S, DSA_HD), BF16),
        compiler_params=_cparams("parallel", "arbitrary"),
        name="dsa_attn",
    )(q, qi, wi, k, v, ki)


_PEER_PAIRS = [(a, b) for a in range(PEER_TOPK) for b in range(PEER_TOPK // (a + 1))]
_PEER_CAND_ROWS = -(-len(_PEER_PAIRS) // SUBLANES) * SUBLANES


def _sort_network(n):
    pairs = []
    p = 1
    while p < n:
        k = p
        while k >= 1:
            for j in range(k % p, n - k, 2 * k):
                for i in range(min(k, n - j - k)):
                    if (i + j) // (2 * p) == (i + j + k) // (2 * p):
                        pairs.append((i + j, i + j + k))
            k //= 2
        p *= 2
    return pairs


def _top_rows(s, n):
    groups = PEER_NKEYS // SUBLANES
    v = [s[g * SUBLANES:(g + 1) * SUBLANES, :] for g in range(groups)]
    for i, j in _sort_network(groups):
        v[i], v[j] = jnp.maximum(v[i], v[j]), jnp.minimum(v[i], v[j])
    vals = []
    for k in range(n):
        m = jnp.max(v[0], axis=0, keepdims=True)
        vals.append(m)
        won = v[0] == m
        for d in range(groups - 1 - k):
            v[d] = jnp.where(won, v[d + 1], v[d])
    return vals


def _peer_route_kernel(x_ref, wq_ref, keys_ref, lim_ref, e1_ref, r2_ref, e2_ref, cand_ref):
    tt = x_ref.shape[0]
    q = _dot(x_ref[...].astype(BF16), wq_ref[...]).astype(BF16)
    half = PEER_QDIM // 2
    pad0 = _PEER_CAND_ROWS - SUBLANES
    cand_ref[pad0:, :] = jnp.full((SUBLANES, tt), NEG_INF, F32)
    for hd in range(PEER_HEADS):
        c0 = hd * PEER_QDIM
        s1 = _dot_nt(keys_ref[0, hd], q[:, c0:c0 + half])
        s2 = _dot_nt(keys_ref[1, hd], q[:, c0 + half:c0 + PEER_QDIM])
        v1 = _top_rows(s1, PEER_TOPK)
        v2 = _top_rows(s2, PEER_TOPK)
        for k, (a, b) in enumerate(_PEER_PAIRS):
            cand_ref[k:k + 1, :] = v1[a] + v2[b]
        cand = cand_ref[...]
        cur = cand
        theta = None
        for _ in range(PEER_TOPK):
            theta = jnp.max(cur, axis=0, keepdims=True)
            cur = jnp.where(cur == theta, NEG_INF, cur)
        top = v1[0] + v2[0]
        z = jnp.sum(jnp.where(cand >= theta, jnp.exp(cand - top), 0.0), axis=0, keepdims=True)
        r2 = jnp.full(s2.shape, float(PEER_TOPK), F32)
        for b in reversed(range(PEER_TOPK)):
            r2 = jnp.where(s2 >= v2[b], float(b), r2)
        lim = jnp.zeros(s1.shape, F32)
        for b in range(PEER_TOPK):
            lim = jnp.where(s1 + v2[b] >= theta, float(b + 1), lim)
        lim_ref[hd] = lim
        r2_ref[hd] = r2.astype(BF16)
        e1_ref[hd] = jnp.exp(s1 - v1[0]) * (1.0 / z)
        e2_ref[hd] = jnp.exp(s2 - v2[0]).astype(BF16)


def _peer_route(xt, w_q, sub_keys, tt):
    T = xt.shape[0]
    wq = w_q.astype(BF16)
    keys = sub_keys.astype(BF16)
    big = jax.ShapeDtypeStruct((PEER_HEADS, PEER_NKEYS, T), F32)
    big_b = jax.ShapeDtypeStruct((PEER_HEADS, PEER_NKEYS, T), BF16)
    big_spec = pl.BlockSpec((PEER_HEADS, PEER_NKEYS, tt), lambda i: (0, 0, i))
    return pl.pallas_call(
        _peer_route_kernel,
        grid=(T // tt,),
        in_specs=[pl.BlockSpec((tt, D_MODEL), lambda i: (i, 0)),
                  pl.BlockSpec(wq.shape, lambda i: (0, 0)),
                  pl.BlockSpec(keys.shape, lambda i: (0, 0, 0, 0))],
        out_specs=[big_spec, big_spec, big_spec, big_spec],
        out_shape=[big, big, big_b, big_b],
        scratch_shapes=[pltpu.VMEM((_PEER_CAND_ROWS, tt), F32)],
        compiler_params=_cparams("parallel"),
        name="peer_route",
    )(xt, wq, keys)


PEER_CHUNK = 8 * PEER_NKEYS


BF16_SUBLANES = 2 * SUBLANES


def _row_bf16(ref, i1, hd, ls):
    row = jnp.broadcast_to(ref[i1, hd:hd + 1, ls], (BF16_SUBLANES, LANES)).astype(BF16)
    return jnp.tile(row, (PEER_NKEYS // BF16_SUBLANES, 1))


def _peer_dense_kernel(x_ref, wd_ref, wu_ref, lim_ref, e1_ref, r2_ref, e2_ref, g_ref, b_ref,
                       o_ref, xb_ref, p_ref, acc_ref, limr_ref, e1r_ref, r2s_ref, e2s_ref, *, te, tt):
    e = pl.program_id(1)

    @pl.when(e == 0)
    def _():
        xb_ref[...] = x_ref[...].T.astype(BF16)
        acc_ref[...] = jnp.zeros_like(acc_ref)
        r2s_ref[...] = r2_ref[...]
        e2s_ref[...] = e2_ref[...]

        def relayout(g, carry):
            g0 = pl.multiple_of(g * SUBLANES, SUBLANES)
            for hd in range(PEER_HEADS):
                limg = lim_ref[hd, pl.ds(g0, SUBLANES), :]
                e1g = e1_ref[hd, pl.ds(g0, SUBLANES), :]
                for r in range(SUBLANES):
                    limr_ref[g0 + r, hd:hd + 1, :] = limg[r:r + 1, :]
                    e1r_ref[g0 + r, hd:hd + 1, :] = e1g[r:r + 1, :]
            return carry
        lax.fori_loop(0, PEER_NKEYS // SUBLANES, relayout, 0)

    xb = xb_ref[...]
    n_ch = te // PEER_CHUNK
    i1_0 = e * (te // PEER_NKEYS)
    a_next = _dot(wd_ref[0:PEER_CHUNK, :], xb)
    for c in range(n_ch):
        a_cur = a_next
        if c + 1 < n_ch:
            a_next = _dot(wd_ref[(c + 1) * PEER_CHUNK:(c + 2) * PEER_CHUNK, :], xb)
        for j in range(PEER_CHUNK // PEER_NKEYS):
            i1 = i1_0 + c * (PEER_CHUNK // PEER_NKEYS) + j
            r0 = c * PEER_CHUNK + j * PEER_NKEYS
            for lg in range(tt // LANES):
                ls = slice(lg * LANES, (lg + 1) * LANES)
                gate = jnp.zeros((PEER_NKEYS, LANES), BF16)
                zero = jnp.zeros((), BF16)
                for hd in range(PEER_HEADS):
                    limb = _row_bf16(limr_ref, i1, hd, ls)
                    e1b = _row_bf16(e1r_ref, i1, hd, ls)
                    gate = gate + jnp.where(r2s_ref[hd, :, ls] < limb, e2s_ref[hd, :, ls] * e1b, zero)
                a = a_cur[j * PEER_NKEYS:(j + 1) * PEER_NKEYS, ls]
                act = 0.5 * a * (1.0 + lax.erf(a * (2.0 ** -0.5)))
                p_ref[r0:r0 + PEER_NKEYS, ls] = act.astype(BF16) * gate
        acc_ref[...] += _dot(wu_ref[c], p_ref[c * PEER_CHUNK:(c + 1) * PEER_CHUNK, :])

    @pl.when(e == pl.num_programs(1) - 1)
    def _():
        y = DN_ALPHA * x_ref[...] + acc_ref[...].T
        o_ref[...] = _layernorm_rows(y, g_ref[...], b_ref[...])


def _peer_dense(xt, w_down, w_up, route, gain, bias, tt, te):
    T = xt.shape[0]
    lim, e1, r2, e2 = route
    wd = w_down.astype(BF16)
    wu = w_up.astype(BF16).reshape(PEER_EXPERTS // PEER_CHUNK, PEER_CHUNK, D_MODEL).transpose(0, 2, 1)
    g = gain.reshape(1, -1)
    b = bias.reshape(1, -1)
    big_spec = pl.BlockSpec((PEER_HEADS, PEER_NKEYS, tt), lambda i, e: (0, 0, i))
    return pl.pallas_call(
        functools.partial(_peer_dense_kernel, te=te, tt=tt),
        grid=(T // tt, PEER_EXPERTS // te),
        in_specs=[pl.BlockSpec((tt, D_MODEL), lambda i, e: (i, 0)),
                  pl.BlockSpec((te, D_MODEL), lambda i, e: (e, 0)),
                  pl.BlockSpec((te // PEER_CHUNK, D_MODEL, PEER_CHUNK), lambda i, e: (e, 0, 0)),
                  big_spec, big_spec, big_spec, big_spec,
                  pl.BlockSpec(g.shape, lambda i, e: (0, 0)),
                  pl.BlockSpec(b.shape, lambda i, e: (0, 0))],
        out_specs=pl.BlockSpec((tt, D_MODEL), lambda i, e: (i, 0)),
        out_shape=jax.ShapeDtypeStruct((T, D_MODEL), F32),
        scratch_shapes=[pltpu.VMEM((D_MODEL, tt), BF16),
                        pltpu.VMEM((te, tt), BF16),
                        pltpu.VMEM((D_MODEL, tt), F32),
                        pltpu.VMEM((PEER_NKEYS, PEER_HEADS, tt), F32),
                        pltpu.VMEM((PEER_NKEYS, PEER_HEADS, tt), F32),
                        pltpu.VMEM((PEER_HEADS, PEER_NKEYS, tt), BF16),
                        pltpu.VMEM((PEER_HEADS, PEER_NKEYS, tt), BF16)],
        compiler_params=_cparams("parallel", "arbitrary"),
        name="peer_dense",
    )(xt, wd, wu, lim, e1, r2, e2, g, b)


def _rope_tables(seq, dim):
    inv = ROPE_THETA ** (-jnp.arange(0, dim, 2, dtype=F32) / dim)
    ang = jnp.arange(seq, dtype=F32)[:, None] * inv[None, :]
    return jnp.cos(ang), jnp.sin(ang)


def _tiles(S, T):
    row = min(512, S)
    tq_mla = min(256, S)
    tq_dsa = min(256, S)
    tt_route = min(256, T)
    tt_dense = min(512, T)
    te_dense = 2048
    return row, tq_mla, tq_dsa, tt_route, tt_dense, te_dense


def _peer_layer(xt, w_q, sub_keys, w_down, w_up, gain, bias, tt_route, tt_dense, te_dense):
    route = _peer_route(xt, w_q, sub_keys, tt_route)
    return _peer_dense(xt, w_down, w_up, route, gain, bias, tt_dense, te_dense)


def kernel(x, mla_w_in, mla_q_norm, mla_kv_norm, mla_w_uq, mla_w_ukv, mla_w_o,
           dsa_w_in, dsa_w_o, peer_w_q, peer_sub_keys, peer_w_down, peer_w_up,
           ln_gain, ln_bias):
    B, S, D = x.shape
    T = B * S
    row, tq_mla, tq_dsa, tt_route, tt_dense, te_dense = _tiles(S, T)
    xt = x.reshape(T, D)

    cos, sin = _rope_tables(S, MLA_ROPE)
    q, k, v = _mla_proj(xt, B, S, mla_w_in[0], mla_q_norm[0], mla_kv_norm[0],
                        mla_w_uq[0], mla_w_ukv[0], cos, sin, row)
    o = _mla_attn(q, k, v, tq_mla).reshape(T, MLA_HEADS * MLA_V)
    xt = _proj_ln(o, mla_w_o[0], xt, ln_gain[0, 0], ln_bias[0, 0], row)
    xt = _peer_layer(xt, peer_w_q[0], peer_sub_keys[0], peer_w_down[0], peer_w_up[0],
                     ln_gain[0, 1], ln_bias[0, 1], tt_route, tt_dense, te_dense)

    q, k, v, qi, ki, wi = _dsa_proj(xt, B, S, dsa_w_in[0], row)
    o = _dsa_attn(q, k, v, qi, ki, wi, tq_dsa).reshape(T, DSA_HD)
    xt = _proj_ln(o, dsa_w_o[0], xt, ln_gain[1, 0], ln_bias[1, 0], row)
    xt = _peer_layer(xt, peer_w_q[1], peer_sub_keys[1], peer_w_down[1], peer_w_up[1],
                     ln_gain[1, 1], ln_bias[1, 1], tt_route, tt_dense, te_dense)
    return xt.reshape(B, S, D)
```

```python
import functools

import jax
import jax.numpy as jnp
from jax import lax
from jax.experimental import pallas as pl
from jax.experimental.pallas import tpu as pltpu

F32 = jnp.float32
BF16 = jnp.bfloat16
I32 = jnp.int32

D_MODEL = 1024
DEPTH = 2
CHUNK = 64
ROPE_THETA = 10000.0
LN_EPS = 1e-5
RMS_EPS = 1e-6
DN_ALPHA = (2 * DEPTH) ** 0.25

MLA_HEADS = 8
MLA_NOPE = 128
MLA_ROPE = 64
MLA_V = 128
MLA_Q_RANK = 384
MLA_KV_RANK = 256
MLA_QK_PAD = 256

DSA_HEADS = 8
DSA_HEAD_DIM = 128
IDX_HEADS = 8
IDX_DIM = 64
DSA_TOPK_MAX = 256
DSA_HD = DSA_HEADS * DSA_HEAD_DIM

PEER_HEADS = 8
PEER_NKEYS = 128
PEER_EXPERTS = PEER_NKEYS * PEER_NKEYS
PEER_QDIM = 256
PEER_TOPK = 16

LANES = 128
SUBLANES = 8
VMEM_LIMIT = 56 * 1024 * 1024

NEG_INF = float("-inf")
INT_MIN = -2 ** 31
KEY_NEG_INF = INT_MIN + 0x7FFFFF


def _cparams(*sem):
    return pltpu.CompilerParams(dimension_semantics=sem, vmem_limit_bytes=VMEM_LIMIT)


def _dot(a, b):
    return jnp.dot(a, b, preferred_element_type=F32)


def _dot_nt(a, b):
    return lax.dot_general(a, b, (((1,), (1,)), ((), ())), preferred_element_type=F32)


def _layernorm_rows(y, g, b):
    mu = jnp.mean(y, axis=-1, keepdims=True)
    yc = y - mu
    var = jnp.mean(yc * yc, axis=-1, keepdims=True)
    return yc * lax.rsqrt(var + LN_EPS) * g + b


def _rms_rows(h, g):
    ms = jnp.mean(h * h, axis=-1, keepdims=True)
    return h * lax.rsqrt(ms + RMS_EPS) * g


def _mla_proj_kernel(x_ref, win_ref, qn_ref, kvn_ref, wuq_ref, wukv_ref, cc_ref, ss_ref,
                     q_ref, k_ref, v_ref):
    xb = x_ref[...].astype(BF16)
    h = _dot(xb, win_ref[...])
    cq = h[:, :MLA_Q_RANK]
    ckv = h[:, MLA_Q_RANK:MLA_Q_RANK + MLA_KV_RANK]
    o = MLA_Q_RANK + MLA_KV_RANK
    cc = cc_ref[...]
    ss = ss_ref[...]
    k_rope = h[:, o:o + LANES] * cc + h[:, o + LANES:o + 2 * LANES] * ss
    qall = _dot(_rms_rows(cq, qn_ref[...]).astype(BF16), wuq_ref[...])
    kvall = _dot(_rms_rows(ckv, kvn_ref[...]).astype(BF16), wukv_ref[...])
    k_rope_b = k_rope.astype(BF16)
    for hd in range(MLA_HEADS):
        b0 = hd * 3 * LANES
        q_ref[0, hd, :, 0:LANES] = qall[:, b0:b0 + LANES].astype(BF16)
        q_rope = qall[:, b0 + LANES:b0 + 2 * LANES] * cc + qall[:, b0 + 2 * LANES:b0 + 3 * LANES] * ss
        q_ref[0, hd, :, LANES:2 * LANES] = q_rope.astype(BF16)
        k_ref[0, hd, :, 0:LANES] = kvall[:, hd * LANES:(hd + 1) * LANES].astype(BF16)
        k_ref[0, hd, :, LANES:2 * LANES] = k_rope_b
        v0 = MLA_HEADS * MLA_NOPE + hd * MLA_V
        v_ref[0, hd] = kvall[:, v0:v0 + MLA_V].astype(BF16)


def _mla_proj(xt, B, S, w_in, q_norm, kv_norm, w_uq, w_ukv, cos, sin, tm):
    T = B * S
    n_s = S // tm
    half = MLA_ROPE // 2
    def swap_rope(w):
        return jnp.concatenate([w[..., half:], w[..., :half]], axis=-1)
    zpad = lambda w: jnp.concatenate([w, jnp.zeros_like(w)], axis=-1)
    w_kr = w_in[:, MLA_Q_RANK + MLA_KV_RANK:]
    win_ext = jnp.concatenate(
        [w_in[:, :MLA_Q_RANK + MLA_KV_RANK], zpad(w_kr), zpad(swap_rope(w_kr))], axis=-1).astype(BF16)
    uq_nope = w_uq[:, :, :MLA_NOPE]
    uq_rope = w_uq[:, :, MLA_NOPE:]
    wuq_ext = jnp.concatenate([uq_nope, zpad(uq_rope), zpad(swap_rope(uq_rope))], axis=-1)
    wuq_ext = wuq_ext.reshape(MLA_Q_RANK, MLA_HEADS * 3 * LANES).astype(BF16)
    wukv_ext = jnp.concatenate(
        [w_ukv[:, :, :MLA_NOPE].reshape(MLA_KV_RANK, -1), w_ukv[:, :, MLA_NOPE:].reshape(MLA_KV_RANK, -1)],
        axis=-1).astype(BF16)
    z = jnp.zeros((S, LANES - MLA_ROPE), F32)
    cc = jnp.concatenate([cos, cos, z], axis=-1)
    ss = jnp.concatenate([-sin, sin, z], axis=-1)
    full = lambda a: pl.BlockSpec(a.shape, lambda i: (0,) * a.ndim)
    qn = q_norm.reshape(1, -1)
    kvn = kv_norm.reshape(1, -1)
    head_map = lambda i: (i // n_s, 0, i % n_s, 0)
    return pl.pallas_call(
        _mla_proj_kernel,
        grid=(T // tm,),
        in_specs=[pl.BlockSpec((tm, D_MODEL), lambda i: (i, 0)),
                  full(win_ext), full(qn), full(kvn), full(wuq_ext), full(wukv_ext),
                  pl.BlockSpec((tm, LANES), lambda i: (i % n_s, 0)),
                  pl.BlockSpec((tm, LANES), lambda i: (i % n_s, 0))],
        out_specs=[pl.BlockSpec((1, MLA_HEADS, tm, MLA_QK_PAD), head_map),
                   pl.BlockSpec((1, MLA_HEADS, tm, MLA_QK_PAD), head_map),
                   pl.BlockSpec((1, MLA_HEADS, tm, MLA_V), head_map)],
        out_shape=[jax.ShapeDtypeStruct((B, MLA_HEADS, S, MLA_QK_PAD), BF16),
                   jax.ShapeDtypeStruct((B, MLA_HEADS, S, MLA_QK_PAD), BF16),
                   jax.ShapeDtypeStruct((B, MLA_HEADS, S, MLA_V), BF16)],
        compiler_params=_cparams("parallel"),
        name="mla_proj",
    )(xt, win_ext, qn, kvn, wuq_ext, wukv_ext, cc, ss)


MLA_HEADS_PER_STEP = 4
MLA_KEY_CLASSES = 8
DSA_KEY_CLASSES = 4


def _mla_attn_block(q_ref, k_ref, v_ref, o_ref, blk, *, tq, L, scale):
    row = lax.broadcasted_iota(I32, (tq, L), 0)
    col = lax.broadcasted_iota(I32, (tq, L), 1)
    adm = col // CHUNK <= (blk * tq + row) // CHUNK
    for h in range(MLA_HEADS_PER_STEP):
        s = _dot_nt(q_ref[0, h], k_ref[0, h, 0:L, :]) * scale
        s = jnp.where(adm, s, NEG_INF)
        m = jnp.max(s, axis=-1, keepdims=True)
        p = jnp.exp(s - m)
        l = jnp.sum(p, axis=-1, keepdims=True)
        o = _dot(p.astype(BF16), v_ref[0, h, 0:L, :])
        o_ref[0, :, h * MLA_V:(h + 1) * MLA_V] = (o * (1.0 / l)).astype(BF16)


def _mla_attn_kernel(q_ref, k_ref, v_ref, o_ref, *, tq, S, scale):
    blk = pl.program_id(2)
    n_cls = min(MLA_KEY_CLASSES, S // tq)
    width = S // n_cls
    cls = ((blk + 1) * tq - 1) // width
    for c in range(n_cls):
        @pl.when(cls == c)
        def _(c=c):
            _mla_attn_block(q_ref, k_ref, v_ref, o_ref, blk, tq=tq, L=(c + 1) * width, scale=scale)


def _mla_attn(q, k, v, tq):
    B, H, S, _ = q.shape
    hp = MLA_HEADS_PER_STEP
    scale = (MLA_NOPE + MLA_ROPE) ** -0.5
    return pl.pallas_call(
        functools.partial(_mla_attn_kernel, tq=tq, S=S, scale=scale),
        grid=(B, H // hp, S // tq),
        in_specs=[pl.BlockSpec((1, hp, tq, MLA_QK_PAD), lambda b, h, i: (b, h, i, 0)),
                  pl.BlockSpec((1, hp, S, MLA_QK_PAD), lambda b, h, i: (b, h, 0, 0)),
                  pl.BlockSpec((1, hp, S, MLA_V), lambda b, h, i: (b, h, 0, 0))],
        out_specs=pl.BlockSpec((1, tq, hp * MLA_V), lambda b, h, i: (b, i, h)),
        out_shape=jax.ShapeDtypeStruct((B, S, H * MLA_V), BF16),
        compiler_params=_cparams("parallel", "parallel", "arbitrary"),
        name="mla_attn",
    )(q, k, v)


def _proj_ln_kernel(a_ref, w_ref, x_ref, g_ref, b_ref, o_ref):
    m = _dot(a_ref[...], w_ref[...])
    y = DN_ALPHA * x_ref[...] + m
    o_ref[...] = _layernorm_rows(y, g_ref[...], b_ref[...])


def _proj_ln(a, w, xt, gain, bias, tm):
    T = xt.shape[0]
    wb = w.astype(BF16)
    g = gain.reshape(1, -1)
    b = bias.reshape(1, -1)
    return pl.pallas_call(
        _proj_ln_kernel,
        grid=(T // tm,),
        in_specs=[pl.BlockSpec((tm, a.shape[1]), lambda i: (i, 0)),
                  pl.BlockSpec(wb.shape, lambda i: (0, 0)),
                  pl.BlockSpec((tm, D_MODEL), lambda i: (i, 0)),
                  pl.BlockSpec(g.shape, lambda i: (0, 0)),
                  pl.BlockSpec(b.shape, lambda i: (0, 0))],
        out_specs=pl.BlockSpec((tm, D_MODEL), lambda i: (i, 0)),
        out_shape=jax.ShapeDtypeStruct((T, D_MODEL), F32),
        compiler_params=_cparams("parallel"),
        name="proj_ln",
    )(a, wb, xt, g, b)


def _dsa_proj_kernel(x_ref, w_ref, ch_ref, sh_ref, ci_ref, ck_ref, sk_ref,
                     q_ref, k_ref, v_ref, qi_ref, ki_ref, wi_ref, *, w_scale):
    xb = x_ref[...].astype(BF16)
    h = _dot(xb, w_ref[...])
    ch = ch_ref[...]
    sh = sh_ref[...]
    ci = ci_ref[...]
    half = LANES // 2
    for hd in range(DSA_HEADS):
        c0 = hd * DSA_HEAD_DIM
        qh = h[:, c0:c0 + LANES]
        q_ref[0, :, c0:c0 + LANES] = (qh * ch + pltpu.roll(qh, half, 1) * sh).astype(BF16)
        kh = h[:, DSA_HD + c0:DSA_HD + c0 + LANES]
        k_ref[0, :, c0:c0 + LANES] = (kh * ch + pltpu.roll(kh, half, 1) * sh).astype(BF16)
        t = h[:, 3 * DSA_HD + c0:3 * DSA_HD + c0 + LANES] * ci
        qi_ref[0, :, c0:c0 + LANES] = (t + pltpu.roll(t, half, 1)).astype(BF16)
    v_ref[0] = h[:, 2 * DSA_HD:3 * DSA_HD].astype(BF16)
    o = 4 * DSA_HD
    ki = h[:, o:o + LANES] * ck_ref[...] + h[:, o + LANES:o + 2 * LANES] * sk_ref[...]
    ki_ref[0] = ki.astype(BF16)
    wi_ref[0] = h[:, o + 2 * LANES:o + 3 * LANES] * w_scale


def _dsa_proj(xt, B, S, w_in, tm):
    T = B * S
    n_s = S // tm
    o1 = 3 * DSA_HD + IDX_HEADS * IDX_DIM
    ih = IDX_DIM // 2
    w_qi = w_in[:, 3 * DSA_HD:o1].reshape(D_MODEL, IDX_HEADS, IDX_DIM)
    w_qi_sw = jnp.concatenate([w_qi[..., ih:], w_qi[..., :ih]], axis=-1)
    w_qi_ext = jnp.concatenate([w_qi, w_qi_sw], axis=-1).reshape(D_MODEL, IDX_HEADS * LANES)
    w_ki = w_in[:, o1:o1 + IDX_DIM]
    w_ki_sw = jnp.concatenate([w_ki[:, ih:], w_ki[:, :ih]], axis=-1)
    zpad = lambda w: jnp.concatenate([w, jnp.zeros((w.shape[0], LANES - w.shape[1]), w.dtype)], axis=-1)
    w_ext = jnp.concatenate(
        [w_in[:, :3 * DSA_HD], w_qi_ext, zpad(w_ki), zpad(w_ki_sw), zpad(w_in[:, o1 + IDX_DIM:])],
        axis=-1).astype(BF16)
    cos_h, sin_h = _rope_tables(S, DSA_HEAD_DIM)
    cos_i, sin_i = _rope_tables(S, IDX_DIM)
    ch = jnp.concatenate([cos_h, cos_h], axis=-1)
    sh = jnp.concatenate([-sin_h, sin_h], axis=-1)
    ci = jnp.concatenate([cos_i, cos_i, -sin_i, sin_i], axis=-1)
    z = jnp.zeros((S, LANES - IDX_DIM), F32)
    ck = jnp.concatenate([cos_i, cos_i, z], axis=-1)
    sk = jnp.concatenate([-sin_i, sin_i, z], axis=-1)
    tab = pl.BlockSpec((tm, LANES), lambda i: (i % n_s, 0))
    row_map = lambda i: (i // n_s, i % n_s, 0)
    wide = pl.BlockSpec((1, tm, DSA_HD), row_map)
    narrow = pl.BlockSpec((1, tm, LANES), row_map)
    w_scale = IDX_HEADS ** -0.5 * IDX_DIM ** -0.5
    return pl.pallas_call(
        functools.partial(_dsa_proj_kernel, w_scale=w_scale),
        grid=(T // tm,),
        in_specs=[pl.BlockSpec((tm, D_MODEL), lambda i: (i, 0)),
                  pl.BlockSpec(w_ext.shape, lambda i: (0, 0)),
                  tab, tab, tab, tab, tab],
        out_specs=[wide, wide, wide, wide, narrow, narrow],
        out_shape=[jax.ShapeDtypeStruct((B, S, DSA_HD), BF16),
                   jax.ShapeDtypeStruct((B, S, DSA_HD), BF16),
                   jax.ShapeDtypeStruct((B, S, DSA_HD), BF16),
                   jax.ShapeDtypeStruct((B, S, IDX_HEADS * LANES), BF16),
                   jax.ShapeDtypeStruct((B, S, LANES), BF16),
                   jax.ShapeDtypeStruct((B, S, LANES), F32)],
        compiler_params=_cparams("parallel"),
        name="dsa_proj",
    )(xt, w_ext, ch, sh, ci, ck, sk)


def _dsa_attn_block(q_ref, qi_ref, wi_ref, k_ref, v_ref, ki_ref, o_ref, blk, *, tq, L, topk, scale):
    ki = ki_ref[0, 0:L, :]
    qi = qi_ref[0]
    wi = wi_ref[0]
    score = jnp.zeros((tq, L), F32)
    for hd in range(IDX_HEADS):
        logit = _dot_nt(qi[:, hd * LANES:(hd + 1) * LANES], ki)
        score = score + wi[:, hd:hd + 1] * jnp.maximum(logit, 0.0)
    row = lax.broadcasted_iota(I32, (tq, L), 0)
    col = lax.broadcasted_iota(I32, (tq, L), 1)
    adm = col // CHUNK <= (blk * tq + row) // CHUNK
    score = jnp.where(adm, score + 0.0, NEG_INF)

    def key_to_float(key):
        return lax.bitcast_convert_type(jnp.where(key < 0, key ^ 0x7FFFFFFF, key), F32)

    def search(i, thr):
        cand = thr + lax.shift_left(jnp.int32(1), 31 - i)
        cnt = jnp.sum((score >= key_to_float(cand)).astype(F32), axis=-1, keepdims=True)
        return jnp.where(cnt >= topk, cand, thr)
    thr = lax.fori_loop(0, 32, search, jnp.full((tq, 1), INT_MIN, I32), unroll=8)
    thr_f = key_to_float(jnp.maximum(thr, KEY_NEG_INF))

    gt = score > thr_f
    eq = score == thr_f
    need = topk - jnp.sum(gt.astype(F32), axis=-1, keepdims=True)
    r_i = lax.broadcasted_iota(I32, (LANES, LANES), 0)
    c_i = lax.broadcasted_iota(I32, (LANES, LANES), 1)
    tri = (r_i < c_i).astype(BF16)
    eq_b = eq.astype(BF16)
    run = jnp.zeros((tq, 1), F32)
    ranks = []
    for j in range(L // LANES):
        e = eq_b[:, j * LANES:(j + 1) * LANES]
        ranks.append(_dot(e, tri) + run)
        run = run + jnp.sum(e.astype(F32), axis=-1, keepdims=True)
    rank = jnp.concatenate(ranks, axis=-1)
    sel = adm & (gt | (eq & (rank < need)))

    for hd in range(DSA_HEADS):
        c0 = hd * DSA_HEAD_DIM
        s = _dot_nt(q_ref[0, :, c0:c0 + LANES], k_ref[0, 0:L, c0:c0 + LANES]) * scale
        s = jnp.where(sel, s, NEG_INF)
        m = jnp.max(s, axis=-1, keepdims=True)
        p = jnp.exp(s - m)
        l = jnp.sum(p, axis=-1, keepdims=True)
        o = _dot(p.astype(BF16), v_ref[0, 0:L, c0:c0 + LANES])
        o_ref[0, :, c0:c0 + LANES] = (o * (1.0 / l)).astype(BF16)


def _dsa_attn_kernel(q_ref, qi_ref, wi_ref, k_ref, v_ref, ki_ref, o_ref, *, tq, S, topk, scale):
    blk = pl.program_id(1)
    n_cls = min(DSA_KEY_CLASSES, S // tq)
    width = S // n_cls
    cls = ((blk + 1) * tq - 1) // width
    for c in range(n_cls):
        @pl.when(cls == c)
        def _(c=c):
            _dsa_attn_block(q_ref, qi_ref, wi_ref, k_ref, v_ref, ki_ref, o_ref, blk,
                            tq=tq, L=(c + 1) * width, topk=topk, scale=scale)


def _dsa_attn(q, k, v, qi, ki, wi, tq):
    B, S, _ = q.shape
    topk = min(DSA_TOPK_MAX, S // 4)
    scale = DSA_HEAD_DIM ** -0.5
    qmap = lambda b, i: (b, i, 0)
    kmap = lambda b, i: (b, 0, 0)
    return pl.pallas_call(
        functools.partial(_dsa_attn_kernel, tq=tq, S=S, topk=topk, scale=scale),
        grid=(B, S // tq),
        in_specs=[pl.BlockSpec((1, tq, DSA_HD), qmap),
                  pl.BlockSpec((1, tq, IDX_HEADS * LANES), qmap),
                  pl.BlockSpec((1, tq, LANES), qmap),
                  pl.BlockSpec((1, S, DSA_HD), kmap),
                  pl.BlockSpec((1, S, DSA_HD), kmap),
                  pl.BlockSpec((1, S, LANES), kmap)],
        out_specs=pl.BlockSpec((1, tq, DSA_HD), qmap),
        out_shape=jax.ShapeDtypeStruct((B, S, DSA_HD), BF16),
        compiler_params=_cparams("parallel", "arbitrary"),
        name="dsa_attn",
    )(q, qi, wi, k, v, ki)


_PEER_PAIRS = [(a, b) for a in range(PEER_TOPK) for b in range(PEER_TOPK // (a + 1))]
_PEER_CAND_ROWS = -(-len(_PEER_PAIRS) // SUBLANES) * SUBLANES


def _sort_network(n):
    pairs = []
    p = 1
    while p < n:
        k = p
        while k >= 1:
            for j in range(k % p, n - k, 2 * k):
                for i in range(min(k, n - j - k)):
                    if (i + j) // (2 * p) == (i + j + k) // (2 * p):
                        pairs.append((i + j, i + j + k))
            k //= 2
        p *= 2
    return pairs


def _top_rows(s, n):
    groups = PEER_NKEYS // SUBLANES
    v = [s[g * SUBLANES:(g + 1) * SUBLANES, :] for g in range(groups)]
    for i, j in _sort_network(groups):
        v[i], v[j] = jnp.maximum(v[i], v[j]), jnp.minimum(v[i], v[j])
    vals = []
    for k in range(n):
        m = jnp.max(v[0], axis=0, keepdims=True)
        vals.append(m)
        won = v[0] == m
        for d in range(groups - 1 - k):
            v[d] = jnp.where(won, v[d + 1], v[d])
    return vals


def _peer_route_kernel(x_ref, wq_ref, keys_ref, lim_ref, e1_ref, r2_ref, e2_ref, cand_ref):
    tt = x_ref.shape[0]
    q = _dot(x_ref[...].astype(BF16), wq_ref[...]).astype(BF16)
    half = PEER_QDIM // 2
    pad0 = _PEER_CAND_ROWS - SUBLANES
    cand_ref[pad0:, :] = jnp.full((SUBLANES, tt), NEG_INF, F32)
    for hd in range(PEER_HEADS):
        c0 = hd * PEER_QDIM
        s1 = _dot_nt(keys_ref[0, hd], q[:, c0:c0 + half])
        s2 = _dot_nt(keys_ref[1, hd], q[:, c0 + half:c0 + PEER_QDIM])
        v1 = _top_rows(s1, PEER_TOPK)
        v2 = _top_rows(s2, PEER_TOPK)
        for k, (a, b) in enumerate(_PEER_PAIRS):
            cand_ref[k:k + 1, :] = v1[a] + v2[b]
        cand = cand_ref[...]
        cur = cand
        theta = None
        for _ in range(PEER_TOPK):
            theta = jnp.max(cur, axis=0, keepdims=True)
            cur = jnp.where(cur == theta, NEG_INF, cur)
        top = v1[0] + v2[0]
        z = jnp.sum(jnp.where(cand >= theta, jnp.exp(cand - top), 0.0), axis=0, keepdims=True)
        r2 = jnp.full(s2.shape, float(PEER_TOPK), F32)
        for b in reversed(range(PEER_TOPK)):
            r2 = jnp.where(s2 >= v2[b], float(b), r2)
        lim = jnp.zeros(s1.shape, F32)
        for b in range(PEER_TOPK):
            lim = jnp.where(s1 + v2[b] >= theta, float(b + 1), lim)
        lim_ref[hd] = lim
        r2_ref[hd] = r2.astype(BF16)
        e1_ref[hd] = jnp.exp(s1 - v1[0]) * (1.0 / z)
        e2_ref[hd] = jnp.exp(s2 - v2[0]).astype(BF16)


def _peer_route(xt, w_q, sub_keys, tt):
    T = xt.shape[0]
    wq = w_q.astype(BF16)
    keys = sub_keys.astype(BF16)
    big = jax.ShapeDtypeStruct((PEER_HEADS, PEER_NKEYS, T), F32)
    big_b = jax.ShapeDtypeStruct((PEER_HEADS, PEER_NKEYS, T), BF16)
    big_spec = pl.BlockSpec((PEER_HEADS, PEER_NKEYS, tt), lambda i: (0, 0, i))
    return pl.pallas_call(
        _peer_route_kernel,
        grid=(T // tt,),
        in_specs=[pl.BlockSpec((tt, D_MODEL), lambda i: (i, 0)),
                  pl.BlockSpec(wq.shape, lambda i: (0, 0)),
                  pl.BlockSpec(keys.shape, lambda i: (0, 0, 0, 0))],
        out_specs=[big_spec, big_spec, big_spec, big_spec],
        out_shape=[big, big, big_b, big_b],
        scratch_shapes=[pltpu.VMEM((_PEER_CAND_ROWS, tt), F32)],
        compiler_params=_cparams("parallel"),
        name="peer_route",
    )(xt, wq, keys)


PEER_CHUNK = 8 * PEER_NKEYS


BF16_SUBLANES = 2 * SUBLANES


def _row_bf16(ref, i1, hd, ls):
    row = jnp.broadcast_to(ref[i1, hd:hd + 1, ls], (BF16_SUBLANES, LANES)).astype(BF16)
    return jnp.tile(row, (PEER_NKEYS // BF16_SUBLANES, 1))


def _peer_dense_kernel(x_ref, wd_ref, wu_ref, lim_ref, e1_ref, r2_ref, e2_ref, g_ref, b_ref,
                       o_ref, xb_ref, p_ref, acc_ref, limr_ref, e1r_ref, r2s_ref, e2s_ref, *, te, tt):
    e = pl.program_id(1)

    @pl.when(e == 0)
    def _():
        xb_ref[...] = x_ref[...].astype(BF16)
        acc_ref[...] = jnp.zeros_like(acc_ref)
        r2s_ref[...] = r2_ref[...]
        e2s_ref[...] = e2_ref[...]

        def relayout(g, carry):
            g0 = pl.multiple_of(g * SUBLANES, SUBLANES)
            for hd in range(PEER_HEADS):
                limg = lim_ref[hd, pl.ds(g0, SUBLANES), :]
                e1g = e1_ref[hd, pl.ds(g0, SUBLANES), :]
                for r in range(SUBLANES):
                    limr_ref[g0 + r, hd:hd + 1, :] = limg[r:r + 1, :]
                    e1r_ref[g0 + r, hd:hd + 1, :] = e1g[r:r + 1, :]
            return carry
        lax.fori_loop(0, PEER_NKEYS // SUBLANES, relayout, 0)

    xb = xb_ref[...]
    n_ch = te // PEER_CHUNK
    i1_0 = e * (te // PEER_NKEYS)
    a_next = _dot_nt(wd_ref[0:PEER_CHUNK, :], xb)
    for c in range(n_ch):
        a_cur = a_next
        if c + 1 < n_ch:
            a_next = _dot_nt(wd_ref[(c + 1) * PEER_CHUNK:(c + 2) * PEER_CHUNK, :], xb)
        for j in range(PEER_CHUNK // PEER_NKEYS):
            i1 = i1_0 + c * (PEER_CHUNK // PEER_NKEYS) + j
            r0 = c * PEER_CHUNK + j * PEER_NKEYS
            for lg in range(tt // LANES):
                ls = slice(lg * LANES, (lg + 1) * LANES)
                gate = jnp.zeros((PEER_NKEYS, LANES), BF16)
                zero = jnp.zeros((), BF16)
                for hd in range(PEER_HEADS):
                    limb = _row_bf16(limr_ref, i1, hd, ls)
                    e1b = _row_bf16(e1r_ref, i1, hd, ls)
                    gate = gate + jnp.where(r2s_ref[hd, :, ls] < limb, e2s_ref[hd, :, ls] * e1b, zero)
                a = a_cur[j * PEER_NKEYS:(j + 1) * PEER_NKEYS, ls]
                act = 0.5 * a * (1.0 + lax.erf(a * (2.0 ** -0.5)))
                p_ref[r0:r0 + PEER_NKEYS, ls] = act.astype(BF16) * gate
        acc_ref[...] += _dot(wu_ref[c], p_ref[c * PEER_CHUNK:(c + 1) * PEER_CHUNK, :])

    @pl.when(e == pl.num_programs(1) - 1)
    def _():
        y = DN_ALPHA * x_ref[...] + acc_ref[...].T
        o_ref[...] = _layernorm_rows(y, g_ref[...], b_ref[...])


def _peer_dense(xt, w_down, w_up, route, gain, bias, tt, te):
    T = xt.shape[0]
    lim, e1, r2, e2 = route
    wd = w_down.astype(BF16)
    wu = w_up.astype(BF16).reshape(PEER_EXPERTS // PEER_CHUNK, PEER_CHUNK, D_MODEL).transpose(0, 2, 1)
    g = gain.reshape(1, -1)
    b = bias.reshape(1, -1)
    big_spec = pl.BlockSpec((PEER_HEADS, PEER_NKEYS, tt), lambda i, e: (0, 0, i))
    return pl.pallas_call(
        functools.partial(_peer_dense_kernel, te=te, tt=tt),
        grid=(T // tt, PEER_EXPERTS // te),
        in_specs=[pl.BlockSpec((tt, D_MODEL), lambda i, e: (i, 0)),
                  pl.BlockSpec((te, D_MODEL), lambda i, e: (e, 0)),
                  pl.BlockSpec((te // PEER_CHUNK, D_MODEL, PEER_CHUNK), lambda i, e: (e, 0, 0)),
                  big_spec, big_spec, big_spec, big_spec,
                  pl.BlockSpec(g.shape, lambda i, e: (0, 0)),
                  pl.BlockSpec(b.shape, lambda i, e: (0, 0))],
        out_specs=pl.BlockSpec((tt, D_MODEL), lambda i, e: (i, 0)),
        out_shape=jax.ShapeDtypeStruct((T, D_MODEL), F32),
        scratch_shapes=[pltpu.VMEM((tt, D_MODEL), BF16),
                        pltpu.VMEM((te, tt), BF16),
                        pltpu.VMEM((D_MODEL, tt), F32),
                        pltpu.VMEM((PEER_NKEYS, PEER_HEADS, tt), F32),
                        pltpu.VMEM((PEER_NKEYS, PEER_HEADS, tt), F32),
                        pltpu.VMEM((PEER_HEADS, PEER_NKEYS, tt), BF16),
                        pltpu.VMEM((PEER_HEADS, PEER_NKEYS, tt), BF16)],
        compiler_params=_cparams("parallel", "arbitrary"),
        name="peer_dense",
    )(xt, wd, wu, lim, e1, r2, e2, g, b)


def _rope_tables(seq, dim):
    inv = ROPE_THETA ** (-jnp.arange(0, dim, 2, dtype=F32) / dim)
    ang = jnp.arange(seq, dtype=F32)[:, None] * inv[None, :]
    return jnp.cos(ang), jnp.sin(ang)


def _tiles(S, T):
    row = min(512, S)
    tq_mla = min(256, S)
    tq_dsa = min(256, S)
    tt_route = min(256, T)
    tt_dense = min(512, T)
    te_dense = 2048
    return row, tq_mla, tq_dsa, tt_route, tt_dense, te_dense


def _peer_layer(xt, w_q, sub_keys, w_down, w_up, gain, bias, tt_route, tt_dense, te_dense):
    route = _peer_route(xt, w_q, sub_keys, tt_route)
    return _peer_dense(xt, w_down, w_up, route, gain, bias, tt_dense, te_dense)


def kernel(x, mla_w_in, mla_q_norm, mla_kv_norm, mla_w_uq, mla_w_ukv, mla_w_o,
           dsa_w_in, dsa_w_o, peer_w_q, peer_sub_keys, peer_w_down, peer_w_up,
           ln_gain, ln_bias):
    B, S, D = x.shape
    T = B * S
    row, tq_mla, tq_dsa, tt_route, tt_dense, te_dense = _tiles(S, T)
    xt = x.reshape(T, D)

    cos, sin = _rope_tables(S, MLA_ROPE)
    q, k, v = _mla_proj(xt, B, S, mla_w_in[0], mla_q_norm[0], mla_kv_norm[0],
                        mla_w_uq[0], mla_w_ukv[0], cos, sin, row)
    o = _mla_attn(q, k, v, tq_mla).reshape(T, MLA_HEADS * MLA_V)
    xt = _proj_ln(o, mla_w_o[0], xt, ln_gain[0, 0], ln_bias[0, 0], row)
    xt = _peer_layer(xt, peer_w_q[0], peer_sub_keys[0], peer_w_down[0], peer_w_up[0],
                     ln_gain[0, 1], ln_bias[0, 1], tt_route, tt_dense, te_dense)

    q, k, v, qi, ki, wi = _dsa_proj(xt, B, S, dsa_w_in[0], row)
    o = _dsa_attn(q, k, v, qi, ki, wi, tq_dsa).reshape(T, DSA_HD)
    xt = _proj_ln(o, dsa_w_o[0], xt, ln_gain[1, 0], ln_bias[1, 0], row)
    xt = _peer_layer(xt, peer_w_q[1], peer_sub_keys[1], peer_w_down[1], peer_w_up[1],
                     ln_gain[1, 1], ln_bias[1, 1], tt_route, tt_dense, te_dense)
    return xt.reshape(B, S, D)
```

```python
import functools

import jax
import jax.numpy as jnp
from jax import lax
from jax.experimental import pallas as pl
from jax.experimental.pallas import tpu as pltpu

F32 = jnp.float32
BF16 = jnp.bfloat16
I32 = jnp.int32

D_MODEL = 1024
DEPTH = 2
CHUNK = 64
ROPE_THETA = 10000.0
LN_EPS = 1e-5
RMS_EPS = 1e-6
DN_ALPHA = (2 * DEPTH) ** 0.25

MLA_HEADS = 8
MLA_NOPE = 128
MLA_ROPE = 64
MLA_V = 128
MLA_Q_RANK = 384
MLA_KV_RANK = 256
MLA_QK_PAD = 256

DSA_HEADS = 8
DSA_HEAD_DIM = 128
IDX_HEADS = 8
IDX_DIM = 64
DSA_TOPK_MAX = 256
DSA_HD = DSA_HEADS * DSA_HEAD_DIM

PEER_HEADS = 8
PEER_NKEYS = 128
PEER_EXPERTS = PEER_NKEYS * PEER_NKEYS
PEER_QDIM = 256
PEER_TOPK = 16

LANES = 128
SUBLANES = 8
VMEM_LIMIT = 56 * 1024 * 1024

NEG_INF = float("-inf")
INT_MIN = -2 ** 31
KEY_NEG_INF = INT_MIN + 0x7FFFFF


def _cparams(*sem):
    return pltpu.CompilerParams(dimension_semantics=sem, vmem_limit_bytes=VMEM_LIMIT)


def _dot(a, b):
    return jnp.dot(a, b, preferred_element_type=F32)


def _dot_nt(a, b):
    return lax.dot_general(a, b, (((1,), (1,)), ((), ())), preferred_element_type=F32)


def _layernorm_rows(y, g, b):
    mu = jnp.mean(y, axis=-1, keepdims=True)
    yc = y - mu
    var = jnp.mean(yc * yc, axis=-1, keepdims=True)
    return yc * lax.rsqrt(var + LN_EPS) * g + b


def _rms_rows(h, g):
    ms = jnp.mean(h * h, axis=-1, keepdims=True)
    return h * lax.rsqrt(ms + RMS_EPS) * g


def _mla_proj_kernel(x_ref, win_ref, qn_ref, kvn_ref, wuq_ref, wukv_ref, cc_ref, ss_ref,
                     q_ref, k_ref, v_ref):
    xb = x_ref[...].astype(BF16)
    h = _dot(xb, win_ref[...])
    cq = h[:, :MLA_Q_RANK]
    ckv = h[:, MLA_Q_RANK:MLA_Q_RANK + MLA_KV_RANK]
    o = MLA_Q_RANK + MLA_KV_RANK
    cc = cc_ref[...]
    ss = ss_ref[...]
    k_rope = h[:, o:o + LANES] * cc + h[:, o + LANES:o + 2 * LANES] * ss
    qall = _dot(_rms_rows(cq, qn_ref[...]).astype(BF16), wuq_ref[...])
    kvall = _dot(_rms_rows(ckv, kvn_ref[...]).astype(BF16), wukv_ref[...])
    k_rope_b = k_rope.astype(BF16)
    for hd in range(MLA_HEADS):
        b0 = hd * 3 * LANES
        q_ref[0, hd, :, 0:LANES] = qall[:, b0:b0 + LANES].astype(BF16)
        q_rope = qall[:, b0 + LANES:b0 + 2 * LANES] * cc + qall[:, b0 + 2 * LANES:b0 + 3 * LANES] * ss
        q_ref[0, hd, :, LANES:2 * LANES] = q_rope.astype(BF16)
        k_ref[0, hd, :, 0:LANES] = kvall[:, hd * LANES:(hd + 1) * LANES].astype(BF16)
        k_ref[0, hd, :, LANES:2 * LANES] = k_rope_b
        v0 = MLA_HEADS * MLA_NOPE + hd * MLA_V
        v_ref[0, hd] = kvall[:, v0:v0 + MLA_V].astype(BF16)


def _mla_proj(xt, B, S, w_in, q_norm, kv_norm, w_uq, w_ukv, cos, sin, tm):
    T = B * S
    n_s = S // tm
    half = MLA_ROPE // 2
    def swap_rope(w):
        return jnp.concatenate([w[..., half:], w[..., :half]], axis=-1)
    zpad = lambda w: jnp.concatenate([w, jnp.zeros_like(w)], axis=-1)
    w_kr = w_in[:, MLA_Q_RANK + MLA_KV_RANK:]
    win_ext = jnp.concatenate(
        [w_in[:, :MLA_Q_RANK + MLA_KV_RANK], zpad(w_kr), zpad(swap_rope(w_kr))], axis=-1).astype(BF16)
    uq_nope = w_uq[:, :, :MLA_NOPE]
    uq_rope = w_uq[:, :, MLA_NOPE:]
    wuq_ext = jnp.concatenate([uq_nope, zpad(uq_rope), zpad(swap_rope(uq_rope))], axis=-1)
    wuq_ext = wuq_ext.reshape(MLA_Q_RANK, MLA_HEADS * 3 * LANES).astype(BF16)
    wukv_ext = jnp.concatenate(
        [w_ukv[:, :, :MLA_NOPE].reshape(MLA_KV_RANK, -1), w_ukv[:, :, MLA_NOPE:].reshape(MLA_KV_RANK, -1)],
        axis=-1).astype(BF16)
    z = jnp.zeros((S, LANES - MLA_ROPE), F32)
    cc = jnp.concatenate([cos, cos, z], axis=-1)
    ss = jnp.concatenate([-sin, sin, z], axis=-1)
    full = lambda a: pl.BlockSpec(a.shape, lambda i: (0,) * a.ndim)
    qn = q_norm.reshape(1, -1)
    kvn = kv_norm.reshape(1, -1)
    head_map = lambda i: (i // n_s, 0, i % n_s, 0)
    return pl.pallas_call(
        _mla_proj_kernel,
        grid=(T // tm,),
        in_specs=[pl.BlockSpec((tm, D_MODEL), lambda i: (i, 0)),
                  full(win_ext), full(qn), full(kvn), full(wuq_ext), full(wukv_ext),
                  pl.BlockSpec((tm, LANES), lambda i: (i % n_s, 0)),
                  pl.BlockSpec((tm, LANES), lambda i: (i % n_s, 0))],
        out_specs=[pl.BlockSpec((1, MLA_HEADS, tm, MLA_QK_PAD), head_map),
                   pl.BlockSpec((1, MLA_HEADS, tm, MLA_QK_PAD), head_map),
                   pl.BlockSpec((1, MLA_HEADS, tm, MLA_V), head_map)],
        out_shape=[jax.ShapeDtypeStruct((B, MLA_HEADS, S, MLA_QK_PAD), BF16),
                   jax.ShapeDtypeStruct((B, MLA_HEADS, S, MLA_QK_PAD), BF16),
                   jax.ShapeDtypeStruct((B, MLA_HEADS, S, MLA_V), BF16)],
        compiler_params=_cparams("parallel"),
        name="mla_proj",
    )(xt, win_ext, qn, kvn, wuq_ext, wukv_ext, cc, ss)


MLA_HEADS_PER_STEP = 4
MLA_KEY_CLASSES = 8
DSA_KEY_CLASSES = 4


def _mla_attn_block(q_ref, k_ref, v_ref, o_ref, blk, *, tq, L, scale):
    row = lax.broadcasted_iota(I32, (tq, L), 0)
    col = lax.broadcasted_iota(I32, (tq, L), 1)
    adm = col // CHUNK <= (blk * tq + row) // CHUNK
    for h in range(MLA_HEADS_PER_STEP):
        s = _dot_nt(q_ref[0, h], k_ref[0, h, 0:L, :]) * scale
        s = jnp.where(adm, s, NEG_INF)
        m = jnp.max(s, axis=-1, keepdims=True)
        p = jnp.exp(s - m)
        l = jnp.sum(p, axis=-1, keepdims=True)
        o = _dot(p.astype(BF16), v_ref[0, h, 0:L, :])
        o_ref[0, :, h * MLA_V:(h + 1) * MLA_V] = (o * (1.0 / l)).astype(BF16)


def _mla_attn_kernel(q_ref, k_ref, v_ref, o_ref, *, tq, S, scale):
    blk = pl.program_id(2)
    n_cls = min(MLA_KEY_CLASSES, S // tq)
    width = S // n_cls
    cls = ((blk + 1) * tq - 1) // width
    for c in range(n_cls):
        @pl.when(cls == c)
        def _(c=c):
            _mla_attn_block(q_ref, k_ref, v_ref, o_ref, blk, tq=tq, L=(c + 1) * width, scale=scale)


def _mla_attn(q, k, v, tq):
    B, H, S, _ = q.shape
    hp = MLA_HEADS_PER_STEP
    scale = (MLA_NOPE + MLA_ROPE) ** -0.5
    return pl.pallas_call(
        functools.partial(_mla_attn_kernel, tq=tq, S=S, scale=scale),
        grid=(B, H // hp, S // tq),
        in_specs=[pl.BlockSpec((1, hp, tq, MLA_QK_PAD), lambda b, h, i: (b, h, i, 0)),
                  pl.BlockSpec((1, hp, S, MLA_QK_PAD), lambda b, h, i: (b, h, 0, 0)),
                  pl.BlockSpec((1, hp, S, MLA_V), lambda b, h, i: (b, h, 0, 0))],
        out_specs=pl.BlockSpec((1, tq, hp * MLA_V), lambda b, h, i: (b, i, h)),
        out_shape=jax.ShapeDtypeStruct((B, S, H * MLA_V), BF16),
        compiler_params=_cparams("parallel", "parallel", "arbitrary"),
        name="mla_attn",
    )(q, k, v)


def _proj_ln_kernel(a_ref, w_ref, x_ref, g_ref, b_ref, o_ref):
    m = _dot(a_ref[...], w_ref[...])
    y = DN_ALPHA * x_ref[...] + m
    o_ref[...] = _layernorm_rows(y, g_ref[...], b_ref[...])


def _proj_ln(a, w, xt, gain, bias, tm):
    T = xt.shape[0]
    wb = w.astype(BF16)
    g = gain.reshape(1, -1)
    b = bias.reshape(1, -1)
    return pl.pallas_call(
        _proj_ln_kernel,
        grid=(T // tm,),
        in_specs=[pl.BlockSpec((tm, a.shape[1]), lambda i: (i, 0)),
                  pl.BlockSpec(wb.shape, lambda i: (0, 0)),
                  pl.BlockSpec((tm, D_MODEL), lambda i: (i, 0)),
                  pl.BlockSpec(g.shape, lambda i: (0, 0)),
                  pl.BlockSpec(b.shape, lambda i: (0, 0))],
        out_specs=pl.BlockSpec((tm, D_MODEL), lambda i: (i, 0)),
        out_shape=jax.ShapeDtypeStruct((T, D_MODEL), F32),
        compiler_params=_cparams("parallel"),
        name="proj_ln",
    )(a, wb, xt, g, b)


def _dsa_proj_kernel(x_ref, w_ref, ch_ref, sh_ref, ci_ref, ck_ref, sk_ref,
                     q_ref, k_ref, v_ref, qi_ref, ki_ref, wi_ref, *, w_scale):
    xb = x_ref[...].astype(BF16)
    h = _dot(xb, w_ref[...])
    ch = ch_ref[...]
    sh = sh_ref[...]
    ci = ci_ref[...]
    half = LANES // 2
    for hd in range(DSA_HEADS):
        c0 = hd * DSA_HEAD_DIM
        qh = h[:, c0:c0 + LANES]
        q_ref[0, :, c0:c0 + LANES] = (qh * ch + pltpu.roll(qh, half, 1) * sh).astype(BF16)
        kh = h[:, DSA_HD + c0:DSA_HD + c0 + LANES]
        k_ref[0, :, c0:c0 + LANES] = (kh * ch + pltpu.roll(kh, half, 1) * sh).astype(BF16)
        t = h[:, 3 * DSA_HD + c0:3 * DSA_HD + c0 + LANES] * ci
        qi_ref[0, :, c0:c0 + LANES] = (t + pltpu.roll(t, half, 1)).astype(BF16)
    v_ref[0] = h[:, 2 * DSA_HD:3 * DSA_HD].astype(BF16)
    o = 4 * DSA_HD
    ki = h[:, o:o + LANES] * ck_ref[...] + h[:, o + LANES:o + 2 * LANES] * sk_ref[...]
    ki_ref[0] = ki.astype(BF16)
    wi_ref[0] = h[:, o + 2 * LANES:o + 3 * LANES] * w_scale


def _dsa_proj(xt, B, S, w_in, tm):
    T = B * S
    n_s = S // tm
    o1 = 3 * DSA_HD + IDX_HEADS * IDX_DIM
    ih = IDX_DIM // 2
    w_qi = w_in[:, 3 * DSA_HD:o1].reshape(D_MODEL, IDX_HEADS, IDX_DIM)
    w_qi_sw = jnp.concatenate([w_qi[..., ih:], w_qi[..., :ih]], axis=-1)
    w_qi_ext = jnp.concatenate([w_qi, w_qi_sw], axis=-1).reshape(D_MODEL, IDX_HEADS * LANES)
    w_ki = w_in[:, o1:o1 + IDX_DIM]
    w_ki_sw = jnp.concatenate([w_ki[:, ih:], w_ki[:, :ih]], axis=-1)
    zpad = lambda w: jnp.concatenate([w, jnp.zeros((w.shape[0], LANES - w.shape[1]), w.dtype)], axis=-1)
    w_ext = jnp.concatenate(
        [w_in[:, :3 * DSA_HD], w_qi_ext, zpad(w_ki), zpad(w_ki_sw), zpad(w_in[:, o1 + IDX_DIM:])],
        axis=-1).astype(BF16)
    cos_h, sin_h = _rope_tables(S, DSA_HEAD_DIM)
    cos_i, sin_i = _rope_tables(S, IDX_DIM)
    ch = jnp.concatenate([cos_h, cos_h], axis=-1)
    sh = jnp.concatenate([-sin_h, sin_h], axis=-1)
    ci = jnp.concatenate([cos_i, cos_i, -sin_i, sin_i], axis=-1)
    z = jnp.zeros((S, LANES - IDX_DIM), F32)
    ck = jnp.concatenate([cos_i, cos_i, z], axis=-1)
    sk = jnp.concatenate([-sin_i, sin_i, z], axis=-1)
    tab = pl.BlockSpec((tm, LANES), lambda i: (i % n_s, 0))
    row_map = lambda i: (i // n_s, i % n_s, 0)
    wide = pl.BlockSpec((1, tm, DSA_HD), row_map)
    narrow = pl.BlockSpec((1, tm, LANES), row_map)
    w_scale = IDX_HEADS ** -0.5 * IDX_DIM ** -0.5
    return pl.pallas_call(
        functools.partial(_dsa_proj_kernel, w_scale=w_scale),
        grid=(T // tm,),
        in_specs=[pl.BlockSpec((tm, D_MODEL), lambda i: (i, 0)),
                  pl.BlockSpec(w_ext.shape, lambda i: (0, 0)),
                  tab, tab, tab, tab, tab],
        out_specs=[wide, wide, wide, wide, narrow, narrow],
        out_shape=[jax.ShapeDtypeStruct((B, S, DSA_HD), BF16),
                   jax.ShapeDtypeStruct((B, S, DSA_HD), BF16),
                   jax.ShapeDtypeStruct((B, S, DSA_HD), BF16),
                   jax.ShapeDtypeStruct((B, S, IDX_HEADS * LANES), BF16),
                   jax.ShapeDtypeStruct((B, S, LANES), BF16),
                   jax.ShapeDtypeStruct((B, S, LANES), F32)],
        compiler_params=_cparams("parallel"),
        name="dsa_proj",
    )(xt, w_ext, ch, sh, ci, ck, sk)


def _dsa_attn_block(q_ref, qi_ref, wi_ref, k_ref, v_ref, ki_ref, o_ref, blk, *, tq, L, topk, scale):
    ki = ki_ref[0, 0:L, :]
    qi = qi_ref[0]
    wi = wi_ref[0]
    score = jnp.zeros((tq, L), F32)
    for hd in range(IDX_HEADS):
        logit = _dot_nt(qi[:, hd * LANES:(hd + 1) * LANES], ki)
        score = score + wi[:, hd:hd + 1] * jnp.maximum(logit, 0.0)
    row = lax.broadcasted_iota(I32, (tq, L), 0)
    col = lax.broadcasted_iota(I32, (tq, L), 1)
    adm = col // CHUNK <= (blk * tq + row) // CHUNK
    score = jnp.where(adm, score + 0.0, NEG_INF)

    def key_to_float(key):
        return lax.bitcast_convert_type(jnp.where(key < 0, key ^ 0x7FFFFFFF, key), F32)

    def search(i, thr):
        cand = thr + lax.shift_left(jnp.int32(1), 31 - i)
        cnt = jnp.sum((score >= key_to_float(cand)).astype(F32), axis=-1, keepdims=True)
        return jnp.where(cnt >= topk, cand, thr)
    thr = lax.fori_loop(0, 32, search, jnp.full((tq, 1), INT_MIN, I32), unroll=8)
    thr_f = key_to_float(jnp.maximum(thr, KEY_NEG_INF))

    gt = score > thr_f
    eq = score == thr_f
    need = topk - jnp.sum(gt.astype(F32), axis=-1, keepdims=True)
    r_i = lax.broadcasted_iota(I32, (LANES, LANES), 0)
    c_i = lax.broadcasted_iota(I32, (LANES, LANES), 1)
    tri = (r_i < c_i).astype(BF16)
    eq_b = eq.astype(BF16)
    run = jnp.zeros((tq, 1), F32)
    ranks = []
    for j in range(L // LANES):
        e = eq_b[:, j * LANES:(j + 1) * LANES]
        ranks.append(_dot(e, tri) + run)
        run = run + jnp.sum(e.astype(F32), axis=-1, keepdims=True)
    rank = jnp.concatenate(ranks, axis=-1)
    sel = adm & (gt | (eq & (rank < need)))

    for hd in range(DSA_HEADS):
        c0 = hd * DSA_HEAD_DIM
        s = _dot_nt(q_ref[0, :, c0:c0 + LANES], k_ref[0, 0:L, c0:c0 + LANES]) * scale
        s = jnp.where(sel, s, NEG_INF)
        m = jnp.max(s, axis=-1, keepdims=True)
        p = jnp.exp(s - m)
        l = jnp.sum(p, axis=-1, keepdims=True)
        o = _dot(p.astype(BF16), v_ref[0, 0:L, c0:c0 + LANES])
        o_ref[0, :, c0:c0 + LANES] = (o * (1.0 / l)).astype(BF16)


def _dsa_attn_kernel(q_ref, qi_ref, wi_ref, k_ref, v_ref, ki_ref, o_ref, *, tq, S, topk, scale):
    blk = pl.program_id(1)
    n_cls = min(DSA_KEY_CLASSES, S // tq)
    width = S // n_cls
    cls = ((blk + 1) * tq - 1) // width
    for c in range(n_cls):
        @pl.when(cls == c)
        def _(c=c):
            _dsa_attn_block(q_ref, qi_ref, wi_ref, k_ref, v_ref, ki_ref, o_ref, blk,
                            tq=tq, L=(c + 1) * width, topk=topk, scale=scale)


def _dsa_attn(q, k, v, qi, ki, wi, tq):
    B, S, _ = q.shape
    topk = min(DSA_TOPK_MAX, S // 4)
    scale = DSA_HEAD_DIM ** -0.5
    qmap = lambda b, i: (b, i, 0)
    kmap = lambda b, i: (b, 0, 0)
    return pl.pallas_call(
        functools.partial(_dsa_attn_kernel, tq=tq, S=S, topk=topk, scale=scale),
        grid=(B, S // tq),
        in_specs=[pl.BlockSpec((1, tq, DSA_HD), qmap),
                  pl.BlockSpec((1, tq, IDX_HEADS * LANES), qmap),
                  pl.BlockSpec((1, tq, LANES), qmap),
                  pl.BlockSpec((1, S, DSA_HD), kmap),
                  pl.BlockSpec((1, S, DSA_HD), kmap),
                  pl.BlockSpec((1, S, LANES), kmap)],
        out_specs=pl.BlockSpec((1, tq, DSA_HD), qmap),
        out_shape=jax.ShapeDtypeStruct((B, S, DSA_HD), BF16),
        compiler_params=_cparams("parallel", "arbitrary"),
        name="dsa_attn",
    )(q, qi, wi, k, v, ki)


_PEER_PAIRS = [(a, b) for a in range(PEER_TOPK) for b in range(PEER_TOPK // (a + 1))]
_PEER_CAND_ROWS = -(-len(_PEER_PAIRS) // SUBLANES) * SUBLANES


def _sort_network(n):
    pairs = []
    p = 1
    while p < n:
        k = p
        while k >= 1:
            for j in range(k % p, n - k, 2 * k):
                for i in range(min(k, n - j - k)):
                    if (i + j) // (2 * p) == (i + j + k) // (2 * p):
                        pairs.append((i + j, i + j + k))
            k //= 2
        p *= 2
    return pairs


def _top_rows(s, n):
    groups = PEER_NKEYS // SUBLANES
    v = [s[g * SUBLANES:(g + 1) * SUBLANES, :] for g in range(groups)]
    for i, j in _sort_network(groups):
        v[i], v[j] = jnp.maximum(v[i], v[j]), jnp.minimum(v[i], v[j])
    vals = []
    for k in range(n):
        m = jnp.max(v[0], axis=0, keepdims=True)
        vals.append(m)
        won = v[0] == m
        for d in range(groups - 1 - k):
            v[d] = jnp.where(won, v[d + 1], v[d])
    return vals


def _peer_route_kernel(x_ref, wq_ref, keys_ref, lim_ref, e1_ref, r2_ref, e2_ref, cand_ref):
    tt = x_ref.shape[0]
    q = _dot(x_ref[...].astype(BF16), wq_ref[...]).astype(BF16)
    half = PEER_QDIM // 2
    pad0 = _PEER_CAND_ROWS - SUBLANES
    cand_ref[pad0:, :] = jnp.full((SUBLANES, tt), NEG_INF, F32)
    for hd in range(PEER_HEADS):
        c0 = hd * PEER_QDIM
        s1_all = _dot_nt(keys_ref[0, hd], q[:, c0:c0 + half])
        s2_all = _dot_nt(keys_ref[1, hd], q[:, c0 + half:c0 + PEER_QDIM])
        for lg in range(tt // LANES):
            ls = slice(lg * LANES, (lg + 1) * LANES)
            s1 = s1_all[:, ls]
            s2 = s2_all[:, ls]
            v1 = _top_rows(s1, PEER_TOPK)
            v2 = _top_rows(s2, PEER_TOPK)
            for k, (a, b) in enumerate(_PEER_PAIRS):
                cand_ref[k:k + 1, ls] = v1[a] + v2[b]
            cand = cand_ref[:, ls]
            cur = cand
            theta = None
            for _ in range(PEER_TOPK):
                theta = jnp.max(cur, axis=0, keepdims=True)
                cur = jnp.where(cur == theta, NEG_INF, cur)
            top = v1[0] + v2[0]
            z = jnp.sum(jnp.where(cand >= theta, jnp.exp(cand - top), 0.0), axis=0, keepdims=True)
            r2 = jnp.full(s2.shape, float(PEER_TOPK), F32)
            for b in reversed(range(PEER_TOPK)):
                r2 = jnp.where(s2 >= v2[b], float(b), r2)
            lim = jnp.zeros(s1.shape, F32)
            for b in range(PEER_TOPK):
                lim = jnp.where(s1 + v2[b] >= theta, float(b + 1), lim)
            lim_ref[hd, :, ls] = lim
            r2_ref[hd, :, ls] = r2.astype(BF16)
            e1_ref[hd, :, ls] = jnp.exp(s1 - v1[0]) * (1.0 / z)
            e2_ref[hd, :, ls] = jnp.exp(s2 - v2[0]).astype(BF16)


def _peer_route(xt, w_q, sub_keys, tt):
    T = xt.shape[0]
    wq = w_q.astype(BF16)
    keys = sub_keys.astype(BF16)
    big = jax.ShapeDtypeStruct((PEER_HEADS, PEER_NKEYS, T), F32)
    big_b = jax.ShapeDtypeStruct((PEER_HEADS, PEER_NKEYS, T), BF16)
    big_spec = pl.BlockSpec((PEER_HEADS, PEER_NKEYS, tt), lambda i: (0, 0, i))
    return pl.pallas_call(
        _peer_route_kernel,
        grid=(T // tt,),
        in_specs=[pl.BlockSpec((tt, D_MODEL), lambda i: (i, 0)),
                  pl.BlockSpec(wq.shape, lambda i: (0, 0)),
                  pl.BlockSpec(keys.shape, lambda i: (0, 0, 0, 0))],
        out_specs=[big_spec, big_spec, big_spec, big_spec],
        out_shape=[big, big, big_b, big_b],
        scratch_shapes=[pltpu.VMEM((_PEER_CAND_ROWS, tt), F32)],
        compiler_params=_cparams("parallel"),
        name="peer_route",
    )(xt, wq, keys)


PEER_CHUNK = 8 * PEER_NKEYS


BF16_SUBLANES = 2 * SUBLANES


def _row_bf16(ref, i1, hd, ls):
    row = jnp.broadcast_to(ref[i1, hd:hd + 1, ls], (BF16_SUBLANES, LANES)).astype(BF16)
    return jnp.tile(row, (PEER_NKEYS // BF16_SUBLANES, 1))


def _peer_dense_kernel(x_ref, wd_ref, wu_ref, lim_ref, e1_ref, r2_ref, e2_ref, g_ref, b_ref,
                       o_ref, xb_ref, p_ref, acc_ref, limr_ref, e1r_ref, r2s_ref, e2s_ref, *, te, tt):
    e = pl.program_id(1)

    @pl.when(e == 0)
    def _():
        xb_ref[...] = x_ref[...].astype(BF16)
        acc_ref[...] = jnp.zeros_like(acc_ref)
        r2s_ref[...] = r2_ref[...]
        e2s_ref[...] = e2_ref[...]

        def relayout(g, carry):
            g0 = pl.multiple_of(g * SUBLANES, SUBLANES)
            for hd in range(PEER_HEADS):
                limg = lim_ref[hd, pl.ds(g0, SUBLANES), :]
                e1g = e1_ref[hd, pl.ds(g0, SUBLANES), :]
                for r in range(SUBLANES):
                    limr_ref[g0 + r, hd:hd + 1, :] = limg[r:r + 1, :]
                    e1r_ref[g0 + r, hd:hd + 1, :] = e1g[r:r + 1, :]
            return carry
        lax.fori_loop(0, PEER_NKEYS // SUBLANES, relayout, 0)

    xb = xb_ref[...]
    n_ch = te // PEER_CHUNK
    i1_0 = e * (te // PEER_NKEYS)
    a_next = _dot_nt(wd_ref[0:PEER_CHUNK, :], xb)
    for c in range(n_ch):
        a_cur = a_next
        if c + 1 < n_ch:
            a_next = _dot_nt(wd_ref[(c + 1) * PEER_CHUNK:(c + 2) * PEER_CHUNK, :], xb)
        for j in range(PEER_CHUNK // PEER_NKEYS):
            i1 = i1_0 + c * (PEER_CHUNK // PEER_NKEYS) + j
            r0 = c * PEER_CHUNK + j * PEER_NKEYS
            for lg in range(tt // LANES):
                ls = slice(lg * LANES, (lg + 1) * LANES)
                gate = jnp.zeros((PEER_NKEYS, LANES), BF16)
                zero = jnp.zeros((), BF16)
                for hd in range(PEER_HEADS):
                    limb = _row_bf16(limr_ref, i1, hd, ls)
                    e1b = _row_bf16(e1r_ref, i1, hd, ls)
                    gate = gate + jnp.where(r2s_ref[hd, :, ls] < limb, e2s_ref[hd, :, ls] * e1b, zero)
                a = a_cur[j * PEER_NKEYS:(j + 1) * PEER_NKEYS, ls]
                act = 0.5 * a * (1.0 + lax.erf(a * (2.0 ** -0.5)))
                p_ref[r0:r0 + PEER_NKEYS, ls] = act.astype(BF16) * gate
        acc_ref[...] += _dot(wu_ref[c], p_ref[c * PEER_CHUNK:(c + 1) * PEER_CHUNK, :])

    @pl.when(e == pl.num_programs(1) - 1)
    def _():
        y = DN_ALPHA * x_ref[...] + acc_ref[...].T
        o_ref[...] = _layernorm_rows(y, g_ref[...], b_ref[...])


def _peer_dense(xt, w_down, w_up, route, gain, bias, tt, te):
    T = xt.shape[0]
    lim, e1, r2, e2 = route
    wd = w_down.astype(BF16)
    wu = w_up.astype(BF16).reshape(PEER_EXPERTS // PEER_CHUNK, PEER_CHUNK, D_MODEL).transpose(0, 2, 1)
    g = gain.reshape(1, -1)
    b = bias.reshape(1, -1)
    big_spec = pl.BlockSpec((PEER_HEADS, PEER_NKEYS, tt), lambda i, e: (0, 0, i))
    return pl.pallas_call(
        functools.partial(_peer_dense_kernel, te=te, tt=tt),
        grid=(T // tt, PEER_EXPERTS // te),
        in_specs=[pl.BlockSpec((tt, D_MODEL), lambda i, e: (i, 0)),
                  pl.BlockSpec((te, D_MODEL), lambda i, e: (e, 0)),
                  pl.BlockSpec((te // PEER_CHUNK, D_MODEL, PEER_CHUNK), lambda i, e: (e, 0, 0)),
                  big_spec, big_spec, big_spec, big_spec,
                  pl.BlockSpec(g.shape, lambda i, e: (0, 0)),
                  pl.BlockSpec(b.shape, lambda i, e: (0, 0))],
        out_specs=pl.BlockSpec((tt, D_MODEL), lambda i, e: (i, 0)),
        out_shape=jax.ShapeDtypeStruct((T, D_MODEL), F32),
        scratch_shapes=[pltpu.VMEM((tt, D_MODEL), BF16),
                        pltpu.VMEM((te, tt), BF16),
                        pltpu.VMEM((D_MODEL, tt), F32),
                        pltpu.VMEM((PEER_NKEYS, PEER_HEADS, tt), F32),
                        pltpu.VMEM((PEER_NKEYS, PEER_HEADS, tt), F32),
                        pltpu.VMEM((PEER_HEADS, PEER_NKEYS, tt), BF16),
                        pltpu.VMEM((PEER_HEADS, PEER_NKEYS, tt), BF16)],
        compiler_params=_cparams("parallel", "arbitrary"),
        name="peer_dense",
    )(xt, wd, wu, lim, e1, r2, e2, g, b)


def _rope_tables(seq, dim):
    inv = ROPE_THETA ** (-jnp.arange(0, dim, 2, dtype=F32) / dim)
    ang = jnp.arange(seq, dtype=F32)[:, None] * inv[None, :]
    return jnp.cos(ang), jnp.sin(ang)


def _tiles(S, T):
    row = min(512, S)
    tq_mla = min(256, S)
    tq_dsa = min(256, S)
    tt_route = min(256, T)
    tt_dense = min(512, T)
    te_dense = 2048
    return row, tq_mla, tq_dsa, tt_route, tt_dense, te_dense


def _peer_layer(xt, w_q, sub_keys, w_down, w_up, gain, bias, tt_route, tt_dense, te_dense):
    route = _peer_route(xt, w_q, sub_keys, tt_route)
    return _peer_dense(xt, w_down, w_up, route, gain, bias, tt_dense, te_dense)


def kernel(x, mla_w_in, mla_q_norm, mla_kv_norm, mla_w_uq, mla_w_ukv, mla_w_o,
           dsa_w_in, dsa_w_o, peer_w_q, peer_sub_keys, peer_w_down, peer_w_up,
           ln_gain, ln_bias):
    B, S, D = x.shape
    T = B * S
    row, tq_mla, tq_dsa, tt_route, tt_dense, te_dense = _tiles(S, T)
    xt = x.reshape(T, D)

    cos, sin = _rope_tables(S, MLA_ROPE)
    q, k, v = _mla_proj(xt, B, S, mla_w_in[0], mla_q_norm[0], mla_kv_norm[0],
                        mla_w_uq[0], mla_w_ukv[0], cos, sin, row)
    o = _mla_attn(q, k, v, tq_mla).reshape(T, MLA_HEADS * MLA_V)
    xt = _proj_ln(o, mla_w_o[0], xt, ln_gain[0, 0], ln_bias[0, 0], row)
    xt = _peer_layer(xt, peer_w_q[0], peer_sub_keys[0], peer_w_down[0], peer_w_up[0],
                     ln_gain[0, 1], ln_bias[0, 1], tt_route, tt_dense, te_dense)

    q, k, v, qi, ki, wi = _dsa_proj(xt, B, S, dsa_w_in[0], row)
    o = _dsa_attn(q, k, v, qi, ki, wi, tq_dsa).reshape(T, DSA_HD)
    xt = _proj_ln(o, dsa_w_o[0], xt, ln_gain[1, 0], ln_bias[1, 0], row)
    xt = _peer_layer(xt, peer_w_q[1], peer_sub_keys[1], peer_w_down[1], peer_w_up[1],
                     ln_gain[1, 1], ln_bias[1, 1], tt_route, tt_dense, te_dense)
    return xt.reshape(B, S, D)
```
